```python
import math
import jax, jax.numpy as jnp
from jax import lax
import numpy as np

D_MODEL = 2048
BATCH = 4
SEQ = 2048
DEPTH = 1

CHUNK = 64
Q_BLOCK = 128
N_DIFF_HEADS = 8
DIFF_HEAD_DIM = D_MODEL // (2 * N_DIFF_HEADS)
DIFF_QK_DIM = DIFF_HEAD_DIM // 2
N_FOX_HEADS = 8
FOX_HEAD_DIM = D_MODEL // (2 * N_FOX_HEADS)
DIFF_WIDTH = N_DIFF_HEADS * DIFF_HEAD_DIM
FOX_WIDTH = N_FOX_HEADS * FOX_HEAD_DIM
N_BRANCHES = 2
PROJ_WIDTH = 3 * DIFF_WIDTH + 3 * FOX_WIDTH + N_FOX_HEADS + N_BRANCHES * D_MODEL
DIFF_SCALE = DIFF_QK_DIM ** -0.5
FOX_SCALE = FOX_HEAD_DIM ** -0.5
REL_BUCKETS = 32
REL_MAX_DIST = 128
N_EXPERTS = 32
TOP_K = 4
D_FF = D_MODEL
SWIGLU_LIMIT = 7.0
SWIGLU_ALPHA = 1.702
EXPERT_BLOCK = 256
PLE_DIM = 256
RMS_EPS = 1e-6

kernel_name = "hybrid_diff_fox_moe_block"


def _rms_norm(x, g):
    xf = x.astype(jnp.float32)
    y = xf * lax.rsqrt(jnp.mean(xf * xf, axis=-1, keepdims=True) + RMS_EPS)
    return (y * g.astype(jnp.float32)).astype(x.dtype)


def _t5_bucket(rel):
    n = -rel
    nb = REL_BUCKETS // 2
    ret = jnp.where(n < 0, nb, 0)
    n = jnp.abs(n)
    max_exact = nb // 2
    large = max_exact + (jnp.log(jnp.maximum(n, 1).astype(jnp.float32) / max_exact)
                         / math.log(REL_MAX_DIST / max_exact) * (nb - max_exact)).astype(jnp.int32)
    large = jnp.minimum(large, nb - 1)
    return ret + jnp.where(n < max_exact, n, large)


def _token_mixers(h, w_in, b_gate, b_forget, dq_g, dk_g, fq_g, fk_g,
                  lq1, lk1, lq2, lk2, lambda_init, subln_g, w_up_a, w_up_b, w_out, rel_bias):
    B, S, D = h.shape
    proj = h @ w_in
    cuts = np.cumsum([DIFF_WIDTH, DIFF_WIDTH, DIFF_WIDTH, FOX_WIDTH, FOX_WIDTH, FOX_WIDTH, N_FOX_HEADS]).tolist()
    dq, dk, dv, fq, fk, fv, f_logit, g_logit = jnp.split(proj, cuts, axis=-1)
    dq = _rms_norm(dq.reshape(B, S, N_DIFF_HEADS, 2, DIFF_QK_DIM), dq_g)
    dk = _rms_norm(dk.reshape(B, S, N_DIFF_HEADS, 2, DIFF_QK_DIM), dk_g)
    dv = dv.reshape(B, S, N_DIFF_HEADS, DIFF_HEAD_DIM)
    fq = _rms_norm(fq.reshape(B, S, N_FOX_HEADS, FOX_HEAD_DIM), fq_g)
    fk = _rms_norm(fk.reshape(B, S, N_FOX_HEADS, FOX_HEAD_DIM), fk_g)
    fv = fv.reshape(B, S, N_FOX_HEADS, FOX_HEAD_DIM)
    log_f = jax.nn.log_sigmoid(f_logit.astype(jnp.float32) + b_forget.astype(jnp.float32))
    cum_f = jnp.transpose(jnp.cumsum(log_f, axis=1), (0, 2, 1))
    lam = (jnp.exp(jnp.sum(lq1.astype(jnp.float32) * lk1.astype(jnp.float32)))
           - jnp.exp(jnp.sum(lq2.astype(jnp.float32) * lk2.astype(jnp.float32))) + lambda_init)
    k_pos = jnp.arange(S)

    def attend_block(start):
        q_pos = start + jnp.arange(Q_BLOCK)
        qd = lax.dynamic_slice_in_dim(dq, start, Q_BLOCK, axis=1)
        s = jnp.einsum('bqhcd,bkhcd->bchqk', qd, dk, preferred_element_type=jnp.float32) * DIFF_SCALE
        rel = k_pos[None, :] - q_pos[:, None]
        bias = jnp.transpose(rel_bias[_t5_bucket(rel)], (2, 0, 1)).astype(jnp.float32)
        chunk_ok = (k_pos[None, :] // CHUNK) <= (q_pos[:, None] // CHUNK)
        s = jnp.where(chunk_ok, s + bias, -jnp.inf)
        pm = jax.nn.softmax(s, axis=-1)
        a = pm[:, 0] - lam * pm[:, 1]
        od = jnp.einsum('bhqk,bkhd->bqhd', a.astype(dv.dtype), dv)
        qf = lax.dynamic_slice_in_dim(fq, start, Q_BLOCK, axis=1)
        sf = jnp.einsum('bqhd,bkhd->bhqk', qf, fk, preferred_element_type=jnp.float32) * FOX_SCALE
        fq_cum = lax.dynamic_slice_in_dim(cum_f, start, Q_BLOCK, axis=2)
        sf = sf + fq_cum[..., None] - cum_f[:, :, None, :]
        causal = k_pos[None, :] <= q_pos[:, None]
        sf = jnp.where(causal, sf, -jnp.inf)
        of = jnp.einsum('bhqk,bkhd->bqhd', jax.nn.softmax(sf, axis=-1).astype(fv.dtype), fv)
        return od, of

    starts = jnp.arange(S // Q_BLOCK) * Q_BLOCK
    od, of = lax.map(attend_block, starts)
    od = jnp.moveaxis(od, 0, 1).reshape(B, S, N_DIFF_HEADS, DIFF_HEAD_DIM)
    of = jnp.moveaxis(of, 0, 1).reshape(B, S, FOX_WIDTH)
    od = (_rms_norm(od, subln_g) * (1.0 - lambda_init)).reshape(B, S, DIFF_WIDTH)
    gates = jax.nn.sigmoid(g_logit.reshape(B, S, N_BRANCHES, D) + b_gate)
    mixed = gates[:, :, 0] * (od @ w_up_a) + gates[:, :, 1] * (of @ w_up_b)
    return mixed @ w_out


def _moe(h, w_router, b_router, w1, b1, w2, b2):
    B, S, D = h.shape
    N = B * S
    t = h.reshape(N, D)
    logits = (t @ w_router).astype(jnp.float32) + b_router.astype(jnp.float32)
    top_val, top_idx = lax.top_k(logits, TOP_K)
    top_w = jax.nn.softmax(top_val, axis=-1)
    A = N * TOP_K
    e_flat = top_idx.reshape(A)
    tok_flat = (jnp.arange(A) // TOP_K).astype(jnp.int32)
    w_flat = top_w.reshape(A)
    order = jnp.argsort(e_flat, stable=True)
    e_sorted = e_flat[order]
    counts = jnp.bincount(e_flat, length=N_EXPERTS)
    padded = ((counts + EXPERT_BLOCK - 1) // EXPERT_BLOCK) * EXPERT_BLOCK
    starts = jnp.cumsum(counts) - counts
    pends = jnp.cumsum(padded)
    pstarts = pends - padded
    dest = pstarts[e_sorted] + (jnp.arange(A) - starts[e_sorted])
    P = A + N_EXPERTS * EXPERT_BLOCK
    slot_tok = jnp.zeros((P,), jnp.int32).at[dest].set(tok_flat[order])
    slot_w = jnp.zeros((P,), jnp.float32).at[dest].set(w_flat[order])
    n_blk = P // EXPERT_BLOCK
    blk_exp = jnp.minimum(jnp.searchsorted(pends, jnp.arange(n_blk) * EXPERT_BLOCK, side='right'),
                          N_EXPERTS - 1)

    def expert_block(args):
        tok, e = args
        hid = t[tok] @ w1[e] + b1[e]
        gate = jnp.minimum(hid[:, :D_FF], SWIGLU_LIMIT)
        lin = jnp.clip(hid[:, D_FF:], -SWIGLU_LIMIT, SWIGLU_LIMIT)
        act = gate * jax.nn.sigmoid(SWIGLU_ALPHA * gate) * (lin + 1.0)
        return act @ w2[e] + b2[e]

    out = lax.map(expert_block, (slot_tok.reshape(n_blk, EXPERT_BLOCK), blk_exp)).reshape(P, D)
    y = jax.ops.segment_sum(out * slot_w[:, None].astype(out.dtype), slot_tok, num_segments=N)
    return y.reshape(B, S, D)


def setup_inputs(seed: int = 0) -> dict:
    key = jax.random.key(seed)
    ks = jax.random.split(key, 32)
    f32 = jnp.float32
    nrm = lambda k, shape, s: jax.random.normal(k, shape, f32) * s
    gain = lambda k, shape: 1.0 + 0.01 * jax.random.normal(k, shape, f32)
    L = DEPTH
    return {
        "x": nrm(ks[0], (BATCH, SEQ, D_MODEL), 1.0),
        "p": nrm(ks[1], (DEPTH, BATCH, SEQ, PLE_DIM), 1.0),
        "w_in": nrm(ks[2], (L, D_MODEL, PROJ_WIDTH), D_MODEL ** -0.5),
        "b_gate": nrm(ks[3], (L, N_BRANCHES, D_MODEL), 0.1),
        "b_forget": 1.0 + nrm(ks[4], (L, N_FOX_HEADS), 0.5),
        "dq_norm": gain(ks[5], (L, DIFF_QK_DIM)),
        "dk_norm": gain(ks[6], (L, DIFF_QK_DIM)),
        "fq_norm": gain(ks[7], (L, FOX_HEAD_DIM)),
        "fk_norm": gain(ks[8], (L, FOX_HEAD_DIM)),
        "lambda_q1": nrm(ks[9], (L, DIFF_QK_DIM), 0.1),
        "lambda_k1": nrm(ks[10], (L, DIFF_QK_DIM), 0.1),
        "lambda_q2": nrm(ks[11], (L, DIFF_QK_DIM), 0.1),
        "lambda_k2": nrm(ks[12], (L, DIFF_QK_DIM), 0.1),
        "subln": gain(ks[13], (L, DIFF_HEAD_DIM)),
        "w_up_a": nrm(ks[14], (L, DIFF_WIDTH, D_MODEL), DIFF_WIDTH ** -0.5),
        "w_up_b": nrm(ks[15], (L, FOX_WIDTH, D_MODEL), FOX_WIDTH ** -0.5),
        "w_out": nrm(ks[16], (L, D_MODEL, D_MODEL), D_MODEL ** -0.5),
        "rel_bias": nrm(ks[17], (REL_BUCKETS, N_DIFF_HEADS), 0.5),
        "norm_mix": gain(ks[18], (L, D_MODEL)),
        "norm_moe": gain(ks[19], (L, D_MODEL)),
        "w_router": nrm(ks[20], (L, D_MODEL, N_EXPERTS), D_MODEL ** -0.5),
        "b_router": nrm(ks[21], (L, N_EXPERTS), 0.01),
        "w1": nrm(ks[22], (L, N_EXPERTS, D_MODEL, 2 * D_FF), D_MODEL ** -0.5),
        "b1": nrm(ks[23], (L, N_EXPERTS, 2 * D_FF), 0.01),
        "w2": nrm(ks[24], (L, N_EXPERTS, D_FF, D_MODEL), D_FF ** -0.5),
        "b2": nrm(ks[25], (L, N_EXPERTS, D_MODEL), 0.01),
        "norm_ple": gain(ks[26], (L, D_MODEL)),
        "w_ple_gate": nrm(ks[27], (L, D_MODEL, D_MODEL), D_MODEL ** -0.5),
        "b_ple_gate": nrm(ks[28], (L, D_MODEL), 0.01),
        "w_ple": nrm(ks[29], (L, PLE_DIM, D_MODEL), PLE_DIM ** -0.5),
    }


def reference(x, p, w_in, b_gate, b_forget, dq_norm, dk_norm, fq_norm, fk_norm,
              lambda_q1, lambda_k1, lambda_q2, lambda_k2, subln, w_up_a, w_up_b, w_out,
              rel_bias, norm_mix, norm_moe, w_router, b_router, w1, b1, w2, b2,
              norm_ple, w_ple_gate, b_ple_gate, w_ple):
    for i in range(DEPTH):
        lambda_init = 0.8 - 0.6 * math.exp(-0.3 * i)
        h = _rms_norm(x, norm_mix[i])
        x = x + _token_mixers(h, w_in[i], b_gate[i], b_forget[i], dq_norm[i], dk_norm[i],
                              fq_norm[i], fk_norm[i], lambda_q1[i], lambda_k1[i], lambda_q2[i],
                              lambda_k2[i], lambda_init, subln[i], w_up_a[i], w_up_b[i], w_out[i],
                              rel_bias)
        x = x + _moe(_rms_norm(x, norm_moe[i]), w_router[i], b_router[i], w1[i], b1[i], w2[i], b2[i])
        g = jax.nn.sigmoid(_rms_norm(x, norm_ple[i]) @ w_ple_gate[i] + b_ple_gate[i])
        x = x + g * (p[i] @ w_ple[i])
    return x
```

```python
import functools
import math

import jax
import jax.numpy as jnp
from jax import lax
from jax.experimental import pallas as pl
from jax.experimental.pallas import tpu as pltpu

F32 = jnp.float32
BF16 = jnp.bfloat16
U32 = jnp.uint32
I32 = jnp.int32
HIGHEST = lax.Precision.HIGHEST

N_HEADS = 8
HEAD_DIM = 128
DIFF_QK_DIM = 64
CHUNK = 64
REL_BUCKETS = 32
REL_MAX_DIST = 128
N_EXPERTS = 32
TOP_K = 4
SWIGLU_LIMIT = 7.0
SWIGLU_ALPHA = 1.702
RMS_EPS = 1e-6
NEG = -1e30

LANES = 128
SLOT_BLOCK = 256
GROUP_SUBS = 8
FF_CHUNK = 256
ATTN_T = 256
VMEM_LIMIT = 56 * 1024 * 1024


def _cparams(sem):
    return pltpu.CompilerParams(dimension_semantics=sem, vmem_limit_bytes=VMEM_LIMIT)


def _rmsnorm_rows(x, g):
    ms = jnp.mean(x * x, axis=-1, keepdims=True)
    return x * lax.rsqrt(ms + RMS_EPS) * g


def _norm_body(x_ref, g_ref, wf_ref, h_ref, fl_ref):
    y = _rmsnorm_rows(x_ref[...], g_ref[...])
    h_ref[...] = y.astype(BF16)
    fl_ref[...] = jnp.dot(y, wf_ref[...], precision=HIGHEST, preferred_element_type=F32)


def _norm_call(x2d, g_row, wf_pad, tm=512):
    n, d = x2d.shape
    tm = min(tm, n)
    return pl.pallas_call(
        _norm_body,
        grid=(n // tm,),
        in_specs=[pl.BlockSpec((tm, d), lambda i: (i, 0)),
                  pl.BlockSpec((1, d), lambda i: (0, 0)),
                  pl.BlockSpec((d, LANES), lambda i: (0, 0))],
        out_specs=[pl.BlockSpec((tm, d), lambda i: (i, 0)),
                   pl.BlockSpec((tm, LANES), lambda i: (i, 0))],
        out_shape=[jax.ShapeDtypeStruct((n, d), BF16),
                   jax.ShapeDtypeStruct((n, LANES), F32)],
        compiler_params=_cparams(("parallel",)),
        name="norm",
    )(x2d, g_row, wf_pad)


def _cum_body(fl_ref, bf_ref, cum_ref):
    s = fl_ref.shape[0]
    r = lax.broadcasted_iota(I32, (LANES, LANES), 0)
    c = lax.broadcasted_iota(I32, (LANES, LANES), 1)
    tri = jnp.where(r >= c, 1.0, 0.0).astype(F32)
    carry = jnp.zeros((1, LANES), F32)
    for blk in range(s // LANES):
        rows = slice(blk * LANES, (blk + 1) * LANES)
        lf = jax.nn.log_sigmoid(fl_ref[rows, :] + bf_ref[...])
        cb = jnp.dot(tri, lf, precision=HIGHEST, preferred_element_type=F32) + carry
        cum_ref[rows, :] = cb
        carry = cb[LANES - 1:LANES, :]


def _cum_call(fl, bf_row, batch, seq):
    return pl.pallas_call(
        _cum_body,
        grid=(batch,),
        in_specs=[pl.BlockSpec((seq, LANES), lambda b: (b, 0)),
                  pl.BlockSpec((1, LANES), lambda b: (0, 0))],
        out_specs=pl.BlockSpec((seq, LANES), lambda b: (b, 0)),
        out_shape=jax.ShapeDtypeStruct(fl.shape, F32),
        compiler_params=_cparams(("parallel",)),
        name="cum",
    )(fl, bf_row)


def _proj_body(h_ref, w_ref, aux_ref, o_ref, *, kind):
    acc = jnp.dot(h_ref[...], w_ref[...].astype(BF16), preferred_element_type=F32)
    tn = acc.shape[1]
    if kind == "plain":
        o_ref[...] = acc.astype(o_ref.dtype)
    elif kind == "gate":
        o_ref[...] = jax.nn.sigmoid(acc + aux_ref[...]).astype(o_ref.dtype)
    else:
        lo = lax.broadcasted_iota(I32, (1, LANES), 1) < DIFF_QK_DIM
        for s in range(tn // LANES):
            cols = slice(s * LANES, (s + 1) * LANES)
            blk = acc[:, cols]
            sq = blk * blk
            if kind == "norm128":
                ms = jnp.mean(sq, axis=-1, keepdims=True)
            else:
                s_lo = jnp.sum(jnp.where(lo, sq, 0.0), axis=-1, keepdims=True)
                s_hi = jnp.sum(jnp.where(lo, 0.0, sq), axis=-1, keepdims=True)
                ms = jnp.where(lo, s_lo, s_hi) * (1.0 / DIFF_QK_DIM)
            o_ref[:, cols] = (blk * lax.rsqrt(ms + RMS_EPS) * aux_ref[:, cols]).astype(o_ref.dtype)


def _proj_call(h, w, col0, ncols, aux_row, kind, name, tm=1024, tn=512):
    m, k = h.shape
    tm = min(tm, m)
    assert col0 % tn == 0 and ncols % tn == 0 and m % tm == 0
    jb0 = col0 // tn
    return pl.pallas_call(
        functools.partial(_proj_body, kind=kind),
        grid=(m // tm, ncols // tn),
        in_specs=[pl.BlockSpec((tm, k), lambda i, j: (i, 0)),
                  pl.BlockSpec((k, tn), lambda i, j: (0, jb0 + j)),
                  pl.BlockSpec((1, tn), lambda i, j: (0, j))],
        out_specs=pl.BlockSpec((tm, tn), lambda i, j: (i, j)),
        out_shape=jax.ShapeDtypeStruct((m, ncols), BF16),
        compiler_params=_cparams(("parallel", "parallel")),
        name=name,
    )(h, w, aux_row)


def _softmax_step(s, m, l, a, v):
    mn = jnp.maximum(m, jnp.max(s, axis=-1, keepdims=True))
    alpha = jnp.exp(m - mn)
    p = jnp.exp(s - mn)
    l = alpha * l + jnp.sum(p, axis=-1, keepdims=True)
    a = alpha * a + jnp.dot(p.astype(BF16), v, preferred_element_type=F32)
    return mn, l, a


_NT = (((1,), (1,)), ((), ()))


def _diff_attn_body(lq1_ref, lk1_ref, lq2_ref, lk2_ref, g_ref, q_ref, k_ref, v_ref, bias_ref,
                    o_ref, *, t, lambda_init):
    qi = pl.program_id(2)
    s_len = k_ref.shape[0]
    q = q_ref[...]
    lane = lax.broadcasted_iota(I32, q.shape, 1)
    zero = jnp.zeros_like(q)
    q1 = jnp.where(lane < DIFF_QK_DIM, q, zero)
    q2 = jnp.where(lane < DIFF_QK_DIM, zero, q)

    def step(kb, carry, masked):
        m1, l1, a1, m2, l2, a2 = carry
        k0 = pl.multiple_of(kb * t, t)
        k = k_ref[pl.ds(k0, t), :]
        v = v_ref[pl.ds(k0, t), :]
        off = pl.multiple_of((kb - qi) * t + (s_len - t), LANES)
        b = bias_ref[:, pl.ds(off, t)]
        s1 = lax.dot_general(q1, k, _NT, preferred_element_type=F32) + b
        s2 = lax.dot_general(q2, k, _NT, preferred_element_type=F32) + b
        if masked:
            r = lax.broadcasted_iota(I32, (t, t), 0)
            c = lax.broadcasted_iota(I32, (t, t), 1)
            ok = (c // CHUNK) <= (r // CHUNK)
            s1 = jnp.where(ok, s1, NEG)
            s2 = jnp.where(ok, s2, NEG)
        m1, l1, a1 = _softmax_step(s1, m1, l1, a1, v)
        m2, l2, a2 = _softmax_step(s2, m2, l2, a2, v)
        return m1, l1, a1, m2, l2, a2

    def init():
        return (jnp.full((t, 1), NEG, F32), jnp.zeros((t, 1), F32), jnp.zeros((t, HEAD_DIM), F32))

    carry = lax.fori_loop(0, qi, lambda kb, cr: step(kb, cr, False), init() + init())
    m1, l1, a1, m2, l2, a2 = step(qi, carry, True)
    lam = (jnp.exp(jnp.sum(lq1_ref[...] * lk1_ref[...], axis=-1, keepdims=True))
           - jnp.exp(jnp.sum(lq2_ref[...] * lk2_ref[...], axis=-1, keepdims=True)) + lambda_init)
    o = a1 / l1 - lam * (a2 / l2)
    o = _rmsnorm_rows(o, g_ref[...]) * (1.0 - lambda_init)
    o_ref[...] = o.astype(o_ref.dtype)


def _diff_attn_call(qk, v, bias, lam_rows, subln_row, lambda_init, t):
    b, s, _ = v.shape
    vec = lambda w: pl.BlockSpec((1, w), lambda h, bb, qi: (0, 0))
    return pl.pallas_call(
        functools.partial(_diff_attn_body, t=t, lambda_init=lambda_init),
        grid=(N_HEADS, b, s // t),
        in_specs=[vec(DIFF_QK_DIM)] * 4 + [
            vec(HEAD_DIM),
            pl.BlockSpec((None, t, HEAD_DIM), lambda h, bb, qi: (bb, qi, h)),
            pl.BlockSpec((None, s, HEAD_DIM), lambda h, bb, qi: (bb, 0, N_HEADS + h)),
            pl.BlockSpec((None, s, HEAD_DIM), lambda h, bb, qi: (bb, 0, h)),
            pl.BlockSpec((None, t, s), lambda h, bb, qi: (h, 0, 0))],
        out_specs=pl.BlockSpec((None, t, HEAD_DIM), lambda h, bb, qi: (bb, qi, h)),
        out_shape=jax.ShapeDtypeStruct(v.shape, BF16),
        compiler_params=_cparams(("parallel", "parallel", "parallel")),
        name="diff_attn",
    )(*lam_rows, subln_row, qk, qk, v, bias)


def _fox_attn_body(q_ref, k_ref, v_ref, cq_ref, ck_ref, o_ref, *, t):
    qi = pl.program_id(2)
    q = q_ref[...]
    cq = cq_ref[...]

    def step(kb, carry, masked):
        m, l, a = carry
        k0 = pl.multiple_of(kb * t, t)
        k = k_ref[pl.ds(k0, t), :]
        v = v_ref[pl.ds(k0, t), :]
        s = lax.dot_general(q, k, _NT, preferred_element_type=F32)
        s = s + cq - ck_ref[:, pl.ds(k0, t)]
        if masked:
            r = lax.broadcasted_iota(I32, (t, t), 0)
            c = lax.broadcasted_iota(I32, (t, t), 1)
            s = jnp.where(c <= r, s, NEG)
        return _softmax_step(s, m, l, a, v)

    init = (jnp.full((t, 1), NEG, F32), jnp.zeros((t, 1), F32), jnp.zeros((t, HEAD_DIM), F32))
    carry = lax.fori_loop(0, qi, lambda kb, cr: step(kb, cr, False), init)
    _, l, a = step(qi, carry, True)
    o_ref[...] = (a / l).astype(o_ref.dtype)


def _fox_attn_call(qk, v, cum_col, cum_row, t):
    b, s, _ = v.shape
    return pl.pallas_call(
        functools.partial(_fox_attn_body, t=t),
        grid=(N_HEADS, b, s // t),
        in_specs=[pl.BlockSpec((None, t, HEAD_DIM), lambda h, bb, qi: (bb, qi, h)),
                  pl.BlockSpec((None, s, HEAD_DIM), lambda h, bb, qi: (bb, 0, N_HEADS + h)),
                  pl.BlockSpec((None, s, HEAD_DIM), lambda h, bb, qi: (bb, 0, h)),
                  pl.BlockSpec((None, None, t, 1), lambda h, bb, qi: (bb, h, qi, 0)),
                  pl.BlockSpec((None, None, 1, s), lambda h, bb, qi: (bb, h, 0, 0))],
        out_specs=pl.BlockSpec((None, t, HEAD_DIM), lambda h, bb, qi: (bb, qi, h)),
        out_shape=jax.ShapeDtypeStruct(v.shape, BF16),
        compiler_params=_cparams(("parallel", "parallel", "parallel")),
        name="fox_attn",
    )(qk, qk, v, cum_col, cum_row)


def _mix_body(od_ref, of_ref, g_ref, x_ref, wa_ref, wb_ref, wo_ref, nm_ref, wr_ref, br_ref,
              x1_ref, hp_ref, lg_ref):
    d = x_ref.shape[1]
    ua = jnp.dot(od_ref[...], wa_ref[...], preferred_element_type=F32)
    ub = jnp.dot(of_ref[...], wb_ref[...], preferred_element_type=F32)
    mixed = g_ref[:, :d].astype(F32) * ua + g_ref[:, d:].astype(F32) * ub
    x1 = x_ref[...] + jnp.dot(mixed.astype(BF16), wo_ref[...], preferred_element_type=F32)
    x1_ref[...] = x1
    hm = _rmsnorm_rows(x1, nm_ref[...])
    lg_ref[...] = jnp.dot(hm, wr_ref[...], precision=HIGHEST, preferred_element_type=F32) + br_ref[...]
    bits = lax.bitcast_convert_type(hm.astype(BF16).astype(F32), U32)
    hp_ref[...] = (bits[:, :d // 2] >> 16) | (bits[:, d // 2:] & jnp.uint32(0xFFFF0000))


def _resident(shape):
    return pl.BlockSpec(shape, lambda i: (0,) * len(shape), pipeline_mode=pl.Buffered(1))


def _mix_call(od, of, gates, x2d, wa, wb, wo, nm_row, wr_pad, br_pad, tm=256):
    n, d = x2d.shape
    tm = min(tm, n)
    wdt = od.shape[1]
    return pl.pallas_call(
        _mix_body,
        grid=(n // tm,),
        in_specs=[pl.BlockSpec((tm, wdt), lambda i: (i, 0)),
                  pl.BlockSpec((tm, wdt), lambda i: (i, 0)),
                  pl.BlockSpec((tm, 2 * d), lambda i: (i, 0)),
                  pl.BlockSpec((tm, d), lambda i: (i, 0)),
                  _resident((wdt, d)), _resident((wdt, d)), _resident((d, d)),
                  _resident((1, d)), _resident((d, LANES)), _resident((1, LANES))],
        out_specs=[pl.BlockSpec((tm, d), lambda i: (i, 0)),
                   pl.BlockSpec((tm, d // 2), lambda i: (i, 0)),
                   pl.BlockSpec((tm, LANES), lambda i: (i, 0))],
        out_shape=[jax.ShapeDtypeStruct((n, d), F32),
                   jax.ShapeDtypeStruct((n, d // 2), U32),
                   jax.ShapeDtypeStruct((n, LANES), F32)],
        compiler_params=_cparams(("parallel",)),
        name="mix",
    )(od, of, gates, x2d, wa, wb, wo, nm_row, wr_pad, br_pad)


def _router_body(lg_ref, rank_ref, jm_ref, wm_ref, cnt_ref, carry_ref):
    i = pl.program_id(0)

    @pl.when(i == 0)
    def _():
        carry_ref[...] = jnp.zeros_like(carry_ref)

    vals = lg_ref[...]
    tm = vals.shape[0]
    lane = lax.broadcasted_iota(I32, vals.shape, 1).astype(F32)
    jm = jnp.zeros(vals.shape, F32)
    tops = []
    for j in range(TOP_K):
        m = jnp.max(vals, axis=-1, keepdims=True)
        idx = jnp.min(jnp.where(vals == m, lane, float(LANES)), axis=-1, keepdims=True)
        sel = lane == idx
        jm = jnp.where(sel, float(j + 1), jm)
        vals = jnp.where(sel, -jnp.inf, vals)
        tops.append(m)
    es = [jnp.exp(m - tops[0]) for m in tops]
    den = es[0] + es[1] + es[2] + es[3]
    wm = jnp.zeros(vals.shape, F32)
    for j in range(TOP_K):
        wm = jnp.where(jm == float(j + 1), es[j] / den, wm)
    sel_any = jnp.where(jm > 0.0, 1.0, 0.0)
    r = lax.broadcasted_iota(I32, (tm, tm), 0)
    c = lax.broadcasted_iota(I32, (tm, tm), 1)
    tri = jnp.where(c < r, 1.0, 0.0).astype(BF16)
    carry = carry_ref[...]
    rank_ref[...] = jnp.dot(tri, sel_any.astype(BF16), preferred_element_type=F32) + carry
    jm_ref[...] = jm
    wm_ref[...] = wm
    carry = carry + jnp.sum(sel_any, axis=0, keepdims=True)
    carry_ref[...] = carry
    cnt_ref[...] = carry


def _router_call(logits, tm=256):
    n = logits.shape[0]
    tm = min(tm, n)
    tile = pl.BlockSpec((tm, LANES), lambda i: (i, 0))
    row = pl.BlockSpec((1, LANES), lambda i: (0, 0))
    return pl.pallas_call(
        _router_body,
        grid=(n // tm,),
        in_specs=[tile],
        out_specs=[tile, tile, tile, row],
        out_shape=[jax.ShapeDtypeStruct((n, LANES), F32)] * 3 + [jax.ShapeDtypeStruct((1, LANES), F32)],
        scratch_shapes=[pltpu.VMEM((1, LANES), F32)],
        compiler_params=_cparams(("arbitrary",)),
        name="router",
    )(logits)


def _dest_body(rank_ref, jm_ref, wm_ref, ps_ref, dest_ref, w4_ref):
    slot = rank_ref[...] + ps_ref[...]
    jm = jm_ref[...]
    wm = wm_ref[...]
    lane = lax.broadcasted_iota(I32, jm.shape, 1)
    dest = jnp.zeros(jm.shape, F32)
    w4 = jnp.zeros(jm.shape, F32)
    for j in range(TOP_K):
        sel = jm == float(j + 1)
        dj = jnp.sum(jnp.where(sel, slot, 0.0), axis=-1, keepdims=True)
        wj = jnp.sum(jnp.where(sel, wm, 0.0), axis=-1, keepdims=True)
        dest = jnp.where(lane == j, dj, dest)
        w4 = jnp.where(lane == j, wj, w4)
    dest_ref[...] = dest.astype(I32)
    w4_ref[...] = w4


def _dest_call(rank, jm, wm, ps_row, tm=512):
    n = rank.shape[0]
    tm = min(tm, n)
    tile = pl.BlockSpec((tm, LANES), lambda i: (i, 0))
    return pl.pallas_call(
        _dest_body,
        grid=(n // tm,),
        in_specs=[tile, tile, tile, pl.BlockSpec((1, LANES), lambda i: (0, 0))],
        out_specs=[tile, tile],
        out_shape=[jax.ShapeDtypeStruct((n, LANES), I32), jax.ShapeDtypeStruct((n, LANES), F32)],
        compiler_params=_cparams(("parallel",)),
        name="dest",
    )(rank, jm, wm, ps_row)


DISPATCH_LAG = 256


def _dispatch_body(dest_ref, cnt_ref, pad_ref, pst_ref, hp_hbm, xs_hbm, sem, *, n_tok):
    def row_copy(src_row, dst_row):
        return pltpu.make_async_copy(hp_hbm.at[pl.ds(src_row, 1), :], xs_hbm.at[pl.ds(dst_row, 1), :], sem)

    def wait_rows(count):
        def w(_, carry):
            row_copy(0, 0).wait()
            return carry
        lax.fori_loop(0, count, w, 0)

    def tok(t, carry):
        for j in range(TOP_K):
            row_copy(t, dest_ref[t * TOP_K + j]).start()

        @pl.when(t >= DISPATCH_LAG)
        def _():
            wait_rows(TOP_K)
        return carry

    lax.fori_loop(0, n_tok, tok, 0)
    wait_rows(min(DISPATCH_LAG, n_tok) * TOP_K)

    def pads(e, carry):
        def one(s, c2):
            row_copy(0, pst_ref[e] + s).start()
            return c2
        lax.fori_loop(cnt_ref[e], pad_ref[e], one, 0)
        wait_rows(pad_ref[e] - cnt_ref[e])
        return carry

    lax.fori_loop(0, N_EXPERTS, pads, 0)


def _dispatch_call(dest_flat, counts, padded, pstart, hp, n_slots):
    n_tok, half = hp.shape
    smem = pl.BlockSpec(memory_space=pltpu.SMEM)
    return pl.pallas_call(
        functools.partial(_dispatch_body, n_tok=n_tok),
        in_specs=[smem, smem, smem, smem, pl.BlockSpec(memory_space=pl.ANY)],
        out_specs=pl.BlockSpec(memory_space=pl.ANY),
        out_shape=jax.ShapeDtypeStruct((n_slots, half), U32),
        scratch_shapes=[pltpu.SemaphoreType.DMA(())],
        compiler_params=pltpu.CompilerParams(vmem_limit_bytes=VMEM_LIMIT, has_side_effects=True),
        name="dispatch",
    )(dest_flat, counts, padded, pstart, hp)


def _expert_body(gexp_ref, gsub0_ref, gns_ref,
                 xs_hbm, w1g_ref, w1l_ref, b1g_ref, b1l_ref, w2_ref, b2_ref,
                 outs_hbm,
                 xlo, xhi, stage, acc, wg_b, wl_b, w2_b, sem_in, sem_out, *, n_chunks):
    g = pl.program_id(0)
    c = pl.program_id(1)
    ns = gns_ref[g]
    s0 = gsub0_ref[g]
    sub = SLOT_BLOCK
    half = xlo.shape[1]

    @pl.when(ns > 0)
    def _():
        @pl.when(c == 0)
        def _load():
            def ld(s, carry):
                row0 = pl.multiple_of((s0 + s) * sub, sub)
                cp = pltpu.make_async_copy(xs_hbm.at[pl.ds(row0, sub), :], stage, sem_in)
                cp.start()
                cp.wait()
                w = stage[...]
                r0 = pl.multiple_of(s * sub, sub)
                xlo[pl.ds(r0, sub), :] = lax.bitcast_convert_type(w << 16, F32).astype(BF16)
                xhi[pl.ds(r0, sub), :] = lax.bitcast_convert_type(w & jnp.uint32(0xFFFF0000), F32).astype(BF16)
                acc[pl.ds(r0, sub), :] = jnp.broadcast_to(b2_ref[...], (sub, acc.shape[1]))
                return carry
            lax.fori_loop(0, ns, ld, 0)

        wg_b[...] = w1g_ref[...].astype(BF16)
        wl_b[...] = w1l_ref[...].astype(BF16)
        w2_b[...] = w2_ref[...].astype(BF16)

        def block(s, carry):
            r0 = pl.multiple_of(s * sub, sub)
            xl = xlo[pl.ds(r0, sub), :]
            xh = xhi[pl.ds(r0, sub), :]
            hg = (jnp.dot(xl, wg_b[:half, :], preferred_element_type=F32)
                  + jnp.dot(xh, wg_b[half:, :], preferred_element_type=F32) + b1g_ref[...])
            hl = (jnp.dot(xl, wl_b[:half, :], preferred_element_type=F32)
                  + jnp.dot(xh, wl_b[half:, :], preferred_element_type=F32) + b1l_ref[...])
            gate = jnp.minimum(hg, SWIGLU_LIMIT)
            lin = jnp.clip(hl, -SWIGLU_LIMIT, SWIGLU_LIMIT)
            act = gate * jax.nn.sigmoid(SWIGLU_ALPHA * gate) * (lin + 1.0)
            acc[pl.ds(r0, sub), :] += jnp.dot(act.astype(BF16), w2_b[...], preferred_element_type=F32)
            return carry
        lax.fori_loop(0, ns, block, 0)

        @pl.when(c == n_chunks - 1)
        def _store():
            def out_copy(s):
                r0 = pl.multiple_of(s * sub, sub)
                row0 = pl.multiple_of((s0 + s) * sub, sub)
                return pltpu.make_async_copy(acc.at[pl.ds(r0, sub), :], outs_hbm.at[pl.ds(row0, sub), :], sem_out)

            def st(s, carry):
                out_copy(s).start()
                return carry

            def wt(s, carry):
                out_copy(s).wait()
                return carry
            lax.fori_loop(0, ns, st, 0)
            lax.fori_loop(0, ns, wt, 0)


def _expert_call(gexp, gsub0, gns, xs, w1, b1, w2, b2):
    n_slots, half = xs.shape
    n_exp, d, ff2 = w1.shape
    ff = ff2 // 2
    tc = FF_CHUNK
    n_chunks = ff // tc
    n_groups = gexp.shape[0]
    rows = GROUP_SUBS * SLOT_BLOCK

    def chunk(c, gn, g):
        return jnp.where(gn[g] > 0, c, n_chunks - 1)

    grid_spec = pltpu.PrefetchScalarGridSpec(
        num_scalar_prefetch=3,
        grid=(n_groups, n_chunks),
        in_specs=[
            pl.BlockSpec(memory_space=pl.ANY),
            pl.BlockSpec((None, d, tc), lambda g, c, ge, gs, gn: (ge[g], 0, chunk(c, gn, g))),
            pl.BlockSpec((None, d, tc), lambda g, c, ge, gs, gn: (ge[g], 0, n_chunks + chunk(c, gn, g))),
            pl.BlockSpec((None, 1, tc), lambda g, c, ge, gs, gn: (ge[g], 0, chunk(c, gn, g))),
            pl.BlockSpec((None, 1, tc), lambda g, c, ge, gs, gn: (ge[g], 0, n_chunks + chunk(c, gn, g))),
            pl.BlockSpec((None, tc, d), lambda g, c, ge, gs, gn: (ge[g], chunk(c, gn, g), 0)),
            pl.BlockSpec((None, 1, d), lambda g, c, ge, gs, gn: (ge[g], 0, 0)),
        ],
        out_specs=pl.BlockSpec(memory_space=pl.ANY),
        scratch_shapes=[
            pltpu.VMEM((rows, half), BF16), pltpu.VMEM((rows, half), BF16),
            pltpu.VMEM((SLOT_BLOCK, half), U32),
            pltpu.VMEM((rows, d), F32),
            pltpu.VMEM((d, tc), BF16), pltpu.VMEM((d, tc), BF16), pltpu.VMEM((tc, d), BF16),
            pltpu.SemaphoreType.DMA(()), pltpu.SemaphoreType.DMA(()),
        ],
    )
    return pl.pallas_call(
        functools.partial(_expert_body, n_chunks=n_chunks),
        grid_spec=grid_spec,
        out_shape=jax.ShapeDtypeStruct((n_slots, d), F32),
        compiler_params=pltpu.CompilerParams(dimension_semantics=("arbitrary", "arbitrary"),
                                             vmem_limit_bytes=VMEM_LIMIT, has_side_effects=True),
        name="expert",
    )(gexp, gsub0, gns, xs, w1, w1, b1.reshape(n_exp, 1, ff2), b1.reshape(n_exp, 1, ff2), w2,
      b2.reshape(n_exp, 1, d))


def _final_body(dest_ref, x1_ref, w4_ref, p_ref, wple_ref, wpg_ref, bpg_ref, npl_ref, outs_hbm,
                o_ref, gbuf, sem, *, tm):
    i = pl.program_id(0)

    def row_copy(slot, j, r):
        return pltpu.make_async_copy(outs_hbm.at[pl.ds(slot, 1), :], gbuf.at[j, pl.ds(r, 1), :], sem)

    def issue(r, carry):
        t = i * tm + r
        for j in range(TOP_K):
            row_copy(dest_ref[t * TOP_K + j], j, r).start()
        return carry
    lax.fori_loop(0, tm, issue, 0)

    pe = jnp.dot(p_ref[...].astype(BF16), wple_ref[...], preferred_element_type=F32)

    def wt(r, carry):
        for j in range(TOP_K):
            row_copy(0, j, r).wait()
        return carry
    lax.fori_loop(0, tm, wt, 0)

    w4 = w4_ref[...]
    y = gbuf[0] * w4[:, 0:1]
    for j in range(1, TOP_K):
        y = y + gbuf[j] * w4[:, j:j + 1]
    x2 = x1_ref[...] + y
    hn = _rmsnorm_rows(x2, npl_ref[...])
    gate = jax.nn.sigmoid(jnp.dot(hn.astype(BF16), wpg_ref[...], preferred_element_type=F32) + bpg_ref[...])
    o_ref[...] = x2 + gate * pe


def _final_call(dest_flat, x1, w4, p2d, wple, wpg, bpg_row, npl_row, outs, tm=256):
    n, d = x1.shape
    tm = min(tm, n)
    pd = p2d.shape[1]
    res = lambda shape: pl.BlockSpec(shape, lambda i, dst: (0,) * len(shape), pipeline_mode=pl.Buffered(1))
    grid_spec = pltpu.PrefetchScalarGridSpec(
        num_scalar_prefetch=1,
        grid=(n // tm,),
        in_specs=[pl.BlockSpec((tm, d), lambda i, dst: (i, 0)),
                  pl.BlockSpec((tm, LANES), lambda i, dst: (i, 0)),
                  pl.BlockSpec((tm, pd), lambda i, dst: (i, 0)),
                  res((pd, d)), res((d, d)), res((1, d)), res((1, d)),
                  pl.BlockSpec(memory_space=pl.ANY)],
        out_specs=pl.BlockSpec((tm, d), lambda i, dst: (i, 0)),
        scratch_shapes=[pltpu.VMEM((TOP_K, tm, d), F32), pltpu.SemaphoreType.DMA(())],
    )
    return pl.pallas_call(
        functools.partial(_final_body, tm=tm),
        grid_spec=grid_spec,
        out_shape=jax.ShapeDtypeStruct((n, d), F32),
        compiler_params=_cparams(("arbitrary",)),
        name="final",
    )(dest_flat, x1, w4, p2d, wple, wpg, bpg_row, npl_row, outs)


def _t5_bucket(rel):
    n = -rel
    nb = REL_BUCKETS // 2
    ret = jnp.where(n < 0, nb, 0)
    n = jnp.abs(n)
    max_exact = nb // 2
    large = max_exact + (jnp.log(jnp.maximum(n, 1).astype(jnp.float32) / max_exact)
                         / math.log(REL_MAX_DIST / max_exact) * (nb - max_exact)).astype(jnp.int32)
    large = jnp.minimum(large, nb - 1)
    return ret + jnp.where(n < max_exact, n, large)


def _diff_bias_table(rel_bias, seq, t):
    i = jnp.arange(t)[:, None]
    cc = jnp.arange(seq)[None, :]
    rel = cc - (seq - t) - i
    return jnp.transpose(rel_bias[_t5_bucket(rel)], (2, 0, 1)).astype(F32)


def _pad_cols(a, width):
    return jnp.pad(a, ((0, 0), (0, width - a.shape[1])))


def _group_table(counts):
    n_sub = (counts + SLOT_BLOCK - 1) // SLOT_BLOCK
    sub_start = jnp.cumsum(n_sub) - n_sub
    n_grp = (n_sub + GROUP_SUBS - 1) // GROUP_SUBS
    grp_end = jnp.cumsum(n_grp)
    total = grp_end[-1]
    return n_sub, sub_start, n_grp, grp_end, total


def _layer(x, p_l, w_in, b_gate, b_forget, dq_norm, dk_norm, fq_norm, fk_norm, lq1, lk1, lq2, lk2,
           lambda_init, subln, w_up_a, w_up_b, w_out, rel_bias, norm_mix, norm_moe, w_router,
           b_router, w1, b1, w2, b2, norm_ple, w_ple_gate, b_ple_gate, w_ple):
    b, s, d = x.shape
    n = b * s
    width = N_HEADS * HEAD_DIM
    t = min(ATTN_T, s)
    x2d = x.reshape(n, d)
    row = lambda v: v.reshape(1, -1).astype(F32)

    f_col = 6 * width
    wf_pad = _pad_cols(w_in[:, f_col:f_col + N_HEADS], LANES)
    h, f_logit = _norm_call(x2d, row(norm_mix), wf_pad)
    cum = _cum_call(f_logit, _pad_cols(row(b_forget), LANES), b, s)
    cum_bhs = jnp.transpose(cum[:, :N_HEADS].reshape(b, s, N_HEADS), (0, 2, 1))
    cum_col = cum_bhs[..., None]
    cum_row = cum_bhs[:, :, None, :]

    diff_scale = DIFF_QK_DIM ** -0.5
    fox_scale = HEAD_DIM ** -0.5
    gain_d = jnp.concatenate([jnp.tile(dq_norm * diff_scale, 2 * N_HEADS), jnp.tile(dk_norm, 2 * N_HEADS)])
    gain_f = jnp.concatenate([jnp.tile(fq_norm * fox_scale, N_HEADS), jnp.tile(fk_norm, N_HEADS)])
    zeros_w = jnp.zeros((1, width), F32)
    dqk = _proj_call(h, w_in, 0, 2 * width, row(gain_d), "norm64", "proj_dqk")
    dv = _proj_call(h, w_in, 2 * width, width, zeros_w, "plain", "proj_dv")
    fqk = _proj_call(h, w_in, 3 * width, 2 * width, row(gain_f), "norm128", "proj_fqk")
    fv = _proj_call(h, w_in, 5 * width, width, zeros_w, "plain", "proj_fv")
    w_gate = w_in[:, f_col + N_HEADS:].astype(BF16)
    gates = _proj_call(h, w_gate, 0, 2 * d, row(b_gate), "gate", "proj_gate")

    lam_rows = [row(v) for v in (lq1, lk1, lq2, lk2)]
    bias = _diff_bias_table(rel_bias, s, t)
    od = _diff_attn_call(dqk.reshape(b, s, 2 * width), dv.reshape(b, s, width), bias, lam_rows,
                         row(subln), lambda_init, t)
    of = _fox_attn_call(fqk.reshape(b, s, 2 * width), fv.reshape(b, s, width), cum_col, cum_row, t)

    br_pad = jnp.full((1, LANES), NEG, F32).at[0, :N_EXPERTS].set(b_router.astype(F32))
    x1, hp, logits = _mix_call(od.reshape(n, width), of.reshape(n, width), gates, x2d,
                               w_up_a.astype(BF16), w_up_b.astype(BF16), w_out.astype(BF16),
                               row(norm_moe), _pad_cols(w_router.astype(F32), LANES), br_pad)

    rank, jm, wm, cnt = _router_call(logits)
    counts = cnt[0, :N_EXPERTS].astype(I32)
    padded = ((counts + SLOT_BLOCK - 1) // SLOT_BLOCK) * SLOT_BLOCK
    pstart = jnp.cumsum(padded) - padded
    ps_row = _pad_cols(pstart.astype(F32).reshape(1, -1), LANES)
    dest128, w4 = _dest_call(rank, jm, wm, ps_row)
    dest_flat = dest128[:, :TOP_K].reshape(-1)

    n_slots = n * TOP_K + N_EXPERTS * SLOT_BLOCK
    xs = _dispatch_call(dest_flat, counts, padded, pstart, hp, n_slots)

    n_sub, sub_start, n_grp, grp_end, total = _group_table(counts)
    max_groups = N_EXPERTS + (n_slots // SLOT_BLOCK) // GROUP_SUBS
    gidx = jnp.arange(max_groups, dtype=I32)
    gvalid = gidx < total
    gsafe = jnp.minimum(gidx, total - 1)
    gexp = jnp.searchsorted(grp_end, gsafe, side="right").astype(I32)
    kth = gsafe - (grp_end - n_grp)[gexp]
    gsub0 = (sub_start[gexp] + kth * GROUP_SUBS).astype(I32)
    gns = jnp.where(gvalid, jnp.minimum(GROUP_SUBS, n_sub[gexp] - kth * GROUP_SUBS), 0).astype(I32)
    outs = _expert_call(gexp, gsub0, gns, xs, w1, b1, w2, b2)

    out = _final_call(dest_flat, x1, w4, p_l.reshape(n, -1), w_ple.astype(BF16),
                      w_ple_gate.astype(BF16), row(b_ple_gate), row(norm_ple), outs)
    return out.reshape(b, s, d)


def kernel(x, p, w_in, b_gate, b_forget, dq_norm, dk_norm, fq_norm, fk_norm, lambda_q1, lambda_k1,
           lambda_q2, lambda_k2, subln, w_up_a, w_up_b, w_out, rel_bias, norm_mix, norm_moe,
           w_router, b_router, w1, b1, w2, b2, norm_ple, w_ple_gate, b_ple_gate, w_ple):
    for i in range(w_in.shape[0]):
        lambda_init = 0.8 - 0.6 * math.exp(-0.3 * i)
        x = _layer(x, p[i], w_in[i], b_gate[i], b_forget[i], dq_norm[i], dk_norm[i], fq_norm[i],
                   fk_norm[i], lambda_q1[i], lambda_k1[i], lambda_q2[i], lambda_k2[i], lambda_init,
                   subln[i], w_up_a[i], w_up_b[i], w_out[i], rel_bias, norm_mix[i], norm_moe[i],
                   w_router[i], b_router[i], w1[i], b1[i], w2[i], b2[i], norm_ple[i],
                   w_ple_gate[i], b_ple_gate[i], w_ple[i])
    return x
```

```python
import functools
import math

import jax
import jax.numpy as jnp
from jax import lax
from jax.experimental import pallas as pl
from jax.experimental.pallas import tpu as pltpu

F32 = jnp.float32
BF16 = jnp.bfloat16
U32 = jnp.uint32
I32 = jnp.int32
HIGHEST = lax.Precision.HIGHEST

N_HEADS = 8
HEAD_DIM = 128
DIFF_QK_DIM = 64
CHUNK = 64
REL_BUCKETS = 32
REL_MAX_DIST = 128
N_EXPERTS = 32
TOP_K = 4
SWIGLU_LIMIT = 7.0
SWIGLU_ALPHA = 1.702
RMS_EPS = 1e-6
NEG = -1e30

LANES = 128
SLOT_BLOCK = 256
GROUP_SUBS = 6
FF_CHUNK = 256
ATTN_T = 256
VMEM_LIMIT = 56 * 1024 * 1024


def _cparams(sem):
    return pltpu.CompilerParams(dimension_semantics=sem, vmem_limit_bytes=VMEM_LIMIT)


def _rmsnorm_rows(x, g):
    ms = jnp.mean(x * x, axis=-1, keepdims=True)
    return x * lax.rsqrt(ms + RMS_EPS) * g


def _norm_body(x_ref, g_ref, wf_ref, h_ref, fl_ref):
    y = _rmsnorm_rows(x_ref[...], g_ref[...])
    h_ref[...] = y.astype(BF16)
    fl_ref[...] = jnp.dot(y, wf_ref[...], precision=HIGHEST, preferred_element_type=F32)


def _norm_call(x2d, g_row, wf_pad, tm=512):
    n, d = x2d.shape
    tm = min(tm, n)
    return pl.pallas_call(
        _norm_body,
        grid=(n // tm,),
        in_specs=[pl.BlockSpec((tm, d), lambda i: (i, 0)),
                  pl.BlockSpec((1, d), lambda i: (0, 0)),
                  pl.BlockSpec((d, LANES), lambda i: (0, 0))],
        out_specs=[pl.BlockSpec((tm, d), lambda i: (i, 0)),
                   pl.BlockSpec((tm, LANES), lambda i: (i, 0))],
        out_shape=[jax.ShapeDtypeStruct((n, d), BF16),
                   jax.ShapeDtypeStruct((n, LANES), F32)],
        compiler_params=_cparams(("parallel",)),
        name="norm",
    )(x2d, g_row, wf_pad)


def _cum_body(fl_ref, bf_ref, cum_ref):
    s = fl_ref.shape[0]
    r = lax.broadcasted_iota(I32, (LANES, LANES), 0)
    c = lax.broadcasted_iota(I32, (LANES, LANES), 1)
    tri = jnp.where(r >= c, 1.0, 0.0).astype(F32)
    carry = jnp.zeros((1, LANES), F32)
    for blk in range(s // LANES):
        rows = slice(blk * LANES, (blk + 1) * LANES)
        lf = jax.nn.log_sigmoid(fl_ref[rows, :] + bf_ref[...])
        cb = jnp.dot(tri, lf, precision=HIGHEST, preferred_element_type=F32) + carry
        cum_ref[rows, :] = cb
        carry = cb[LANES - 1:LANES, :]


def _cum_call(fl, bf_row, batch, seq):
    return pl.pallas_call(
        _cum_body,
        grid=(batch,),
        in_specs=[pl.BlockSpec((seq, LANES), lambda b: (b, 0)),
                  pl.BlockSpec((1, LANES), lambda b: (0, 0))],
        out_specs=pl.BlockSpec((seq, LANES), lambda b: (b, 0)),
        out_shape=jax.ShapeDtypeStruct(fl.shape, F32),
        compiler_params=_cparams(("parallel",)),
        name="cum",
    )(fl, bf_row)


def _proj_body(h_ref, w_ref, aux_ref, o_ref, *, kind):
    acc = jnp.dot(h_ref[...], w_ref[...].astype(BF16), preferred_element_type=F32)
    tn = acc.shape[1]
    if kind == "plain":
        o_ref[...] = acc.astype(o_ref.dtype)
    elif kind == "gate":
        o_ref[...] = jax.nn.sigmoid(acc + aux_ref[...]).astype(o_ref.dtype)
    else:
        lo = lax.broadcasted_iota(I32, (1, LANES), 1) < DIFF_QK_DIM
        for s in range(tn // LANES):
            cols = slice(s * LANES, (s + 1) * LANES)
            blk = acc[:, cols]
            sq = blk * blk
            if kind == "norm128":
                ms = jnp.mean(sq, axis=-1, keepdims=True)
            else:
                s_lo = jnp.sum(jnp.where(lo, sq, 0.0), axis=-1, keepdims=True)
                s_hi = jnp.sum(jnp.where(lo, 0.0, sq), axis=-1, keepdims=True)
                ms = jnp.where(lo, s_lo, s_hi) * (1.0 / DIFF_QK_DIM)
            o_ref[:, cols] = (blk * lax.rsqrt(ms + RMS_EPS) * aux_ref[:, cols]).astype(o_ref.dtype)


def _proj_call(h, w, col0, ncols, aux_row, kind, name, tm=1024, tn=512):
    m, k = h.shape
    tm = min(tm, m)
    assert col0 % tn == 0 and ncols % tn == 0 and m % tm == 0
    jb0 = col0 // tn
    return pl.pallas_call(
        functools.partial(_proj_body, kind=kind),
        grid=(m // tm, ncols // tn),
        in_specs=[pl.BlockSpec((tm, k), lambda i, j: (i, 0)),
                  pl.BlockSpec((k, tn), lambda i, j: (0, jb0 + j)),
                  pl.BlockSpec((1, tn), lambda i, j: (0, j))],
        out_specs=pl.BlockSpec((tm, tn), lambda i, j: (i, j)),
        out_shape=jax.ShapeDtypeStruct((m, ncols), BF16),
        compiler_params=_cparams(("parallel", "parallel")),
        name=name,
    )(h, w, aux_row)


def _softmax_step(s, m, l, a, v):
    mn = jnp.maximum(m, jnp.max(s, axis=-1, keepdims=True))
    alpha = jnp.exp(m - mn)
    p = jnp.exp(s - mn)
    l = alpha * l + jnp.sum(p, axis=-1, keepdims=True)
    a = alpha * a + jnp.dot(p.astype(BF16), v, preferred_element_type=F32)
    return mn, l, a


_NT = (((1,), (1,)), ((), ()))


def _diff_attn_body(lq1_ref, lk1_ref, lq2_ref, lk2_ref, g_ref, q_ref, k_ref, v_ref, bias_ref,
                    o_ref, *, t, lambda_init):
    qi = pl.program_id(2)
    s_len = k_ref.shape[0]
    q = q_ref[...]
    lane = lax.broadcasted_iota(I32, q.shape, 1)
    zero = jnp.zeros_like(q)
    q1 = jnp.where(lane < DIFF_QK_DIM, q, zero)
    q2 = jnp.where(lane < DIFF_QK_DIM, zero, q)

    def step(kb, carry, masked):
        m1, l1, a1, m2, l2, a2 = carry
        k0 = pl.multiple_of(kb * t, t)
        k = k_ref[pl.ds(k0, t), :]
        v = v_ref[pl.ds(k0, t), :]
        off = pl.multiple_of((kb - qi) * t + (s_len - t), LANES)
        b = bias_ref[:, pl.ds(off, t)]
        s1 = lax.dot_general(q1, k, _NT, preferred_element_type=F32) + b
        s2 = lax.dot_general(q2, k, _NT, preferred_element_type=F32) + b
        if masked:
            r = lax.broadcasted_iota(I32, (t, t), 0)
            c = lax.broadcasted_iota(I32, (t, t), 1)
            ok = (c // CHUNK) <= (r // CHUNK)
            s1 = jnp.where(ok, s1, NEG)
            s2 = jnp.where(ok, s2, NEG)
        m1, l1, a1 = _softmax_step(s1, m1, l1, a1, v)
        m2, l2, a2 = _softmax_step(s2, m2, l2, a2, v)
        return m1, l1, a1, m2, l2, a2

    def init():
        return (jnp.full((t, 1), NEG, F32), jnp.zeros((t, 1), F32), jnp.zeros((t, HEAD_DIM), F32))

    carry = lax.fori_loop(0, qi, lambda kb, cr: step(kb, cr, False), init() + init())
    m1, l1, a1, m2, l2, a2 = step(qi, carry, True)
    lam = (jnp.exp(jnp.sum(lq1_ref[...] * lk1_ref[...], axis=-1, keepdims=True))
           - jnp.exp(jnp.sum(lq2_ref[...] * lk2_ref[...], axis=-1, keepdims=True)) + lambda_init)
    o = a1 / l1 - lam * (a2 / l2)
    o = _rmsnorm_rows(o, g_ref[...]) * (1.0 - lambda_init)
    o_ref[...] = o.astype(o_ref.dtype)


def _diff_attn_call(qk, v, bias, lam_rows, subln_row, lambda_init, t):
    b, s, _ = v.shape
    vec = lambda w: pl.BlockSpec((1, w), lambda h, bb, qi: (0, 0))
    return pl.pallas_call(
        functools.partial(_diff_attn_body, t=t, lambda_init=lambda_init),
        grid=(N_HEADS, b, s // t),
        in_specs=[vec(DIFF_QK_DIM)] * 4 + [
            vec(HEAD_DIM),
            pl.BlockSpec((None, t, HEAD_DIM), lambda h, bb, qi: (bb, qi, h)),
            pl.BlockSpec((None, s, HEAD_DIM), lambda h, bb, qi: (bb, 0, N_HEADS + h)),
            pl.BlockSpec((None, s, HEAD_DIM), lambda h, bb, qi: (bb, 0, h)),
            pl.BlockSpec((None, t, s), lambda h, bb, qi: (h, 0, 0))],
        out_specs=pl.BlockSpec((None, t, HEAD_DIM), lambda h, bb, qi: (bb, qi, h)),
        out_shape=jax.ShapeDtypeStruct(v.shape, BF16),
        compiler_params=_cparams(("parallel", "parallel", "parallel")),
        name="diff_attn",
    )(*lam_rows, subln_row, qk, qk, v, bias)


def _fox_attn_body(q_ref, k_ref, v_ref, cq_ref, ck_ref, o_ref, *, t):
    qi = pl.program_id(2)
    q = q_ref[...]
    cq = cq_ref[...]

    def step(kb, carry, masked):
        m, l, a = carry
        k0 = pl.multiple_of(kb * t, t)
        k = k_ref[pl.ds(k0, t), :]
        v = v_ref[pl.ds(k0, t), :]
        s = lax.dot_general(q, k, _NT, preferred_element_type=F32)
        s = s + cq - ck_ref[:, pl.ds(k0, t)]
        if masked:
            r = lax.broadcasted_iota(I32, (t, t), 0)
            c = lax.broadcasted_iota(I32, (t, t), 1)
            s = jnp.where(c <= r, s, NEG)
        return _softmax_step(s, m, l, a, v)

    init = (jnp.full((t, 1), NEG, F32), jnp.zeros((t, 1), F32), jnp.zeros((t, HEAD_DIM), F32))
    carry = lax.fori_loop(0, qi, lambda kb, cr: step(kb, cr, False), init)
    _, l, a = step(qi, carry, True)
    o_ref[...] = (a / l).astype(o_ref.dtype)


def _fox_attn_call(qk, v, cum_col, cum_row, t):
    b, s, _ = v.shape
    return pl.pallas_call(
        functools.partial(_fox_attn_body, t=t),
        grid=(N_HEADS, b, s // t),
        in_specs=[pl.BlockSpec((None, t, HEAD_DIM), lambda h, bb, qi: (bb, qi, h)),
                  pl.BlockSpec((None, s, HEAD_DIM), lambda h, bb, qi: (bb, 0, N_HEADS + h)),
                  pl.BlockSpec((None, s, HEAD_DIM), lambda h, bb, qi: (bb, 0, h)),
                  pl.BlockSpec((None, None, t, 1), lambda h, bb, qi: (bb, h, qi, 0)),
                  pl.BlockSpec((None, None, 1, s), lambda h, bb, qi: (bb, h, 0, 0))],
        out_specs=pl.BlockSpec((None, t, HEAD_DIM), lambda h, bb, qi: (bb, qi, h)),
        out_shape=jax.ShapeDtypeStruct(v.shape, BF16),
        compiler_params=_cparams(("parallel", "parallel", "parallel")),
        name="fox_attn",
    )(qk, qk, v, cum_col, cum_row)


def _mix_body(od_ref, of_ref, g_ref, x_ref, wa_ref, wb_ref, wo_ref, nm_ref, wr_ref, br_ref,
              x1_ref, hp_ref, lg_ref):
    d = x_ref.shape[1]
    ua = jnp.dot(od_ref[...], wa_ref[...], preferred_element_type=F32)
    ub = jnp.dot(of_ref[...], wb_ref[...], preferred_element_type=F32)
    mixed = g_ref[:, :d].astype(F32) * ua + g_ref[:, d:].astype(F32) * ub
    x1 = x_ref[...] + jnp.dot(mixed.astype(BF16), wo_ref[...], preferred_element_type=F32)
    x1_ref[...] = x1
    hm = _rmsnorm_rows(x1, nm_ref[...])
    lg_ref[...] = jnp.dot(hm, wr_ref[...], precision=HIGHEST, preferred_element_type=F32) + br_ref[...]
    bits = lax.bitcast_convert_type(hm.astype(BF16).astype(F32), U32)
    hp_ref[...] = (bits[:, :d // 2] >> 16) | (bits[:, d // 2:] & jnp.uint32(0xFFFF0000))


def _resident(shape):
    return pl.BlockSpec(shape, lambda i: (0,) * len(shape), pipeline_mode=pl.Buffered(1))


def _mix_call(od, of, gates, x2d, wa, wb, wo, nm_row, wr_pad, br_pad, tm=256):
    n, d = x2d.shape
    tm = min(tm, n)
    wdt = od.shape[1]
    return pl.pallas_call(
        _mix_body,
        grid=(n // tm,),
        in_specs=[pl.BlockSpec((tm, wdt), lambda i: (i, 0)),
                  pl.BlockSpec((tm, wdt), lambda i: (i, 0)),
                  pl.BlockSpec((tm, 2 * d), lambda i: (i, 0)),
                  pl.BlockSpec((tm, d), lambda i: (i, 0)),
                  _resident((wdt, d)), _resident((wdt, d)), _resident((d, d)),
                  _resident((1, d)), _resident((d, LANES)), _resident((1, LANES))],
        out_specs=[pl.BlockSpec((tm, d), lambda i: (i, 0)),
                   pl.BlockSpec((tm, d // 2), lambda i: (i, 0)),
                   pl.BlockSpec((tm, LANES), lambda i: (i, 0))],
        out_shape=[jax.ShapeDtypeStruct((n, d), F32),
                   jax.ShapeDtypeStruct((n, d // 2), U32),
                   jax.ShapeDtypeStruct((n, LANES), F32)],
        compiler_params=_cparams(("parallel",)),
        name="mix",
    )(od, of, gates, x2d, wa, wb, wo, nm_row, wr_pad, br_pad)


def _router_body(lg_ref, rank_ref, jm_ref, wm_ref, cnt_ref, carry_ref):
    i = pl.program_id(0)

    @pl.when(i == 0)
    def _():
        carry_ref[...] = jnp.zeros_like(carry_ref)

    vals = lg_ref[...]
    tm = vals.shape[0]
    lane = lax.broadcasted_iota(I32, vals.shape, 1).astype(F32)
    jm = jnp.zeros(vals.shape, F32)
    tops = []
    for j in range(TOP_K):
        m = jnp.max(vals, axis=-1, keepdims=True)
        idx = jnp.min(jnp.where(vals == m, lane, float(LANES)), axis=-1, keepdims=True)
        sel = lane == idx
        jm = jnp.where(sel, float(j + 1), jm)
        vals = jnp.where(sel, -jnp.inf, vals)
        tops.append(m)
    es = [jnp.exp(m - tops[0]) for m in tops]
    den = es[0] + es[1] + es[2] + es[3]
    wm = jnp.zeros(vals.shape, F32)
    for j in range(TOP_K):
        wm = jnp.where(jm == float(j + 1), es[j] / den, wm)
    sel_any = jnp.where(jm > 0.0, 1.0, 0.0)
    r = lax.broadcasted_iota(I32, (tm, tm), 0)
    c = lax.broadcasted_iota(I32, (tm, tm), 1)
    tri = jnp.where(c < r, 1.0, 0.0).astype(BF16)
    carry = carry_ref[...]
    rank_ref[...] = jnp.dot(tri, sel_any.astype(BF16), preferred_element_type=F32) + carry
    jm_ref[...] = jm
    wm_ref[...] = wm
    carry = carry + jnp.sum(sel_any, axis=0, keepdims=True)
    carry_ref[...] = carry
    cnt_ref[...] = carry


def _router_call(logits, tm=256):
    n = logits.shape[0]
    tm = min(tm, n)
    tile = pl.BlockSpec((tm, LANES), lambda i: (i, 0))
    row = pl.BlockSpec((1, LANES), lambda i: (0, 0))
    return pl.pallas_call(
        _router_body,
        grid=(n // tm,),
        in_specs=[tile],
        out_specs=[tile, tile, tile, row],
        out_shape=[jax.ShapeDtypeStruct((n, LANES), F32)] * 3 + [jax.ShapeDtypeStruct((1, LANES), F32)],
        scratch_shapes=[pltpu.VMEM((1, LANES), F32)],
        compiler_params=_cparams(("arbitrary",)),
        name="router",
    )(logits)


def _dest_body(rank_ref, jm_ref, wm_ref, ps_ref, dest_ref, w4_ref):
    slot = rank_ref[...] + ps_ref[...]
    jm = jm_ref[...]
    wm = wm_ref[...]
    lane = lax.broadcasted_iota(I32, jm.shape, 1)
    dest = jnp.zeros(jm.shape, F32)
    w4 = jnp.zeros(jm.shape, F32)
    for j in range(TOP_K):
        sel = jm == float(j + 1)
        dj = jnp.sum(jnp.where(sel, slot, 0.0), axis=-1, keepdims=True)
        wj = jnp.sum(jnp.where(sel, wm, 0.0), axis=-1, keepdims=True)
        dest = jnp.where(lane == j, dj, dest)
        w4 = jnp.where(lane == j, wj, w4)
    dest_ref[...] = dest.astype(I32)
    w4_ref[...] = w4


def _dest_call(rank, jm, wm, ps_row, tm=512):
    n = rank.shape[0]
    tm = min(tm, n)
    tile = pl.BlockSpec((tm, LANES), lambda i: (i, 0))
    return pl.pallas_call(
        _dest_body,
        grid=(n // tm,),
        in_specs=[tile, tile, tile, pl.BlockSpec((1, LANES), lambda i: (0, 0))],
        out_specs=[tile, tile],
        out_shape=[jax.ShapeDtypeStruct((n, LANES), I32), jax.ShapeDtypeStruct((n, LANES), F32)],
        compiler_params=_cparams(("parallel",)),
        name="dest",
    )(rank, jm, wm, ps_row)


def _slot_table_body(dest_ref, cnt_ref, pad_ref, pst_ref, slot_ref, *, n_tok):
    def pads(e, carry):
        def one(s, c2):
            slot_ref[pst_ref[e] + s] = 0
            return c2
        lax.fori_loop(cnt_ref[e], pad_ref[e], one, 0)
        return carry
    lax.fori_loop(0, N_EXPERTS, pads, 0)

    def tail(s, carry):
        slot_ref[s] = 0
        return carry
    lax.fori_loop(pst_ref[N_EXPERTS - 1] + pad_ref[N_EXPERTS - 1], slot_ref.shape[0], tail, 0)

    def tok(t, carry):
        for j in range(TOP_K):
            slot_ref[dest_ref[t * TOP_K + j]] = t
        return carry
    lax.fori_loop(0, n_tok, tok, 0)


def _slot_table_call(dest_flat, counts, padded, pstart, n_slots):
    smem = pl.BlockSpec(memory_space=pltpu.SMEM)
    return pl.pallas_call(
        functools.partial(_slot_table_body, n_tok=dest_flat.shape[0] // TOP_K),
        in_specs=[smem, smem, smem, smem],
        out_specs=smem,
        out_shape=jax.ShapeDtypeStruct((n_slots,), I32),
        name="slot_table",
    )(dest_flat, counts, padded, pstart)


ROW_UNROLL = 8


def _expert_body(gexp_ref, gsub0_ref, gns_ref, slot_ref,
                 hp_hbm, w1g_ref, w1l_ref, b1g_ref, b1l_ref, w2_ref, b2_ref,
                 outs_hbm,
                 xlo, xhi, stage, acc, sem_in, sem_out, *, n_chunks, n_groups):
    g = pl.program_id(0)
    c = pl.program_id(1)
    ns = gns_ref[g]
    s0 = gsub0_ref[g]
    sub = SLOT_BLOCK
    half = xlo.shape[1]
    last = n_chunks - 1

    def row_copy(tok, r):
        return pltpu.make_async_copy(hp_hbm.at[pl.ds(tok, 1), :], stage.at[pl.ds(r, 1), :], sem_in)

    def issue_rows(gi):
        base = gsub0_ref[gi] * sub

        def trip(i, carry):
            for u in range(ROW_UNROLL):
                r = i * ROW_UNROLL + u
                row_copy(slot_ref[base + r], r).start()
            return carry
        lax.fori_loop(0, gns_ref[gi] * (sub // ROW_UNROLL), trip, 0)

    def out_copy(r0, rows):
        row0 = s0 * sub + r0
        return pltpu.make_async_copy(acc.at[pl.ds(r0, rows), :], outs_hbm.at[pl.ds(row0, rows), :], sem_out)

    def compute(r0, rows):
        xl = xlo[pl.ds(r0, rows), :]
        xh = xhi[pl.ds(r0, rows), :]
        hg = (jnp.dot(xl, w1g_ref[:half, :].astype(BF16), preferred_element_type=F32)
              + jnp.dot(xh, w1g_ref[half:, :].astype(BF16), preferred_element_type=F32) + b1g_ref[...])
        hl = (jnp.dot(xl, w1l_ref[:half, :].astype(BF16), preferred_element_type=F32)
              + jnp.dot(xh, w1l_ref[half:, :].astype(BF16), preferred_element_type=F32) + b1l_ref[...])
        gate = jnp.minimum(hg, SWIGLU_LIMIT)
        lin = jnp.clip(hl, -SWIGLU_LIMIT, SWIGLU_LIMIT)
        act = gate * jax.nn.sigmoid(SWIGLU_ALPHA * gate) * (lin + 1.0)
        acc[pl.ds(r0, rows), :] += jnp.dot(act.astype(BF16), w2_ref[...].astype(BF16),
                                           preferred_element_type=F32)

        @pl.when(c == last)
        def _():
            out_copy(r0, rows).start()

    @pl.when(ns > 0)
    def _():
        @pl.when(c == 0)
        def _load():
            @pl.when(g == 0)
            def _():
                issue_rows(0)

            def wt(i, carry):
                for u in range(ROW_UNROLL):
                    row_copy(0, i * ROW_UNROLL + u).wait()
                return carry
            lax.fori_loop(0, ns * (sub // ROW_UNROLL), wt, 0)

            def unpack(s, carry):
                r0 = pl.multiple_of(s * sub, sub)
                w = stage[pl.ds(r0, sub), :]
                xlo[pl.ds(r0, sub), :] = lax.bitcast_convert_type(w << 16, F32).astype(BF16)
                xhi[pl.ds(r0, sub), :] = lax.bitcast_convert_type(w & jnp.uint32(0xFFFF0000), F32).astype(BF16)
                acc[pl.ds(r0, sub), :] = jnp.broadcast_to(b2_ref[...], (sub, acc.shape[1]))
                return carry
            lax.fori_loop(0, ns, unpack, 0)

        @pl.when(c == min(1, last))
        def _prefetch():
            nxt = jnp.minimum(g + 1, n_groups - 1)

            @pl.when(jnp.logical_and(g + 1 < n_groups, gns_ref[nxt] > 0))
            def _():
                issue_rows(nxt)

        def pair(i, carry):
            compute(pl.multiple_of(i * (2 * sub), 2 * sub), 2 * sub)
            return carry
        lax.fori_loop(0, ns // 2, pair, 0)

        @pl.when(ns % 2 == 1)
        def _():
            compute(pl.multiple_of((ns - 1) * sub, sub), sub)

        @pl.when(c == last)
        def _drain():
            def wt(i, carry):
                out_copy(pl.multiple_of(i * (2 * sub), 2 * sub), 2 * sub).wait()
                return carry
            lax.fori_loop(0, ns // 2, wt, 0)

            @pl.when(ns % 2 == 1)
            def _():
                out_copy(pl.multiple_of((ns - 1) * sub, sub), sub).wait()

            nxt = jnp.minimum(g + 1, n_groups - 1)

            @pl.when(jnp.logical_or(g + 1 >= n_groups, gns_ref[nxt] == 0))
            def _tail():
                acc[pl.ds(0, sub), :] = jnp.zeros((sub, acc.shape[1]), F32)
                first = s0 + ns
                n_tail = outs_hbm.shape[0] // sub - first

                def tail_copy(i):
                    row0 = pl.multiple_of((first + i) * sub, sub)
                    return pltpu.make_async_copy(acc.at[pl.ds(0, sub), :], outs_hbm.at[pl.ds(row0, sub), :],
                                                 sem_out)

                def st(i, carry):
                    tail_copy(i).start()
                    return carry

                def wt(i, carry):
                    tail_copy(i).wait()
                    return carry
                lax.fori_loop(0, n_tail, st, 0)
                lax.fori_loop(0, n_tail, wt, 0)


def _expert_call(gexp, gsub0, gns, slot_tok, hp, w1, b1, w2, b2):
    n_slots = slot_tok.shape[0]
    half = hp.shape[1]
    n_exp, d, ff2 = w1.shape
    ff = ff2 // 2
    tc = FF_CHUNK
    n_chunks = ff // tc
    n_groups = gexp.shape[0]
    rows = GROUP_SUBS * SLOT_BLOCK

    def chunk(c, gn, g):
        return jnp.where(gn[g] > 0, c, n_chunks - 1)

    grid_spec = pltpu.PrefetchScalarGridSpec(
        num_scalar_prefetch=4,
        grid=(n_groups, n_chunks),
        in_specs=[
            pl.BlockSpec(memory_space=pl.ANY),
            pl.BlockSpec((None, d, tc), lambda g, c, ge, gs, gn, st: (ge[g], 0, chunk(c, gn, g))),
            pl.BlockSpec((None, d, tc), lambda g, c, ge, gs, gn, st: (ge[g], 0, n_chunks + chunk(c, gn, g))),
            pl.BlockSpec((None, 1, tc), lambda g, c, ge, gs, gn, st: (ge[g], 0, chunk(c, gn, g))),
            pl.BlockSpec((None, 1, tc), lambda g, c, ge, gs, gn, st: (ge[g], 0, n_chunks + chunk(c, gn, g))),
            pl.BlockSpec((None, tc, d), lambda g, c, ge, gs, gn, st: (ge[g], chunk(c, gn, g), 0)),
            pl.BlockSpec((None, 1, d), lambda g, c, ge, gs, gn, st: (ge[g], 0, 0)),
        ],
        out_specs=pl.BlockSpec(memory_space=pl.ANY),
        scratch_shapes=[
            pltpu.VMEM((rows, half), BF16), pltpu.VMEM((rows, half), BF16),
            pltpu.VMEM((rows, half), U32),
            pltpu.VMEM((rows, d), F32),
            pltpu.SemaphoreType.DMA(()), pltpu.SemaphoreType.DMA(()),
        ],
    )
    return pl.pallas_call(
        functools.partial(_expert_body, n_chunks=n_chunks, n_groups=n_groups),
        grid_spec=grid_spec,
        out_shape=jax.ShapeDtypeStruct((n_slots, d), F32),
        compiler_params=_cparams(("arbitrary", "arbitrary")),
        name="expert",
    )(gexp, gsub0, gns, slot_tok, hp, w1, w1, b1.reshape(n_exp, 1, ff2), b1.reshape(n_exp, 1, ff2), w2,
      b2.reshape(n_exp, 1, d))


def _final_body(dest_ref, x1_ref, w4_ref, p_ref, wple_ref, wpg_ref, bpg_ref, npl_ref, outs_hbm,
                o_ref, gbuf, sem, *, tm):
    i = pl.program_id(0)

    def row_copy(slot, j, r):
        return pltpu.make_async_copy(outs_hbm.at[pl.ds(slot, 1), :], gbuf.at[j, pl.ds(r, 1), :], sem)

    def issue(r, carry):
        t = i * tm + r
        for j in range(TOP_K):
            row_copy(dest_ref[t * TOP_K + j], j, r).start()
        return carry
    lax.fori_loop(0, tm, issue, 0)

    pe = jnp.dot(p_ref[...].astype(BF16), wple_ref[...], preferred_element_type=F32)

    def wt(r, carry):
        for j in range(TOP_K):
            row_copy(0, j, r).wait()
        return carry
    lax.fori_loop(0, tm, wt, 0)

    w4 = w4_ref[...]
    y = gbuf[0] * w4[:, 0:1]
    for j in range(1, TOP_K):
        y = y + gbuf[j] * w4[:, j:j + 1]
    x2 = x1_ref[...] + y
    hn = _rmsnorm_rows(x2, npl_ref[...])
    gate = jax.nn.sigmoid(jnp.dot(hn.astype(BF16), wpg_ref[...], preferred_element_type=F32) + bpg_ref[...])
    o_ref[...] = x2 + gate * pe


def _final_call(dest_flat, x1, w4, p2d, wple, wpg, bpg_row, npl_row, outs, tm=256):
    n, d = x1.shape
    tm = min(tm, n)
    pd = p2d.shape[1]
    res = lambda shape: pl.BlockSpec(shape, lambda i, dst: (0,) * len(shape), pipeline_mode=pl.Buffered(1))
    grid_spec = pltpu.PrefetchScalarGridSpec(
        num_scalar_prefetch=1,
        grid=(n // tm,),
        in_specs=[pl.BlockSpec((tm, d), lambda i, dst: (i, 0)),
                  pl.BlockSpec((tm, LANES), lambda i, dst: (i, 0)),
                  pl.BlockSpec((tm, pd), lambda i, dst: (i, 0)),
                  res((pd, d)), res((d, d)), res((1, d)), res((1, d)),
                  pl.BlockSpec(memory_space=pl.ANY)],
        out_specs=pl.BlockSpec((tm, d), lambda i, dst: (i, 0)),
        scratch_shapes=[pltpu.VMEM((TOP_K, tm, d), F32), pltpu.SemaphoreType.DMA(())],
    )
    return pl.pallas_call(
        functools.partial(_final_body, tm=tm),
        grid_spec=grid_spec,
        out_shape=jax.ShapeDtypeStruct((n, d), F32),
        compiler_params=_cparams(("arbitrary",)),
        name="final",
    )(dest_flat, x1, w4, p2d, wple, wpg, bpg_row, npl_row, outs)


def _t5_bucket(rel):
    n = -rel
    nb = REL_BUCKETS // 2
    ret = jnp.where(n < 0, nb, 0)
    n = jnp.abs(n)
    max_exact = nb // 2
    large = max_exact + (jnp.log(jnp.maximum(n, 1).astype(jnp.float32) / max_exact)
                         / math.log(REL_MAX_DIST / max_exact) * (nb - max_exact)).astype(jnp.int32)
    large = jnp.minimum(large, nb - 1)
    return ret + jnp.where(n < max_exact, n, large)


def _bias_body(rv_ref, o_ref, *, t):
    x = jnp.broadcast_to(rv_ref[...], (t, rv_ref.shape[1]))
    o_ref[...] = pltpu.roll(x, 1, 1, stride=1, stride_axis=0)[:, t:]


def _diff_bias_table(rel_bias, seq, t):
    rel = jnp.arange(seq + t) - (seq - 1)
    rv = jnp.transpose(rel_bias[_t5_bucket(rel)], (1, 0)).astype(F32)[:, None, :]
    return pl.pallas_call(
        functools.partial(_bias_body, t=t),
        grid=(N_HEADS,),
        in_specs=[pl.BlockSpec((None, 1, seq + t), lambda h: (h, 0, 0))],
        out_specs=pl.BlockSpec((None, t, seq), lambda h: (h, 0, 0)),
        out_shape=jax.ShapeDtypeStruct((N_HEADS, t, seq), F32),
        compiler_params=_cparams(("parallel",)),
        name="bias_table",
    )(rv)


def _pad_cols(a, width):
    return jnp.pad(a, ((0, 0), (0, width - a.shape[1])))


def _group_table(counts):
    n_sub = (counts + SLOT_BLOCK - 1) // SLOT_BLOCK
    sub_start = jnp.cumsum(n_sub) - n_sub
    n_grp = (n_sub + GROUP_SUBS - 1) // GROUP_SUBS
    grp_end = jnp.cumsum(n_grp)
    total = grp_end[-1]
    return n_sub, sub_start, n_grp, grp_end, total


def _layer(x, p_l, w_in, b_gate, b_forget, dq_norm, dk_norm, fq_norm, fk_norm, lq1, lk1, lq2, lk2,
           lambda_init, subln, w_up_a, w_up_b, w_out, rel_bias, norm_mix, norm_moe, w_router,
           b_router, w1, b1, w2, b2, norm_ple, w_ple_gate, b_ple_gate, w_ple):
    b, s, d = x.shape
    n = b * s
    width = N_HEADS * HEAD_DIM
    t = min(ATTN_T, s)
    x2d = x.reshape(n, d)
    row = lambda v: v.reshape(1, -1).astype(F32)

    f_col = 6 * width
    wf_pad = _pad_cols(w_in[:, f_col:f_col + N_HEADS], LANES)
    h, f_logit = _norm_call(x2d, row(norm_mix), wf_pad)
    cum = _cum_call(f_logit, _pad_cols(row(b_forget), LANES), b, s)
    cum_bhs = jnp.transpose(cum[:, :N_HEADS].reshape(b, s, N_HEADS), (0, 2, 1))
    cum_col = cum_bhs[..., None]
    cum_row = cum_bhs[:, :, None, :]

    diff_scale = DIFF_QK_DIM ** -0.5
    fox_scale = HEAD_DIM ** -0.5
    gain_d = jnp.concatenate([jnp.tile(dq_norm * diff_scale, 2 * N_HEADS), jnp.tile(dk_norm, 2 * N_HEADS)])
    gain_f = jnp.concatenate([jnp.tile(fq_norm * fox_scale, N_HEADS), jnp.tile(fk_norm, N_HEADS)])
    zeros_w = jnp.zeros((1, width), F32)
    dqk = _proj_call(h, w_in, 0, 2 * width, row(gain_d), "norm64", "proj_dqk")
    dv = _proj_call(h, w_in, 2 * width, width, zeros_w, "plain", "proj_dv")
    fqk = _proj_call(h, w_in, 3 * width, 2 * width, row(gain_f), "norm128", "proj_fqk")
    fv = _proj_call(h, w_in, 5 * width, width, zeros_w, "plain", "proj_fv")
    w_gate = w_in[:, f_col + N_HEADS:].astype(BF16)
    gates = _proj_call(h, w_gate, 0, 2 * d, row(b_gate), "gate", "proj_gate")

    lam_rows = [row(v) for v in (lq1, lk1, lq2, lk2)]
    bias = _diff_bias_table(rel_bias, s, t)
    od = _diff_attn_call(dqk.reshape(b, s, 2 * width), dv.reshape(b, s, width), bias, lam_rows,
                         row(subln), lambda_init, t)
    of = _fox_attn_call(fqk.reshape(b, s, 2 * width), fv.reshape(b, s, width), cum_col, cum_row, t)

    br_pad = jnp.full((1, LANES), NEG, F32).at[0, :N_EXPERTS].set(b_router.astype(F32))
    x1, hp, logits = _mix_call(od.reshape(n, width), of.reshape(n, width), gates, x2d,
                               w_up_a.astype(BF16), w_up_b.astype(BF16), w_out.astype(BF16),
                               row(norm_moe), _pad_cols(w_router.astype(F32), LANES), br_pad)

    rank, jm, wm, cnt = _router_call(logits)
    counts = cnt[0, :N_EXPERTS].astype(I32)
    padded = ((counts + SLOT_BLOCK - 1) // SLOT_BLOCK) * SLOT_BLOCK
    pstart = jnp.cumsum(padded) - padded
    ps_row = _pad_cols(pstart.astype(F32).reshape(1, -1), LANES)
    dest128, w4 = _dest_call(rank, jm, wm, ps_row)
    dest_flat = dest128[:, :TOP_K].reshape(-1)

    n_slots = n * TOP_K + N_EXPERTS * SLOT_BLOCK
    slot_tok = _slot_table_call(dest_flat, counts, padded, pstart, n_slots)

    n_sub, sub_start, n_grp, grp_end, total = _group_table(counts)
    max_groups = N_EXPERTS + (n_slots // SLOT_BLOCK) // GROUP_SUBS
    gidx = jnp.arange(max_groups, dtype=I32)
    gvalid = gidx < total
    gsafe = jnp.minimum(gidx, total - 1)
    gexp = jnp.searchsorted(grp_end, gsafe, side="right").astype(I32)
    kth = gsafe - (grp_end - n_grp)[gexp]
    gsub0 = (sub_start[gexp] + kth * GROUP_SUBS).astype(I32)
    gns = jnp.where(gvalid, jnp.minimum(GROUP_SUBS, n_sub[gexp] - kth * GROUP_SUBS), 0).astype(I32)
    outs = _expert_call(gexp, gsub0, gns, slot_tok, hp, w1, b1, w2, b2)

    out = _final_call(dest_flat, x1, w4, p_l.reshape(n, -1), w_ple.astype(BF16),
                      w_ple_gate.astype(BF16), row(b_ple_gate), row(norm_ple), outs)
    return out.reshape(b, s, d)


def kernel(x, p, w_in, b_gate, b_forget, dq_norm, dk_norm, fq_norm, fk_norm, lambda_q1, lambda_k1,
           lambda_q2, lambda_k2, subln, w_up_a, w_up_b, w_out, rel_bias, norm_mix, norm_moe,
           w_router, b_router, w1, b1, w2, b2, norm_ple, w_ple_gate, b_ple_gate, w_ple):
    for i in range(w_in.shape[0]):
        lambda_init = 0.8 - 0.6 * math.exp(-0.3 * i)
        x = _layer(x, p[i], w_in[i], b_gate[i], b_forget[i], dq_norm[i], dk_norm[i], fq_norm[i],
                   fk_norm[i], lambda_q1[i], lambda_k1[i], lambda_q2[i], lambda_k2[i], lambda_init,
                   subln[i], w_up_a[i], w_up_b[i], w_out[i], rel_bias, norm_mix[i], norm_moe[i],
                   w_router[i], b_router[i], w1[i], b1[i], w2[i], b2[i], norm_ple[i],
                   w_ple_gate[i], b_ple_gate[i], w_ple[i])
    return x
```

```python
import functools
import math

import jax
import jax.numpy as jnp
from jax import lax
from jax.experimental import pallas as pl
from jax.experimental.pallas import tpu as pltpu

F32 = jnp.float32
BF16 = jnp.bfloat16
U32 = jnp.uint32
I32 = jnp.int32
HIGHEST = lax.Precision.HIGHEST

N_HEADS = 8
HEAD_DIM = 128
DIFF_QK_DIM = 64
CHUNK = 64
REL_BUCKETS = 32
REL_MAX_DIST = 128
N_EXPERTS = 32
TOP_K = 4
SWIGLU_LIMIT = 7.0
SWIGLU_ALPHA = 1.702
RMS_EPS = 1e-6
NEG = -1e30
LOG2E = math.log2(math.e)

LANES = 128
SLOT_BLOCK = 256
GROUP_SUBS = 6
FF_CHUNK = 256
ATTN_T = 256
VMEM_LIMIT = 56 * 1024 * 1024


def _cparams(sem):
    return pltpu.CompilerParams(dimension_semantics=sem, vmem_limit_bytes=VMEM_LIMIT)


def _rmsnorm_rows(x, g):
    ms = jnp.mean(x * x, axis=-1, keepdims=True)
    return x * lax.rsqrt(ms + RMS_EPS) * g


def _norm_body(x_ref, g_ref, wf_ref, h_ref, fl_ref):
    y = _rmsnorm_rows(x_ref[...], g_ref[...])
    h_ref[...] = y.astype(BF16)
    fl_ref[...] = jnp.dot(y, wf_ref[...], precision=HIGHEST, preferred_element_type=F32)


def _norm_call(x2d, g_row, wf_pad, tm=512):
    n, d = x2d.shape
    tm = min(tm, n)
    return pl.pallas_call(
        _norm_body,
        grid=(n // tm,),
        in_specs=[pl.BlockSpec((tm, d), lambda i: (i, 0)),
                  pl.BlockSpec((1, d), lambda i: (0, 0)),
                  pl.BlockSpec((d, LANES), lambda i: (0, 0))],
        out_specs=[pl.BlockSpec((tm, d), lambda i: (i, 0)),
                   pl.BlockSpec((tm, LANES), lambda i: (i, 0))],
        out_shape=[jax.ShapeDtypeStruct((n, d), BF16),
                   jax.ShapeDtypeStruct((n, LANES), F32)],
        compiler_params=_cparams(("parallel",)),
        name="norm",
    )(x2d, g_row, wf_pad)


def _cum_body(fl_ref, bf_ref, cum_ref):
    s = fl_ref.shape[0]
    r = lax.broadcasted_iota(I32, (LANES, LANES), 0)
    c = lax.broadcasted_iota(I32, (LANES, LANES), 1)
    tri = jnp.where(r >= c, 1.0, 0.0).astype(F32)
    carry = jnp.zeros((1, LANES), F32)
    for blk in range(s // LANES):
        rows = slice(blk * LANES, (blk + 1) * LANES)
        lf = jax.nn.log_sigmoid(fl_ref[rows, :] + bf_ref[...])
        cb = jnp.dot(tri, lf, precision=HIGHEST, preferred_element_type=F32) + carry
        cum_ref[rows, :] = cb * LOG2E
        carry = cb[LANES - 1:LANES, :]


def _cum_call(fl, bf_row, batch, seq):
    return pl.pallas_call(
        _cum_body,
        grid=(batch,),
        in_specs=[pl.BlockSpec((seq, LANES), lambda b: (b, 0)),
                  pl.BlockSpec((1, LANES), lambda b: (0, 0))],
        out_specs=pl.BlockSpec((seq, LANES), lambda b: (b, 0)),
        out_shape=jax.ShapeDtypeStruct(fl.shape, F32),
        compiler_params=_cparams(("parallel",)),
        name="cum",
    )(fl, bf_row)


def _proj_body(h_ref, w_ref, aux_ref, o_ref, *, kind):
    acc = jnp.dot(h_ref[...], w_ref[...].astype(BF16), preferred_element_type=F32)
    tn = acc.shape[1]
    if kind == "plain":
        o_ref[...] = acc.astype(o_ref.dtype)
    elif kind == "gate":
        o_ref[...] = jax.nn.sigmoid(acc + aux_ref[...]).astype(o_ref.dtype)
    else:
        lo = lax.broadcasted_iota(I32, (1, LANES), 1) < DIFF_QK_DIM
        for s in range(tn // LANES):
            cols = slice(s * LANES, (s + 1) * LANES)
            blk = acc[:, cols]
            sq = blk * blk
            if kind == "norm128":
                ms = jnp.mean(sq, axis=-1, keepdims=True)
            else:
                s_lo = jnp.sum(jnp.where(lo, sq, 0.0), axis=-1, keepdims=True)
                s_hi = jnp.sum(jnp.where(lo, 0.0, sq), axis=-1, keepdims=True)
                ms = jnp.where(lo, s_lo, s_hi) * (1.0 / DIFF_QK_DIM)
            o_ref[:, cols] = (blk * lax.rsqrt(ms + RMS_EPS) * aux_ref[:, cols]).astype(o_ref.dtype)


def _proj_call(h, w, col0, ncols, aux_row, kind, name, tm=1024, tn=512):
    m, k = h.shape
    tm = min(tm, m)
    assert col0 % tn == 0 and ncols % tn == 0 and m % tm == 0
    jb0 = col0 // tn
    return pl.pallas_call(
        functools.partial(_proj_body, kind=kind),
        grid=(m // tm, ncols // tn),
        in_specs=[pl.BlockSpec((tm, k), lambda i, j: (i, 0)),
                  pl.BlockSpec((k, tn), lambda i, j: (0, jb0 + j)),
                  pl.BlockSpec((1, tn), lambda i, j: (0, j))],
        out_specs=pl.BlockSpec((tm, tn), lambda i, j: (i, j)),
        out_shape=jax.ShapeDtypeStruct((m, ncols), BF16),
        compiler_params=_cparams(("parallel", "parallel")),
        name=name,
    )(h, w, aux_row)


DIFF_HEADS_PER_STEP = 4
FOX_HEADS_PER_STEP = 8


def _softmax_step(s, m_ref, acc_ref, v1):
    m_old = m_ref[...]
    mn = jnp.maximum(m_old, jnp.max(s, axis=-1, keepdims=True))
    alpha = jnp.exp2(m_old - mn)
    p = jnp.exp2(s - jnp.tile(mn, (1, s.shape[1] // LANES))).astype(BF16)
    acc_ref[...] = (jnp.tile(alpha, (1, acc_ref.shape[1] // LANES)) * acc_ref[...]
                    + jnp.dot(p, v1, preferred_element_type=F32))
    m_ref[...] = mn


def _with_ones(v):
    return jnp.concatenate([v, jnp.ones_like(v)], axis=1)


def _normalised(acc):
    return acc[:, :HEAD_DIM] / acc[:, HEAD_DIM:]


_NT = (((1,), (1,)), ((), ()))


def _diff_attn_body(lq1_ref, lk1_ref, lq2_ref, lk2_ref, g_ref, q_ref, k_ref, v_ref, bias_ref,
                    o_ref, m_sc, acc_sc, *, t, nh, lambda_init):
    qi = pl.program_id(2)
    s_len = k_ref.shape[0]
    m_sc[...] = jnp.full(m_sc.shape, NEG, F32)
    acc_sc[...] = jnp.zeros(acc_sc.shape, F32)
    lo = lax.broadcasted_iota(I32, (t, HEAD_DIM), 1) < DIFF_QK_DIM

    def step(kb, masked):
        k0 = pl.multiple_of(kb * t, t)
        off = pl.multiple_of((kb - qi) * t + (s_len - t), LANES)
        if masked:
            r = lax.broadcasted_iota(I32, (t, t), 0)
            c = lax.broadcasted_iota(I32, (t, t), 1)
            ok = (c // CHUNK) <= (r // CHUNK)
        for h in range(nh):
            cols = slice(h * HEAD_DIM, (h + 1) * HEAD_DIM)
            q = q_ref[:, cols]
            zero = jnp.zeros_like(q)
            k = k_ref[pl.ds(k0, t), cols]
            v1 = _with_ones(v_ref[pl.ds(k0, t), cols])
            b = bias_ref[h, :, pl.ds(off, t)]
            for mp, qm in enumerate((jnp.where(lo, q, zero), jnp.where(lo, zero, q))):
                s = lax.dot_general(qm, k, _NT, preferred_element_type=F32) + b
                if masked:
                    s = jnp.where(ok, s, NEG)
                _softmax_step(s, m_sc.at[2 * h + mp], acc_sc.at[2 * h + mp], v1)

    def body(kb, carry):
        step(kb, False)
        return carry
    lax.fori_loop(0, qi, body, 0)
    step(qi, True)

    lam = (jnp.exp(jnp.sum(lq1_ref[...] * lk1_ref[...], axis=-1, keepdims=True))
           - jnp.exp(jnp.sum(lq2_ref[...] * lk2_ref[...], axis=-1, keepdims=True)) + lambda_init)
    for h in range(nh):
        o = _normalised(acc_sc[2 * h]) - lam * _normalised(acc_sc[2 * h + 1])
        o = _rmsnorm_rows(o, g_ref[...]) * (1.0 - lambda_init)
        o_ref[:, h * HEAD_DIM:(h + 1) * HEAD_DIM] = o.astype(o_ref.dtype)


def _diff_attn_call(qk, v, bias, lam_rows, subln_row, lambda_init, t):
    b, s, _ = v.shape
    nh = DIFF_HEADS_PER_STEP
    wb = nh * HEAD_DIM
    n_hg = N_HEADS // nh
    vec = lambda w: pl.BlockSpec((1, w), lambda hg, bb, qi: (0, 0))
    return pl.pallas_call(
        functools.partial(_diff_attn_body, t=t, nh=nh, lambda_init=lambda_init),
        grid=(n_hg, b, s // t),
        in_specs=[vec(DIFF_QK_DIM)] * 4 + [
            vec(HEAD_DIM),
            pl.BlockSpec((None, t, wb), lambda hg, bb, qi: (bb, qi, hg)),
            pl.BlockSpec((None, s, wb), lambda hg, bb, qi: (bb, 0, n_hg + hg)),
            pl.BlockSpec((None, s, wb), lambda hg, bb, qi: (bb, 0, hg)),
            pl.BlockSpec((nh, t, s), lambda hg, bb, qi: (hg, 0, 0))],
        out_specs=pl.BlockSpec((None, t, wb), lambda hg, bb, qi: (bb, qi, hg)),
        out_shape=jax.ShapeDtypeStruct(v.shape, BF16),
        scratch_shapes=[pltpu.VMEM((2 * nh, t, LANES), F32), pltpu.VMEM((2 * nh, t, 2 * HEAD_DIM), F32)],
        compiler_params=_cparams(("parallel", "parallel", "parallel")),
        name="diff_attn",
    )(*lam_rows, subln_row, qk, qk, v, bias)


def _fox_attn_body(q_ref, k_ref, v_ref, cq_ref, ck_ref, o_ref, m_sc, acc_sc, cq_sc, *, t, nh):
    qi = pl.program_id(2)
    m_sc[...] = jnp.full(m_sc.shape, NEG, F32)
    acc_sc[...] = jnp.zeros(acc_sc.shape, F32)
    for h in range(nh):
        cq_sc[h] = jnp.broadcast_to(cq_ref[h], (t, LANES))

    def step(kb, masked):
        k0 = pl.multiple_of(kb * t, t)
        if masked:
            r = lax.broadcasted_iota(I32, (t, t), 0)
            c = lax.broadcasted_iota(I32, (t, t), 1)
            ok = c <= r
        for h in range(nh):
            cols = slice(h * HEAD_DIM, (h + 1) * HEAD_DIM)
            k = k_ref[pl.ds(k0, t), cols]
            v1 = _with_ones(v_ref[pl.ds(k0, t), cols])
            s = lax.dot_general(q_ref[:, cols], k, _NT, preferred_element_type=F32)
            s = s + jnp.tile(cq_sc[h], (1, t // LANES)) - ck_ref[h, :, pl.ds(k0, t)]
            if masked:
                s = jnp.where(ok, s, NEG)
            _softmax_step(s, m_sc.at[h], acc_sc.at[h], v1)

    def body(kb, carry):
        step(kb, False)
        return carry
    lax.fori_loop(0, qi, body, 0)
    step(qi, True)
    for h in range(nh):
        o_ref[:, h * HEAD_DIM:(h + 1) * HEAD_DIM] = _normalised(acc_sc[h]).astype(o_ref.dtype)


def _fox_attn_call(qk, v, cum_col, cum_row, t):
    b, s, _ = v.shape
    nh = FOX_HEADS_PER_STEP
    wb = nh * HEAD_DIM
    n_hg = N_HEADS // nh
    return pl.pallas_call(
        functools.partial(_fox_attn_body, t=t, nh=nh),
        grid=(n_hg, b, s // t),
        in_specs=[pl.BlockSpec((None, t, wb), lambda hg, bb, qi: (bb, qi, hg)),
                  pl.BlockSpec((None, s, wb), lambda hg, bb, qi: (bb, 0, n_hg + hg)),
                  pl.BlockSpec((None, s, wb), lambda hg, bb, qi: (bb, 0, hg)),
                  pl.BlockSpec((None, nh, t, 1), lambda hg, bb, qi: (bb, hg, qi, 0)),
                  pl.BlockSpec((None, nh, 1, s), lambda hg, bb, qi: (bb, hg, 0, 0))],
        out_specs=pl.BlockSpec((None, t, wb), lambda hg, bb, qi: (bb, qi, hg)),
        out_shape=jax.ShapeDtypeStruct(v.shape, BF16),
        scratch_shapes=[pltpu.VMEM((nh, t, LANES), F32), pltpu.VMEM((nh, t, 2 * HEAD_DIM), F32),
                        pltpu.VMEM((nh, t, LANES), F32)],
        compiler_params=_cparams(("parallel", "parallel", "parallel")),
        name="fox_attn",
    )(qk, qk, v, cum_col, cum_row)


def _mix_body(od_ref, of_ref, g_ref, x_ref, wa_ref, wb_ref, wo_ref, nm_ref, wr_ref, br_ref,
              x1_ref, hp_ref, lg_ref):
    d = x_ref.shape[1]
    ua = jnp.dot(od_ref[...], wa_ref[...], preferred_element_type=F32)
    ub = jnp.dot(of_ref[...], wb_ref[...], preferred_element_type=F32)
    mixed = g_ref[:, :d].astype(F32) * ua + g_ref[:, d:].astype(F32) * ub
    x1 = x_ref[...] + jnp.dot(mixed.astype(BF16), wo_ref[...], preferred_element_type=F32)
    x1_ref[...] = x1
    hm = _rmsnorm_rows(x1, nm_ref[...])
    lg_ref[...] = jnp.dot(hm, wr_ref[...], precision=HIGHEST, preferred_element_type=F32) + br_ref[...]
    bits = lax.bitcast_convert_type(hm.astype(BF16).astype(F32), U32)
    hp_ref[...] = (bits[:, :d // 2] >> 16) | (bits[:, d // 2:] & jnp.uint32(0xFFFF0000))


def _resident(shape):
    return pl.BlockSpec(shape, lambda i: (0,) * len(shape), pipeline_mode=pl.Buffered(1))


def _mix_call(od, of, gates, x2d, wa, wb, wo, nm_row, wr_pad, br_pad, tm=256):
    n, d = x2d.shape
    tm = min(tm, n)
    wdt = od.shape[1]
    return pl.pallas_call(
        _mix_body,
        grid=(n // tm,),
        in_specs=[pl.BlockSpec((tm, wdt), lambda i: (i, 0)),
                  pl.BlockSpec((tm, wdt), lambda i: (i, 0)),
                  pl.BlockSpec((tm, 2 * d), lambda i: (i, 0)),
                  pl.BlockSpec((tm, d), lambda i: (i, 0)),
                  _resident((wdt, d)), _resident((wdt, d)), _resident((d, d)),
                  _resident((1, d)), _resident((d, LANES)), _resident((1, LANES))],
        out_specs=[pl.BlockSpec((tm, d), lambda i: (i, 0)),
                   pl.BlockSpec((tm, d // 2), lambda i: (i, 0)),
                   pl.BlockSpec((tm, LANES), lambda i: (i, 0))],
        out_shape=[jax.ShapeDtypeStruct((n, d), F32),
                   jax.ShapeDtypeStruct((n, d // 2), U32),
                   jax.ShapeDtypeStruct((n, LANES), F32)],
        compiler_params=_cparams(("parallel",)),
        name="mix",
    )(od, of, gates, x2d, wa, wb, wo, nm_row, wr_pad, br_pad)


def _router_body(lg_ref, rank_ref, jm_ref, wm_ref, cnt_ref, carry_ref):
    i = pl.program_id(0)

    @pl.when(i == 0)
    def _():
        carry_ref[...] = jnp.zeros_like(carry_ref)

    vals = lg_ref[...]
    tm = vals.shape[0]
    lane = lax.broadcasted_iota(I32, vals.shape, 1).astype(F32)
    jm = jnp.zeros(vals.shape, F32)
    tops = []
    for j in range(TOP_K):
        m = jnp.max(vals, axis=-1, keepdims=True)
        idx = jnp.min(jnp.where(vals == m, lane, float(LANES)), axis=-1, keepdims=True)
        sel = lane == idx
        jm = jnp.where(sel, float(j + 1), jm)
        vals = jnp.where(sel, -jnp.inf, vals)
        tops.append(m)
    es = [jnp.exp(m - tops[0]) for m in tops]
    den = es[0] + es[1] + es[2] + es[3]
    wm = jnp.zeros(vals.shape, F32)
    for j in range(TOP_K):
        wm = jnp.where(jm == float(j + 1), es[j] / den, wm)
    sel_any = jnp.where(jm > 0.0, 1.0, 0.0)
    r = lax.broadcasted_iota(I32, (tm, tm), 0)
    c = lax.broadcasted_iota(I32, (tm, tm), 1)
    tri = jnp.where(c < r, 1.0, 0.0).astype(BF16)
    carry = carry_ref[...]
    rank_ref[...] = jnp.dot(tri, sel_any.astype(BF16), preferred_element_type=F32) + carry
    jm_ref[...] = jm
    wm_ref[...] = wm
    carry = carry + jnp.sum(sel_any, axis=0, keepdims=True)
    carry_ref[...] = carry
    cnt_ref[...] = carry


def _router_call(logits, tm=256):
    n = logits.shape[0]
    tm = min(tm, n)
    tile = pl.BlockSpec((tm, LANES), lambda i: (i, 0))
    row = pl.BlockSpec((1, LANES), lambda i: (0, 0))
    return pl.pallas_call(
        _router_body,
        grid=(n // tm,),
        in_specs=[tile],
        out_specs=[tile, tile, tile, row],
        out_shape=[jax.ShapeDtypeStruct((n, LANES), F32)] * 3 + [jax.ShapeDtypeStruct((1, LANES), F32)],
        scratch_shapes=[pltpu.VMEM((1, LANES), F32)],
        compiler_params=_cparams(("arbitrary",)),
        name="router",
    )(logits)


def _dest_body(rank_ref, jm_ref, wm_ref, ps_ref, dest_ref, w4_ref):
    slot = rank_ref[...] + ps_ref[...]
    jm = jm_ref[...]
    wm = wm_ref[...]
    lane = lax.broadcasted_iota(I32, jm.shape, 1)
    dest = jnp.zeros(jm.shape, F32)
    w4 = jnp.zeros(jm.shape, F32)
    for j in range(TOP_K):
        sel = jm == float(j + 1)
        dj = jnp.sum(jnp.where(sel, slot, 0.0), axis=-1, keepdims=True)
        wj = jnp.sum(jnp.where(sel, wm, 0.0), axis=-1, keepdims=True)
        dest = jnp.where(lane == j, dj, dest)
        w4 = jnp.where(lane == j, wj, w4)
    dest_ref[...] = dest.astype(I32)
    w4_ref[...] = w4


def _dest_call(rank, jm, wm, ps_row, tm=512):
    n = rank.shape[0]
    tm = min(tm, n)
    tile = pl.BlockSpec((tm, LANES), lambda i: (i, 0))
    return pl.pallas_call(
        _dest_body,
        grid=(n // tm,),
        in_specs=[tile, tile, tile, pl.BlockSpec((1, LANES), lambda i: (0, 0))],
        out_specs=[tile, tile],
        out_shape=[jax.ShapeDtypeStruct((n, LANES), I32), jax.ShapeDtypeStruct((n, LANES), F32)],
        compiler_params=_cparams(("parallel",)),
        name="dest",
    )(rank, jm, wm, ps_row)


def _slot_table_body(dest_ref, cnt_ref, pad_ref, pst_ref, slot_ref, *, n_tok):
    def pads(e, carry):
        def one(s, c2):
            slot_ref[pst_ref[e] + s] = 0
            return c2
        lax.fori_loop(cnt_ref[e], pad_ref[e], one, 0)
        return carry
    lax.fori_loop(0, N_EXPERTS, pads, 0)

    def tail(s, carry):
        slot_ref[s] = 0
        return carry
    lax.fori_loop(pst_ref[N_EXPERTS - 1] + pad_ref[N_EXPERTS - 1], slot_ref.shape[0], tail, 0)

    unroll = 8
    assert n_tok % unroll == 0

    def toks(i, carry):
        for u in range(unroll):
            t = i * unroll + u
            for j in range(TOP_K):
                slot_ref[dest_ref[t * TOP_K + j]] = t
        return carry
    lax.fori_loop(0, n_tok // unroll, toks, 0)


def _slot_table_call(dest_flat, counts, padded, pstart, n_slots):
    smem = pl.BlockSpec(memory_space=pltpu.SMEM)
    return pl.pallas_call(
        functools.partial(_slot_table_body, n_tok=dest_flat.shape[0] // TOP_K),
        in_specs=[smem, smem, smem, smem],
        out_specs=smem,
        out_shape=jax.ShapeDtypeStruct((n_slots,), I32),
        name="slot_table",
    )(dest_flat, counts, padded, pstart)


ROW_UNROLL = 8


def _expert_body(gexp_ref, gsub0_ref, gns_ref, slot_ref,
                 hp_hbm, w1g_ref, w1l_ref, b1g_ref, b1l_ref, w2_ref, b2_ref,
                 outs_hbm,
                 xlo, xhi, stage, acc, sem_in, sem_out, *, n_chunks, n_groups):
    g = pl.program_id(0)
    c = pl.program_id(1)
    ns = gns_ref[g]
    s0 = gsub0_ref[g]
    sub = SLOT_BLOCK
    half = xlo.shape[1]
    last = n_chunks - 1

    def row_copy(tok, r):
        return pltpu.make_async_copy(hp_hbm.at[pl.ds(tok, 1), :], stage.at[pl.ds(r, 1), :], sem_in)

    def issue_rows(gi):
        base = gsub0_ref[gi] * sub

        def trip(i, carry):
            for u in range(ROW_UNROLL):
                r = i * ROW_UNROLL + u
                row_copy(slot_ref[base + r], r).start()
            return carry
        lax.fori_loop(0, gns_ref[gi] * (sub // ROW_UNROLL), trip, 0)

    def out_copy(r0, rows):
        row0 = s0 * sub + r0
        return pltpu.make_async_copy(acc.at[pl.ds(r0, rows), :], outs_hbm.at[pl.ds(row0, rows), :], sem_out)

    def compute(r0, rows):
        xl = xlo[pl.ds(r0, rows), :]
        xh = xhi[pl.ds(r0, rows), :]
        hg = (jnp.dot(xl, w1g_ref[:half, :].astype(BF16), preferred_element_type=F32)
              + jnp.dot(xh, w1g_ref[half:, :].astype(BF16), preferred_element_type=F32) + b1g_ref[...])
        hl = (jnp.dot(xl, w1l_ref[:half, :].astype(BF16), preferred_element_type=F32)
              + jnp.dot(xh, w1l_ref[half:, :].astype(BF16), preferred_element_type=F32) + b1l_ref[...])
        gate = jnp.minimum(hg, SWIGLU_LIMIT)
        lin = jnp.clip(hl, -SWIGLU_LIMIT, SWIGLU_LIMIT)
        act = gate * jax.nn.sigmoid(SWIGLU_ALPHA * gate) * (lin + 1.0)
        acc[pl.ds(r0, rows), :] += jnp.dot(act.astype(BF16), w2_ref[...].astype(BF16),
                                           preferred_element_type=F32)

        @pl.when(c == last)
        def _():
            out_copy(r0, rows).start()

    @pl.when(ns > 0)
    def _():
        @pl.when(c == 0)
        def _load():
            @pl.when(g == 0)
            def _():
                issue_rows(0)

            def wt(i, carry):
                for u in range(ROW_UNROLL):
                    row_copy(0, i * ROW_UNROLL + u).wait()
                return carry
            lax.fori_loop(0, ns * (sub // ROW_UNROLL), wt, 0)

            def unpack(s, carry):
                r0 = pl.multiple_of(s * sub, sub)
                w = stage[pl.ds(r0, sub), :]
                xlo[pl.ds(r0, sub), :] = lax.bitcast_convert_type(w << 16, F32).astype(BF16)
                xhi[pl.ds(r0, sub), :] = lax.bitcast_convert_type(w & jnp.uint32(0xFFFF0000), F32).astype(BF16)
                acc[pl.ds(r0, sub), :] = jnp.broadcast_to(b2_ref[...], (sub, acc.shape[1]))
                return carry
            lax.fori_loop(0, ns, unpack, 0)

        @pl.when(c == min(1, last))
        def _prefetch():
            nxt = jnp.minimum(g + 1, n_groups - 1)

            @pl.when(jnp.logical_and(g + 1 < n_groups, gns_ref[nxt] > 0))
            def _():
                issue_rows(nxt)

        def pair(i, carry):
            compute(pl.multiple_of(i * (2 * sub), 2 * sub), 2 * sub)
            return carry
        lax.fori_loop(0, ns // 2, pair, 0)

        @pl.when(ns % 2 == 1)
        def _():
            compute(pl.multiple_of((ns - 1) * sub, sub), sub)

        @pl.when(c == last)
        def _drain():
            def wt(i, carry):
                out_copy(pl.multiple_of(i * (2 * sub), 2 * sub), 2 * sub).wait()
                return carry
            lax.fori_loop(0, ns // 2, wt, 0)

            @pl.when(ns % 2 == 1)
            def _():
                out_copy(pl.multiple_of((ns - 1) * sub, sub), sub).wait()

            nxt = jnp.minimum(g + 1, n_groups - 1)

            @pl.when(jnp.logical_or(g + 1 >= n_groups, gns_ref[nxt] == 0))
            def _tail():
                acc[pl.ds(0, sub), :] = jnp.zeros((sub, acc.shape[1]), F32)
                first = s0 + ns
                n_tail = outs_hbm.shape[0] // sub - first

                def tail_copy(i):
                    row0 = pl.multiple_of((first + i) * sub, sub)
                    return pltpu.make_async_copy(acc.at[pl.ds(0, sub), :], outs_hbm.at[pl.ds(row0, sub), :],
                                                 sem_out)

                def st(i, carry):
                    tail_copy(i).start()
                    return carry

                def wt(i, carry):
                    tail_copy(i).wait()
                    return carry
                lax.fori_loop(0, n_tail, st, 0)
                lax.fori_loop(0, n_tail, wt, 0)


def _expert_call(gexp, gsub0, gns, slot_tok, hp, w1, b1, w2, b2):
    n_slots = slot_tok.shape[0]
    half = hp.shape[1]
    n_exp, d, ff2 = w1.shape
    ff = ff2 // 2
    tc = FF_CHUNK
    n_chunks = ff // tc
    n_groups = gexp.shape[0]
    rows = GROUP_SUBS * SLOT_BLOCK

    def chunk(c, gn, g):
        return jnp.where(gn[g] > 0, c, n_chunks - 1)

    grid_spec = pltpu.PrefetchScalarGridSpec(
        num_scalar_prefetch=4,
        grid=(n_groups, n_chunks),
        in_specs=[
            pl.BlockSpec(memory_space=pl.ANY),
            pl.BlockSpec((None, d, tc), lambda g, c, ge, gs, gn, st: (ge[g], 0, chunk(c, gn, g))),
            pl.BlockSpec((None, d, tc), lambda g, c, ge, gs, gn, st: (ge[g], 0, n_chunks + chunk(c, gn, g))),
            pl.BlockSpec((None, 1, tc), lambda g, c, ge, gs, gn, st: (ge[g], 0, chunk(c, gn, g))),
            pl.BlockSpec((None, 1, tc), lambda g, c, ge, gs, gn, st: (ge[g], 0, n_chunks + chunk(c, gn, g))),
            pl.BlockSpec((None, tc, d), lambda g, c, ge, gs, gn, st: (ge[g], chunk(c, gn, g), 0)),
            pl.BlockSpec((None, 1, d), lambda g, c, ge, gs, gn, st: (ge[g], 0, 0)),
        ],
        out_specs=pl.BlockSpec(memory_space=pl.ANY),
        scratch_shapes=[
            pltpu.VMEM((rows, half), BF16), pltpu.VMEM((rows, half), BF16),
            pltpu.VMEM((rows, half), U32),
            pltpu.VMEM((rows, d), F32),
            pltpu.SemaphoreType.DMA(()), pltpu.SemaphoreType.DMA(()),
        ],
    )
    return pl.pallas_call(
        functools.partial(_expert_body, n_chunks=n_chunks, n_groups=n_groups),
        grid_spec=grid_spec,
        out_shape=jax.ShapeDtypeStruct((n_slots, d), F32),
        compiler_params=_cparams(("arbitrary", "arbitrary")),
        name="expert",
    )(gexp, gsub0, gns, slot_tok, hp, w1, w1, b1.reshape(n_exp, 1, ff2), b1.reshape(n_exp, 1, ff2), w2,
      b2.reshape(n_exp, 1, d))


def _final_body(dest_ref, x1_ref, w4_ref, p_ref, wple_ref, wpg_ref, bpg_ref, npl_ref, outs_hbm,
                o_ref, gbuf, sems, *, tm, n_steps):
    i = pl.program_id(0)
    cur = i % 2

    def row_copy(slot, buf, j, r):
        return pltpu.make_async_copy(outs_hbm.at[pl.ds(slot, 1), :], gbuf.at[buf, j, pl.ds(r, 1), :],
                                     sems.at[buf])

    def issue(step, buf):
        def rows(r, carry):
            t = step * tm + r
            for j in range(TOP_K):
                row_copy(dest_ref[t * TOP_K + j], buf, j, r).start()
            return carry
        lax.fori_loop(0, tm, rows, 0)

    @pl.when(i == 0)
    def _():
        issue(0, 0)

    @pl.when(i + 1 < n_steps)
    def _():
        issue(i + 1, 1 - cur)

    pe = jnp.dot(p_ref[...].astype(BF16), wple_ref[...], preferred_element_type=F32)

    def wt(r, carry):
        for j in range(TOP_K):
            row_copy(0, cur, j, r).wait()
        return carry
    lax.fori_loop(0, tm, wt, 0)

    w4 = w4_ref[...]
    y = gbuf[cur, 0] * w4[:, 0:1]
    for j in range(1, TOP_K):
        y = y + gbuf[cur, j] * w4[:, j:j + 1]
    x2 = x1_ref[...] + y
    hn = _rmsnorm_rows(x2, npl_ref[...])
    gate = jax.nn.sigmoid(jnp.dot(hn.astype(BF16), wpg_ref[...], preferred_element_type=F32) + bpg_ref[...])
    o_ref[...] = x2 + gate * pe


def _final_call(dest_flat, x1, w4, p2d, wple, wpg, bpg_row, npl_row, outs, tm=256):
    n, d = x1.shape
    tm = min(tm, n)
    pd = p2d.shape[1]
    res = lambda shape: pl.BlockSpec(shape, lambda i, dst: (0,) * len(shape), pipeline_mode=pl.Buffered(1))
    grid_spec = pltpu.PrefetchScalarGridSpec(
        num_scalar_prefetch=1,
        grid=(n // tm,),
        in_specs=[pl.BlockSpec((tm, d), lambda i, dst: (i, 0)),
                  pl.BlockSpec((tm, LANES), lambda i, dst: (i, 0)),
                  pl.BlockSpec((tm, pd), lambda i, dst: (i, 0)),
                  res((pd, d)), res((d, d)), res((1, d)), res((1, d)),
                  pl.BlockSpec(memory_space=pl.ANY)],
        out_specs=pl.BlockSpec((tm, d), lambda i, dst: (i, 0)),
        scratch_shapes=[pltpu.VMEM((2, TOP_K, tm, d), F32), pltpu.SemaphoreType.DMA((2,))],
    )
    return pl.pallas_call(
        functools.partial(_final_body, tm=tm, n_steps=n // tm),
        grid_spec=grid_spec,
        out_shape=jax.ShapeDtypeStruct((n, d), F32),
        compiler_params=_cparams(("arbitrary",)),
        name="final",
    )(dest_flat, x1, w4, p2d, wple, wpg, bpg_row, npl_row, outs)


def _t5_bucket(rel):
    n = -rel
    nb = REL_BUCKETS // 2
    ret = jnp.where(n < 0, nb, 0)
    n = jnp.abs(n)
    max_exact = nb // 2
    large = max_exact + (jnp.log(jnp.maximum(n, 1).astype(jnp.float32) / max_exact)
                         / math.log(REL_MAX_DIST / max_exact) * (nb - max_exact)).astype(jnp.int32)
    large = jnp.minimum(large, nb - 1)
    return ret + jnp.where(n < max_exact, n, large)


def _bias_body(rv_ref, o_ref, *, t):
    x = jnp.broadcast_to(rv_ref[...], (t, rv_ref.shape[1]))
    o_ref[...] = pltpu.roll(x, 1, 1, stride=1, stride_axis=0)[:, t:]


def _diff_bias_table(rel_bias, seq, t):
    rel = jnp.arange(seq + t) - (seq - 1)
    rv = (jnp.transpose(rel_bias[_t5_bucket(rel)], (1, 0)).astype(F32) * LOG2E)[:, None, :]
    return pl.pallas_call(
        functools.partial(_bias_body, t=t),
        grid=(N_HEADS,),
        in_specs=[pl.BlockSpec((None, 1, seq + t), lambda h: (h, 0, 0))],
        out_specs=pl.BlockSpec((None, t, seq), lambda h: (h, 0, 0)),
        out_shape=jax.ShapeDtypeStruct((N_HEADS, t, seq), F32),
        compiler_params=_cparams(("parallel",)),
        name="bias_table",
    )(rv)


def _pad_cols(a, width):
    return jnp.pad(a, ((0, 0), (0, width - a.shape[1])))


def _group_table(counts):
    n_sub = (counts + SLOT_BLOCK - 1) // SLOT_BLOCK
    sub_start = jnp.cumsum(n_sub) - n_sub
    n_grp = (n_sub + GROUP_SUBS - 1) // GROUP_SUBS
    grp_end = jnp.cumsum(n_grp)
    total = grp_end[-1]
    return n_sub, sub_start, n_grp, grp_end, total


def _layer(x, p_l, w_in, b_gate, b_forget, dq_norm, dk_norm, fq_norm, fk_norm, lq1, lk1, lq2, lk2,
           lambda_init, subln, w_up_a, w_up_b, w_out, rel_bias, norm_mix, norm_moe, w_router,
           b_router, w1, b1, w2, b2, norm_ple, w_ple_gate, b_ple_gate, w_ple):
    b, s, d = x.shape
    n = b * s
    width = N_HEADS * HEAD_DIM
    t = min(ATTN_T, s)
    x2d = x.reshape(n, d)
    row = lambda v: v.reshape(1, -1).astype(F32)

    f_col = 6 * width
    wf_pad = _pad_cols(w_in[:, f_col:f_col + N_HEADS], LANES)
    h, f_logit = _norm_call(x2d, row(norm_mix), wf_pad)
    cum = _cum_call(f_logit, _pad_cols(row(b_forget), LANES), b, s)
    cum_bhs = jnp.transpose(cum[:, :N_HEADS].reshape(b, s, N_HEADS), (0, 2, 1))
    cum_col = cum_bhs[..., None]
    cum_row = cum_bhs[:, :, None, :]

    diff_scale = DIFF_QK_DIM ** -0.5 * LOG2E
    fox_scale = HEAD_DIM ** -0.5 * LOG2E
    gain_d = jnp.concatenate([jnp.tile(dq_norm * diff_scale, 2 * N_HEADS), jnp.tile(dk_norm, 2 * N_HEADS)])
    gain_f = jnp.concatenate([jnp.tile(fq_norm * fox_scale, N_HEADS), jnp.tile(fk_norm, N_HEADS)])
    zeros_w = jnp.zeros((1, width), F32)
    dqk = _proj_call(h, w_in, 0, 2 * width, row(gain_d), "norm64", "proj_dqk")
    dv = _proj_call(h, w_in, 2 * width, width, zeros_w, "plain", "proj_dv")
    fqk = _proj_call(h, w_in, 3 * width, 2 * width, row(gain_f), "norm128", "proj_fqk")
    fv = _proj_call(h, w_in, 5 * width, width, zeros_w, "plain", "proj_fv")
    w_gate = w_in[:, f_col + N_HEADS:].astype(BF16)
    gates = _proj_call(h, w_gate, 0, 2 * d, row(b_gate), "gate", "proj_gate")

    lam_rows = [row(v) for v in (lq1, lk1, lq2, lk2)]
    bias = _diff_bias_table(rel_bias, s, t)
    od = _diff_attn_call(dqk.reshape(b, s, 2 * width), dv.reshape(b, s, width), bias, lam_rows,
                         row(subln), lambda_init, t)
    of = _fox_attn_call(fqk.reshape(b, s, 2 * width), fv.reshape(b, s, width), cum_col, cum_row, t)

    br_pad = jnp.full((1, LANES), NEG, F32).at[0, :N_EXPERTS].set(b_router.astype(F32))
    x1, hp, logits = _mix_call(od.reshape(n, width), of.reshape(n, width), gates, x2d,
                               w_up_a.astype(BF16), w_up_b.astype(BF16), w_out.astype(BF16),
                               row(norm_moe), _pad_cols(w_router.astype(F32), LANES), br_pad)

    rank, jm, wm, cnt = _router_call(logits)
    counts = cnt[0, :N_EXPERTS].astype(I32)
    padded = ((counts + SLOT_BLOCK - 1) // SLOT_BLOCK) * SLOT_BLOCK
    pstart = jnp.cumsum(padded) - padded
    ps_row = _pad_cols(pstart.astype(F32).reshape(1, -1), LANES)
    dest128, w4 = _dest_call(rank, jm, wm, ps_row)
    dest_flat = dest128[:, :TOP_K].reshape(-1)

    n_slots = n * TOP_K + N_EXPERTS * SLOT_BLOCK
    slot_tok = _slot_table_call(dest_flat, counts, padded, pstart, n_slots)

    n_sub, sub_start, n_grp, grp_end, total = _group_table(counts)
    max_groups = N_EXPERTS + (n_slots // SLOT_BLOCK) // GROUP_SUBS
    gidx = jnp.arange(max_groups, dtype=I32)
    gvalid = gidx < total
    gsafe = jnp.minimum(gidx, total - 1)
    gexp = jnp.searchsorted(grp_end, gsafe, side="right").astype(I32)
    kth = gsafe - (grp_end - n_grp)[gexp]
    gsub0 = (sub_start[gexp] + kth * GROUP_SUBS).astype(I32)
    gns = jnp.where(gvalid, jnp.minimum(GROUP_SUBS, n_sub[gexp] - kth * GROUP_SUBS), 0).astype(I32)
    outs = _expert_call(gexp, gsub0, gns, slot_tok, hp, w1, b1, w2, b2)

    out = _final_call(dest_flat, x1, w4, p_l.reshape(n, -1), w_ple.astype(BF16),
                      w_ple_gate.astype(BF16), row(b_ple_gate), row(norm_ple), outs)
    return out.reshape(b, s, d)


def kernel(x, p, w_in, b_gate, b_forget, dq_norm, dk_norm, fq_norm, fk_norm, lambda_q1, lambda_k1,
           lambda_q2, lambda_k2, subln, w_up_a, w_up_b, w_out, rel_bias, norm_mix, norm_moe,
           w_router, b_router, w1, b1, w2, b2, norm_ple, w_ple_gate, b_ple_gate, w_ple):
    for i in range(w_in.shape[0]):
        lambda_init = 0.8 - 0.6 * math.exp(-0.3 * i)
        x = _layer(x, p[i], w_in[i], b_gate[i], b_forget[i], dq_norm[i], dk_norm[i], fq_norm[i],
                   fk_norm[i], lambda_q1[i], lambda_k1[i], lambda_q2[i], lambda_k2[i], lambda_init,
                   subln[i], w_up_a[i], w_up_b[i], w_out[i], rel_bias, norm_mix[i], norm_moe[i],
                   w_router[i], b_router[i], w1[i], b1[i], w2[i], b2[i], norm_ple[i],
                   w_ple_gate[i], b_ple_gate[i], w_ple[i])
    return x
```

```python
import functools
import math

import jax
import jax.numpy as jnp
from jax import lax
from jax.experimental import pallas as pl
from jax.experimental.pallas import tpu as pltpu

F32 = jnp.float32
BF16 = jnp.bfloat16
U32 = jnp.uint32
I32 = jnp.int32
HIGHEST = lax.Precision.HIGHEST

N_HEADS = 8
HEAD_DIM = 128
DIFF_QK_DIM = 64
CHUNK = 64
REL_BUCKETS = 32
REL_MAX_DIST = 128
N_EXPERTS = 32
TOP_K = 4
SWIGLU_LIMIT = 7.0
SWIGLU_ALPHA = 1.702
RMS_EPS = 1e-6
NEG = -1e30
LOG2E = math.log2(math.e)

LANES = 128
SUBLANES = 8
SLOT_BLOCK = 256
GROUP_SUBS = 6
FF_CHUNK = 256
ATTN_T = 256
VMEM_LIMIT = 56 * 1024 * 1024


def _cparams(sem):
    return pltpu.CompilerParams(dimension_semantics=sem, vmem_limit_bytes=VMEM_LIMIT)


def _rmsnorm_rows(x, g):
    ms = jnp.mean(x * x, axis=-1, keepdims=True)
    return x * lax.rsqrt(ms + RMS_EPS) * g


def _norm_body(x_ref, g_ref, wf_ref, h_ref, fl_ref):
    y = _rmsnorm_rows(x_ref[...], g_ref[...])
    h_ref[...] = y.astype(BF16)
    lane = lax.broadcasted_iota(I32, (1, LANES), 1)
    wf = jnp.where(lane < N_HEADS, wf_ref[...], 0.0)
    fl_ref[...] = jnp.dot(y, wf, precision=HIGHEST, preferred_element_type=F32)


def _norm_call(x2d, g_row, w_in, f_col, tm=512):
    n, d = x2d.shape
    tm = min(tm, n)
    assert f_col % LANES == 0
    return pl.pallas_call(
        _norm_body,
        grid=(n // tm,),
        in_specs=[pl.BlockSpec((tm, d), lambda i: (i, 0)),
                  pl.BlockSpec((1, d), lambda i: (0, 0)),
                  pl.BlockSpec((d, LANES), lambda i: (0, f_col // LANES))],
        out_specs=[pl.BlockSpec((tm, d), lambda i: (i, 0)),
                   pl.BlockSpec((tm, LANES), lambda i: (i, 0))],
        out_shape=[jax.ShapeDtypeStruct((n, d), BF16),
                   jax.ShapeDtypeStruct((n, LANES), F32)],
        compiler_params=_cparams(("parallel",)),
        name="norm",
    )(x2d, g_row, w_in)


def _cum_body(fl_ref, bf_ref, cum_ref):
    s = fl_ref.shape[0]
    r = lax.broadcasted_iota(I32, (LANES, LANES), 0)
    c = lax.broadcasted_iota(I32, (LANES, LANES), 1)
    tri = jnp.where(r >= c, 1.0, 0.0).astype(F32)
    carry = jnp.zeros((1, LANES), F32)
    for blk in range(s // LANES):
        rows = slice(blk * LANES, (blk + 1) * LANES)
        lf = jax.nn.log_sigmoid(fl_ref[rows, :] + bf_ref[...])
        cb = jnp.dot(tri, lf, precision=HIGHEST, preferred_element_type=F32) + carry
        cum_ref[rows, :] = cb * LOG2E
        carry = cb[LANES - 1:LANES, :]


def _cum_call(fl, bf_row, batch, seq):
    return pl.pallas_call(
        _cum_body,
        grid=(batch,),
        in_specs=[pl.BlockSpec((seq, LANES), lambda b: (b, 0)),
                  pl.BlockSpec((1, LANES), lambda b: (0, 0))],
        out_specs=pl.BlockSpec((seq, LANES), lambda b: (b, 0)),
        out_shape=jax.ShapeDtypeStruct(fl.shape, F32),
        compiler_params=_cparams(("parallel",)),
        name="cum",
    )(fl, bf_row)


def _proj_body(h_ref, w_ref, aux_ref, o_ref, *, kind):
    acc = jnp.dot(h_ref[...], w_ref[...].astype(BF16), preferred_element_type=F32)
    tn = acc.shape[1]
    if kind == "plain":
        o_ref[...] = acc.astype(o_ref.dtype)
    elif kind == "gate":
        o_ref[...] = jax.nn.sigmoid(acc + aux_ref[...]).astype(o_ref.dtype)
    else:
        lo = lax.broadcasted_iota(I32, (1, LANES), 1) < DIFF_QK_DIM
        for s in range(tn // LANES):
            cols = slice(s * LANES, (s + 1) * LANES)
            blk = acc[:, cols]
            sq = blk * blk
            if kind == "norm128":
                ms = jnp.mean(sq, axis=-1, keepdims=True)
            else:
                s_lo = jnp.sum(jnp.where(lo, sq, 0.0), axis=-1, keepdims=True)
                s_hi = jnp.sum(jnp.where(lo, 0.0, sq), axis=-1, keepdims=True)
                ms = jnp.where(lo, s_lo, s_hi) * (1.0 / DIFF_QK_DIM)
            o_ref[:, cols] = (blk * lax.rsqrt(ms + RMS_EPS) * aux_ref[:, cols]).astype(o_ref.dtype)


def _proj_call(h, w, col0, ncols, aux_row, kind, name, tm=1024, tn=512):
    m, k = h.shape
    tm = min(tm, m)
    assert col0 % tn == 0 and ncols % tn == 0 and m % tm == 0
    jb0 = col0 // tn
    return pl.pallas_call(
        functools.partial(_proj_body, kind=kind),
        grid=(m // tm, ncols // tn),
        in_specs=[pl.BlockSpec((tm, k), lambda i, j: (i, 0)),
                  pl.BlockSpec((k, tn), lambda i, j: (0, jb0 + j)),
                  pl.BlockSpec((1, tn), lambda i, j: (0, j))],
        out_specs=pl.BlockSpec((tm, tn), lambda i, j: (i, j)),
        out_shape=jax.ShapeDtypeStruct((m, ncols), BF16),
        compiler_params=_cparams(("parallel", "parallel")),
        name=name,
    )(h, w, aux_row)


def _proj_gate_body(h_ref, wa_ref, wb_ref, aux_ref, o_ref, *, shift):
    tn = wa_ref.shape[1]
    w = jnp.concatenate([wa_ref[...], wb_ref[...]], axis=1)
    w = pltpu.roll(w, w.shape[1] - shift, 1)[:, :tn]
    acc = jnp.dot(h_ref[...], w.astype(BF16), preferred_element_type=F32)
    o_ref[...] = jax.nn.sigmoid(acc + aux_ref[...]).astype(o_ref.dtype)


def _proj_gate_call(h, w, col0, ncols, aux_row, tm=1024, tn=512):
    m, k = h.shape
    tm = min(tm, m)
    shift = col0 % LANES
    base = col0 - shift
    assert base % tn == 0 and ncols % tn == 0 and m % tm == 0
    return pl.pallas_call(
        functools.partial(_proj_gate_body, shift=shift),
        grid=(m // tm, ncols // tn),
        in_specs=[pl.BlockSpec((tm, k), lambda i, j: (i, 0)),
                  pl.BlockSpec((k, tn), lambda i, j: (0, base // tn + j)),
                  pl.BlockSpec((k, LANES), lambda i, j: (0, (base + (j + 1) * tn) // LANES)),
                  pl.BlockSpec((1, tn), lambda i, j: (0, j))],
        out_specs=pl.BlockSpec((tm, tn), lambda i, j: (i, j)),
        out_shape=jax.ShapeDtypeStruct((m, ncols), BF16),
        compiler_params=_cparams(("parallel", "parallel")),
        name="proj_gate",
    )(h, w, w, aux_row)


DIFF_HEADS_PER_STEP = 4
FOX_HEADS_PER_STEP = 8


def _softmax_step(s, m_ref, acc_ref, v1):
    m_old = m_ref[...]
    mn = jnp.maximum(m_old, jnp.max(s, axis=-1, keepdims=True))
    alpha = jnp.exp2(m_old - mn)
    p = jnp.exp2(s - jnp.tile(mn, (1, s.shape[1] // LANES))).astype(BF16)
    acc_ref[...] = (jnp.tile(alpha, (1, acc_ref.shape[1] // LANES)) * acc_ref[...]
                    + jnp.dot(p, v1, preferred_element_type=F32))
    m_ref[...] = mn


def _with_ones(v):
    return jnp.concatenate([v, jnp.ones_like(v)], axis=1)


def _normalised(acc):
    return acc[:, :HEAD_DIM] / acc[:, HEAD_DIM:]


_NT = (((1,), (1,)), ((), ()))


def _diff_attn_body(lq1_ref, lk1_ref, lq2_ref, lk2_ref, g_ref, q_ref, k_ref, v_ref, bias_ref,
                    o_ref, m_sc, acc_sc, *, t, nh, lambda_init):
    qi = pl.program_id(2)
    s_len = k_ref.shape[0]
    m_sc[...] = jnp.full(m_sc.shape, NEG, F32)
    acc_sc[...] = jnp.zeros(acc_sc.shape, F32)
    lo = lax.broadcasted_iota(I32, (t, HEAD_DIM), 1) < DIFF_QK_DIM

    def step(kb, masked):
        k0 = pl.multiple_of(kb * t, t)
        off = pl.multiple_of((kb - qi) * t + (s_len - t), LANES)
        if masked:
            r = lax.broadcasted_iota(I32, (t, t), 0)
            c = lax.broadcasted_iota(I32, (t, t), 1)
            ok = (c // CHUNK) <= (r // CHUNK)
        for h in range(nh):
            cols = slice(h * HEAD_DIM, (h + 1) * HEAD_DIM)
            q = q_ref[:, cols]
            zero = jnp.zeros_like(q)
            k = k_ref[pl.ds(k0, t), cols]
            v1 = _with_ones(v_ref[pl.ds(k0, t), cols])
            b = bias_ref[h, :, pl.ds(off, t)]
            for mp, qm in enumerate((jnp.where(lo, q, zero), jnp.where(lo, zero, q))):
                s = lax.dot_general(qm, k, _NT, preferred_element_type=F32) + b
                if masked:
                    s = jnp.where(ok, s, NEG)
                _softmax_step(s, m_sc.at[2 * h + mp], acc_sc.at[2 * h + mp], v1)

    def body(kb, carry):
        step(kb, False)
        return carry
    lax.fori_loop(0, qi, body, 0)
    step(qi, True)

    lam = (jnp.exp(jnp.sum(lq1_ref[...] * lk1_ref[...], axis=-1, keepdims=True))
           - jnp.exp(jnp.sum(lq2_ref[...] * lk2_ref[...], axis=-1, keepdims=True)) + lambda_init)
    for h in range(nh):
        o = _normalised(acc_sc[2 * h]) - lam * _normalised(acc_sc[2 * h + 1])
        o = _rmsnorm_rows(o, g_ref[...]) * (1.0 - lambda_init)
        o_ref[:, h * HEAD_DIM:(h + 1) * HEAD_DIM] = o.astype(o_ref.dtype)


def _diff_attn_call(qk, v, bias, lam_rows, subln_row, lambda_init, t):
    b, s, _ = v.shape
    nh = DIFF_HEADS_PER_STEP
    wb = nh * HEAD_DIM
    n_hg = N_HEADS // nh
    vec = lambda w: pl.BlockSpec((1, w), lambda hg, bb, qi: (0, 0))
    return pl.pallas_call(
        functools.partial(_diff_attn_body, t=t, nh=nh, lambda_init=lambda_init),
        grid=(n_hg, b, s // t),
        in_specs=[vec(DIFF_QK_DIM)] * 4 + [
            vec(HEAD_DIM),
            pl.BlockSpec((None, t, wb), lambda hg, bb, qi: (bb, qi, hg)),
            pl.BlockSpec((None, s, wb), lambda hg, bb, qi: (bb, 0, n_hg + hg)),
            pl.BlockSpec((None, s, wb), lambda hg, bb, qi: (bb, 0, hg)),
            pl.BlockSpec((nh, t, s), lambda hg, bb, qi: (hg, 0, 0))],
        out_specs=pl.BlockSpec((None, t, wb), lambda hg, bb, qi: (bb, qi, hg)),
        out_shape=jax.ShapeDtypeStruct(v.shape, BF16),
        scratch_shapes=[pltpu.VMEM((2 * nh, t, LANES), F32), pltpu.VMEM((2 * nh, t, 2 * HEAD_DIM), F32)],
        compiler_params=_cparams(("parallel", "parallel", "parallel")),
        name="diff_attn",
    )(*lam_rows, subln_row, qk, qk, v, bias)


def _fox_attn_body(q_ref, k_ref, v_ref, cq_ref, ck_ref, o_ref, m_sc, acc_sc, cq_sc, *, t, nh):
    qi = pl.program_id(2)
    m_sc[...] = jnp.full(m_sc.shape, NEG, F32)
    acc_sc[...] = jnp.zeros(acc_sc.shape, F32)
    for h in range(nh):
        cq_sc[h] = jnp.broadcast_to(cq_ref[h], (t, LANES))

    def step(kb, masked):
        k0 = pl.multiple_of(kb * t, t)
        if masked:
            r = lax.broadcasted_iota(I32, (t, t), 0)
            c = lax.broadcasted_iota(I32, (t, t), 1)
            ok = c <= r
        for h in range(nh):
            cols = slice(h * HEAD_DIM, (h + 1) * HEAD_DIM)
            k = k_ref[pl.ds(k0, t), cols]
            v1 = _with_ones(v_ref[pl.ds(k0, t), cols])
            s = lax.dot_general(q_ref[:, cols], k, _NT, preferred_element_type=F32)
            s = s + jnp.tile(cq_sc[h], (1, t // LANES)) - ck_ref[h, :, pl.ds(k0, t)]
            if masked:
                s = jnp.where(ok, s, NEG)
            _softmax_step(s, m_sc.at[h], acc_sc.at[h], v1)

    def body(kb, carry):
        step(kb, False)
        return carry
    lax.fori_loop(0, qi, body, 0)
    step(qi, True)
    for h in range(nh):
        o_ref[:, h * HEAD_DIM:(h + 1) * HEAD_DIM] = _normalised(acc_sc[h]).astype(o_ref.dtype)


def _fox_attn_call(qk, v, cum_col, cum_row, t):
    b, s, _ = v.shape
    nh = FOX_HEADS_PER_STEP
    wb = nh * HEAD_DIM
    n_hg = N_HEADS // nh
    return pl.pallas_call(
        functools.partial(_fox_attn_body, t=t, nh=nh),
        grid=(n_hg, b, s // t),
        in_specs=[pl.BlockSpec((None, t, wb), lambda hg, bb, qi: (bb, qi, hg)),
                  pl.BlockSpec((None, s, wb), lambda hg, bb, qi: (bb, 0, n_hg + hg)),
                  pl.BlockSpec((None, s, wb), lambda hg, bb, qi: (bb, 0, hg)),
                  pl.BlockSpec((None, nh, t, 1), lambda hg, bb, qi: (bb, hg, qi, 0)),
                  pl.BlockSpec((None, nh, 1, s), lambda hg, bb, qi: (bb, hg, 0, 0))],
        out_specs=pl.BlockSpec((None, t, wb), lambda hg, bb, qi: (bb, qi, hg)),
        out_shape=jax.ShapeDtypeStruct(v.shape, BF16),
        scratch_shapes=[pltpu.VMEM((nh, t, LANES), F32), pltpu.VMEM((nh, t, 2 * HEAD_DIM), F32),
                        pltpu.VMEM((nh, t, LANES), F32)],
        compiler_params=_cparams(("parallel", "parallel", "parallel")),
        name="fox_attn",
    )(qk, qk, v, cum_col, cum_row)


def _mix_body(od_ref, of_ref, g_ref, x_ref, wa_ref, wb_ref, wo_ref, nm_ref, wrh_ref, wrl_ref, br_ref,
              x1_ref, hp_ref, lg_ref):
    d = x_ref.shape[1]
    ua = jnp.dot(od_ref[...], wa_ref[...], preferred_element_type=F32)
    ub = jnp.dot(of_ref[...], wb_ref[...], preferred_element_type=F32)
    mixed = g_ref[:, :d].astype(F32) * ua + g_ref[:, d:].astype(F32) * ub
    x1 = x_ref[...] + jnp.dot(mixed.astype(BF16), wo_ref[...], preferred_element_type=F32)
    x1_ref[...] = x1
    hm = _rmsnorm_rows(x1, nm_ref[...])
    h_hi = hm.astype(BF16)
    h_hi32 = h_hi.astype(F32)
    h_lo = (hm - h_hi32).astype(BF16)
    lg_ref[...] = (jnp.dot(h_hi, wrh_ref[...], preferred_element_type=F32)
                   + jnp.dot(h_lo, wrh_ref[...], preferred_element_type=F32)
                   + jnp.dot(h_hi, wrl_ref[...], preferred_element_type=F32) + br_ref[...])
    bits = lax.bitcast_convert_type(h_hi32, U32)
    packed = (bits[:, :d // 2] >> 16) | (bits[:, d // 2:] & jnp.uint32(0xFFFF0000))
    tm = packed.shape[0]
    pieces = packed.shape[1] // LANES
    for j in range(pieces):
        hp_ref[pl.ds(j, tm, stride=pieces), :] = packed[:, j * LANES:(j + 1) * LANES]


def _resident(shape):
    return pl.BlockSpec(shape, lambda i: (0,) * len(shape), pipeline_mode=pl.Buffered(1))


def _mix_call(od, of, gates, x2d, wa, wb, wo, nm_row, wr_hi, wr_lo, br_pad, tm=256):
    n, d = x2d.shape
    tm = min(tm, n)
    wdt = od.shape[1]
    return pl.pallas_call(
        _mix_body,
        grid=(n // tm,),
        in_specs=[pl.BlockSpec((tm, wdt), lambda i: (i, 0)),
                  pl.BlockSpec((tm, wdt), lambda i: (i, 0)),
                  pl.BlockSpec((tm, 2 * d), lambda i: (i, 0)),
                  pl.BlockSpec((tm, d), lambda i: (i, 0)),
                  _resident((wdt, d)), _resident((wdt, d)), _resident((d, d)),
                  _resident((1, d)), _resident((d, LANES)), _resident((d, LANES)), _resident((1, LANES))],
        out_specs=[pl.BlockSpec((tm, d), lambda i: (i, 0)),
                   pl.BlockSpec((tm * (d // 2 // LANES), LANES), lambda i: (i, 0)),
                   pl.BlockSpec((tm, LANES), lambda i: (i, 0))],
        out_shape=[jax.ShapeDtypeStruct((n, d), F32),
                   jax.ShapeDtypeStruct((n * (d // 2 // LANES), LANES), U32),
                   jax.ShapeDtypeStruct((n, LANES), F32)],
        compiler_params=_cparams(("parallel",)),
        name="mix",
    )(od, of, gates, x2d, wa, wb, wo, nm_row, wr_hi, wr_lo, br_pad)


def _router_body(lg_ref, rank_ref, jm_ref, wm_ref, cnt_ref, carry_ref):
    i = pl.program_id(0)

    @pl.when(i == 0)
    def _():
        carry_ref[...] = jnp.zeros_like(carry_ref)

    vals = lg_ref[...]
    tm = vals.shape[0]
    lane = lax.broadcasted_iota(I32, vals.shape, 1).astype(F32)
    jm = jnp.zeros(vals.shape, F32)
    tops = []
    for j in range(TOP_K):
        m = jnp.max(vals, axis=-1, keepdims=True)
        idx = jnp.min(jnp.where(vals == m, lane, float(LANES)), axis=-1, keepdims=True)
        sel = lane == idx
        jm = jnp.where(sel, float(j + 1), jm)
        vals = jnp.where(sel, -jnp.inf, vals)
        tops.append(m)
    es = [jnp.exp(m - tops[0]) for m in tops]
    den = es[0] + es[1] + es[2] + es[3]
    wm = jnp.zeros(vals.shape, F32)
    for j in range(TOP_K):
        wm = jnp.where(jm == float(j + 1), es[j] / den, wm)
    sel_any = jnp.where(jm > 0.0, 1.0, 0.0)
    r = lax.broadcasted_iota(I32, (tm, tm), 0)
    c = lax.broadcasted_iota(I32, (tm, tm), 1)
    tri = jnp.where(c < r, 1.0, 0.0).astype(BF16)
    carry = carry_ref[...]
    rank_ref[...] = jnp.dot(tri, sel_any.astype(BF16), preferred_element_type=F32) + carry
    jm_ref[...] = jm
    wm_ref[...] = wm
    carry = carry + jnp.sum(sel_any, axis=0, keepdims=True)
    carry_ref[...] = carry
    cnt_ref[...] = carry


def _router_call(logits, tm=256):
    n = logits.shape[0]
    tm = min(tm, n)
    tile = pl.BlockSpec((tm, LANES), lambda i: (i, 0))
    row = pl.BlockSpec((1, LANES), lambda i: (0, 0))
    return pl.pallas_call(
        _router_body,
        grid=(n // tm,),
        in_specs=[tile],
        out_specs=[tile, tile, tile, row],
        out_shape=[jax.ShapeDtypeStruct((n, LANES), F32)] * 3 + [jax.ShapeDtypeStruct((1, LANES), F32)],
        scratch_shapes=[pltpu.VMEM((1, LANES), F32)],
        compiler_params=_cparams(("arbitrary",)),
        name="router",
    )(logits)


def _dest_body(rank_ref, jm_ref, wm_ref, ps_ref, dest_ref, w4_ref):
    slot = rank_ref[...] + ps_ref[...]
    jm = jm_ref[...]
    wm = wm_ref[...]
    lane = lax.broadcasted_iota(I32, jm.shape, 1)
    dest = jnp.zeros(jm.shape, F32)
    w4 = jnp.zeros(jm.shape, F32)
    for j in range(TOP_K):
        sel = jm == float(j + 1)
        dj = jnp.sum(jnp.where(sel, slot, 0.0), axis=-1, keepdims=True)
        wj = jnp.sum(jnp.where(sel, wm, 0.0), axis=-1, keepdims=True)
        dest = jnp.where(lane == j, dj, dest)
        w4 = jnp.where(lane == j, wj, w4)
    dest_ref[...] = dest.astype(I32)
    w4_ref[...] = w4


def _dest_call(rank, jm, wm, ps_row, tm=512):
    n = rank.shape[0]
    tm = min(tm, n)
    tile = pl.BlockSpec((tm, LANES), lambda i: (i, 0))
    return pl.pallas_call(
        _dest_body,
        grid=(n // tm,),
        in_specs=[tile, tile, tile, pl.BlockSpec((1, LANES), lambda i: (0, 0))],
        out_specs=[tile, tile],
        out_shape=[jax.ShapeDtypeStruct((n, LANES), I32), jax.ShapeDtypeStruct((n, LANES), F32)],
        compiler_params=_cparams(("parallel",)),
        name="dest",
    )(rank, jm, wm, ps_row)


def _slot_table_body(dest_ref, cnt_ref, pad_ref, pst_ref, slot_ref, *, n_tok):
    def pads(e, carry):
        def one(s, c2):
            slot_ref[pst_ref[e] + s] = 0
            return c2
        lax.fori_loop(cnt_ref[e], pad_ref[e], one, 0)
        return carry
    lax.fori_loop(0, N_EXPERTS, pads, 0)

    def tail(s, carry):
        slot_ref[s] = 0
        return carry
    lax.fori_loop(pst_ref[N_EXPERTS - 1] + pad_ref[N_EXPERTS - 1], slot_ref.shape[0], tail, 0)

    unroll = 8
    assert n_tok % unroll == 0

    def toks(i, carry):
        for u in range(unroll):
            t = i * unroll + u
            for j in range(TOP_K):
                slot_ref[dest_ref[t * TOP_K + j]] = t
        return carry
    lax.fori_loop(0, n_tok // unroll, toks, 0)


def _slot_table_call(dest_flat, counts, padded, pstart, n_slots):
    smem = pl.BlockSpec(memory_space=pltpu.SMEM)
    return pl.pallas_call(
        functools.partial(_slot_table_body, n_tok=dest_flat.shape[0] // TOP_K),
        in_specs=[smem, smem, smem, smem],
        out_specs=smem,
        out_shape=jax.ShapeDtypeStruct((n_slots,), I32),
        name="slot_table",
    )(dest_flat, counts, padded, pstart)


ROW_UNROLL = 8


def _expert_body(gexp_ref, gsub0_ref, gns_ref, slot_ref,
                 hp_hbm, w1g_ref, w1l_ref, b1g_ref, b1l_ref, w2_ref, b2_ref,
                 outs_hbm,
                 xlo, xhi, stage, acc, sem_in, sem_out, *, n_chunks, n_groups):
    g = pl.program_id(0)
    c = pl.program_id(1)
    ns = gns_ref[g]
    s0 = gsub0_ref[g]
    sub = SLOT_BLOCK
    half = xlo.shape[1]
    last = n_chunks - 1

    pieces = half // LANES

    def row_copy(tok, r):
        return pltpu.make_async_copy(hp_hbm.at[pl.ds(pl.multiple_of(tok * pieces, pieces), pieces), :],
                                     stage.at[pl.ds(pl.multiple_of(r * pieces, pieces), pieces), :], sem_in)

    def issue_rows(gi):
        base = gsub0_ref[gi] * sub

        def trip(i, carry):
            for u in range(ROW_UNROLL):
                r = i * ROW_UNROLL + u
                row_copy(slot_ref[base + r], r).start()
            return carry
        lax.fori_loop(0, gns_ref[gi] * (sub // ROW_UNROLL), trip, 0)

    def out_copy(r0, rows):
        row0 = s0 * sub + r0
        return pltpu.make_async_copy(acc.at[pl.ds(r0, rows), :], outs_hbm.at[pl.ds(row0, rows), :], sem_out)

    def compute(r0, rows):
        xl = xlo[pl.ds(r0, rows), :]
        xh = xhi[pl.ds(r0, rows), :]
        hg = (jnp.dot(xl, w1g_ref[:half, :].astype(BF16), preferred_element_type=F32)
              + jnp.dot(xh, w1g_ref[half:, :].astype(BF16), preferred_element_type=F32) + b1g_ref[...])
        hl = (jnp.dot(xl, w1l_ref[:half, :].astype(BF16), preferred_element_type=F32)
              + jnp.dot(xh, w1l_ref[half:, :].astype(BF16), preferred_element_type=F32) + b1l_ref[...])
        gate = jnp.minimum(hg, SWIGLU_LIMIT)
        lin = jnp.clip(hl, -SWIGLU_LIMIT, SWIGLU_LIMIT)
        act = gate * jax.nn.sigmoid(SWIGLU_ALPHA * gate) * (lin + 1.0)
        acc[pl.ds(r0, rows), :] += jnp.dot(act.astype(BF16), w2_ref[...].astype(BF16),
                                           preferred_element_type=F32)

        @pl.when(c == last)
        def _():
            out_copy(r0, rows).start()

    @pl.when(ns > 0)
    def _():
        @pl.when(c == 0)
        def _load():
            @pl.when(g == 0)
            def _():
                issue_rows(0)

            def wt(i, carry):
                for u in range(ROW_UNROLL):
                    row_copy(0, i * ROW_UNROLL + u).wait()
                return carry
            lax.fori_loop(0, ns * (sub // ROW_UNROLL), wt, 0)

            def unpack(s, carry):
                r0 = pl.multiple_of(s * sub, sub)
                for j in range(pieces):
                    w = stage[pl.ds(r0 * pieces + j, sub, stride=pieces), :]
                    cols = slice(j * LANES, (j + 1) * LANES)
                    xlo[pl.ds(r0, sub), cols] = lax.bitcast_convert_type(w << 16, F32).astype(BF16)
                    xhi[pl.ds(r0, sub), cols] = lax.bitcast_convert_type(
                        w & jnp.uint32(0xFFFF0000), F32).astype(BF16)
                acc[pl.ds(r0, sub), :] = jnp.broadcast_to(b2_ref[...], (sub, acc.shape[1]))
                return carry
            lax.fori_loop(0, ns, unpack, 0)

        @pl.when(c == min(1, last))
        def _prefetch():
            nxt = jnp.minimum(g + 1, n_groups - 1)

            @pl.when(jnp.logical_and(g + 1 < n_groups, gns_ref[nxt] > 0))
            def _():
                issue_rows(nxt)

        def pair(i, carry):
            compute(pl.multiple_of(i * (2 * sub), 2 * sub), 2 * sub)
            return carry
        lax.fori_loop(0, ns // 2, pair, 0)

        @pl.when(ns % 2 == 1)
        def _():
            compute(pl.multiple_of((ns - 1) * sub, sub), sub)

        @pl.when(c == last)
        def _drain():
            def wt(i, carry):
                out_copy(pl.multiple_of(i * (2 * sub), 2 * sub), 2 * sub).wait()
                return carry
            lax.fori_loop(0, ns // 2, wt, 0)

            @pl.when(ns % 2 == 1)
            def _():
                out_copy(pl.multiple_of((ns - 1) * sub, sub), sub).wait()

            nxt = jnp.minimum(g + 1, n_groups - 1)

            @pl.when(jnp.logical_or(g + 1 >= n_groups, gns_ref[nxt] == 0))
            def _tail():
                acc[pl.ds(0, sub), :] = jnp.zeros((sub, acc.shape[1]), F32)
                first = s0 + ns
                n_tail = outs_hbm.shape[0] // sub - first

                def tail_copy(i):
                    row0 = pl.multiple_of((first + i) * sub, sub)
                    return pltpu.make_async_copy(acc.at[pl.ds(0, sub), :], outs_hbm.at[pl.ds(row0, sub), :],
                                                 sem_out)

                def st(i, carry):
                    tail_copy(i).start()
                    return carry

                def wt(i, carry):
                    tail_copy(i).wait()
                    return carry
                lax.fori_loop(0, n_tail, st, 0)
                lax.fori_loop(0, n_tail, wt, 0)


def _expert_call(gexp, gsub0, gns, slot_tok, hp, w1, b1, w2, b2):
    n_slots = slot_tok.shape[0]
    half = w1.shape[1] // 2
    n_exp, d, ff2 = w1.shape
    ff = ff2 // 2
    tc = FF_CHUNK
    n_chunks = ff // tc
    n_groups = gexp.shape[0]
    rows = GROUP_SUBS * SLOT_BLOCK

    def chunk(c, gn, g):
        return jnp.where(gn[g] > 0, c, n_chunks - 1)

    grid_spec = pltpu.PrefetchScalarGridSpec(
        num_scalar_prefetch=4,
        grid=(n_groups, n_chunks),
        in_specs=[
            pl.BlockSpec(memory_space=pl.ANY),
            pl.BlockSpec((None, d, tc), lambda g, c, ge, gs, gn, st: (ge[g], 0, chunk(c, gn, g))),
            pl.BlockSpec((None, d, tc), lambda g, c, ge, gs, gn, st: (ge[g], 0, n_chunks + chunk(c, gn, g))),
            pl.BlockSpec((None, 1, tc), lambda g, c, ge, gs, gn, st: (ge[g], 0, chunk(c, gn, g))),
            pl.BlockSpec((None, 1, tc), lambda g, c, ge, gs, gn, st: (ge[g], 0, n_chunks + chunk(c, gn, g))),
            pl.BlockSpec((None, tc, d), lambda g, c, ge, gs, gn, st: (ge[g], chunk(c, gn, g), 0)),
            pl.BlockSpec((None, 1, d), lambda g, c, ge, gs, gn, st: (ge[g], 0, 0)),
        ],
        out_specs=pl.BlockSpec(memory_space=pl.ANY),
        scratch_shapes=[
            pltpu.VMEM((rows, half), BF16), pltpu.VMEM((rows, half), BF16),
            pltpu.VMEM((rows * (half // LANES), LANES), U32),
            pltpu.VMEM((rows, d), F32),
            pltpu.SemaphoreType.DMA(()), pltpu.SemaphoreType.DMA(()),
        ],
    )
    return pl.pallas_call(
        functools.partial(_expert_body, n_chunks=n_chunks, n_groups=n_groups),
        grid_spec=grid_spec,
        out_shape=jax.ShapeDtypeStruct((n_slots, d), F32),
        compiler_params=_cparams(("arbitrary", "arbitrary")),
        name="expert",
    )(gexp, gsub0, gns, slot_tok, hp, w1, w1, b1.reshape(n_exp, 1, ff2), b1.reshape(n_exp, 1, ff2), w2,
      b2.reshape(n_exp, 1, d))


def _final_body(dest_ref, x1_ref, w4_ref, p_ref, wple_ref, wpg_ref, bpg_ref, npl_ref, outs_hbm,
                o_ref, gbuf, sems, *, tm, n_steps):
    i = pl.program_id(0)
    cur = i % 2

    def row_copy(slot, buf, j, q, u):
        return pltpu.make_async_copy(outs_hbm.at[pl.ds(slot, 1), :], gbuf.at[buf, j, q, pl.ds(u, 1), :],
                                     sems.at[buf])

    def issue(step, buf):
        def rows(q, carry):
            for u in range(SUBLANES):
                t = step * tm + q * SUBLANES + u
                for j in range(TOP_K):
                    row_copy(dest_ref[t * TOP_K + j], buf, j, q, u).start(priority=j % 2)
            return carry
        lax.fori_loop(0, tm // SUBLANES, rows, 0)

    @pl.when(i == 0)
    def _():
        issue(0, 0)

    @pl.when(i + 1 < n_steps)
    def _():
        issue(i + 1, 1 - cur)

    pe = jnp.dot(p_ref[...].astype(BF16), wple_ref[...], preferred_element_type=F32)

    def wt(q, carry):
        for u in range(SUBLANES):
            for j in range(TOP_K):
                row_copy(0, cur, j, q, u).wait()
        return carry
    lax.fori_loop(0, tm // SUBLANES, wt, 0)

    w4 = w4_ref[...]
    d = o_ref.shape[1]
    y = gbuf[cur, 0].reshape(tm, d) * w4[:, 0:1]
    for j in range(1, TOP_K):
        y = y + gbuf[cur, j].reshape(tm, d) * w4[:, j:j + 1]
    x2 = x1_ref[...] + y
    hn = _rmsnorm_rows(x2, npl_ref[...])
    gate = jax.nn.sigmoid(jnp.dot(hn.astype(BF16), wpg_ref[...], preferred_element_type=F32) + bpg_ref[...])
    o_ref[...] = x2 + gate * pe


def _final_call(dest_flat, x1, w4, p2d, wple, wpg, bpg_row, npl_row, outs, tm=256):
    n, d = x1.shape
    tm = min(tm, n)
    pd = p2d.shape[1]
    res = lambda shape: pl.BlockSpec(shape, lambda i, dst: (0,) * len(shape), pipeline_mode=pl.Buffered(1))
    grid_spec = pltpu.PrefetchScalarGridSpec(
        num_scalar_prefetch=1,
        grid=(n // tm,),
        in_specs=[pl.BlockSpec((tm, d), lambda i, dst: (i, 0)),
                  pl.BlockSpec((tm, LANES), lambda i, dst: (i, 0)),
                  pl.BlockSpec((tm, pd), lambda i, dst: (i, 0)),
                  res((pd, d)), res((d, d)), res((1, d)), res((1, d)),
                  pl.BlockSpec(memory_space=pl.ANY)],
        out_specs=pl.BlockSpec((tm, d), lambda i, dst: (i, 0)),
        scratch_shapes=[pltpu.VMEM((2, TOP_K, tm // SUBLANES, SUBLANES, d), F32),
                        pltpu.SemaphoreType.DMA((2,))],
    )
    return pl.pallas_call(
        functools.partial(_final_body, tm=tm, n_steps=n // tm),
        grid_spec=grid_spec,
        out_shape=jax.ShapeDtypeStruct((n, d), F32),
        compiler_params=_cparams(("arbitrary",)),
        name="final",
    )(dest_flat, x1, w4, p2d, wple, wpg, bpg_row, npl_row, outs)


def _t5_bucket(rel):
    n = -rel
    nb = REL_BUCKETS // 2
    ret = jnp.where(n < 0, nb, 0)
    n = jnp.abs(n)
    max_exact = nb // 2
    large = max_exact + (jnp.log(jnp.maximum(n, 1).astype(jnp.float32) / max_exact)
                         / math.log(REL_MAX_DIST / max_exact) * (nb - max_exact)).astype(jnp.int32)
    large = jnp.minimum(large, nb - 1)
    return ret + jnp.where(n < max_exact, n, large)


def _bias_body(rv_ref, o_ref, *, t):
    x = jnp.broadcast_to(rv_ref[...], (t, rv_ref.shape[1]))
    o_ref[...] = pltpu.roll(x, 1, 1, stride=1, stride_axis=0)[:, t:]


def _diff_bias_table(rel_bias, seq, t):
    rel = jnp.arange(seq + t) - (seq - 1)
    rv = (jnp.transpose(rel_bias[_t5_bucket(rel)], (1, 0)).astype(F32) * LOG2E)[:, None, :]
    return pl.pallas_call(
        functools.partial(_bias_body, t=t),
        grid=(N_HEADS,),
        in_specs=[pl.BlockSpec((None, 1, seq + t), lambda h: (h, 0, 0))],
        out_specs=pl.BlockSpec((None, t, seq), lambda h: (h, 0, 0)),
        out_shape=jax.ShapeDtypeStruct((N_HEADS, t, seq), F32),
        compiler_params=_cparams(("parallel",)),
        name="bias_table",
    )(rv)


def _pad_cols(a, width):
    return jnp.pad(a, ((0, 0), (0, width - a.shape[1])))


def _group_table(counts):
    n_sub = (counts + SLOT_BLOCK - 1) // SLOT_BLOCK
    sub_start = jnp.cumsum(n_sub) - n_sub
    n_grp = (n_sub + GROUP_SUBS - 1) // GROUP_SUBS
    grp_end = jnp.cumsum(n_grp)
    total = grp_end[-1]
    return n_sub, sub_start, n_grp, grp_end, total


def _layer(x, p_l, w_in, b_gate, b_forget, dq_norm, dk_norm, fq_norm, fk_norm, lq1, lk1, lq2, lk2,
           lambda_init, subln, w_up_a, w_up_b, w_out, rel_bias, norm_mix, norm_moe, w_router,
           b_router, w1, b1, w2, b2, norm_ple, w_ple_gate, b_ple_gate, w_ple):
    b, s, d = x.shape
    n = b * s
    width = N_HEADS * HEAD_DIM
    t = min(ATTN_T, s)
    x2d = x.reshape(n, d)
    row = lambda v: v.reshape(1, -1).astype(F32)

    f_col = 6 * width
    h, f_logit = _norm_call(x2d, row(norm_mix), w_in, f_col)
    cum = _cum_call(f_logit, _pad_cols(row(b_forget), LANES), b, s)
    cum_bhs = jnp.transpose(cum[:, :N_HEADS].reshape(b, s, N_HEADS), (0, 2, 1))
    cum_col = cum_bhs[..., None]
    cum_row = cum_bhs[:, :, None, :]

    diff_scale = DIFF_QK_DIM ** -0.5 * LOG2E
    fox_scale = HEAD_DIM ** -0.5 * LOG2E
    gain_d = jnp.concatenate([jnp.tile(dq_norm * diff_scale, 2 * N_HEADS), jnp.tile(dk_norm, 2 * N_HEADS)])
    gain_f = jnp.concatenate([jnp.tile(fq_norm * fox_scale, N_HEADS), jnp.tile(fk_norm, N_HEADS)])
    zeros_w = jnp.zeros((1, width), F32)
    dqk = _proj_call(h, w_in, 0, 2 * width, row(gain_d), "norm64", "proj_dqk")
    dv = _proj_call(h, w_in, 2 * width, width, zeros_w, "plain", "proj_dv")
    fqk = _proj_call(h, w_in, 3 * width, 2 * width, row(gain_f), "norm128", "proj_fqk")
    fv = _proj_call(h, w_in, 5 * width, width, zeros_w, "plain", "proj_fv")
    gates = _proj_gate_call(h, w_in, f_col + N_HEADS, 2 * d, row(b_gate))

    lam_rows = [row(v) for v in (lq1, lk1, lq2, lk2)]
    bias = _diff_bias_table(rel_bias, s, t)
    od = _diff_attn_call(dqk.reshape(b, s, 2 * width), dv.reshape(b, s, width), bias, lam_rows,
                         row(subln), lambda_init, t)
    of = _fox_attn_call(fqk.reshape(b, s, 2 * width), fv.reshape(b, s, width), cum_col, cum_row, t)

    br_pad = jnp.full((1, LANES), NEG, F32).at[0, :N_EXPERTS].set(b_router.astype(F32))
    wr_pad = _pad_cols(w_router.astype(F32), LANES)
    wr_hi = wr_pad.astype(BF16)
    wr_lo = (wr_pad - wr_hi.astype(F32)).astype(BF16)
    x1, hp, logits = _mix_call(od.reshape(n, width), of.reshape(n, width), gates, x2d,
                               w_up_a.astype(BF16), w_up_b.astype(BF16), w_out.astype(BF16),
                               row(norm_moe), wr_hi, wr_lo, br_pad)

    rank, jm, wm, cnt = _router_call(logits)
    counts = cnt[0, :N_EXPERTS].astype(I32)
    padded = ((counts + SLOT_BLOCK - 1) // SLOT_BLOCK) * SLOT_BLOCK
    pstart = jnp.cumsum(padded) - padded
    ps_row = _pad_cols(pstart.astype(F32).reshape(1, -1), LANES)
    dest128, w4 = _dest_call(rank, jm, wm, ps_row)
    dest_flat = dest128[:, :TOP_K].reshape(-1)

    n_slots = n * TOP_K + N_EXPERTS * SLOT_BLOCK
    slot_tok = _slot_table_call(dest_flat, counts, padded, pstart, n_slots)

    n_sub, sub_start, n_grp, grp_end, total = _group_table(counts)
    max_groups = N_EXPERTS + (n_slots // SLOT_BLOCK) // GROUP_SUBS
    gidx = jnp.arange(max_groups, dtype=I32)
    gvalid = gidx < total
    gsafe = jnp.minimum(gidx, total - 1)
    gexp = jnp.searchsorted(grp_end, gsafe, side="right").astype(I32)
    kth = gsafe - (grp_end - n_grp)[gexp]
    gsub0 = (sub_start[gexp] + kth * GROUP_SUBS).astype(I32)
    gns = jnp.where(gvalid, jnp.minimum(GROUP_SUBS, n_sub[gexp] - kth * GROUP_SUBS), 0).astype(I32)
    outs = _expert_call(gexp, gsub0, gns, slot_tok, hp, w1, b1, w2, b2)

    out = _final_call(dest_flat, x1, w4, p_l.reshape(n, -1), w_ple.astype(BF16),
                      w_ple_gate.astype(BF16), row(b_ple_gate), row(norm_ple), outs)
    return out.reshape(b, s, d)


def kernel(x, p, w_in, b_gate, b_forget, dq_norm, dk_norm, fq_norm, fk_norm, lambda_q1, lambda_k1,
           lambda_q2, lambda_k2, subln, w_up_a, w_up_b, w_out, rel_bias, norm_mix, norm_moe,
           w_router, b_router, w1, b1, w2, b2, norm_ple, w_ple_gate, b_ple_gate, w_ple):
    for i in range(w_in.shape[0]):
        lambda_init = 0.8 - 0.6 * math.exp(-0.3 * i)
        x = _layer(x, p[i], w_in[i], b_gate[i], b_forget[i], dq_norm[i], dk_norm[i], fq_norm[i],
                   fk_norm[i], lambda_q1[i], lambda_k1[i], lambda_q2[i], lambda_k2[i], lambda_init,
                   subln[i], w_up_a[i], w_up_b[i], w_out[i], rel_bias, norm_mix[i], norm_moe[i],
                   w_router[i], b_router[i], w1[i], b1[i], w2[i], b2[i], norm_ple[i],
                   w_ple_gate[i], b_ple_gate[i], w_ple[i])
    return x
```

```python
import functools
import math

import jax
import jax.numpy as jnp
from jax import lax
from jax.experimental import pallas as pl
from jax.experimental.pallas import tpu as pltpu

F32 = jnp.float32
BF16 = jnp.bfloat16
U32 = jnp.uint32
I32 = jnp.int32
HIGHEST = lax.Precision.HIGHEST

N_HEADS = 8
HEAD_DIM = 128
DIFF_QK_DIM = 64
CHUNK = 64
REL_BUCKETS = 32
REL_MAX_DIST = 128
N_EXPERTS = 32
TOP_K = 4
SWIGLU_LIMIT = 7.0
SWIGLU_ALPHA = 1.702
RMS_EPS = 1e-6
NEG = -1e30
LOG2E = math.log2(math.e)

LANES = 128
SUBLANES = 8
SLOT_BLOCK = 256
GROUP_SUBS = 6
FF_CHUNK = 512
ATTN_T = 256
VMEM_LIMIT = 56 * 1024 * 1024


_NT = (((1,), (1,)), ((), ()))


def _cparams(sem):
    return pltpu.CompilerParams(dimension_semantics=sem, vmem_limit_bytes=VMEM_LIMIT)


def _rmsnorm_rows(x, g):
    ms = jnp.mean(x * x, axis=-1, keepdims=True)
    return x * lax.rsqrt(ms + RMS_EPS) * g


def _norm_body(x_ref, g_ref, wf_ref, h_ref, fl_ref):
    y = _rmsnorm_rows(x_ref[...], g_ref[...])
    h_ref[...] = y.astype(BF16)
    rowi = lax.broadcasted_iota(I32, (LANES, 1), 0)
    wf = jnp.where(rowi < N_HEADS, wf_ref[...], 0.0)
    fl_ref[...] = lax.dot_general(y, wf, _NT, precision=HIGHEST, preferred_element_type=F32)


def _norm_call(x2d, g_row, wt_in, f_row, tm=512):
    n, d = x2d.shape
    tm = min(tm, n)
    assert f_row % LANES == 0
    return pl.pallas_call(
        _norm_body,
        grid=(n // tm,),
        in_specs=[pl.BlockSpec((tm, d), lambda i: (i, 0)),
                  pl.BlockSpec((1, d), lambda i: (0, 0)),
                  pl.BlockSpec((LANES, d), lambda i: (f_row // LANES, 0))],
        out_specs=[pl.BlockSpec((tm, d), lambda i: (i, 0)),
                   pl.BlockSpec((tm, LANES), lambda i: (i, 0))],
        out_shape=[jax.ShapeDtypeStruct((n, d), BF16),
                   jax.ShapeDtypeStruct((n, LANES), F32)],
        compiler_params=_cparams(("parallel",)),
        name="norm",
    )(x2d, g_row, wt_in)


def _cum_body(fl_ref, bf_ref, cum_ref):
    s = fl_ref.shape[0]
    r = lax.broadcasted_iota(I32, (LANES, LANES), 0)
    c = lax.broadcasted_iota(I32, (LANES, LANES), 1)
    tri = jnp.where(r >= c, 1.0, 0.0).astype(F32)
    carry = jnp.zeros((1, LANES), F32)
    for blk in range(s // LANES):
        rows = slice(blk * LANES, (blk + 1) * LANES)
        lf = jax.nn.log_sigmoid(fl_ref[rows, :] + bf_ref[...])
        cb = jnp.dot(tri, lf, precision=HIGHEST, preferred_element_type=F32) + carry
        cum_ref[rows, :] = cb * LOG2E
        carry = cb[LANES - 1:LANES, :]


def _cum_call(fl, bf_row, batch, seq):
    return pl.pallas_call(
        _cum_body,
        grid=(batch,),
        in_specs=[pl.BlockSpec((seq, LANES), lambda b: (b, 0)),
                  pl.BlockSpec((1, LANES), lambda b: (0, 0))],
        out_specs=pl.BlockSpec((seq, LANES), lambda b: (b, 0)),
        out_shape=jax.ShapeDtypeStruct(fl.shape, F32),
        compiler_params=_cparams(("parallel",)),
        name="cum",
    )(fl, bf_row)


def _proj_epilogue(acc, aux_ref, o_ref, kind):
    tn = acc.shape[1]
    if kind == "plain":
        o_ref[...] = acc.astype(o_ref.dtype)
    elif kind == "gate":
        o_ref[...] = jax.nn.sigmoid(acc + aux_ref[...]).astype(o_ref.dtype)
    else:
        lo = lax.broadcasted_iota(I32, (1, LANES), 1) < DIFF_QK_DIM
        for s in range(tn // LANES):
            cols = slice(s * LANES, (s + 1) * LANES)
            blk = acc[:, cols]
            sq = blk * blk
            if kind == "norm128":
                ms = jnp.mean(sq, axis=-1, keepdims=True)
            else:
                s_lo = jnp.sum(jnp.where(lo, sq, 0.0), axis=-1, keepdims=True)
                s_hi = jnp.sum(jnp.where(lo, 0.0, sq), axis=-1, keepdims=True)
                ms = jnp.where(lo, s_lo, s_hi) * (1.0 / DIFF_QK_DIM)
            o_ref[:, cols] = (blk * lax.rsqrt(ms + RMS_EPS) * aux_ref[:, cols]).astype(o_ref.dtype)


def _proj_body(h_ref, wt_ref, aux_ref, o_ref, *, kind):
    acc = lax.dot_general(h_ref[...], wt_ref[...].astype(BF16), _NT, preferred_element_type=F32)
    _proj_epilogue(acc, aux_ref, o_ref, kind)


def _proj_shift_body(h_ref, wa_ref, wb_ref, aux_ref, o_ref, *, kind, shift):
    wt = jnp.concatenate([wa_ref[shift:, :], wb_ref[...]], axis=0)
    acc = lax.dot_general(h_ref[...], wt.astype(BF16), _NT, preferred_element_type=F32)
    _proj_epilogue(acc, aux_ref, o_ref, kind)


def _proj_call(h, wt, row0, nrows, aux_row, kind, name, tm=1024, tn=512):
    m, k = h.shape
    tm = min(tm, m)
    assert nrows % tn == 0 and m % tm == 0
    shift = row0 % tn
    base = row0 - shift
    h_spec = pl.BlockSpec((tm, k), lambda i, j: (i, 0))
    w_spec = pl.BlockSpec((tn, k), lambda i, j: (base // tn + j, 0))
    aux_spec = pl.BlockSpec((1, tn), lambda i, j: (0, j))
    if shift == 0:
        body, w_specs, ws = functools.partial(_proj_body, kind=kind), [w_spec], [wt]
    else:
        assert shift % SUBLANES == 0 and tn % shift == 0
        body = functools.partial(_proj_shift_body, kind=kind, shift=shift)
        w_specs = [w_spec, pl.BlockSpec((shift, k), lambda i, j: ((base + (j + 1) * tn) // shift, 0))]
        ws = [wt, wt]
    return pl.pallas_call(
        body,
        grid=(m // tm, nrows // tn),
        in_specs=[h_spec] + w_specs + [aux_spec],
        out_specs=pl.BlockSpec((tm, tn), lambda i, j: (i, j)),
        out_shape=jax.ShapeDtypeStruct((m, nrows), BF16),
        compiler_params=_cparams(("parallel", "parallel")),
        name=name,
    )(h, *ws, aux_row)


DIFF_HEADS_PER_STEP = 4
FOX_HEADS_PER_STEP = 8


def _softmax_step(s, m_ref, acc_ref, v1):
    m_old = m_ref[...]
    mn = jnp.maximum(m_old, jnp.max(s, axis=-1, keepdims=True))
    alpha = jnp.exp2(m_old - mn)
    p = jnp.exp2(s - jnp.tile(mn, (1, s.shape[1] // LANES))).astype(BF16)
    acc_ref[...] = (jnp.tile(alpha, (1, acc_ref.shape[1] // LANES)) * acc_ref[...]
                    + jnp.dot(p, v1, preferred_element_type=F32))
    m_ref[...] = mn


def _with_ones(v):
    return jnp.concatenate([v, jnp.ones_like(v)], axis=1)


def _normalised(acc):
    return acc[:, :HEAD_DIM] / acc[:, HEAD_DIM:]


def _diff_attn_body(lq1_ref, lk1_ref, lq2_ref, lk2_ref, g_ref, q_ref, k_ref, v_ref, bias_ref,
                    o_ref, m_sc, acc_sc, *, t, nh, lambda_init):
    qi = pl.program_id(2)
    s_len = k_ref.shape[0]
    m_sc[...] = jnp.full(m_sc.shape, NEG, F32)
    acc_sc[...] = jnp.zeros(acc_sc.shape, F32)
    lo = lax.broadcasted_iota(I32, (t, HEAD_DIM), 1) < DIFF_QK_DIM

    def step(kb, masked):
        k0 = pl.multiple_of(kb * t, t)
        off = pl.multiple_of((kb - qi) * t + (s_len - t), LANES)
        if masked:
            r = lax.broadcasted_iota(I32, (t, t), 0)
            c = lax.broadcasted_iota(I32, (t, t), 1)
            ok = (c // CHUNK) <= (r // CHUNK)
        for h in range(nh):
            cols = slice(h * HEAD_DIM, (h + 1) * HEAD_DIM)
            q = q_ref[:, cols]
            zero = jnp.zeros_like(q)
            k = k_ref[pl.ds(k0, t), cols]
            v1 = _with_ones(v_ref[pl.ds(k0, t), cols])
            b = bias_ref[h, :, pl.ds(off, t)]
            for mp, qm in enumerate((jnp.where(lo, q, zero), jnp.where(lo, zero, q))):
                s = lax.dot_general(qm, k, _NT, preferred_element_type=F32) + b
                if masked:
                    s = jnp.where(ok, s, NEG)
                _softmax_step(s, m_sc.at[2 * h + mp], acc_sc.at[2 * h + mp], v1)

    def body(kb, carry):
        step(kb, False)
        return carry
    lax.fori_loop(0, qi, body, 0)
    step(qi, True)

    lam = (jnp.exp(jnp.sum(lq1_ref[...] * lk1_ref[...], axis=-1, keepdims=True))
           - jnp.exp(jnp.sum(lq2_ref[...] * lk2_ref[...], axis=-1, keepdims=True)) + lambda_init)
    for h in range(nh):
        o = _normalised(acc_sc[2 * h]) - lam * _normalised(acc_sc[2 * h + 1])
        o = _rmsnorm_rows(o, g_ref[...]) * (1.0 - lambda_init)
        o_ref[:, h * HEAD_DIM:(h + 1) * HEAD_DIM] = o.astype(o_ref.dtype)


def _diff_attn_call(qk, v, bias, lam_rows, subln_row, lambda_init, t):
    b, s, _ = v.shape
    nh = DIFF_HEADS_PER_STEP
    wb = nh * HEAD_DIM
    n_hg = N_HEADS // nh
    vec = lambda w: pl.BlockSpec((1, w), lambda hg, bb, qi: (0, 0))
    return pl.pallas_call(
        functools.partial(_diff_attn_body, t=t, nh=nh, lambda_init=lambda_init),
        grid=(n_hg, b, s // t),
        in_specs=[vec(DIFF_QK_DIM)] * 4 + [
            vec(HEAD_DIM),
            pl.BlockSpec((None, t, wb), lambda hg, bb, qi: (bb, qi, hg)),
            pl.BlockSpec((None, s, wb), lambda hg, bb, qi: (bb, 0, n_hg + hg)),
            pl.BlockSpec((None, s, wb), lambda hg, bb, qi: (bb, 0, hg)),
            pl.BlockSpec((nh, t, s), lambda hg, bb, qi: (hg, 0, 0))],
        out_specs=pl.BlockSpec((None, t, wb), lambda hg, bb, qi: (bb, qi, hg)),
        out_shape=jax.ShapeDtypeStruct(v.shape, BF16),
        scratch_shapes=[pltpu.VMEM((2 * nh, t, LANES), F32), pltpu.VMEM((2 * nh, t, 2 * HEAD_DIM), F32)],
        compiler_params=_cparams(("parallel", "parallel", "parallel")),
        name="diff_attn",
    )(*lam_rows, subln_row, qk, qk, v, bias)


def _fox_attn_body(q_ref, k_ref, v_ref, cq_ref, ck_ref, o_ref, m_sc, acc_sc, cq_sc, *, t, nh):
    qi = pl.program_id(2)
    m_sc[...] = jnp.full(m_sc.shape, NEG, F32)
    acc_sc[...] = jnp.zeros(acc_sc.shape, F32)
    hg = pl.program_id(0)
    lane = lax.broadcasted_iota(I32, (t, LANES), 1)
    cq_all = cq_ref[...]
    for h in range(nh):
        col = jnp.sum(jnp.where(lane == hg * nh + h, cq_all, 0.0), axis=-1, keepdims=True)
        cq_sc[h] = jnp.broadcast_to(col, (t, LANES))

    def step(kb, masked):
        k0 = pl.multiple_of(kb * t, t)
        if masked:
            r = lax.broadcasted_iota(I32, (t, t), 0)
            c = lax.broadcasted_iota(I32, (t, t), 1)
            ok = c <= r
        for h in range(nh):
            cols = slice(h * HEAD_DIM, (h + 1) * HEAD_DIM)
            k = k_ref[pl.ds(k0, t), cols]
            v1 = _with_ones(v_ref[pl.ds(k0, t), cols])
            s = lax.dot_general(q_ref[:, cols], k, _NT, preferred_element_type=F32)
            s = s + jnp.tile(cq_sc[h], (1, t // LANES)) - ck_ref[h, :, pl.ds(k0, t)]
            if masked:
                s = jnp.where(ok, s, NEG)
            _softmax_step(s, m_sc.at[h], acc_sc.at[h], v1)

    def body(kb, carry):
        step(kb, False)
        return carry
    lax.fori_loop(0, qi, body, 0)
    step(qi, True)
    for h in range(nh):
        o_ref[:, h * HEAD_DIM:(h + 1) * HEAD_DIM] = _normalised(acc_sc[h]).astype(o_ref.dtype)


def _fox_attn_call(qk, v, cum, cum_row, t):
    b, s, _ = v.shape
    nh = FOX_HEADS_PER_STEP
    wb = nh * HEAD_DIM
    n_hg = N_HEADS // nh
    return pl.pallas_call(
        functools.partial(_fox_attn_body, t=t, nh=nh),
        grid=(n_hg, b, s // t),
        in_specs=[pl.BlockSpec((None, t, wb), lambda hg, bb, qi: (bb, qi, hg)),
                  pl.BlockSpec((None, s, wb), lambda hg, bb, qi: (bb, 0, n_hg + hg)),
                  pl.BlockSpec((None, s, wb), lambda hg, bb, qi: (bb, 0, hg)),
                  pl.BlockSpec((None, t, LANES), lambda hg, bb, qi: (bb, qi, 0)),
                  pl.BlockSpec((None, nh, 1, s), lambda hg, bb, qi: (bb, hg, 0, 0))],
        out_specs=pl.BlockSpec((None, t, wb), lambda hg, bb, qi: (bb, qi, hg)),
        out_shape=jax.ShapeDtypeStruct(v.shape, BF16),
        scratch_shapes=[pltpu.VMEM((nh, t, LANES), F32), pltpu.VMEM((nh, t, 2 * HEAD_DIM), F32),
                        pltpu.VMEM((nh, t, LANES), F32)],
        compiler_params=_cparams(("parallel", "parallel", "parallel")),
        name="fox_attn",
    )(qk, qk, v, cum, cum_row)


def _mix_body(od_ref, of_ref, g_ref, x_ref, wa_ref, wb_ref, wo_ref, nm_ref, wrh_ref, wrl_ref, br_ref,
              x1_ref, hp_ref, lg_ref):
    d = x_ref.shape[1]
    ua = jnp.dot(od_ref[...], wa_ref[...], preferred_element_type=F32)
    ub = jnp.dot(of_ref[...], wb_ref[...], preferred_element_type=F32)
    mixed = g_ref[:, :d].astype(F32) * ua + g_ref[:, d:].astype(F32) * ub
    x1 = x_ref[...] + jnp.dot(mixed.astype(BF16), wo_ref[...], preferred_element_type=F32)
    x1_ref[...] = x1
    hm = _rmsnorm_rows(x1, nm_ref[...])
    h_hi = hm.astype(BF16)
    h_hi32 = h_hi.astype(F32)
    h_lo = (hm - h_hi32).astype(BF16)
    lg_ref[...] = (lax.dot_general(h_hi, wrh_ref[...], _NT, preferred_element_type=F32)
                   + lax.dot_general(h_lo, wrh_ref[...], _NT, preferred_element_type=F32)
                   + lax.dot_general(h_hi, wrl_ref[...], _NT, preferred_element_type=F32) + br_ref[...])
    bits = lax.bitcast_convert_type(h_hi32, U32)
    packed = (bits[:, :d // 2] >> 16) | (bits[:, d // 2:] & jnp.uint32(0xFFFF0000))
    tm = packed.shape[0]
    pieces = packed.shape[1] // LANES
    for j in range(pieces):
        hp_ref[pl.ds(j, tm, stride=pieces), :] = packed[:, j * LANES:(j + 1) * LANES]


def _resident(shape):
    return pl.BlockSpec(shape, lambda i: (0,) * len(shape), pipeline_mode=pl.Buffered(1))


def _mix_call(od, of, gates, x2d, wa, wb, wo, nm_row, wr_hi, wr_lo, br_pad, tm=256):
    n, d = x2d.shape
    tm = min(tm, n)
    wdt = od.shape[1]
    return pl.pallas_call(
        _mix_body,
        grid=(n // tm,),
        in_specs=[pl.BlockSpec((tm, wdt), lambda i: (i, 0)),
                  pl.BlockSpec((tm, wdt), lambda i: (i, 0)),
                  pl.BlockSpec((tm, 2 * d), lambda i: (i, 0)),
                  pl.BlockSpec((tm, d), lambda i: (i, 0)),
                  _resident((wdt, d)), _resident((wdt, d)), _resident((d, d)),
                  _resident((1, d)), _resident((LANES, d)), _resident((LANES, d)), _resident((1, LANES))],
        out_specs=[pl.BlockSpec((tm, d), lambda i: (i, 0)),
                   pl.BlockSpec((tm * (d // 2 // LANES), LANES), lambda i: (i, 0)),
                   pl.BlockSpec((tm, LANES), lambda i: (i, 0))],
        out_shape=[jax.ShapeDtypeStruct((n, d), F32),
                   jax.ShapeDtypeStruct((n * (d // 2 // LANES), LANES), U32),
                   jax.ShapeDtypeStruct((n, LANES), F32)],
        compiler_params=_cparams(("parallel",)),
        name="mix",
    )(od, of, gates, x2d, wa, wb, wo, nm_row, wr_hi, wr_lo, br_pad)


def _router_body(lg_ref, rank_ref, jm_ref, wm_ref, cnt_ref, carry_ref):
    i = pl.program_id(0)

    @pl.when(i == 0)
    def _():
        carry_ref[...] = jnp.zeros_like(carry_ref)

    vals = lg_ref[...]
    tm = vals.shape[0]
    lane = lax.broadcasted_iota(I32, vals.shape, 1).astype(F32)
    jm = jnp.zeros(vals.shape, F32)
    tops = []
    for j in range(TOP_K):
        m = jnp.max(vals, axis=-1, keepdims=True)
        idx = jnp.min(jnp.where(vals == m, lane, float(LANES)), axis=-1, keepdims=True)
        sel = lane == idx
        jm = jnp.where(sel, float(j + 1), jm)
        vals = jnp.where(sel, -jnp.inf, vals)
        tops.append(m)
    es = [jnp.exp(m - tops[0]) for m in tops]
    den = es[0] + es[1] + es[2] + es[3]
    wm = jnp.zeros(vals.shape, F32)
    for j in range(TOP_K):
        wm = jnp.where(jm == float(j + 1), es[j] / den, wm)
    sel_any = jnp.where(jm > 0.0, 1.0, 0.0)
    r = lax.broadcasted_iota(I32, (tm, tm), 0)
    c = lax.broadcasted_iota(I32, (tm, tm), 1)
    tri = jnp.where(c < r, 1.0, 0.0).astype(BF16)
    carry = carry_ref[...]
    rank_ref[...] = jnp.dot(tri, sel_any.astype(BF16), preferred_element_type=F32) + carry
    jm_ref[...] = jm
    wm_ref[...] = wm
    carry = carry + jnp.sum(sel_any, axis=0, keepdims=True)
    carry_ref[...] = carry
    cnt_ref[...] = carry


def _router_call(logits, tm=256):
    n = logits.shape[0]
    tm = min(tm, n)
    tile = pl.BlockSpec((tm, LANES), lambda i: (i, 0))
    row = pl.BlockSpec((1, LANES), lambda i: (0, 0))
    return pl.pallas_call(
        _router_body,
        grid=(n // tm,),
        in_specs=[tile],
        out_specs=[tile, tile, tile, row],
        out_shape=[jax.ShapeDtypeStruct((n, LANES), F32)] * 3 + [jax.ShapeDtypeStruct((1, LANES), F32)],
        scratch_shapes=[pltpu.VMEM((1, LANES), F32)],
        compiler_params=_cparams(("arbitrary",)),
        name="router",
    )(logits)


def _dest_body(rank_ref, jm_ref, wm_ref, ps_ref, dest_ref, w4_ref):
    slot = rank_ref[...] + ps_ref[...]
    jm = jm_ref[...]
    wm = wm_ref[...]
    lane = lax.broadcasted_iota(I32, jm.shape, 1)
    dest = jnp.zeros(jm.shape, F32)
    w4 = jnp.zeros(jm.shape, F32)
    for j in range(TOP_K):
        sel = jm == float(j + 1)
        dj = jnp.sum(jnp.where(sel, slot, 0.0), axis=-1, keepdims=True)
        wj = jnp.sum(jnp.where(sel, wm, 0.0), axis=-1, keepdims=True)
        dest = jnp.where(lane == j, dj, dest)
        w4 = jnp.where(lane == j, wj, w4)
    dest_ref[...] = dest.astype(I32)
    w4_ref[...] = w4


def _dest_call(rank, jm, wm, ps_row, tm=512):
    n = rank.shape[0]
    tm = min(tm, n)
    tile = pl.BlockSpec((tm, LANES), lambda i: (i, 0))
    return pl.pallas_call(
        _dest_body,
        grid=(n // tm,),
        in_specs=[tile, tile, tile, pl.BlockSpec((1, LANES), lambda i: (0, 0))],
        out_specs=[tile, tile],
        out_shape=[jax.ShapeDtypeStruct((n, LANES), I32), jax.ShapeDtypeStruct((n, LANES), F32)],
        compiler_params=_cparams(("parallel",)),
        name="dest",
    )(rank, jm, wm, ps_row)


def _slot_table_body(dest_ref, cnt_ref, pad_ref, pst_ref, slot_ref, *, n_tok):
    def pads(e, carry):
        def one(s, c2):
            slot_ref[pst_ref[e] + s] = 0
            return c2
        lax.fori_loop(cnt_ref[e], pad_ref[e], one, 0)
        return carry
    lax.fori_loop(0, N_EXPERTS, pads, 0)

    def tail(s, carry):
        slot_ref[s] = 0
        return carry
    lax.fori_loop(pst_ref[N_EXPERTS - 1] + pad_ref[N_EXPERTS - 1], slot_ref.shape[0], tail, 0)

    unroll = 8
    assert n_tok % unroll == 0

    def toks(i, carry):
        for u in range(unroll):
            t = i * unroll + u
            for j in range(TOP_K):
                slot_ref[dest_ref[t * TOP_K + j]] = t
        return carry
    lax.fori_loop(0, n_tok // unroll, toks, 0)


def _slot_table_call(dest_flat, counts, padded, pstart, n_slots):
    smem = pl.BlockSpec(memory_space=pltpu.SMEM)
    return pl.pallas_call(
        functools.partial(_slot_table_body, n_tok=dest_flat.shape[0] // TOP_K),
        in_specs=[smem, smem, smem, smem],
        out_specs=smem,
        out_shape=jax.ShapeDtypeStruct((n_slots,), I32),
        name="slot_table",
    )(dest_flat, counts, padded, pstart)


ROW_UNROLL = 8


def _expert_body(gexp_ref, gsub0_ref, gns_ref, slot_ref,
                 hp_hbm, w1g_ref, w1l_ref, b1g_ref, b1l_ref, w2_ref, b2_ref,
                 outs_hbm,
                 xlo, xhi, stage, acc, sem_in, sem_out, *, n_chunks, n_groups):
    g = pl.program_id(0)
    c = pl.program_id(1)
    ns = gns_ref[g]
    s0 = gsub0_ref[g]
    sub = SLOT_BLOCK
    half = xlo.shape[1]
    last = n_chunks - 1

    pieces = half // LANES

    def row_copy(tok, r):
        return pltpu.make_async_copy(hp_hbm.at[pl.ds(pl.multiple_of(tok * pieces, pieces), pieces), :],
                                     stage.at[pl.ds(pl.multiple_of(r * pieces, pieces), pieces), :], sem_in)

    def issue_rows(gi):
        base = gsub0_ref[gi] * sub

        def trip(i, carry):
            for u in range(ROW_UNROLL):
                r = i * ROW_UNROLL + u
                row_copy(slot_ref[base + r], r).start()
            return carry
        lax.fori_loop(0, gns_ref[gi] * (sub // ROW_UNROLL), trip, 0)

    def out_copy(r0, rows):
        row0 = s0 * sub + r0
        return pltpu.make_async_copy(acc.at[pl.ds(r0, rows), :], outs_hbm.at[pl.ds(row0, rows), :], sem_out)

    def compute(r0, rows):
        xl = xlo[pl.ds(r0, rows), :]
        xh = xhi[pl.ds(r0, rows), :]
        hg = (jnp.dot(xl, w1g_ref[:half, :].astype(BF16), preferred_element_type=F32)
              + jnp.dot(xh, w1g_ref[half:, :].astype(BF16), preferred_element_type=F32) + b1g_ref[...])
        hl = (jnp.dot(xl, w1l_ref[:half, :].astype(BF16), preferred_element_type=F32)
              + jnp.dot(xh, w1l_ref[half:, :].astype(BF16), preferred_element_type=F32) + b1l_ref[...])
        gate = jnp.minimum(hg, SWIGLU_LIMIT)
        lin = jnp.clip(hl, -SWIGLU_LIMIT, SWIGLU_LIMIT)
        act = gate * jax.nn.sigmoid(SWIGLU_ALPHA * gate) * (lin + 1.0)
        acc[pl.ds(r0, rows), :] += jnp.dot(act.astype(BF16), w2_ref[...].astype(BF16),
                                           preferred_element_type=F32)

        @pl.when(c == last)
        def _():
            out_copy(r0, rows).start()

    @pl.when(ns > 0)
    def _():
        @pl.when(c == 0)
        def _load():
            @pl.when(g == 0)
            def _():
                issue_rows(0)

            def wt(i, carry):
                for u in range(ROW_UNROLL):
                    row_copy(0, i * ROW_UNROLL + u).wait()
                return carry
            lax.fori_loop(0, ns * (sub // ROW_UNROLL), wt, 0)

            def unpack(s, carry):
                r0 = pl.multiple_of(s * sub, sub)
                for j in range(pieces):
                    w = stage[pl.ds(r0 * pieces + j, sub, stride=pieces), :]
                    cols = slice(j * LANES, (j + 1) * LANES)
                    xlo[pl.ds(r0, sub), cols] = lax.bitcast_convert_type(w << 16, F32).astype(BF16)
                    xhi[pl.ds(r0, sub), cols] = lax.bitcast_convert_type(
                        w & jnp.uint32(0xFFFF0000), F32).astype(BF16)
                acc[pl.ds(r0, sub), :] = jnp.broadcast_to(b2_ref[...], (sub, acc.shape[1]))
                return carry
            lax.fori_loop(0, ns, unpack, 0)

        @pl.when(c == min(1, last))
        def _prefetch():
            nxt = jnp.minimum(g + 1, n_groups - 1)

            @pl.when(jnp.logical_and(g + 1 < n_groups, gns_ref[nxt] > 0))
            def _():
                issue_rows(nxt)

        def pair(i, carry):
            compute(pl.multiple_of(i * (2 * sub), 2 * sub), 2 * sub)
            return carry
        lax.fori_loop(0, ns // 2, pair, 0)

        @pl.when(ns % 2 == 1)
        def _():
            compute(pl.multiple_of((ns - 1) * sub, sub), sub)

        @pl.when(c == last)
        def _drain():
            def wt(i, carry):
                out_copy(pl.multiple_of(i * (2 * sub), 2 * sub), 2 * sub).wait()
                return carry
            lax.fori_loop(0, ns // 2, wt, 0)

            @pl.when(ns % 2 == 1)
            def _():
                out_copy(pl.multiple_of((ns - 1) * sub, sub), sub).wait()

            nxt = jnp.minimum(g + 1, n_groups - 1)

            @pl.when(jnp.logical_or(g + 1 >= n_groups, gns_ref[nxt] == 0))
            def _tail():
                acc[pl.ds(0, sub), :] = jnp.zeros((sub, acc.shape[1]), F32)
                first = s0 + ns
                n_tail = outs_hbm.shape[0] // sub - first

                def tail_copy(i):
                    row0 = pl.multiple_of((first + i) * sub, sub)
                    return pltpu.make_async_copy(acc.at[pl.ds(0, sub), :], outs_hbm.at[pl.ds(row0, sub), :],
                                                 sem_out)

                def st(i, carry):
                    tail_copy(i).start()
                    return carry

                def wt(i, carry):
                    tail_copy(i).wait()
                    return carry
                lax.fori_loop(0, n_tail, st, 0)
                lax.fori_loop(0, n_tail, wt, 0)


def _expert_call(gexp, gsub0, gns, slot_tok, hp, w1, b1, w2, b2):
    n_slots = slot_tok.shape[0]
    half = w1.shape[1] // 2
    n_exp, d, ff2 = w1.shape
    ff = ff2 // 2
    tc = FF_CHUNK
    n_chunks = ff // tc
    n_groups = gexp.shape[0]
    rows = GROUP_SUBS * SLOT_BLOCK

    def chunk(c, gn, g):
        return jnp.where(gn[g] > 0, c, n_chunks - 1)

    grid_spec = pltpu.PrefetchScalarGridSpec(
        num_scalar_prefetch=4,
        grid=(n_groups, n_chunks),
        in_specs=[
            pl.BlockSpec(memory_space=pl.ANY),
            pl.BlockSpec((None, d, tc), lambda g, c, ge, gs, gn, st: (ge[g], 0, chunk(c, gn, g))),
            pl.BlockSpec((None, d, tc), lambda g, c, ge, gs, gn, st: (ge[g], 0, n_chunks + chunk(c, gn, g))),
            pl.BlockSpec((None, 1, tc), lambda g, c, ge, gs, gn, st: (ge[g], 0, chunk(c, gn, g))),
            pl.BlockSpec((None, 1, tc), lambda g, c, ge, gs, gn, st: (ge[g], 0, n_chunks + chunk(c, gn, g))),
            pl.BlockSpec((None, tc, d), lambda g, c, ge, gs, gn, st: (ge[g], chunk(c, gn, g), 0)),
            pl.BlockSpec((None, 1, d), lambda g, c, ge, gs, gn, st: (ge[g], 0, 0)),
        ],
        out_specs=pl.BlockSpec(memory_space=pl.ANY),
        scratch_shapes=[
            pltpu.VMEM((rows, half), BF16), pltpu.VMEM((rows, half), BF16),
            pltpu.VMEM((rows * (half // LANES), LANES), U32),
            pltpu.VMEM((rows, d), F32),
            pltpu.SemaphoreType.DMA(()), pltpu.SemaphoreType.DMA(()),
        ],
    )
    return pl.pallas_call(
        functools.partial(_expert_body, n_chunks=n_chunks, n_groups=n_groups),
        grid_spec=grid_spec,
        out_shape=jax.ShapeDtypeStruct((n_slots, d), F32),
        compiler_params=_cparams(("arbitrary", "arbitrary")),
        name="expert",
    )(gexp, gsub0, gns, slot_tok, hp, w1, w1, b1.reshape(n_exp, 1, ff2), b1.reshape(n_exp, 1, ff2), w2,
      b2.reshape(n_exp, 1, d))


def _final_body(dest_ref, x1_ref, w4_ref, p_ref, wple_ref, wpg_ref, bpg_ref, npl_ref, outs_hbm,
                o_ref, gbuf, sems, *, tm, n_steps):
    i = pl.program_id(0)
    cur = i % 2

    def row_copy(slot, buf, j, q, u):
        return pltpu.make_async_copy(outs_hbm.at[pl.ds(slot, 1), :], gbuf.at[buf, j, q, pl.ds(u, 1), :],
                                     sems.at[buf])

    def issue(step, buf):
        def rows(q, carry):
            for u in range(SUBLANES):
                t = step * tm + q * SUBLANES + u
                for j in range(TOP_K):
                    row_copy(dest_ref[t * TOP_K + j], buf, j, q, u).start(priority=j % 2)
            return carry
        lax.fori_loop(0, tm // SUBLANES, rows, 0)

    @pl.when(i == 0)
    def _():
        issue(0, 0)

    @pl.when(i + 1 < n_steps)
    def _():
        issue(i + 1, 1 - cur)

    pe = jnp.dot(p_ref[...].astype(BF16), wple_ref[...], preferred_element_type=F32)

    def wt(q, carry):
        for u in range(SUBLANES):
            for j in range(TOP_K):
                row_copy(0, cur, j, q, u).wait()
        return carry
    lax.fori_loop(0, tm // SUBLANES, wt, 0)

    w4 = w4_ref[...]
    d = o_ref.shape[1]
    y = gbuf[cur, 0].reshape(tm, d) * w4[:, 0:1]
    for j in range(1, TOP_K):
        y = y + gbuf[cur, j].reshape(tm, d) * w4[:, j:j + 1]
    x2 = x1_ref[...] + y
    hn = _rmsnorm_rows(x2, npl_ref[...])
    gate = jax.nn.sigmoid(jnp.dot(hn.astype(BF16), wpg_ref[...], preferred_element_type=F32) + bpg_ref[...])
    o_ref[...] = x2 + gate * pe


def _final_call(dest_flat, x1, w4, p2d, wple, wpg, bpg_row, npl_row, outs, tm=256):
    n, d = x1.shape
    tm = min(tm, n)
    pd = p2d.shape[1]
    res = lambda shape: pl.BlockSpec(shape, lambda i, dst: (0,) * len(shape), pipeline_mode=pl.Buffered(1))
    grid_spec = pltpu.PrefetchScalarGridSpec(
        num_scalar_prefetch=1,
        grid=(n // tm,),
        in_specs=[pl.BlockSpec((tm, d), lambda i, dst: (i, 0)),
                  pl.BlockSpec((tm, LANES), lambda i, dst: (i, 0)),
                  pl.BlockSpec((tm, pd), lambda i, dst: (i, 0)),
                  res((pd, d)), res((d, d)), res((1, d)), res((1, d)),
                  pl.BlockSpec(memory_space=pl.ANY)],
        out_specs=pl.BlockSpec((tm, d), lambda i, dst: (i, 0)),
        scratch_shapes=[pltpu.VMEM((2, TOP_K, tm // SUBLANES, SUBLANES, d), F32),
                        pltpu.SemaphoreType.DMA((2,))],
    )
    return pl.pallas_call(
        functools.partial(_final_body, tm=tm, n_steps=n // tm),
        grid_spec=grid_spec,
        out_shape=jax.ShapeDtypeStruct((n, d), F32),
        compiler_params=_cparams(("arbitrary",)),
        name="final",
    )(dest_flat, x1, w4, p2d, wple, wpg, bpg_row, npl_row, outs)


def _t5_bucket(rel):
    n = -rel
    nb = REL_BUCKETS // 2
    ret = jnp.where(n < 0, nb, 0)
    n = jnp.abs(n)
    max_exact = nb // 2
    large = max_exact + (jnp.log(jnp.maximum(n, 1).astype(jnp.float32) / max_exact)
                         / math.log(REL_MAX_DIST / max_exact) * (nb - max_exact)).astype(jnp.int32)
    large = jnp.minimum(large, nb - 1)
    return ret + jnp.where(n < max_exact, n, large)


def _bias_body(rv_ref, o_ref, *, t):
    x = jnp.broadcast_to(rv_ref[...], (t, rv_ref.shape[1]))
    o_ref[...] = pltpu.roll(x, 1, 1, stride=1, stride_axis=0)[:, t:]


def _diff_bias_table(rel_bias, seq, t):
    rel = jnp.arange(seq + t) - (seq - 1)
    rv = (jnp.transpose(rel_bias[_t5_bucket(rel)], (1, 0)).astype(F32) * LOG2E)[:, None, :]
    return pl.pallas_call(
        functools.partial(_bias_body, t=t),
        grid=(N_HEADS,),
        in_specs=[pl.BlockSpec((None, 1, seq + t), lambda h: (h, 0, 0))],
        out_specs=pl.BlockSpec((None, t, seq), lambda h: (h, 0, 0)),
        out_shape=jax.ShapeDtypeStruct((N_HEADS, t, seq), F32),
        compiler_params=_cparams(("parallel",)),
        name="bias_table",
    )(rv)


def _pad_cols(a, width):
    return jnp.pad(a, ((0, 0), (0, width - a.shape[1])))


def _group_table(counts):
    n_sub = (counts + SLOT_BLOCK - 1) // SLOT_BLOCK
    sub_start = jnp.cumsum(n_sub) - n_sub
    n_grp = (n_sub + GROUP_SUBS - 1) // GROUP_SUBS
    grp_end = jnp.cumsum(n_grp)
    total = grp_end[-1]
    return n_sub, sub_start, n_grp, grp_end, total


def _layer(x, p_l, w_in, b_gate, b_forget, dq_norm, dk_norm, fq_norm, fk_norm, lq1, lk1, lq2, lk2,
           lambda_init, subln, w_up_a, w_up_b, w_out, rel_bias, norm_mix, norm_moe, w_router,
           b_router, w1, b1, w2, b2, norm_ple, w_ple_gate, b_ple_gate, w_ple):
    b, s, d = x.shape
    n = b * s
    width = N_HEADS * HEAD_DIM
    t = min(ATTN_T, s)
    x2d = x.reshape(n, d)
    row = lambda v: v.reshape(1, -1).astype(F32)

    f_row = 6 * width
    wt_in = jnp.swapaxes(w_in, 0, 1)
    h, f_logit = _norm_call(x2d, row(norm_mix), wt_in, f_row)
    cum = _cum_call(f_logit, _pad_cols(row(b_forget), LANES), b, s)
    cum_bhs = jnp.transpose(cum[:, :N_HEADS].reshape(b, s, N_HEADS), (0, 2, 1))
    cum_row = cum_bhs[:, :, None, :]

    diff_scale = DIFF_QK_DIM ** -0.5 * LOG2E
    fox_scale = HEAD_DIM ** -0.5 * LOG2E
    gain_d = jnp.concatenate([jnp.tile(dq_norm * diff_scale, 2 * N_HEADS), jnp.tile(dk_norm, 2 * N_HEADS)])
    gain_f = jnp.concatenate([jnp.tile(fq_norm * fox_scale, N_HEADS), jnp.tile(fk_norm, N_HEADS)])
    zeros_w = jnp.zeros((1, width), F32)
    dqk = _proj_call(h, wt_in, 0, 2 * width, row(gain_d), "norm64", "proj_dqk")
    dv = _proj_call(h, wt_in, 2 * width, width, zeros_w, "plain", "proj_dv")
    fqk = _proj_call(h, wt_in, 3 * width, 2 * width, row(gain_f), "norm128", "proj_fqk")
    fv = _proj_call(h, wt_in, 5 * width, width, zeros_w, "plain", "proj_fv")
    gates = _proj_call(h, wt_in, f_row + N_HEADS, 2 * d, row(b_gate), "gate", "proj_gate")

    lam_rows = [row(v) for v in (lq1, lk1, lq2, lk2)]
    bias = _diff_bias_table(rel_bias, s, t)
    od = _diff_attn_call(dqk.reshape(b, s, 2 * width), dv.reshape(b, s, width), bias, lam_rows,
                         row(subln), lambda_init, t)
    of = _fox_attn_call(fqk.reshape(b, s, 2 * width), fv.reshape(b, s, width), cum.reshape(b, s, LANES),
                        cum_row, t)

    br_pad = jnp.full((1, LANES), NEG, F32).at[0, :N_EXPERTS].set(b_router.astype(F32))
    wr_pad = jnp.pad(jnp.swapaxes(w_router.astype(F32), 0, 1), ((0, LANES - N_EXPERTS), (0, 0)))
    wr_hi = wr_pad.astype(BF16)
    wr_lo = (wr_pad - wr_hi.astype(F32)).astype(BF16)
    x1, hp, logits = _mix_call(od.reshape(n, width), of.reshape(n, width), gates, x2d,
                               w_up_a.astype(BF16), w_up_b.astype(BF16), w_out.astype(BF16),
                               row(norm_moe), wr_hi, wr_lo, br_pad)

    rank, jm, wm, cnt = _router_call(logits)
    counts = cnt[0, :N_EXPERTS].astype(I32)
    padded = ((counts + SLOT_BLOCK - 1) // SLOT_BLOCK) * SLOT_BLOCK
    pstart = jnp.cumsum(padded) - padded
    ps_row = _pad_cols(pstart.astype(F32).reshape(1, -1), LANES)
    dest128, w4 = _dest_call(rank, jm, wm, ps_row)
    dest_flat = dest128[:, :TOP_K].reshape(-1)

    n_slots = n * TOP_K + N_EXPERTS * SLOT_BLOCK
    slot_tok = _slot_table_call(dest_flat, counts, padded, pstart, n_slots)

    n_sub, sub_start, n_grp, grp_end, total = _group_table(counts)
    max_groups = N_EXPERTS + (n_slots // SLOT_BLOCK) // GROUP_SUBS
    gidx = jnp.arange(max_groups, dtype=I32)
    gvalid = gidx < total
    gsafe = jnp.minimum(gidx, total - 1)
    gexp = jnp.searchsorted(grp_end, gsafe, side="right").astype(I32)
    kth = gsafe - (grp_end - n_grp)[gexp]
    gsub0 = (sub_start[gexp] + kth * GROUP_SUBS).astype(I32)
    gns = jnp.where(gvalid, jnp.minimum(GROUP_SUBS, n_sub[gexp] - kth * GROUP_SUBS), 0).astype(I32)
    outs = _expert_call(gexp, gsub0, gns, slot_tok, hp, w1, b1, w2, b2)

    out = _final_call(dest_flat, x1, w4, p_l.reshape(n, -1), w_ple.astype(BF16),
                      w_ple_gate.astype(BF16), row(b_ple_gate), row(norm_ple), outs)
    return out.reshape(b, s, d)


def kernel(x, p, w_in, b_gate, b_forget, dq_norm, dk_norm, fq_norm, fk_norm, lambda_q1, lambda_k1,
           lambda_q2, lambda_k2, subln, w_up_a, w_up_b, w_out, rel_bias, norm_mix, norm_moe,
           w_router, b_router, w1, b1, w2, b2, norm_ple, w_ple_gate, b_ple_gate, w_ple):
    for i in range(w_in.shape[0]):
        lambda_init = 0.8 - 0.6 * math.exp(-0.3 * i)
        x = _layer(x, p[i], w_in[i], b_gate[i], b_forget[i], dq_norm[i], dk_norm[i], fq_norm[i],
                   fk_norm[i], lambda_q1[i], lambda_k1[i], lambda_q2[i], lambda_k2[i], lambda_init,
                   subln[i], w_up_a[i], w_up_b[i], w_out[i], rel_bias, norm_mix[i], norm_moe[i],
                   w_router[i], b_router[i], w1[i], b1[i], w2[i], b2[i], norm_ple[i],
                   w_ple_gate[i], b_ple_gate[i], w_ple[i])
    return x
```

```python
import functools
import math

import jax
import jax.numpy as jnp
from jax import lax
from jax.experimental import pallas as pl
from jax.experimental.pallas import tpu as pltpu

F32 = jnp.float32
BF16 = jnp.bfloat16
U32 = jnp.uint32
I32 = jnp.int32
HIGHEST = lax.Precision.HIGHEST

N_HEADS = 8
HEAD_DIM = 128
DIFF_QK_DIM = 64
CHUNK = 64
REL_BUCKETS = 32
REL_MAX_DIST = 128
N_EXPERTS = 32
TOP_K = 4
SWIGLU_LIMIT = 7.0
SWIGLU_ALPHA = 1.702
RMS_EPS = 1e-6
NEG = -1e30
LOG2E = math.log2(math.e)

LANES = 128
SUBLANES = 8
SLOT_BLOCK = 256
GROUP_SUBS = 6
FF_CHUNK = 512
ATTN_T = 256
VMEM_LIMIT = 56 * 1024 * 1024


_NT = (((1,), (1,)), ((), ()))


def _cparams(sem):
    return pltpu.CompilerParams(dimension_semantics=sem, vmem_limit_bytes=VMEM_LIMIT)


def _rmsnorm_rows(x, g):
    ms = jnp.mean(x * x, axis=-1, keepdims=True)
    return x * lax.rsqrt(ms + RMS_EPS) * g


def _norm_body(x_ref, g_ref, wf_ref, h_ref, fl_ref):
    y = _rmsnorm_rows(x_ref[...], g_ref[...])
    y_hi = y.astype(BF16)
    h_ref[...] = y_hi
    rowi = lax.broadcasted_iota(I32, (LANES, 1), 0)
    wf = jnp.where(rowi < N_HEADS, wf_ref[...], 0.0)
    y_lo = (y - y_hi.astype(F32)).astype(BF16)
    wf_hi = wf.astype(BF16)
    wf_lo = (wf - wf_hi.astype(F32)).astype(BF16)
    fl_ref[...] = (lax.dot_general(y_hi, wf_hi, _NT, preferred_element_type=F32)
                   + lax.dot_general(y_lo, wf_hi, _NT, preferred_element_type=F32)
                   + lax.dot_general(y_hi, wf_lo, _NT, preferred_element_type=F32))


def _norm_call(x2d, g_row, wt_in, f_row, tm=512):
    n, d = x2d.shape
    tm = min(tm, n)
    assert f_row % LANES == 0
    return pl.pallas_call(
        _norm_body,
        grid=(n // tm,),
        in_specs=[pl.BlockSpec((tm, d), lambda i: (i, 0)),
                  pl.BlockSpec((1, d), lambda i: (0, 0)),
                  pl.BlockSpec((LANES, d), lambda i: (f_row // LANES, 0))],
        out_specs=[pl.BlockSpec((tm, d), lambda i: (i, 0)),
                   pl.BlockSpec((tm, LANES), lambda i: (i, 0))],
        out_shape=[jax.ShapeDtypeStruct((n, d), BF16),
                   jax.ShapeDtypeStruct((n, LANES), F32)],
        compiler_params=_cparams(("parallel",)),
        name="norm",
    )(x2d, g_row, wt_in)


def _cum_body(fl_ref, bf_ref, cum_ref):
    s = fl_ref.shape[0]
    r = lax.broadcasted_iota(I32, (LANES, LANES), 0)
    c = lax.broadcasted_iota(I32, (LANES, LANES), 1)
    tri = jnp.where(r >= c, 1.0, 0.0).astype(F32)
    carry = jnp.zeros((1, LANES), F32)
    for blk in range(s // LANES):
        rows = slice(blk * LANES, (blk + 1) * LANES)
        lf = jax.nn.log_sigmoid(fl_ref[rows, :] + bf_ref[...])
        cb = jnp.dot(tri, lf, precision=HIGHEST, preferred_element_type=F32) + carry
        cum_ref[rows, :] = cb * LOG2E
        carry = cb[LANES - 1:LANES, :]


def _cum_call(fl, bf_row, batch, seq):
    return pl.pallas_call(
        _cum_body,
        grid=(batch,),
        in_specs=[pl.BlockSpec((seq, LANES), lambda b: (b, 0)),
                  pl.BlockSpec((1, LANES), lambda b: (0, 0))],
        out_specs=pl.BlockSpec((seq, LANES), lambda b: (b, 0)),
        out_shape=jax.ShapeDtypeStruct(fl.shape, F32),
        compiler_params=_cparams(("parallel",)),
        name="cum",
    )(fl, bf_row)


def _proj_epilogue(acc, aux_ref, o_ref, kind):
    tn = acc.shape[1]
    if kind == "plain":
        o_ref[...] = acc.astype(o_ref.dtype)
    elif kind == "gate":
        o_ref[...] = jax.nn.sigmoid(acc + aux_ref[...]).astype(o_ref.dtype)
    else:
        lo = lax.broadcasted_iota(I32, (1, LANES), 1) < DIFF_QK_DIM
        for s in range(tn // LANES):
            cols = slice(s * LANES, (s + 1) * LANES)
            blk = acc[:, cols]
            sq = blk * blk
            if kind == "norm128":
                ms = jnp.mean(sq, axis=-1, keepdims=True)
            else:
                s_lo = jnp.sum(jnp.where(lo, sq, 0.0), axis=-1, keepdims=True)
                s_hi = jnp.sum(jnp.where(lo, 0.0, sq), axis=-1, keepdims=True)
                ms = jnp.where(lo, s_lo, s_hi) * (1.0 / DIFF_QK_DIM)
            o_ref[:, cols] = (blk * lax.rsqrt(ms + RMS_EPS) * aux_ref[:, cols]).astype(o_ref.dtype)


def _proj_body(h_ref, wt_ref, aux_ref, o_ref, *, kind):
    acc = lax.dot_general(h_ref[...], wt_ref[...].astype(BF16), _NT, preferred_element_type=F32)
    _proj_epilogue(acc, aux_ref, o_ref, kind)


def _proj_shift_body(h_ref, wa_ref, wb_ref, aux_ref, o_ref, *, kind, shift):
    wt = jnp.concatenate([wa_ref[shift:, :], wb_ref[...]], axis=0)
    acc = lax.dot_general(h_ref[...], wt.astype(BF16), _NT, preferred_element_type=F32)
    _proj_epilogue(acc, aux_ref, o_ref, kind)


def _proj_call(h, wt, row0, nrows, aux_row, kind, name, tm=1024, tn=1024):
    m, k = h.shape
    tm = min(tm, m)
    assert nrows % tn == 0 and m % tm == 0
    shift = row0 % tn
    base = row0 - shift
    h_spec = pl.BlockSpec((tm, k), lambda i, j: (i, 0))
    w_spec = pl.BlockSpec((tn, k), lambda i, j: (base // tn + j, 0))
    aux_spec = pl.BlockSpec((1, tn), lambda i, j: (0, j))
    if shift == 0:
        body, w_specs, ws = functools.partial(_proj_body, kind=kind), [w_spec], [wt]
    else:
        assert shift % SUBLANES == 0 and tn % shift == 0
        body = functools.partial(_proj_shift_body, kind=kind, shift=shift)
        w_specs = [w_spec, pl.BlockSpec((shift, k), lambda i, j: ((base + (j + 1) * tn) // shift, 0))]
        ws = [wt, wt]
    return pl.pallas_call(
        body,
        grid=(m // tm, nrows // tn),
        in_specs=[h_spec] + w_specs + [aux_spec],
        out_specs=pl.BlockSpec((tm, tn), lambda i, j: (i, j)),
        out_shape=jax.ShapeDtypeStruct((m, nrows), BF16),
        compiler_params=_cparams(("parallel", "parallel")),
        name=name,
    )(h, *ws, aux_row)


DIFF_HEADS_PER_STEP = 4
FOX_HEADS_PER_STEP = 8


def _softmax_step(s, m_ref, acc_ref, v1):
    m_old = m_ref[...]
    mn = jnp.maximum(m_old, jnp.max(s, axis=-1, keepdims=True))
    alpha = jnp.exp2(m_old - mn)
    p = jnp.exp2(s - jnp.tile(mn, (1, s.shape[1] // LANES))).astype(BF16)
    acc_ref[...] = (jnp.tile(alpha, (1, acc_ref.shape[1] // LANES)) * acc_ref[...]
                    + jnp.dot(p, v1, preferred_element_type=F32))
    m_ref[...] = mn


def _causal_sweep(step, qi, t):
    def pair(i, carry):
        step(2 * i, 2 * t, False)
        return carry
    lax.fori_loop(0, qi // 2, pair, 0)

    @pl.when(qi % 2 == 1)
    def _():
        step(qi - 1, t, False)
    step(qi, t, True)


def _with_ones(v):
    return jnp.concatenate([v, jnp.ones_like(v)], axis=1)


def _normalised(acc):
    return acc[:, :HEAD_DIM] / acc[:, HEAD_DIM:]


def _diff_attn_body(lq1_ref, lk1_ref, lq2_ref, lk2_ref, g_ref, q_ref, k_ref, v_ref, bias_ref,
                    o_ref, m_sc, acc_sc, *, t, nh, lambda_init):
    qi = pl.program_id(2)
    s_len = k_ref.shape[0]
    m_sc[...] = jnp.full(m_sc.shape, NEG, F32)
    acc_sc[...] = jnp.zeros(acc_sc.shape, F32)
    lo = lax.broadcasted_iota(I32, (t, HEAD_DIM), 1) < DIFF_QK_DIM

    def step(kb, w, masked):
        k0 = pl.multiple_of(kb * t, t)
        off = pl.multiple_of((kb - qi) * t + (s_len - t), LANES)
        if masked:
            r = lax.broadcasted_iota(I32, (t, w), 0)
            c = lax.broadcasted_iota(I32, (t, w), 1)
            ok = (c // CHUNK) <= (r // CHUNK)
        for h in range(nh):
            cols = slice(h * HEAD_DIM, (h + 1) * HEAD_DIM)
            q = q_ref[:, cols]
            zero = jnp.zeros_like(q)
            k = k_ref[pl.ds(k0, w), cols]
            v1 = _with_ones(v_ref[pl.ds(k0, w), cols])
            b = bias_ref[h, :, pl.ds(off, w)]
            for mp, qm in enumerate((jnp.where(lo, q, zero), jnp.where(lo, zero, q))):
                s = lax.dot_general(qm, k, _NT, preferred_element_type=F32) + b
                if masked:
                    s = jnp.where(ok, s, NEG)
                _softmax_step(s, m_sc.at[2 * h + mp], acc_sc.at[2 * h + mp], v1)

    _causal_sweep(step, qi, t)

    lam = (jnp.exp(jnp.sum(lq1_ref[...] * lk1_ref[...], axis=-1, keepdims=True))
           - jnp.exp(jnp.sum(lq2_ref[...] * lk2_ref[...], axis=-1, keepdims=True)) + lambda_init)
    for h in range(nh):
        o = _normalised(acc_sc[2 * h]) - lam * _normalised(acc_sc[2 * h + 1])
        o = _rmsnorm_rows(o, g_ref[...]) * (1.0 - lambda_init)
        o_ref[:, h * HEAD_DIM:(h + 1) * HEAD_DIM] = o.astype(o_ref.dtype)


def _diff_attn_call(qk, v, bias, lam_rows, subln_row, lambda_init, t):
    b, s, _ = v.shape
    nh = DIFF_HEADS_PER_STEP
    wb = nh * HEAD_DIM
    n_hg = N_HEADS // nh
    vec = lambda w: pl.BlockSpec((1, w), lambda hg, bb, qi: (0, 0))
    return pl.pallas_call(
        functools.partial(_diff_attn_body, t=t, nh=nh, lambda_init=lambda_init),
        grid=(n_hg, b, s // t),
        in_specs=[vec(DIFF_QK_DIM)] * 4 + [
            vec(HEAD_DIM),
            pl.BlockSpec((None, t, wb), lambda hg, bb, qi: (bb, qi, hg)),
            pl.BlockSpec((None, s, wb), lambda hg, bb, qi: (bb, 0, n_hg + hg)),
            pl.BlockSpec((None, s, wb), lambda hg, bb, qi: (bb, 0, hg)),
            pl.BlockSpec((nh, t, s), lambda hg, bb, qi: (hg, 0, 0))],
        out_specs=pl.BlockSpec((None, t, wb), lambda hg, bb, qi: (bb, qi, hg)),
        out_shape=jax.ShapeDtypeStruct(v.shape, BF16),
        scratch_shapes=[pltpu.VMEM((2 * nh, t, LANES), F32), pltpu.VMEM((2 * nh, t, 2 * HEAD_DIM), F32)],
        compiler_params=_cparams(("parallel", "parallel", "parallel")),
        name="diff_attn",
    )(*lam_rows, subln_row, qk, qk, v, bias)


def _fox_attn_body(q_ref, k_ref, v_ref, cq_ref, ck_ref, o_ref, m_sc, acc_sc, cq_sc, *, t, nh):
    qi = pl.program_id(2)
    m_sc[...] = jnp.full(m_sc.shape, NEG, F32)
    acc_sc[...] = jnp.zeros(acc_sc.shape, F32)
    hg = pl.program_id(0)
    lane = lax.broadcasted_iota(I32, (t, LANES), 1)
    cq_all = cq_ref[...]
    for h in range(nh):
        col = jnp.sum(jnp.where(lane == hg * nh + h, cq_all, 0.0), axis=-1, keepdims=True)
        cq_sc[h] = jnp.broadcast_to(col, (t, LANES))

    def step(kb, w, masked):
        k0 = pl.multiple_of(kb * t, t)
        if masked:
            r = lax.broadcasted_iota(I32, (t, w), 0)
            c = lax.broadcasted_iota(I32, (t, w), 1)
            ok = c <= r
        for h in range(nh):
            cols = slice(h * HEAD_DIM, (h + 1) * HEAD_DIM)
            k = k_ref[pl.ds(k0, w), cols]
            v1 = _with_ones(v_ref[pl.ds(k0, w), cols])
            s = lax.dot_general(q_ref[:, cols], k, _NT, preferred_element_type=F32)
            s = s + jnp.tile(cq_sc[h], (1, w // LANES)) - ck_ref[h, :, pl.ds(k0, w)]
            if masked:
                s = jnp.where(ok, s, NEG)
            _softmax_step(s, m_sc.at[h], acc_sc.at[h], v1)

    _causal_sweep(step, qi, t)
    for h in range(nh):
        o_ref[:, h * HEAD_DIM:(h + 1) * HEAD_DIM] = _normalised(acc_sc[h]).astype(o_ref.dtype)


def _fox_attn_call(qk, v, cum, cum_row, t):
    b, s, _ = v.shape
    nh = FOX_HEADS_PER_STEP
    wb = nh * HEAD_DIM
    n_hg = N_HEADS // nh
    return pl.pallas_call(
        functools.partial(_fox_attn_body, t=t, nh=nh),
        grid=(n_hg, b, s // t),
        in_specs=[pl.BlockSpec((None, t, wb), lambda hg, bb, qi: (bb, qi, hg)),
                  pl.BlockSpec((None, s, wb), lambda hg, bb, qi: (bb, 0, n_hg + hg)),
                  pl.BlockSpec((None, s, wb), lambda hg, bb, qi: (bb, 0, hg)),
                  pl.BlockSpec((None, t, LANES), lambda hg, bb, qi: (bb, qi, 0)),
                  pl.BlockSpec((None, nh, 1, s), lambda hg, bb, qi: (bb, hg, 0, 0))],
        out_specs=pl.BlockSpec((None, t, wb), lambda hg, bb, qi: (bb, qi, hg)),
        out_shape=jax.ShapeDtypeStruct(v.shape, BF16),
        scratch_shapes=[pltpu.VMEM((nh, t, LANES), F32), pltpu.VMEM((nh, t, 2 * HEAD_DIM), F32),
                        pltpu.VMEM((nh, t, LANES), F32)],
        compiler_params=_cparams(("parallel", "parallel", "parallel")),
        name="fox_attn",
    )(qk, qk, v, cum, cum_row)


def _mix_body(od_ref, of_ref, g_ref, x_ref, wa_ref, wb_ref, wo_ref, nm_ref, wrh_ref, wrl_ref, br_ref,
              x1_ref, hp_ref, lg_ref):
    d = x_ref.shape[1]
    ua = jnp.dot(od_ref[...], wa_ref[...], preferred_element_type=F32)
    ub = jnp.dot(of_ref[...], wb_ref[...], preferred_element_type=F32)
    mixed = g_ref[:, :d].astype(F32) * ua + g_ref[:, d:].astype(F32) * ub
    x1 = x_ref[...] + jnp.dot(mixed.astype(BF16), wo_ref[...], preferred_element_type=F32)
    x1_ref[...] = x1
    hm = _rmsnorm_rows(x1, nm_ref[...])
    h_hi = hm.astype(BF16)
    h_hi32 = h_hi.astype(F32)
    h_lo = (hm - h_hi32).astype(BF16)
    lg_ref[...] = (lax.dot_general(h_hi, wrh_ref[...], _NT, preferred_element_type=F32)
                   + lax.dot_general(h_lo, wrh_ref[...], _NT, preferred_element_type=F32)
                   + lax.dot_general(h_hi, wrl_ref[...], _NT, preferred_element_type=F32) + br_ref[...])
    bits = lax.bitcast_convert_type(h_hi32, U32)
    packed = (bits[:, :d // 2] >> 16) | (bits[:, d // 2:] & jnp.uint32(0xFFFF0000))
    tm = packed.shape[0]
    pieces = packed.shape[1] // LANES
    for j in range(pieces):
        hp_ref[pl.ds(j, tm, stride=pieces), :] = packed[:, j * LANES:(j + 1) * LANES]


def _resident(shape):
    return pl.BlockSpec(shape, lambda i: (0,) * len(shape), pipeline_mode=pl.Buffered(1))


def _mix_call(od, of, gates, x2d, wa, wb, wo, nm_row, wr_hi, wr_lo, br_pad, tm=256):
    n, d = x2d.shape
    tm = min(tm, n)
    wdt = od.shape[1]
    return pl.pallas_call(
        _mix_body,
        grid=(n // tm,),
        in_specs=[pl.BlockSpec((tm, wdt), lambda i: (i, 0)),
                  pl.BlockSpec((tm, wdt), lambda i: (i, 0)),
                  pl.BlockSpec((tm, 2 * d), lambda i: (i, 0)),
                  pl.BlockSpec((tm, d), lambda i: (i, 0)),
                  _resident((wdt, d)), _resident((wdt, d)), _resident((d, d)),
                  _resident((1, d)), _resident((LANES, d)), _resident((LANES, d)), _resident((1, LANES))],
        out_specs=[pl.BlockSpec((tm, d), lambda i: (i, 0)),
                   pl.BlockSpec((tm * (d // 2 // LANES), LANES), lambda i: (i, 0)),
                   pl.BlockSpec((tm, LANES), lambda i: (i, 0))],
        out_shape=[jax.ShapeDtypeStruct((n, d), F32),
                   jax.ShapeDtypeStruct((n * (d // 2 // LANES), LANES), U32),
                   jax.ShapeDtypeStruct((n, LANES), F32)],
        compiler_params=_cparams(("parallel",)),
        name="mix",
    )(od, of, gates, x2d, wa, wb, wo, nm_row, wr_hi, wr_lo, br_pad)


def _router_body(lg_ref, rank_ref, jm_ref, wm_ref, cnt_ref, carry_ref):
    i = pl.program_id(0)

    @pl.when(i == 0)
    def _():
        carry_ref[...] = jnp.zeros_like(carry_ref)

    vals = lg_ref[...]
    tm = vals.shape[0]
    lane = lax.broadcasted_iota(I32, vals.shape, 1).astype(F32)
    jm = jnp.zeros(vals.shape, F32)
    tops = []
    for j in range(TOP_K):
        m = jnp.max(vals, axis=-1, keepdims=True)
        idx = jnp.min(jnp.where(vals == m, lane, float(LANES)), axis=-1, keepdims=True)
        sel = lane == idx
        jm = jnp.where(sel, float(j + 1), jm)
        vals = jnp.where(sel, -jnp.inf, vals)
        tops.append(m)
    es = [jnp.exp(m - tops[0]) for m in tops]
    den = es[0] + es[1] + es[2] + es[3]
    wm = jnp.zeros(vals.shape, F32)
    for j in range(TOP_K):
        wm = jnp.where(jm == float(j + 1), es[j] / den, wm)
    sel_any = jnp.where(jm > 0.0, 1.0, 0.0)
    r = lax.broadcasted_iota(I32, (tm, tm), 0)
    c = lax.broadcasted_iota(I32, (tm, tm), 1)
    tri = jnp.where(c < r, 1.0, 0.0).astype(BF16)
    carry = carry_ref[...]
    rank_ref[...] = jnp.dot(tri, sel_any.astype(BF16), preferred_element_type=F32) + carry
    jm_ref[...] = jm
    wm_ref[...] = wm
    carry = carry + jnp.sum(sel_any, axis=0, keepdims=True)
    carry_ref[...] = carry
    cnt_ref[...] = carry


def _router_call(logits, tm=256):
    n = logits.shape[0]
    tm = min(tm, n)
    tile = pl.BlockSpec((tm, LANES), lambda i: (i, 0))
    row = pl.BlockSpec((1, LANES), lambda i: (0, 0))
    return pl.pallas_call(
        _router_body,
        grid=(n // tm,),
        in_specs=[tile],
        out_specs=[tile, tile, tile, row],
        out_shape=[jax.ShapeDtypeStruct((n, LANES), F32)] * 3 + [jax.ShapeDtypeStruct((1, LANES), F32)],
        scratch_shapes=[pltpu.VMEM((1, LANES), F32)],
        compiler_params=_cparams(("arbitrary",)),
        name="router",
    )(logits)


def _dest_body(rank_ref, jm_ref, wm_ref, ps_ref, dest_ref, w4_ref):
    slot = rank_ref[...] + ps_ref[...]
    jm = jm_ref[...]
    wm = wm_ref[...]
    lane = lax.broadcasted_iota(I32, jm.shape, 1)
    dest = jnp.zeros(jm.shape, F32)
    w4 = jnp.zeros(jm.shape, F32)
    for j in range(TOP_K):
        sel = jm == float(j + 1)
        dj = jnp.sum(jnp.where(sel, slot, 0.0), axis=-1, keepdims=True)
        wj = jnp.sum(jnp.where(sel, wm, 0.0), axis=-1, keepdims=True)
        dest = jnp.where(lane == j, dj, dest)
        w4 = jnp.where(lane == j, wj, w4)
    dest_ref[...] = dest.astype(I32)
    w4_ref[...] = w4


def _dest_call(rank, jm, wm, ps_row, tm=512):
    n = rank.shape[0]
    tm = min(tm, n)
    tile = pl.BlockSpec((tm, LANES), lambda i: (i, 0))
    return pl.pallas_call(
        _dest_body,
        grid=(n // tm,),
        in_specs=[tile, tile, tile, pl.BlockSpec((1, LANES), lambda i: (0, 0))],
        out_specs=[tile, tile],
        out_shape=[jax.ShapeDtypeStruct((n, LANES), I32), jax.ShapeDtypeStruct((n, LANES), F32)],
        compiler_params=_cparams(("parallel",)),
        name="dest",
    )(rank, jm, wm, ps_row)


ROW_UNROLL = 8
SLOT_UNROLL = 16


def _build_slot_table(dest_ref, cnt_ref, pst_ref, slot_ref, n_tok):
    def pads(e, carry):
        base = pst_ref[e] + cnt_ref[e]

        def zeros(i, c2):
            for u in range(SLOT_UNROLL):
                slot_ref[base + i * SLOT_UNROLL + u] = 0
            return c2
        lax.fori_loop(0, SLOT_BLOCK // SLOT_UNROLL, zeros, 0)
        return carry
    lax.fori_loop(0, N_EXPERTS, pads, 0)

    per_trip = SLOT_UNROLL // TOP_K
    assert n_tok % per_trip == 0

    def toks(i, carry):
        for u in range(per_trip):
            t = i * per_trip + u
            for j in range(TOP_K):
                slot_ref[dest_ref[t * TOP_K + j]] = t
        return carry
    lax.fori_loop(0, n_tok // per_trip, toks, 0)


def _expert_body(gexp_ref, gsub0_ref, gns_ref, dest_ref, cnt_ref, pst_ref,
                 hp_hbm, w1g_ref, w1l_ref, b1g_ref, b1l_ref, w2_ref, b2_ref,
                 outs_hbm,
                 xlo, xhi, stage, acc, slot_ref, sem_in, sem_out, *, n_chunks, n_groups, n_tok):
    g = pl.program_id(0)
    c = pl.program_id(1)
    ns = gns_ref[g]
    s0 = gsub0_ref[g]
    sub = SLOT_BLOCK
    half = xlo.shape[1]
    last = n_chunks - 1

    pieces = half // LANES

    def row_copy(tok, r):
        return pltpu.make_async_copy(hp_hbm.at[pl.ds(pl.multiple_of(tok * pieces, pieces), pieces), :],
                                     stage.at[pl.ds(pl.multiple_of(r * pieces, pieces), pieces), :], sem_in)

    def issue_rows(gi):
        base = gsub0_ref[gi] * sub

        def trip(i, carry):
            for u in range(ROW_UNROLL):
                r = i * ROW_UNROLL + u
                row_copy(slot_ref[base + r], r).start()
            return carry
        lax.fori_loop(0, gns_ref[gi] * (sub // ROW_UNROLL), trip, 0)

    def out_copy(r0, rows):
        row0 = s0 * sub + r0
        return pltpu.make_async_copy(acc.at[pl.ds(r0, rows), :], outs_hbm.at[pl.ds(row0, rows), :], sem_out)

    def compute(r0, rows):
        xl = xlo[pl.ds(r0, rows), :]
        xh = xhi[pl.ds(r0, rows), :]
        hg = (jnp.dot(xl, w1g_ref[:half, :].astype(BF16), preferred_element_type=F32)
              + jnp.dot(xh, w1g_ref[half:, :].astype(BF16), preferred_element_type=F32) + b1g_ref[...])
        hl = (jnp.dot(xl, w1l_ref[:half, :].astype(BF16), preferred_element_type=F32)
              + jnp.dot(xh, w1l_ref[half:, :].astype(BF16), preferred_element_type=F32) + b1l_ref[...])
        gate = jnp.minimum(hg, SWIGLU_LIMIT)
        lin = jnp.clip(hl, -SWIGLU_LIMIT, SWIGLU_LIMIT)
        act = gate * jax.nn.sigmoid(SWIGLU_ALPHA * gate) * (lin + 1.0)
        acc[pl.ds(r0, rows), :] += jnp.dot(act.astype(BF16), w2_ref[...].astype(BF16),
                                           preferred_element_type=F32)

        @pl.when(c == last)
        def _():
            out_copy(r0, rows).start()

    @pl.when(ns > 0)
    def _():
        @pl.when(c == 0)
        def _load():
            @pl.when(g == 0)
            def _():
                _build_slot_table(dest_ref, cnt_ref, pst_ref, slot_ref, n_tok)
                issue_rows(0)

            def wt(i, carry):
                for u in range(ROW_UNROLL):
                    row_copy(0, i * ROW_UNROLL + u).wait()
                return carry
            lax.fori_loop(0, ns * (sub // ROW_UNROLL), wt, 0)

            def unpack(s, carry):
                r0 = pl.multiple_of(s * sub, sub)
                for j in range(pieces):
                    w = stage[pl.ds(r0 * pieces + j, sub, stride=pieces), :]
                    cols = slice(j * LANES, (j + 1) * LANES)
                    xlo[pl.ds(r0, sub), cols] = lax.bitcast_convert_type(w << 16, F32).astype(BF16)
                    xhi[pl.ds(r0, sub), cols] = lax.bitcast_convert_type(
                        w & jnp.uint32(0xFFFF0000), F32).astype(BF16)
                acc[pl.ds(r0, sub), :] = jnp.broadcast_to(b2_ref[...], (sub, acc.shape[1]))
                return carry
            lax.fori_loop(0, ns, unpack, 0)

        @pl.when(c == min(1, last))
        def _prefetch():
            nxt = jnp.minimum(g + 1, n_groups - 1)

            @pl.when(jnp.logical_and(g + 1 < n_groups, gns_ref[nxt] > 0))
            def _():
                issue_rows(nxt)

        def pair(i, carry):
            compute(pl.multiple_of(i * (2 * sub), 2 * sub), 2 * sub)
            return carry
        lax.fori_loop(0, ns // 2, pair, 0)

        @pl.when(ns % 2 == 1)
        def _():
            compute(pl.multiple_of((ns - 1) * sub, sub), sub)

        @pl.when(c == last)
        def _drain():
            def wt(i, carry):
                out_copy(pl.multiple_of(i * (2 * sub), 2 * sub), 2 * sub).wait()
                return carry
            lax.fori_loop(0, ns // 2, wt, 0)

            @pl.when(ns % 2 == 1)
            def _():
                out_copy(pl.multiple_of((ns - 1) * sub, sub), sub).wait()

            nxt = jnp.minimum(g + 1, n_groups - 1)

            @pl.when(jnp.logical_or(g + 1 >= n_groups, gns_ref[nxt] == 0))
            def _tail():
                acc[pl.ds(0, sub), :] = jnp.zeros((sub, acc.shape[1]), F32)
                first = s0 + ns
                n_tail = outs_hbm.shape[0] // sub - first

                def tail_copy(i):
                    row0 = pl.multiple_of((first + i) * sub, sub)
                    return pltpu.make_async_copy(acc.at[pl.ds(0, sub), :], outs_hbm.at[pl.ds(row0, sub), :],
                                                 sem_out)

                def st(i, carry):
                    tail_copy(i).start()
                    return carry

                def wt(i, carry):
                    tail_copy(i).wait()
                    return carry
                lax.fori_loop(0, n_tail, st, 0)
                lax.fori_loop(0, n_tail, wt, 0)


def _expert_call(gexp, gsub0, gns, dest_flat, counts, pstart, n_slots, hp, w1, b1, w2, b2):
    half = w1.shape[1] // 2
    n_exp, d, ff2 = w1.shape
    ff = ff2 // 2
    tc = FF_CHUNK
    n_chunks = ff // tc
    n_groups = gexp.shape[0]
    rows = GROUP_SUBS * SLOT_BLOCK

    def chunk(c, gn, g):
        return jnp.where(gn[g] > 0, c, n_chunks - 1)

    grid_spec = pltpu.PrefetchScalarGridSpec(
        num_scalar_prefetch=6,
        grid=(n_groups, n_chunks),
        in_specs=[
            pl.BlockSpec(memory_space=pl.ANY),
            pl.BlockSpec((None, d, tc), lambda g, c, ge, gs, gn, *_: (ge[g], 0, chunk(c, gn, g))),
            pl.BlockSpec((None, d, tc), lambda g, c, ge, gs, gn, *_: (ge[g], 0, n_chunks + chunk(c, gn, g))),
            pl.BlockSpec((None, 1, tc), lambda g, c, ge, gs, gn, *_: (ge[g], 0, chunk(c, gn, g))),
            pl.BlockSpec((None, 1, tc), lambda g, c, ge, gs, gn, *_: (ge[g], 0, n_chunks + chunk(c, gn, g))),
            pl.BlockSpec((None, tc, d), lambda g, c, ge, gs, gn, *_: (ge[g], chunk(c, gn, g), 0)),
            pl.BlockSpec((None, 1, d), lambda g, c, ge, gs, gn, *_: (ge[g], 0, 0)),
        ],
        out_specs=pl.BlockSpec(memory_space=pl.ANY),
        scratch_shapes=[
            pltpu.VMEM((rows, half), BF16), pltpu.VMEM((rows, half), BF16),
            pltpu.VMEM((rows * (half // LANES), LANES), U32),
            pltpu.VMEM((rows, d), F32),
            pltpu.SMEM((n_slots + SLOT_BLOCK,), I32),
            pltpu.SemaphoreType.DMA(()), pltpu.SemaphoreType.DMA(()),
        ],
    )
    return pl.pallas_call(
        functools.partial(_expert_body, n_chunks=n_chunks, n_groups=n_groups,
                          n_tok=dest_flat.shape[0] // TOP_K),
        grid_spec=grid_spec,
        out_shape=jax.ShapeDtypeStruct((n_slots, d), F32),
        compiler_params=_cparams(("arbitrary", "arbitrary")),
        name="expert",
    )(gexp, gsub0, gns, dest_flat, counts, pstart, hp, w1, w1, b1.reshape(n_exp, 1, ff2),
      b1.reshape(n_exp, 1, ff2), w2, b2.reshape(n_exp, 1, d))


def _final_body(dest_ref, x1_ref, w4_ref, p_ref, wple_ref, wpg_ref, bpg_ref, npl_ref, outs_hbm,
                o_ref, gbuf, sems, *, tm, n_steps):
    i = pl.program_id(0)
    cur = i % 2

    def row_copy(slot, buf, j, q, u):
        return pltpu.make_async_copy(outs_hbm.at[pl.ds(slot, 1), :], gbuf.at[buf, j, q, pl.ds(u, 1), :],
                                     sems.at[buf])

    def issue(step, buf):
        def rows(q, carry):
            for u in range(SUBLANES):
                t = step * tm + q * SUBLANES + u
                for j in range(TOP_K):
                    row_copy(dest_ref[t * TOP_K + j], buf, j, q, u).start(priority=j % 2)
            return carry
        lax.fori_loop(0, tm // SUBLANES, rows, 0)

    @pl.when(i == 0)
    def _():
        issue(0, 0)

    @pl.when(i + 1 < n_steps)
    def _():
        issue(i + 1, 1 - cur)

    pe = jnp.dot(p_ref[...].astype(BF16), wple_ref[...], preferred_element_type=F32)

    def wt(q, carry):
        for u in range(SUBLANES):
            for j in range(TOP_K):
                row_copy(0, cur, j, q, u).wait()
        return carry
    lax.fori_loop(0, tm // SUBLANES, wt, 0)

    w4 = w4_ref[...]
    d = o_ref.shape[1]
    y = gbuf[cur, 0].reshape(tm, d) * w4[:, 0:1]
    for j in range(1, TOP_K):
        y = y + gbuf[cur, j].reshape(tm, d) * w4[:, j:j + 1]
    x2 = x1_ref[...] + y
    hn = _rmsnorm_rows(x2, npl_ref[...])
    gate = jax.nn.sigmoid(jnp.dot(hn.astype(BF16), wpg_ref[...], preferred_element_type=F32) + bpg_ref[...])
    o_ref[...] = x2 + gate * pe


def _final_call(dest_flat, x1, w4, p2d, wple, wpg, bpg_row, npl_row, outs, tm=256):
    n, d = x1.shape
    tm = min(tm, n)
    pd = p2d.shape[1]
    res = lambda shape: pl.BlockSpec(shape, lambda i, dst: (0,) * len(shape), pipeline_mode=pl.Buffered(1))
    grid_spec = pltpu.PrefetchScalarGridSpec(
        num_scalar_prefetch=1,
        grid=(n // tm,),
        in_specs=[pl.BlockSpec((tm, d), lambda i, dst: (i, 0)),
                  pl.BlockSpec((tm, LANES), lambda i, dst: (i, 0)),
                  pl.BlockSpec((tm, pd), lambda i, dst: (i, 0)),
                  res((pd, d)), res((d, d)), res((1, d)), res((1, d)),
                  pl.BlockSpec(memory_space=pl.ANY)],
        out_specs=pl.BlockSpec((tm, d), lambda i, dst: (i, 0)),
        scratch_shapes=[pltpu.VMEM((2, TOP_K, tm // SUBLANES, SUBLANES, d), F32),
                        pltpu.SemaphoreType.DMA((2,))],
    )
    return pl.pallas_call(
        functools.partial(_final_body, tm=tm, n_steps=n // tm),
        grid_spec=grid_spec,
        out_shape=jax.ShapeDtypeStruct((n, d), F32),
        compiler_params=_cparams(("arbitrary",)),
        name="final",
    )(dest_flat, x1, w4, p2d, wple, wpg, bpg_row, npl_row, outs)


def _t5_bucket(rel):
    n = -rel
    nb = REL_BUCKETS // 2
    ret = jnp.where(n < 0, nb, 0)
    n = jnp.abs(n)
    max_exact = nb // 2
    large = max_exact + (jnp.log(jnp.maximum(n, 1).astype(jnp.float32) / max_exact)
                         / math.log(REL_MAX_DIST / max_exact) * (nb - max_exact)).astype(jnp.int32)
    large = jnp.minimum(large, nb - 1)
    return ret + jnp.where(n < max_exact, n, large)


def _bias_body(rv_ref, o_ref, *, t):
    x = jnp.broadcast_to(rv_ref[...], (t, rv_ref.shape[1]))
    o_ref[...] = pltpu.roll(x, 1, 1, stride=1, stride_axis=0)[:, t:]


def _diff_bias_table(rel_bias, seq, t):
    rel = jnp.arange(seq + t) - (seq - 1)
    rv = (jnp.transpose(rel_bias[_t5_bucket(rel)], (1, 0)).astype(F32) * LOG2E)[:, None, :]
    return pl.pallas_call(
        functools.partial(_bias_body, t=t),
        grid=(N_HEADS,),
        in_specs=[pl.BlockSpec((None, 1, seq + t), lambda h: (h, 0, 0))],
        out_specs=pl.BlockSpec((None, t, seq), lambda h: (h, 0, 0)),
        out_shape=jax.ShapeDtypeStruct((N_HEADS, t, seq), F32),
        compiler_params=_cparams(("parallel",)),
        name="bias_table",
    )(rv)


def _pad_cols(a, width):
    return jnp.pad(a, ((0, 0), (0, width - a.shape[1])))


def _group_table(counts):
    n_sub = (counts + SLOT_BLOCK - 1) // SLOT_BLOCK
    sub_start = jnp.cumsum(n_sub) - n_sub
    n_grp = (n_sub + GROUP_SUBS - 1) // GROUP_SUBS
    grp_end = jnp.cumsum(n_grp)
    total = grp_end[-1]
    return n_sub, sub_start, n_grp, grp_end, total


def _layer(x, p_l, w_in, b_gate, b_forget, dq_norm, dk_norm, fq_norm, fk_norm, lq1, lk1, lq2, lk2,
           lambda_init, subln, w_up_a, w_up_b, w_out, rel_bias, norm_mix, norm_moe, w_router,
           b_router, w1, b1, w2, b2, norm_ple, w_ple_gate, b_ple_gate, w_ple):
    b, s, d = x.shape
    n = b * s
    width = N_HEADS * HEAD_DIM
    t = min(ATTN_T, s)
    x2d = x.reshape(n, d)
    row = lambda v: v.reshape(1, -1).astype(F32)

    f_row = 6 * width
    wt_in = jnp.swapaxes(w_in, 0, 1)
    h, f_logit = _norm_call(x2d, row(norm_mix), wt_in, f_row)
    cum = _cum_call(f_logit, _pad_cols(row(b_forget), LANES), b, s)
    cum_bhs = jnp.transpose(cum[:, :N_HEADS].reshape(b, s, N_HEADS), (0, 2, 1))
    cum_row = cum_bhs[:, :, None, :]

    diff_scale = DIFF_QK_DIM ** -0.5 * LOG2E
    fox_scale = HEAD_DIM ** -0.5 * LOG2E
    gain_d = jnp.concatenate([jnp.tile(dq_norm * diff_scale, 2 * N_HEADS), jnp.tile(dk_norm, 2 * N_HEADS)])
    gain_f = jnp.concatenate([jnp.tile(fq_norm * fox_scale, N_HEADS), jnp.tile(fk_norm, N_HEADS)])
    zeros_w = jnp.zeros((1, width), F32)
    dqk = _proj_call(h, wt_in, 0, 2 * width, row(gain_d), "norm64", "proj_dqk")
    dv = _proj_call(h, wt_in, 2 * width, width, zeros_w, "plain", "proj_dv")
    fqk = _proj_call(h, wt_in, 3 * width, 2 * width, row(gain_f), "norm128", "proj_fqk")
    fv = _proj_call(h, wt_in, 5 * width, width, zeros_w, "plain", "proj_fv")
    gates = _proj_call(h, wt_in, f_row + N_HEADS, 2 * d, row(b_gate), "gate", "proj_gate")

    lam_rows = [row(v) for v in (lq1, lk1, lq2, lk2)]
    bias = _diff_bias_table(rel_bias, s, t)
    od = _diff_attn_call(dqk.reshape(b, s, 2 * width), dv.reshape(b, s, width), bias, lam_rows,
                         row(subln), lambda_init, t)
    of = _fox_attn_call(fqk.reshape(b, s, 2 * width), fv.reshape(b, s, width), cum.reshape(b, s, LANES),
                        cum_row, t)

    br_pad = jnp.full((1, LANES), NEG, F32).at[0, :N_EXPERTS].set(b_router.astype(F32))
    wr_pad = jnp.pad(jnp.swapaxes(w_router.astype(F32), 0, 1), ((0, LANES - N_EXPERTS), (0, 0)))
    wr_hi = wr_pad.astype(BF16)
    wr_lo = (wr_pad - wr_hi.astype(F32)).astype(BF16)
    x1, hp, logits = _mix_call(od.reshape(n, width), of.reshape(n, width), gates, x2d,
                               w_up_a.astype(BF16), w_up_b.astype(BF16), w_out.astype(BF16),
                               row(norm_moe), wr_hi, wr_lo, br_pad)

    rank, jm, wm, cnt = _router_call(logits)
    counts = cnt[0, :N_EXPERTS].astype(I32)
    padded = ((counts + SLOT_BLOCK - 1) // SLOT_BLOCK) * SLOT_BLOCK
    pstart = jnp.cumsum(padded) - padded
    ps_row = _pad_cols(pstart.astype(F32).reshape(1, -1), LANES)
    dest128, w4 = _dest_call(rank, jm, wm, ps_row)
    dest_flat = dest128[:, :TOP_K].reshape(-1)

    n_slots = n * TOP_K + N_EXPERTS * SLOT_BLOCK

    n_sub, sub_start, n_grp, grp_end, total = _group_table(counts)
    max_groups = N_EXPERTS + (n_slots // SLOT_BLOCK) // GROUP_SUBS
    gidx = jnp.arange(max_groups, dtype=I32)
    gvalid = gidx < total
    gsafe = jnp.minimum(gidx, total - 1)
    gexp = jnp.searchsorted(grp_end, gsafe, side="right").astype(I32)
    kth = gsafe - (grp_end - n_grp)[gexp]
    gsub0 = (sub_start[gexp] + kth * GROUP_SUBS).astype(I32)
    gns = jnp.where(gvalid, jnp.minimum(GROUP_SUBS, n_sub[gexp] - kth * GROUP_SUBS), 0).astype(I32)
    outs = _expert_call(gexp, gsub0, gns, dest_flat, counts, pstart, n_slots, hp, w1, b1, w2, b2)

    out = _final_call(dest_flat, x1, w4, p_l.reshape(n, -1), w_ple.astype(BF16),
                      w_ple_gate.astype(BF16), row(b_ple_gate), row(norm_ple), outs)
    return out.reshape(b, s, d)


def kernel(x, p, w_in, b_gate, b_forget, dq_norm, dk_norm, fq_norm, fk_norm, lambda_q1, lambda_k1,
           lambda_q2, lambda_k2, subln, w_up_a, w_up_b, w_out, rel_bias, norm_mix, norm_moe,
           w_router, b_router, w1, b1, w2, b2, norm_ple, w_ple_gate, b_ple_gate, w_ple):
    for i in range(w_in.shape[0]):
        lambda_init = 0.8 - 0.6 * math.exp(-0.3 * i)
        x = _layer(x, p[i], w_in[i], b_gate[i], b_forget[i], dq_norm[i], dk_norm[i], fq_norm[i],
                   fk_norm[i], lambda_q1[i], lambda_k1[i], lambda_q2[i], lambda_k2[i], lambda_init,
                   subln[i], w_up_a[i], w_up_b[i], w_out[i], rel_bias, norm_mix[i], norm_moe[i],
                   w_router[i], b_router[i], w1[i], b1[i], w2[i], b2[i], norm_ple[i],
                   w_ple_gate[i], b_ple_gate[i], w_ple[i])
    return x
```

```python
import functools
import math

import jax
import jax.numpy as jnp
from jax import lax
from jax.experimental import pallas as pl
from jax.experimental.pallas import tpu as pltpu

F32 = jnp.float32
BF16 = jnp.bfloat16
U32 = jnp.uint32
I32 = jnp.int32
HIGHEST = lax.Precision.HIGHEST

N_HEADS = 8
HEAD_DIM = 128
DIFF_QK_DIM = 64
CHUNK = 64
REL_BUCKETS = 32
REL_MAX_DIST = 128
N_EXPERTS = 32
TOP_K = 4
SWIGLU_LIMIT = 7.0
SWIGLU_ALPHA = 1.702
RMS_EPS = 1e-6
NEG = -1e30
LOG2E = math.log2(math.e)

LANES = 128
SUBLANES = 8
SLOT_BLOCK = 128
GROUP_SUBS = 12
TRIP_SUBS = 4
FF_CHUNK = 512
ATTN_T = 256
VMEM_LIMIT = 56 * 1024 * 1024


_NT = (((1,), (1,)), ((), ()))


def _cparams(sem):
    return pltpu.CompilerParams(dimension_semantics=sem, vmem_limit_bytes=VMEM_LIMIT)


def _rmsnorm_rows(x, g):
    ms = jnp.mean(x * x, axis=-1, keepdims=True)
    return x * lax.rsqrt(ms + RMS_EPS) * g


def _norm_body(x_ref, g_ref, wf_ref, h_ref, fl_ref):
    y = _rmsnorm_rows(x_ref[...], g_ref[...])
    y_hi = y.astype(BF16)
    h_ref[...] = y_hi
    rowi = lax.broadcasted_iota(I32, (LANES, 1), 0)
    wf = jnp.where(rowi < N_HEADS, wf_ref[...], 0.0)
    y_lo = (y - y_hi.astype(F32)).astype(BF16)
    wf_hi = wf.astype(BF16)
    wf_lo = (wf - wf_hi.astype(F32)).astype(BF16)
    fl_ref[...] = (lax.dot_general(y_hi, wf_hi, _NT, preferred_element_type=F32)
                   + lax.dot_general(y_lo, wf_hi, _NT, preferred_element_type=F32)
                   + lax.dot_general(y_hi, wf_lo, _NT, preferred_element_type=F32))


def _norm_call(x2d, g_row, wt_in, f_row, tm=512):
    n, d = x2d.shape
    tm = min(tm, n)
    assert f_row % LANES == 0
    return pl.pallas_call(
        _norm_body,
        grid=(n // tm,),
        in_specs=[pl.BlockSpec((tm, d), lambda i: (i, 0)),
                  pl.BlockSpec((1, d), lambda i: (0, 0)),
                  pl.BlockSpec((LANES, d), lambda i: (f_row // LANES, 0))],
        out_specs=[pl.BlockSpec((tm, d), lambda i: (i, 0)),
                   pl.BlockSpec((tm, LANES), lambda i: (i, 0))],
        out_shape=[jax.ShapeDtypeStruct((n, d), BF16),
                   jax.ShapeDtypeStruct((n, LANES), F32)],
        compiler_params=_cparams(("parallel",)),
        name="norm",
    )(x2d, g_row, wt_in)


def _cum_body(fl_ref, bf_ref, cum_ref):
    s = fl_ref.shape[0]
    r = lax.broadcasted_iota(I32, (LANES, LANES), 0)
    c = lax.broadcasted_iota(I32, (LANES, LANES), 1)
    tri = jnp.where(r >= c, 1.0, 0.0).astype(F32)
    carry = jnp.zeros((1, LANES), F32)
    for blk in range(s // LANES):
        rows = slice(blk * LANES, (blk + 1) * LANES)
        lf = jax.nn.log_sigmoid(fl_ref[rows, :] + bf_ref[...])
        cb = jnp.dot(tri, lf, precision=HIGHEST, preferred_element_type=F32) + carry
        cum_ref[rows, :] = cb * LOG2E
        carry = cb[LANES - 1:LANES, :]


def _cum_call(fl, bf_row, batch, seq):
    return pl.pallas_call(
        _cum_body,
        grid=(batch,),
        in_specs=[pl.BlockSpec((seq, LANES), lambda b: (b, 0)),
                  pl.BlockSpec((1, LANES), lambda b: (0, 0))],
        out_specs=pl.BlockSpec((seq, LANES), lambda b: (b, 0)),
        out_shape=jax.ShapeDtypeStruct(fl.shape, F32),
        compiler_params=_cparams(("parallel",)),
        name="cum",
    )(fl, bf_row)


def _proj_epilogue(acc, aux_ref, o_ref, kind):
    tn = acc.shape[1]
    if kind == "plain":
        o_ref[...] = acc.astype(o_ref.dtype)
    elif kind == "gate":
        o_ref[...] = jax.nn.sigmoid(acc + aux_ref[...]).astype(o_ref.dtype)
    else:
        lo = lax.broadcasted_iota(I32, (1, LANES), 1) < DIFF_QK_DIM
        for s in range(tn // LANES):
            cols = slice(s * LANES, (s + 1) * LANES)
            blk = acc[:, cols]
            sq = blk * blk
            if kind == "norm128":
                ms = jnp.mean(sq, axis=-1, keepdims=True)
            else:
                s_lo = jnp.sum(jnp.where(lo, sq, 0.0), axis=-1, keepdims=True)
                s_hi = jnp.sum(jnp.where(lo, 0.0, sq), axis=-1, keepdims=True)
                ms = jnp.where(lo, s_lo, s_hi) * (1.0 / DIFF_QK_DIM)
            o_ref[:, cols] = (blk * lax.rsqrt(ms + RMS_EPS) * aux_ref[:, cols]).astype(o_ref.dtype)


def _proj_body(h_ref, wt_ref, aux_ref, o_ref, *, kind):
    acc = lax.dot_general(h_ref[...], wt_ref[...].astype(BF16), _NT, preferred_element_type=F32)
    _proj_epilogue(acc, aux_ref, o_ref, kind)


def _proj_shift_body(h_ref, wa_ref, wb_ref, aux_ref, o_ref, *, kind, shift):
    wt = jnp.concatenate([wa_ref[shift:, :], wb_ref[...]], axis=0)
    acc = lax.dot_general(h_ref[...], wt.astype(BF16), _NT, preferred_element_type=F32)
    _proj_epilogue(acc, aux_ref, o_ref, kind)


def _proj_call(h, wt, row0, nrows, aux_row, kind, name, tm=1024, tn=1024):
    m, k = h.shape
    tm = min(tm, m)
    assert nrows % tn == 0 and m % tm == 0
    shift = row0 % tn
    base = row0 - shift
    h_spec = pl.BlockSpec((tm, k), lambda i, j: (i, 0))
    w_spec = pl.BlockSpec((tn, k), lambda i, j: (base // tn + j, 0))
    aux_spec = pl.BlockSpec((1, tn), lambda i, j: (0, j))
    if shift == 0:
        body, w_specs, ws = functools.partial(_proj_body, kind=kind), [w_spec], [wt]
    else:
        assert shift % SUBLANES == 0 and tn % shift == 0
        body = functools.partial(_proj_shift_body, kind=kind, shift=shift)
        w_specs = [w_spec, pl.BlockSpec((shift, k), lambda i, j: ((base + (j + 1) * tn) // shift, 0))]
        ws = [wt, wt]
    return pl.pallas_call(
        body,
        grid=(m // tm, nrows // tn),
        in_specs=[h_spec] + w_specs + [aux_spec],
        out_specs=pl.BlockSpec((tm, tn), lambda i, j: (i, j)),
        out_shape=jax.ShapeDtypeStruct((m, nrows), BF16),
        compiler_params=_cparams(("parallel", "parallel")),
        name=name,
    )(h, *ws, aux_row)


DIFF_HEADS_PER_STEP = 4
FOX_HEADS_PER_STEP = 8


def _softmax_step(s, m_ref, acc_ref, v1):
    m_old = m_ref[...]
    mn = jnp.maximum(m_old, jnp.max(s, axis=-1, keepdims=True))
    alpha = jnp.exp2(m_old - mn)
    p = jnp.exp2(s - jnp.tile(mn, (1, s.shape[1] // LANES))).astype(BF16)
    acc_ref[...] = (jnp.tile(alpha, (1, acc_ref.shape[1] // LANES)) * acc_ref[...]
                    + jnp.dot(p, v1, preferred_element_type=F32))
    m_ref[...] = mn


def _causal_sweep(step, qi, t):
    def pair(i, carry):
        step(2 * i, 2 * t, False)
        return carry
    lax.fori_loop(0, qi // 2, pair, 0)

    @pl.when(qi % 2 == 1)
    def _():
        step(qi - 1, t, False)
    step(qi, t, True)


def _with_ones(v):
    return jnp.concatenate([v, jnp.ones_like(v)], axis=1)


def _normalised(acc):
    return acc[:, :HEAD_DIM] / acc[:, HEAD_DIM:]


def _diff_attn_body(lq1_ref, lk1_ref, lq2_ref, lk2_ref, g_ref, q_ref, k_ref, v_ref, bias_ref,
                    o_ref, m_sc, acc_sc, *, t, nh, lambda_init):
    qi = pl.program_id(2)
    s_len = k_ref.shape[0]
    m_sc[...] = jnp.full(m_sc.shape, NEG, F32)
    acc_sc[...] = jnp.zeros(acc_sc.shape, F32)
    lo = lax.broadcasted_iota(I32, (t, HEAD_DIM), 1) < DIFF_QK_DIM

    def step(kb, w, masked):
        k0 = pl.multiple_of(kb * t, t)
        off = pl.multiple_of((kb - qi) * t + (s_len - t), LANES)
        if masked:
            r = lax.broadcasted_iota(I32, (t, w), 0)
            c = lax.broadcasted_iota(I32, (t, w), 1)
            ok = (c // CHUNK) <= (r // CHUNK)
        for h in range(nh):
            cols = slice(h * HEAD_DIM, (h + 1) * HEAD_DIM)
            q = q_ref[:, cols]
            zero = jnp.zeros_like(q)
            k = k_ref[pl.ds(k0, w), cols]
            v1 = _with_ones(v_ref[pl.ds(k0, w), cols])
            b = bias_ref[h, :, pl.ds(off, w)]
            for mp, qm in enumerate((jnp.where(lo, q, zero), jnp.where(lo, zero, q))):
                s = lax.dot_general(qm, k, _NT, preferred_element_type=F32) + b
                if masked:
                    s = jnp.where(ok, s, NEG)
                _softmax_step(s, m_sc.at[2 * h + mp], acc_sc.at[2 * h + mp], v1)

    _causal_sweep(step, qi, t)

    lam = (jnp.exp(jnp.sum(lq1_ref[...] * lk1_ref[...], axis=-1, keepdims=True))
           - jnp.exp(jnp.sum(lq2_ref[...] * lk2_ref[...], axis=-1, keepdims=True)) + lambda_init)
    for h in range(nh):
        o = _normalised(acc_sc[2 * h]) - lam * _normalised(acc_sc[2 * h + 1])
        o = _rmsnorm_rows(o, g_ref[...]) * (1.0 - lambda_init)
        o_ref[:, h * HEAD_DIM:(h + 1) * HEAD_DIM] = o.astype(o_ref.dtype)


def _diff_attn_call(qk, v, bias, lam_rows, subln_row, lambda_init, t):
    b, s, _ = v.shape
    nh = DIFF_HEADS_PER_STEP
    wb = nh * HEAD_DIM
    n_hg = N_HEADS // nh
    vec = lambda w: pl.BlockSpec((1, w), lambda hg, bb, qi: (0, 0))
    return pl.pallas_call(
        functools.partial(_diff_attn_body, t=t, nh=nh, lambda_init=lambda_init),
        grid=(n_hg, b, s // t),
        in_specs=[vec(DIFF_QK_DIM)] * 4 + [
            vec(HEAD_DIM),
            pl.BlockSpec((None, t, wb), lambda hg, bb, qi: (bb, qi, hg)),
            pl.BlockSpec((None, s, wb), lambda hg, bb, qi: (bb, 0, n_hg + hg)),
            pl.BlockSpec((None, s, wb), lambda hg, bb, qi: (bb, 0, hg)),
            pl.BlockSpec((nh, t, s), lambda hg, bb, qi: (hg, 0, 0))],
        out_specs=pl.BlockSpec((None, t, wb), lambda hg, bb, qi: (bb, qi, hg)),
        out_shape=jax.ShapeDtypeStruct(v.shape, BF16),
        scratch_shapes=[pltpu.VMEM((2 * nh, t, LANES), F32), pltpu.VMEM((2 * nh, t, 2 * HEAD_DIM), F32)],
        compiler_params=_cparams(("parallel", "parallel", "parallel")),
        name="diff_attn",
    )(*lam_rows, subln_row, qk, qk, v, bias)


def _fox_attn_body(q_ref, k_ref, v_ref, cq_ref, ck_ref, o_ref, m_sc, acc_sc, cq_sc, *, t, nh):
    qi = pl.program_id(2)
    m_sc[...] = jnp.full(m_sc.shape, NEG, F32)
    acc_sc[...] = jnp.zeros(acc_sc.shape, F32)
    hg = pl.program_id(0)
    lane = lax.broadcasted_iota(I32, (t, LANES), 1)
    cq_all = cq_ref[...]
    for h in range(nh):
        col = jnp.sum(jnp.where(lane == hg * nh + h, cq_all, 0.0), axis=-1, keepdims=True)
        cq_sc[h] = jnp.broadcast_to(col, (t, LANES))

    def step(kb, w, masked):
        k0 = pl.multiple_of(kb * t, t)
        if masked:
            r = lax.broadcasted_iota(I32, (t, w), 0)
            c = lax.broadcasted_iota(I32, (t, w), 1)
            ok = c <= r
        for h in range(nh):
            cols = slice(h * HEAD_DIM, (h + 1) * HEAD_DIM)
            k = k_ref[pl.ds(k0, w), cols]
            v1 = _with_ones(v_ref[pl.ds(k0, w), cols])
            s = lax.dot_general(q_ref[:, cols], k, _NT, preferred_element_type=F32)
            s = s + jnp.tile(cq_sc[h], (1, w // LANES)) - ck_ref[h, :, pl.ds(k0, w)]
            if masked:
                s = jnp.where(ok, s, NEG)
            _softmax_step(s, m_sc.at[h], acc_sc.at[h], v1)

    _causal_sweep(step, qi, t)
    for h in range(nh):
        o_ref[:, h * HEAD_DIM:(h + 1) * HEAD_DIM] = _normalised(acc_sc[h]).astype(o_ref.dtype)


def _fox_attn_call(qk, v, cum, cum_row, t):
    b, s, _ = v.shape
    nh = FOX_HEADS_PER_STEP
    wb = nh * HEAD_DIM
    n_hg = N_HEADS // nh
    return pl.pallas_call(
        functools.partial(_fox_attn_body, t=t, nh=nh),
        grid=(n_hg, b, s // t),
        in_specs=[pl.BlockSpec((None, t, wb), lambda hg, bb, qi: (bb, qi, hg)),
                  pl.BlockSpec((None, s, wb), lambda hg, bb, qi: (bb, 0, n_hg + hg)),
                  pl.BlockSpec((None, s, wb), lambda hg, bb, qi: (bb, 0, hg)),
                  pl.BlockSpec((None, t, LANES), lambda hg, bb, qi: (bb, qi, 0)),
                  pl.BlockSpec((None, nh, 1, s), lambda hg, bb, qi: (bb, hg, 0, 0))],
        out_specs=pl.BlockSpec((None, t, wb), lambda hg, bb, qi: (bb, qi, hg)),
        out_shape=jax.ShapeDtypeStruct(v.shape, BF16),
        scratch_shapes=[pltpu.VMEM((nh, t, LANES), F32), pltpu.VMEM((nh, t, 2 * HEAD_DIM), F32),
                        pltpu.VMEM((nh, t, LANES), F32)],
        compiler_params=_cparams(("parallel", "parallel", "parallel")),
        name="fox_attn",
    )(qk, qk, v, cum, cum_row)


def _mix_body(od_ref, of_ref, g_ref, x_ref, wa_ref, wb_ref, wo_ref, nm_ref, wrh_ref, wrl_ref, br_ref,
              x1_ref, hp_ref, lg_ref):
    d = x_ref.shape[1]
    ua = jnp.dot(od_ref[...], wa_ref[...], preferred_element_type=F32)
    ub = jnp.dot(of_ref[...], wb_ref[...], preferred_element_type=F32)
    mixed = g_ref[:, :d].astype(F32) * ua + g_ref[:, d:].astype(F32) * ub
    x1 = x_ref[...] + jnp.dot(mixed.astype(BF16), wo_ref[...], preferred_element_type=F32)
    x1_ref[...] = x1
    hm = _rmsnorm_rows(x1, nm_ref[...])
    h_hi = hm.astype(BF16)
    h_hi32 = h_hi.astype(F32)
    h_lo = (hm - h_hi32).astype(BF16)
    lg_ref[...] = (lax.dot_general(h_hi, wrh_ref[...], _NT, preferred_element_type=F32)
                   + lax.dot_general(h_lo, wrh_ref[...], _NT, preferred_element_type=F32)
                   + lax.dot_general(h_hi, wrl_ref[...], _NT, preferred_element_type=F32) + br_ref[...])
    bits = lax.bitcast_convert_type(h_hi32, U32)
    packed = (bits[:, :d // 2] >> 16) | (bits[:, d // 2:] & jnp.uint32(0xFFFF0000))
    tm = packed.shape[0]
    pieces = packed.shape[1] // LANES
    for j in range(pieces):
        hp_ref[pl.ds(j, tm, stride=pieces), :] = packed[:, j * LANES:(j + 1) * LANES]


def _resident(shape):
    return pl.BlockSpec(shape, lambda i: (0,) * len(shape), pipeline_mode=pl.Buffered(1))


def _mix_call(od, of, gates, x2d, wa, wb, wo, nm_row, wr_hi, wr_lo, br_pad, tm=256):
    n, d = x2d.shape
    tm = min(tm, n)
    wdt = od.shape[1]
    return pl.pallas_call(
        _mix_body,
        grid=(n // tm,),
        in_specs=[pl.BlockSpec((tm, wdt), lambda i: (i, 0)),
                  pl.BlockSpec((tm, wdt), lambda i: (i, 0)),
                  pl.BlockSpec((tm, 2 * d), lambda i: (i, 0)),
                  pl.BlockSpec((tm, d), lambda i: (i, 0)),
                  _resident((wdt, d)), _resident((wdt, d)), _resident((d, d)),
                  _resident((1, d)), _resident((LANES, d)), _resident((LANES, d)), _resident((1, LANES))],
        out_specs=[pl.BlockSpec((tm, d), lambda i: (i, 0)),
                   pl.BlockSpec((tm * (d // 2 // LANES), LANES), lambda i: (i, 0)),
                   pl.BlockSpec((tm, LANES), lambda i: (i, 0))],
        out_shape=[jax.ShapeDtypeStruct((n, d), F32),
                   jax.ShapeDtypeStruct((n * (d // 2 // LANES), LANES), U32),
                   jax.ShapeDtypeStruct((n, LANES), F32)],
        compiler_params=_cparams(("parallel",)),
        name="mix",
    )(od, of, gates, x2d, wa, wb, wo, nm_row, wr_hi, wr_lo, br_pad)


def _router_body(lg_ref, rank_ref, jm_ref, wm_ref, cnt_ref, carry_ref):
    i = pl.program_id(0)

    @pl.when(i == 0)
    def _():
        carry_ref[...] = jnp.zeros_like(carry_ref)

    vals = lg_ref[...]
    tm = vals.shape[0]
    lane = lax.broadcasted_iota(I32, vals.shape, 1).astype(F32)
    jm = jnp.zeros(vals.shape, F32)
    tops = []
    for j in range(TOP_K):
        m = jnp.max(vals, axis=-1, keepdims=True)
        idx = jnp.min(jnp.where(vals == m, lane, float(LANES)), axis=-1, keepdims=True)
        sel = lane == idx
        jm = jnp.where(sel, float(j + 1), jm)
        vals = jnp.where(sel, -jnp.inf, vals)
        tops.append(m)
    es = [jnp.exp(m - tops[0]) for m in tops]
    den = es[0] + es[1] + es[2] + es[3]
    wm = jnp.zeros(vals.shape, F32)
    for j in range(TOP_K):
        wm = jnp.where(jm == float(j + 1), es[j] / den, wm)
    sel_any = jnp.where(jm > 0.0, 1.0, 0.0)
    r = lax.broadcasted_iota(I32, (tm, tm), 0)
    c = lax.broadcasted_iota(I32, (tm, tm), 1)
    tri = jnp.where(c < r, 1.0, 0.0).astype(BF16)
    carry = carry_ref[...]
    rank_ref[...] = jnp.dot(tri, sel_any.astype(BF16), preferred_element_type=F32) + carry
    jm_ref[...] = jm
    wm_ref[...] = wm
    carry = carry + jnp.sum(sel_any, axis=0, keepdims=True)
    carry_ref[...] = carry
    cnt_ref[...] = carry


def _router_call(logits, tm=256):
    n = logits.shape[0]
    tm = min(tm, n)
    tile = pl.BlockSpec((tm, LANES), lambda i: (i, 0))
    row = pl.BlockSpec((1, LANES), lambda i: (0, 0))
    return pl.pallas_call(
        _router_body,
        grid=(n // tm,),
        in_specs=[tile],
        out_specs=[tile, tile, tile, row],
        out_shape=[jax.ShapeDtypeStruct((n, LANES), F32)] * 3 + [jax.ShapeDtypeStruct((1, LANES), F32)],
        scratch_shapes=[pltpu.VMEM((1, LANES), F32)],
        compiler_params=_cparams(("arbitrary",)),
        name="router",
    )(logits)


def _dest_body(rank_ref, jm_ref, wm_ref, ps_ref, dest_ref, w4_ref):
    slot = rank_ref[...] + ps_ref[...]
    jm = jm_ref[...]
    wm = wm_ref[...]
    lane = lax.broadcasted_iota(I32, jm.shape, 1)
    dest = jnp.zeros(jm.shape, F32)
    w4 = jnp.zeros(jm.shape, F32)
    for j in range(TOP_K):
        sel = jm == float(j + 1)
        dj = jnp.sum(jnp.where(sel, slot, 0.0), axis=-1, keepdims=True)
        wj = jnp.sum(jnp.where(sel, wm, 0.0), axis=-1, keepdims=True)
        dest = jnp.where(lane == j, dj, dest)
        w4 = jnp.where(lane == j, wj, w4)
    dest_ref[...] = dest.astype(I32)
    w4_ref[...] = w4


def _dest_call(rank, jm, wm, ps_row, tm=512):
    n = rank.shape[0]
    tm = min(tm, n)
    tile = pl.BlockSpec((tm, LANES), lambda i: (i, 0))
    return pl.pallas_call(
        _dest_body,
        grid=(n // tm,),
        in_specs=[tile, tile, tile, pl.BlockSpec((1, LANES), lambda i: (0, 0))],
        out_specs=[tile, tile],
        out_shape=[jax.ShapeDtypeStruct((n, LANES), I32), jax.ShapeDtypeStruct((n, LANES), F32)],
        compiler_params=_cparams(("parallel",)),
        name="dest",
    )(rank, jm, wm, ps_row)


ROW_UNROLL = 8
SLOT_UNROLL = 16


def _build_slot_table(dest_ref, cnt_ref, pst_ref, slot_ref, n_tok):
    def pads(e, carry):
        base = pst_ref[e] + cnt_ref[e]

        def zeros(i, c2):
            for u in range(SLOT_UNROLL):
                slot_ref[base + i * SLOT_UNROLL + u] = 0
            return c2
        lax.fori_loop(0, SLOT_BLOCK // SLOT_UNROLL, zeros, 0)
        return carry
    lax.fori_loop(0, N_EXPERTS, pads, 0)

    per_trip = SLOT_UNROLL // TOP_K
    assert n_tok % per_trip == 0

    def toks(i, carry):
        for u in range(per_trip):
            t = i * per_trip + u
            for j in range(TOP_K):
                slot_ref[dest_ref[t * TOP_K + j]] = t
        return carry
    lax.fori_loop(0, n_tok // per_trip, toks, 0)


def _expert_body(gexp_ref, gsub0_ref, gns_ref, dest_ref, cnt_ref, pst_ref,
                 hp_hbm, w1g_ref, w1l_ref, b1g_ref, b1l_ref, w2_ref, b2_ref,
                 outs_hbm,
                 stage, acc, slot_ref, sem_in, sem_out, *, n_chunks, n_groups, n_tok):
    g = pl.program_id(0)
    c = pl.program_id(1)
    ns = gns_ref[g]
    s0 = gsub0_ref[g]
    sub = SLOT_BLOCK
    half = w1g_ref.shape[0] // 2
    last = n_chunks - 1
    cur = g % 2
    pieces = half // LANES

    def row_copy(tok, buf, r):
        return pltpu.make_async_copy(hp_hbm.at[pl.ds(pl.multiple_of(tok * pieces, pieces), pieces), :],
                                     stage.at[buf, pl.ds(pl.multiple_of(r * pieces, pieces), pieces), :],
                                     sem_in.at[buf])

    def issue_rows(gi, buf):
        base = gsub0_ref[gi] * sub

        def trip(i, carry):
            for u in range(ROW_UNROLL):
                r = i * ROW_UNROLL + u
                row_copy(slot_ref[base + r], buf, r).start()
            return carry
        lax.fori_loop(0, gns_ref[gi] * (sub // ROW_UNROLL), trip, 0)

    def out_copy(r0, rows):
        row0 = s0 * sub + r0
        return pltpu.make_async_copy(acc.at[pl.ds(r0, rows), :], outs_hbm.at[pl.ds(row0, rows), :], sem_out)

    def compute(r0, rows):
        los, his = [], []
        for j in range(pieces):
            w = stage[cur, pl.ds(r0 * pieces + j, rows, stride=pieces), :]
            los.append(lax.bitcast_convert_type(w << 16, F32).astype(BF16))
            his.append(lax.bitcast_convert_type(w & jnp.uint32(0xFFFF0000), F32).astype(BF16))
        xl = jnp.concatenate(los, axis=1)
        xh = jnp.concatenate(his, axis=1)
        hg =(jnp.dot(xl, w1g_ref[:half, :].astype(BF16), preferred_element_type=F32)
              + jnp.dot(xh, w1g_ref[half:, :].astype(BF16), preferred_element_type=F32) + b1g_ref[...])
        hl = (jnp.dot(xl, w1l_ref[:half, :].astype(BF16), preferred_element_type=F32)
              + jnp.dot(xh, w1l_ref[half:, :].astype(BF16), preferred_element_type=F32) + b1l_ref[...])
        gate = jnp.minimum(hg, SWIGLU_LIMIT)
        lin = jnp.clip(hl, -SWIGLU_LIMIT, SWIGLU_LIMIT)
        act = gate * jax.nn.sigmoid(SWIGLU_ALPHA * gate) * (lin + 1.0)
        acc[pl.ds(r0, rows), :] += jnp.dot(act.astype(BF16), w2_ref[...].astype(BF16),
                                           preferred_element_type=F32)

        @pl.when(c == last)
        def _():
            out_copy(r0, rows).start()

    @pl.when(ns > 0)
    def _():
        @pl.when(c == 0)
        def _load():
            @pl.when(g == 0)
            def _():
                _build_slot_table(dest_ref, cnt_ref, pst_ref, slot_ref, n_tok)
                issue_rows(0, 0)

            def wt(i, carry):
                for u in range(ROW_UNROLL):
                    row_copy(0, cur, i * ROW_UNROLL + u).wait()
                return carry
            lax.fori_loop(0, ns * (sub // ROW_UNROLL), wt, 0)

            nxt = jnp.minimum(g + 1, n_groups - 1)

            @pl.when(jnp.logical_and(g + 1 < n_groups, gns_ref[nxt] > 0))
            def _():
                issue_rows(nxt, 1 - cur)

            def init(s, carry):
                acc[pl.ds(pl.multiple_of(s * sub, sub), sub), :] = jnp.broadcast_to(
                    b2_ref[...], (sub, acc.shape[1]))
                return carry
            lax.fori_loop(0, ns, init, 0)

        def for_each_trip(fn):
            full = TRIP_SUBS * sub

            def trip(i, carry):
                fn(pl.multiple_of(i * full, full), full)
                return carry
            lax.fori_loop(0, ns // TRIP_SUBS, trip, 0)
            r0 = (ns // TRIP_SUBS) * full
            width = full // 2
            while width >= sub:
                has = (ns * sub) & width

                @pl.when(has != 0)
                def _(r0=r0, width=width):
                    fn(pl.multiple_of(r0, width), width)
                r0 = r0 + has
                width //= 2

        for_each_trip(compute)

        @pl.when(c == last)
        def _drain():
            for_each_trip(lambda r0, rows: out_copy(r0, rows).wait())

            nxt = jnp.minimum(g + 1, n_groups - 1)

            @pl.when(jnp.logical_or(g + 1 >= n_groups, gns_ref[nxt] == 0))
            def _tail():
                acc[pl.ds(0, sub), :] = jnp.zeros((sub, acc.shape[1]), F32)
                first = s0 + ns
                n_tail = outs_hbm.shape[0] // sub - first

                def tail_copy(i):
                    row0 = pl.multiple_of((first + i) * sub, sub)
                    return pltpu.make_async_copy(acc.at[pl.ds(0, sub), :], outs_hbm.at[pl.ds(row0, sub), :],
                                                 sem_out)

                def st(i, carry):
                    tail_copy(i).start()
                    return carry

                def wt(i, carry):
                    tail_copy(i).wait()
                    return carry
                lax.fori_loop(0, n_tail, st, 0)
                lax.fori_loop(0, n_tail, wt, 0)


def _expert_call(gexp, gsub0, gns, dest_flat, counts, pstart, n_slots, hp, w1, b1, w2, b2):
    half = w1.shape[1] // 2
    n_exp, d, ff2 = w1.shape
    ff = ff2 // 2
    tc = FF_CHUNK
    n_chunks = ff // tc
    n_groups = gexp.shape[0]
    rows = GROUP_SUBS * SLOT_BLOCK

    def chunk(c, gn, g):
        return jnp.where(gn[g] > 0, c, n_chunks - 1)

    grid_spec = pltpu.PrefetchScalarGridSpec(
        num_scalar_prefetch=6,
        grid=(n_groups, n_chunks),
        in_specs=[
            pl.BlockSpec(memory_space=pl.ANY),
            pl.BlockSpec((None, d, tc), lambda g, c, ge, gs, gn, *_: (ge[g], 0, chunk(c, gn, g))),
            pl.BlockSpec((None, d, tc), lambda g, c, ge, gs, gn, *_: (ge[g], 0, n_chunks + chunk(c, gn, g))),
            pl.BlockSpec((None, 1, tc), lambda g, c, ge, gs, gn, *_: (ge[g], 0, chunk(c, gn, g))),
            pl.BlockSpec((None, 1, tc), lambda g, c, ge, gs, gn, *_: (ge[g], 0, n_chunks + chunk(c, gn, g))),
            pl.BlockSpec((None, tc, d), lambda g, c, ge, gs, gn, *_: (ge[g], chunk(c, gn, g), 0)),
            pl.BlockSpec((None, 1, d), lambda g, c, ge, gs, gn, *_: (ge[g], 0, 0)),
        ],
        out_specs=pl.BlockSpec(memory_space=pl.ANY),
        scratch_shapes=[
            pltpu.VMEM((2, rows * (half // LANES), LANES), U32),
            pltpu.VMEM((rows, d), F32),
            pltpu.SMEM((n_slots + SLOT_BLOCK,), I32),
            pltpu.SemaphoreType.DMA((2,)), pltpu.SemaphoreType.DMA(()),
        ],
    )
    return pl.pallas_call(
        functools.partial(_expert_body, n_chunks=n_chunks, n_groups=n_groups,
                          n_tok=dest_flat.shape[0] // TOP_K),
        grid_spec=grid_spec,
        out_shape=jax.ShapeDtypeStruct((n_slots, d), F32),
        compiler_params=_cparams(("arbitrary", "arbitrary")),
        name="expert",
    )(gexp, gsub0, gns, dest_flat, counts, pstart, hp, w1, w1, b1.reshape(n_exp, 1, ff2),
      b1.reshape(n_exp, 1, ff2), w2, b2.reshape(n_exp, 1, d))


def _final_body(dest_ref, x1_ref, w4_ref, p_ref, wple_ref, wpg_ref, bpg_ref, npl_ref, outs_hbm,
                o_ref, gbuf, sems, *, tm, n_steps):
    i = pl.program_id(0)
    cur = i % 2

    def row_copy(slot, buf, j, q, u):
        return pltpu.make_async_copy(outs_hbm.at[pl.ds(slot, 1), :], gbuf.at[buf, j, q, pl.ds(u, 1), :],
                                     sems.at[buf])

    def issue(step, buf):
        def rows(q, carry):
            for u in range(SUBLANES):
                t = step * tm + q * SUBLANES + u
                for j in range(TOP_K):
                    row_copy(dest_ref[t * TOP_K + j], buf, j, q, u).start(priority=j % 2)
            return carry
        lax.fori_loop(0, tm // SUBLANES, rows, 0)

    @pl.when(i == 0)
    def _():
        issue(0, 0)

    @pl.when(i + 1 < n_steps)
    def _():
        issue(i + 1, 1 - cur)

    pe = jnp.dot(p_ref[...].astype(BF16), wple_ref[...], preferred_element_type=F32)

    def wt(q, carry):
        for u in range(SUBLANES):
            for j in range(TOP_K):
                row_copy(0, cur, j, q, u).wait()
        return carry
    lax.fori_loop(0, tm // SUBLANES, wt, 0)

    w4 = w4_ref[...]
    d = o_ref.shape[1]
    y = gbuf[cur, 0].reshape(tm, d) * w4[:, 0:1]
    for j in range(1, TOP_K):
        y = y + gbuf[cur, j].reshape(tm, d) * w4[:, j:j + 1]
    x2 = x1_ref[...] + y
    hn = _rmsnorm_rows(x2, npl_ref[...])
    gate = jax.nn.sigmoid(jnp.dot(hn.astype(BF16), wpg_ref[...], preferred_element_type=F32) + bpg_ref[...])
    o_ref[...] = x2 + gate * pe


def _final_call(dest_flat, x1, w4, p2d, wple, wpg, bpg_row, npl_row, outs, tm=256):
    n, d = x1.shape
    tm = min(tm, n)
    pd = p2d.shape[1]
    res = lambda shape: pl.BlockSpec(shape, lambda i, dst: (0,) * len(shape), pipeline_mode=pl.Buffered(1))
    grid_spec = pltpu.PrefetchScalarGridSpec(
        num_scalar_prefetch=1,
        grid=(n // tm,),
        in_specs=[pl.BlockSpec((tm, d), lambda i, dst: (i, 0)),
                  pl.BlockSpec((tm, LANES), lambda i, dst: (i, 0)),
                  pl.BlockSpec((tm, pd), lambda i, dst: (i, 0)),
                  res((pd, d)), res((d, d)), res((1, d)), res((1, d)),
                  pl.BlockSpec(memory_space=pl.ANY)],
        out_specs=pl.BlockSpec((tm, d), lambda i, dst: (i, 0)),
        scratch_shapes=[pltpu.VMEM((2, TOP_K, tm // SUBLANES, SUBLANES, d), F32),
                        pltpu.SemaphoreType.DMA((2,))],
    )
    return pl.pallas_call(
        functools.partial(_final_body, tm=tm, n_steps=n // tm),
        grid_spec=grid_spec,
        out_shape=jax.ShapeDtypeStruct((n, d), F32),
        compiler_params=_cparams(("arbitrary",)),
        name="final",
    )(dest_flat, x1, w4, p2d, wple, wpg, bpg_row, npl_row, outs)


def _t5_bucket(rel):
    n = -rel
    nb = REL_BUCKETS // 2
    ret = jnp.where(n < 0, nb, 0)
    n = jnp.abs(n)
    max_exact = nb // 2
    large = max_exact + (jnp.log(jnp.maximum(n, 1).astype(jnp.float32) / max_exact)
                         / math.log(REL_MAX_DIST / max_exact) * (nb - max_exact)).astype(jnp.int32)
    large = jnp.minimum(large, nb - 1)
    return ret + jnp.where(n < max_exact, n, large)


def _bias_body(rv_ref, o_ref, *, t):
    x = jnp.broadcast_to(rv_ref[...], (t, rv_ref.shape[1]))
    o_ref[...] = pltpu.roll(x, 1, 1, stride=1, stride_axis=0)[:, t:]


def _diff_bias_table(rel_bias, seq, t):
    rel = jnp.arange(seq + t) - (seq - 1)
    rv = (jnp.transpose(rel_bias[_t5_bucket(rel)], (1, 0)).astype(F32) * LOG2E)[:, None, :]
    return pl.pallas_call(
        functools.partial(_bias_body, t=t),
        grid=(N_HEADS,),
        in_specs=[pl.BlockSpec((None, 1, seq + t), lambda h: (h, 0, 0))],
        out_specs=pl.BlockSpec((None, t, seq), lambda h: (h, 0, 0)),
        out_shape=jax.ShapeDtypeStruct((N_HEADS, t, seq), F32),
        compiler_params=_cparams(("parallel",)),
        name="bias_table",
    )(rv)


def _pad_cols(a, width):
    return jnp.pad(a, ((0, 0), (0, width - a.shape[1])))


def _group_table(counts):
    n_sub = (counts + SLOT_BLOCK - 1) // SLOT_BLOCK
    sub_start = jnp.cumsum(n_sub) - n_sub
    n_grp = (n_sub + GROUP_SUBS - 1) // GROUP_SUBS
    grp_end = jnp.cumsum(n_grp)
    total = grp_end[-1]
    return n_sub, sub_start, n_grp, grp_end, total


def _layer(x, p_l, w_in, b_gate, b_forget, dq_norm, dk_norm, fq_norm, fk_norm, lq1, lk1, lq2, lk2,
           lambda_init, subln, w_up_a, w_up_b, w_out, rel_bias, norm_mix, norm_moe, w_router,
           b_router, w1, b1, w2, b2, norm_ple, w_ple_gate, b_ple_gate, w_ple):
    b, s, d = x.shape
    n = b * s
    width = N_HEADS * HEAD_DIM
    t = min(ATTN_T, s)
    x2d = x.reshape(n, d)
    row = lambda v: v.reshape(1, -1).astype(F32)

    f_row = 6 * width
    wt_in = jnp.swapaxes(w_in, 0, 1)
    h, f_logit = _norm_call(x2d, row(norm_mix), wt_in, f_row)
    cum = _cum_call(f_logit, _pad_cols(row(b_forget), LANES), b, s)
    cum_bhs = jnp.transpose(cum[:, :N_HEADS].reshape(b, s, N_HEADS), (0, 2, 1))
    cum_row = cum_bhs[:, :, None, :]

    diff_scale = DIFF_QK_DIM ** -0.5 * LOG2E
    fox_scale = HEAD_DIM ** -0.5 * LOG2E
    gain_d = jnp.concatenate([jnp.tile(dq_norm * diff_scale, 2 * N_HEADS), jnp.tile(dk_norm, 2 * N_HEADS)])
    gain_f = jnp.concatenate([jnp.tile(fq_norm * fox_scale, N_HEADS), jnp.tile(fk_norm, N_HEADS)])
    zeros_w = jnp.zeros((1, width), F32)
    dqk = _proj_call(h, wt_in, 0, 2 * width, row(gain_d), "norm64", "proj_dqk")
    dv = _proj_call(h, wt_in, 2 * width, width, zeros_w, "plain", "proj_dv")
    fqk = _proj_call(h, wt_in, 3 * width, 2 * width, row(gain_f), "norm128", "proj_fqk")
    fv = _proj_call(h, wt_in, 5 * width, width, zeros_w, "plain", "proj_fv")
    gates = _proj_call(h, wt_in, f_row + N_HEADS, 2 * d, row(b_gate), "gate", "proj_gate")

    lam_rows = [row(v) for v in (lq1, lk1, lq2, lk2)]
    bias = _diff_bias_table(rel_bias, s, t)
    od = _diff_attn_call(dqk.reshape(b, s, 2 * width), dv.reshape(b, s, width), bias, lam_rows,
                         row(subln), lambda_init, t)
    of = _fox_attn_call(fqk.reshape(b, s, 2 * width), fv.reshape(b, s, width), cum.reshape(b, s, LANES),
                        cum_row, t)

    br_pad = jnp.full((1, LANES), NEG, F32).at[0, :N_EXPERTS].set(b_router.astype(F32))
    wr_pad = jnp.pad(jnp.swapaxes(w_router.astype(F32), 0, 1), ((0, LANES - N_EXPERTS), (0, 0)))
    wr_hi = wr_pad.astype(BF16)
    wr_lo = (wr_pad - wr_hi.astype(F32)).astype(BF16)
    x1, hp, logits = _mix_call(od.reshape(n, width), of.reshape(n, width), gates, x2d,
                               w_up_a.astype(BF16), w_up_b.astype(BF16), w_out.astype(BF16),
                               row(norm_moe), wr_hi, wr_lo, br_pad)

    rank, jm, wm, cnt = _router_call(logits)
    counts = cnt[0, :N_EXPERTS].astype(I32)
    padded = ((counts + SLOT_BLOCK - 1) // SLOT_BLOCK) * SLOT_BLOCK
    pstart = jnp.cumsum(padded) - padded
    ps_row = _pad_cols(pstart.astype(F32).reshape(1, -1), LANES)
    dest128, w4 = _dest_call(rank, jm, wm, ps_row)
    dest_flat = dest128[:, :TOP_K].reshape(-1)

    n_slots = n * TOP_K + N_EXPERTS * SLOT_BLOCK

    n_sub, sub_start, n_grp, grp_end, total = _group_table(counts)
    max_groups = N_EXPERTS + (n_slots // SLOT_BLOCK) // GROUP_SUBS
    gidx = jnp.arange(max_groups, dtype=I32)
    gvalid = gidx < total
    gsafe = jnp.minimum(gidx, total - 1)
    gexp = jnp.searchsorted(grp_end, gsafe, side="right").astype(I32)
    kth = gsafe - (grp_end - n_grp)[gexp]
    gsub0 = (sub_start[gexp] + kth * GROUP_SUBS).astype(I32)
    gns = jnp.where(gvalid, jnp.minimum(GROUP_SUBS, n_sub[gexp] - kth * GROUP_SUBS), 0).astype(I32)
    outs = _expert_call(gexp, gsub0, gns, dest_flat, counts, pstart, n_slots, hp, w1, b1, w2, b2)

    out = _final_call(dest_flat, x1, w4, p_l.reshape(n, -1), w_ple.astype(BF16),
                      w_ple_gate.astype(BF16), row(b_ple_gate), row(norm_ple), outs)
    return out.reshape(b, s, d)


def kernel(x, p, w_in, b_gate, b_forget, dq_norm, dk_norm, fq_norm, fk_norm, lambda_q1, lambda_k1,
           lambda_q2, lambda_k2, subln, w_up_a, w_up_b, w_out, rel_bias, norm_mix, norm_moe,
           w_router, b_router, w1, b1, w2, b2, norm_ple, w_ple_gate, b_ple_gate, w_ple):
    for i in range(w_in.shape[0]):
        lambda_init = 0.8 - 0.6 * math.exp(-0.3 * i)
        x = _layer(x, p[i], w_in[i], b_gate[i], b_forget[i], dq_norm[i], dk_norm[i], fq_norm[i],
                   fk_norm[i], lambda_q1[i], lambda_k1[i], lambda_q2[i], lambda_k2[i], lambda_init,
                   subln[i], w_up_a[i], w_up_b[i], w_out[i], rel_bias, norm_mix[i], norm_moe[i],
                   w_router[i], b_router[i], w1[i], b1[i], w2[i], b2[i], norm_ple[i],
                   w_ple_gate[i], b_ple_gate[i], w_ple[i])
    return x
```

```python
import functools
import math

import jax
import jax.numpy as jnp
from jax import lax
from jax.experimental import pallas as pl
from jax.experimental.pallas import tpu as pltpu

F32 = jnp.float32
BF16 = jnp.bfloat16
U32 = jnp.uint32
I32 = jnp.int32
HIGHEST = lax.Precision.HIGHEST

N_HEADS = 8
HEAD_DIM = 128
DIFF_QK_DIM = 64
CHUNK = 64
REL_BUCKETS = 32
REL_MAX_DIST = 128
N_EXPERTS = 32
TOP_K = 4
SWIGLU_LIMIT = 7.0
SWIGLU_ALPHA = 1.702
RMS_EPS = 1e-6
NEG = -1e30
LOG2E = math.log2(math.e)

LANES = 128
SUBLANES = 8
SLOT_BLOCK = 128
GROUP_SUBS = 12
TRIP_SUBS = 4
FF_CHUNK = 512
ATTN_T = 256
VMEM_LIMIT = 56 * 1024 * 1024


_NT = (((1,), (1,)), ((), ()))


def _cparams(sem):
    return pltpu.CompilerParams(dimension_semantics=sem, vmem_limit_bytes=VMEM_LIMIT)


def _rmsnorm_rows(x, g):
    ms = jnp.mean(x * x, axis=-1, keepdims=True)
    return x * lax.rsqrt(ms + RMS_EPS) * g


def _norm_body(x_ref, g_ref, wf_ref, h_ref, fl_ref):
    y = _rmsnorm_rows(x_ref[...], g_ref[...])
    y_hi = y.astype(BF16)
    h_ref[...] = y_hi
    rowi = lax.broadcasted_iota(I32, (LANES, 1), 0)
    wf = jnp.where(rowi < N_HEADS, wf_ref[...], 0.0)
    y_lo = (y - y_hi.astype(F32)).astype(BF16)
    wf_hi = wf.astype(BF16)
    wf_lo = (wf - wf_hi.astype(F32)).astype(BF16)
    fl_ref[...] = (lax.dot_general(y_hi, wf_hi, _NT, preferred_element_type=F32)
                   + lax.dot_general(y_lo, wf_hi, _NT, preferred_element_type=F32)
                   + lax.dot_general(y_hi, wf_lo, _NT, preferred_element_type=F32))


def _norm_call(x2d, g_row, wt_in, f_row, tm=512):
    n, d = x2d.shape
    tm = min(tm, n)
    assert f_row % LANES == 0
    return pl.pallas_call(
        _norm_body,
        grid=(n // tm,),
        in_specs=[pl.BlockSpec((tm, d), lambda i: (i, 0)),
                  pl.BlockSpec((1, d), lambda i: (0, 0)),
                  pl.BlockSpec((LANES, d), lambda i: (f_row // LANES, 0))],
        out_specs=[pl.BlockSpec((tm, d), lambda i: (i, 0)),
                   pl.BlockSpec((tm, LANES), lambda i: (i, 0))],
        out_shape=[jax.ShapeDtypeStruct((n, d), BF16),
                   jax.ShapeDtypeStruct((n, LANES), F32)],
        compiler_params=_cparams(("parallel",)),
        name="norm",
    )(x2d, g_row, wt_in)


def _cum_body(fl_ref, bf_ref, cum_ref):
    s = fl_ref.shape[0]
    r = lax.broadcasted_iota(I32, (LANES, LANES), 0)
    c = lax.broadcasted_iota(I32, (LANES, LANES), 1)
    tri = jnp.where(r >= c, 1.0, 0.0).astype(F32)
    carry = jnp.zeros((1, LANES), F32)
    for blk in range(s // LANES):
        rows = slice(blk * LANES, (blk + 1) * LANES)
        lf = jax.nn.log_sigmoid(fl_ref[rows, :] + bf_ref[...])
        cb = jnp.dot(tri, lf, precision=HIGHEST, preferred_element_type=F32) + carry
        cum_ref[rows, :] = cb * LOG2E
        carry = cb[LANES - 1:LANES, :]


def _cum_call(fl, bf_row, batch, seq):
    return pl.pallas_call(
        _cum_body,
        grid=(batch,),
        in_specs=[pl.BlockSpec((seq, LANES), lambda b: (b, 0)),
                  pl.BlockSpec((1, LANES), lambda b: (0, 0))],
        out_specs=pl.BlockSpec((seq, LANES), lambda b: (b, 0)),
        out_shape=jax.ShapeDtypeStruct(fl.shape, F32),
        compiler_params=_cparams(("parallel",)),
        name="cum",
    )(fl, bf_row)


def _proj_epilogue(acc, aux_ref, o_ref, kind):
    tn = acc.shape[1]
    if kind == "plain":
        o_ref[...] = acc.astype(o_ref.dtype)
    elif kind == "gate":
        o_ref[...] = jax.nn.sigmoid(acc + aux_ref[...]).astype(o_ref.dtype)
    else:
        lo = lax.broadcasted_iota(I32, (1, LANES), 1) < DIFF_QK_DIM
        for s in range(tn // LANES):
            cols = slice(s * LANES, (s + 1) * LANES)
            blk = acc[:, cols]
            sq = blk * blk
            if kind == "norm128":
                ms = jnp.mean(sq, axis=-1, keepdims=True)
            else:
                s_lo = jnp.sum(jnp.where(lo, sq, 0.0), axis=-1, keepdims=True)
                s_hi = jnp.sum(jnp.where(lo, 0.0, sq), axis=-1, keepdims=True)
                ms = jnp.where(lo, s_lo, s_hi) * (1.0 / DIFF_QK_DIM)
            o_ref[:, cols] = (blk * lax.rsqrt(ms + RMS_EPS) * aux_ref[:, cols]).astype(o_ref.dtype)


def _proj_body(h_ref, wt_ref, aux_ref, o_ref, *, kind):
    acc = lax.dot_general(h_ref[...], wt_ref[...].astype(BF16), _NT, preferred_element_type=F32)
    _proj_epilogue(acc, aux_ref, o_ref, kind)


def _proj_shift_body(h_ref, wa_ref, wb_ref, aux_ref, o_ref, *, kind, shift):
    wt = jnp.concatenate([wa_ref[shift:, :], wb_ref[...]], axis=0)
    acc = lax.dot_general(h_ref[...], wt.astype(BF16), _NT, preferred_element_type=F32)
    _proj_epilogue(acc, aux_ref, o_ref, kind)


def _proj_call(h, wt, row0, nrows, aux_row, kind, name, tm=1024, tn=1024):
    m, k = h.shape
    tm = min(tm, m)
    assert nrows % tn == 0 and m % tm == 0
    shift = row0 % tn
    base = row0 - shift
    h_spec = pl.BlockSpec((tm, k), lambda i, j: (i, 0))
    w_spec = pl.BlockSpec((tn, k), lambda i, j: (base // tn + j, 0))
    aux_spec = pl.BlockSpec((1, tn), lambda i, j: (0, j))
    if shift == 0:
        body, w_specs, ws = functools.partial(_proj_body, kind=kind), [w_spec], [wt]
    else:
        assert shift % SUBLANES == 0 and tn % shift == 0
        body = functools.partial(_proj_shift_body, kind=kind, shift=shift)
        w_specs = [w_spec, pl.BlockSpec((shift, k), lambda i, j: ((base + (j + 1) * tn) // shift, 0))]
        ws = [wt, wt]
    return pl.pallas_call(
        body,
        grid=(m // tm, nrows // tn),
        in_specs=[h_spec] + w_specs + [aux_spec],
        out_specs=pl.BlockSpec((tm, tn), lambda i, j: (i, j)),
        out_shape=jax.ShapeDtypeStruct((m, nrows), BF16),
        compiler_params=_cparams(("parallel", "parallel")),
        name=name,
    )(h, *ws, aux_row)


DIFF_HEADS_PER_STEP = 4
FOX_HEADS_PER_STEP = 8


def _softmax_step(s, m_ref, acc_ref, v1):
    m_old = m_ref[...]
    mn = jnp.maximum(m_old, jnp.max(s, axis=-1, keepdims=True))
    alpha = jnp.exp2(m_old - mn)
    p = jnp.exp2(s - jnp.tile(mn, (1, s.shape[1] // LANES))).astype(BF16)
    acc_ref[...] = (jnp.tile(alpha, (1, acc_ref.shape[1] // LANES)) * acc_ref[...]
                    + jnp.dot(p, v1, preferred_element_type=F32))
    m_ref[...] = mn


def _causal_sweep(step, qi, t):
    def pair(i, carry):
        step(2 * i, 2 * t, False)
        return carry
    lax.fori_loop(0, qi // 2, pair, 0)

    @pl.when(qi % 2 == 1)
    def _():
        step(qi - 1, t, False)
    step(qi, t, True)


def _with_ones(v):
    return jnp.concatenate([v, jnp.ones_like(v)], axis=1)


def _normalised(acc):
    return acc[:, :HEAD_DIM] / acc[:, HEAD_DIM:]


def _diff_attn_body(lq1_ref, lk1_ref, lq2_ref, lk2_ref, g_ref, q_ref, k_ref, v_ref, bias_ref,
                    o_ref, m_sc, acc_sc, *, t, nh, lambda_init):
    qi = pl.program_id(2)
    s_len = k_ref.shape[0]
    m_sc[...] = jnp.full(m_sc.shape, NEG, F32)
    acc_sc[...] = jnp.zeros(acc_sc.shape, F32)
    lo = lax.broadcasted_iota(I32, (t, HEAD_DIM), 1) < DIFF_QK_DIM

    def step(kb, w, masked):
        k0 = pl.multiple_of(kb * t, t)
        off = pl.multiple_of((kb - qi) * t + (s_len - t), LANES)
        if masked:
            r = lax.broadcasted_iota(I32, (t, w), 0)
            c = lax.broadcasted_iota(I32, (t, w), 1)
            ok = (c // CHUNK) <= (r // CHUNK)
        for h in range(nh):
            cols = slice(h * HEAD_DIM, (h + 1) * HEAD_DIM)
            q = q_ref[:, cols]
            zero = jnp.zeros_like(q)
            k = k_ref[pl.ds(k0, w), cols]
            v1 = _with_ones(v_ref[pl.ds(k0, w), cols])
            b = bias_ref[h, :, pl.ds(off, w)]
            for mp, qm in enumerate((jnp.where(lo, q, zero), jnp.where(lo, zero, q))):
                s = lax.dot_general(qm, k, _NT, preferred_element_type=F32) + b
                if masked:
                    s = jnp.where(ok, s, NEG)
                _softmax_step(s, m_sc.at[2 * h + mp], acc_sc.at[2 * h + mp], v1)

    _causal_sweep(step, qi, t)

    lam = (jnp.exp(jnp.sum(lq1_ref[...] * lk1_ref[...], axis=-1, keepdims=True))
           - jnp.exp(jnp.sum(lq2_ref[...] * lk2_ref[...], axis=-1, keepdims=True)) + lambda_init)
    for h in range(nh):
        o = _normalised(acc_sc[2 * h]) - lam * _normalised(acc_sc[2 * h + 1])
        o = _rmsnorm_rows(o, g_ref[...]) * (1.0 - lambda_init)
        o_ref[:, h * HEAD_DIM:(h + 1) * HEAD_DIM] = o.astype(o_ref.dtype)


def _diff_attn_call(qk, v, bias, lam_rows, subln_row, lambda_init, t):
    b, s, _ = v.shape
    nh = DIFF_HEADS_PER_STEP
    wb = nh * HEAD_DIM
    n_hg = N_HEADS // nh
    vec = lambda w: pl.BlockSpec((1, w), lambda hg, bb, qi: (0, 0))
    return pl.pallas_call(
        functools.partial(_diff_attn_body, t=t, nh=nh, lambda_init=lambda_init),
        grid=(n_hg, b, s // t),
        in_specs=[vec(DIFF_QK_DIM)] * 4 + [
            vec(HEAD_DIM),
            pl.BlockSpec((None, t, wb), lambda hg, bb, qi: (bb, qi, hg)),
            pl.BlockSpec((None, s, wb), lambda hg, bb, qi: (bb, 0, n_hg + hg)),
            pl.BlockSpec((None, s, wb), lambda hg, bb, qi: (bb, 0, hg)),
            pl.BlockSpec((nh, t, s), lambda hg, bb, qi: (hg, 0, 0))],
        out_specs=pl.BlockSpec((None, t, wb), lambda hg, bb, qi: (bb, qi, hg)),
        out_shape=jax.ShapeDtypeStruct(v.shape, BF16),
        scratch_shapes=[pltpu.VMEM((2 * nh, t, LANES), F32), pltpu.VMEM((2 * nh, t, 2 * HEAD_DIM), F32)],
        compiler_params=_cparams(("parallel", "parallel", "parallel")),
        name="diff_attn",
    )(*lam_rows, subln_row, qk, qk, v, bias)


def _fox_attn_body(q_ref, k_ref, v_ref, cq_ref, ck_ref, o_ref, m_sc, acc_sc, cq_sc, *, t, nh):
    qi = pl.program_id(2)
    m_sc[...] = jnp.full(m_sc.shape, NEG, F32)
    acc_sc[...] = jnp.zeros(acc_sc.shape, F32)
    hg = pl.program_id(0)
    lane = lax.broadcasted_iota(I32, (t, LANES), 1)
    cq_all = cq_ref[...]
    for h in range(nh):
        col = jnp.sum(jnp.where(lane == hg * nh + h, cq_all, 0.0), axis=-1, keepdims=True)
        cq_sc[h] = jnp.broadcast_to(col, (t, LANES))

    def step(kb, w, masked):
        k0 = pl.multiple_of(kb * t, t)
        if masked:
            r = lax.broadcasted_iota(I32, (t, w), 0)
            c = lax.broadcasted_iota(I32, (t, w), 1)
            ok = c <= r
        for h in range(nh):
            cols = slice(h * HEAD_DIM, (h + 1) * HEAD_DIM)
            k = k_ref[pl.ds(k0, w), cols]
            v1 = _with_ones(v_ref[pl.ds(k0, w), cols])
            s = lax.dot_general(q_ref[:, cols], k, _NT, preferred_element_type=F32)
            s = s + jnp.tile(cq_sc[h], (1, w // LANES)) - ck_ref[h, :, pl.ds(k0, w)]
            if masked:
                s = jnp.where(ok, s, NEG)
            _softmax_step(s, m_sc.at[h], acc_sc.at[h], v1)

    _causal_sweep(step, qi, t)
    for h in range(nh):
        o_ref[:, h * HEAD_DIM:(h + 1) * HEAD_DIM] = _normalised(acc_sc[h]).astype(o_ref.dtype)


def _fox_attn_call(qk, v, cum, cum_row, t):
    b, s, _ = v.shape
    nh = FOX_HEADS_PER_STEP
    wb = nh * HEAD_DIM
    n_hg = N_HEADS // nh
    return pl.pallas_call(
        functools.partial(_fox_attn_body, t=t, nh=nh),
        grid=(n_hg, b, s // t),
        in_specs=[pl.BlockSpec((None, t, wb), lambda hg, bb, qi: (bb, qi, hg)),
                  pl.BlockSpec((None, s, wb), lambda hg, bb, qi: (bb, 0, n_hg + hg)),
                  pl.BlockSpec((None, s, wb), lambda hg, bb, qi: (bb, 0, hg)),
                  pl.BlockSpec((None, t, LANES), lambda hg, bb, qi: (bb, qi, 0)),
                  pl.BlockSpec((None, nh, 1, s), lambda hg, bb, qi: (bb, hg, 0, 0))],
        out_specs=pl.BlockSpec((None, t, wb), lambda hg, bb, qi: (bb, qi, hg)),
        out_shape=jax.ShapeDtypeStruct(v.shape, BF16),
        scratch_shapes=[pltpu.VMEM((nh, t, LANES), F32), pltpu.VMEM((nh, t, 2 * HEAD_DIM), F32),
                        pltpu.VMEM((nh, t, LANES), F32)],
        compiler_params=_cparams(("parallel", "parallel", "parallel")),
        name="fox_attn",
    )(qk, qk, v, cum, cum_row)


def _mix_body(od_ref, of_ref, g_ref, x_ref, wa_ref, wb_ref, wo_ref, nm_ref, wrh_ref, wrl_ref, br_ref,
              x1_ref, hp_ref, lg_ref):
    d = x_ref.shape[1]
    ua = jnp.dot(od_ref[...], wa_ref[...], preferred_element_type=F32)
    ub = jnp.dot(of_ref[...], wb_ref[...], preferred_element_type=F32)
    mixed = g_ref[:, :d].astype(F32) * ua + g_ref[:, d:].astype(F32) * ub
    x1 = x_ref[...] + jnp.dot(mixed.astype(BF16), wo_ref[...], preferred_element_type=F32)
    x1_ref[...] = x1
    hm = _rmsnorm_rows(x1, nm_ref[...])
    h_hi = hm.astype(BF16)
    h_hi32 = h_hi.astype(F32)
    h_lo = (hm - h_hi32).astype(BF16)
    lg_ref[...] = (lax.dot_general(h_hi, wrh_ref[...], _NT, preferred_element_type=F32)
                   + lax.dot_general(h_lo, wrh_ref[...], _NT, preferred_element_type=F32)
                   + lax.dot_general(h_hi, wrl_ref[...], _NT, preferred_element_type=F32) + br_ref[...])
    bits = lax.bitcast_convert_type(h_hi32, U32)
    packed = (bits[:, :d // 2] >> 16) | (bits[:, d // 2:] & jnp.uint32(0xFFFF0000))
    tm = packed.shape[0]
    pieces = packed.shape[1] // LANES
    for j in range(pieces):
        hp_ref[pl.ds(j, tm, stride=pieces), :] = packed[:, j * LANES:(j + 1) * LANES]


def _resident(shape):
    return pl.BlockSpec(shape, lambda i: (0,) * len(shape), pipeline_mode=pl.Buffered(1))


def _mix_call(od, of, gates, x2d, wa, wb, wo, nm_row, wr_hi, wr_lo, br_pad, tm=256):
    n, d = x2d.shape
    tm = min(tm, n)
    wdt = od.shape[1]
    return pl.pallas_call(
        _mix_body,
        grid=(n // tm,),
        in_specs=[pl.BlockSpec((tm, wdt), lambda i: (i, 0)),
                  pl.BlockSpec((tm, wdt), lambda i: (i, 0)),
                  pl.BlockSpec((tm, 2 * d), lambda i: (i, 0)),
                  pl.BlockSpec((tm, d), lambda i: (i, 0)),
                  _resident((wdt, d)), _resident((wdt, d)), _resident((d, d)),
                  _resident((1, d)), _resident((LANES, d)), _resident((LANES, d)), _resident((1, LANES))],
        out_specs=[pl.BlockSpec((tm, d), lambda i: (i, 0)),
                   pl.BlockSpec((tm * (d // 2 // LANES), LANES), lambda i: (i, 0)),
                   pl.BlockSpec((tm, LANES), lambda i: (i, 0))],
        out_shape=[jax.ShapeDtypeStruct((n, d), F32),
                   jax.ShapeDtypeStruct((n * (d // 2 // LANES), LANES), U32),
                   jax.ShapeDtypeStruct((n, LANES), F32)],
        compiler_params=_cparams(("parallel",)),
        name="mix",
    )(od, of, gates, x2d, wa, wb, wo, nm_row, wr_hi, wr_lo, br_pad)


def _router_body(lg_ref, rank_ref, jm_ref, wm_ref, cnt_ref, carry_ref):
    i = pl.program_id(0)

    @pl.when(i == 0)
    def _():
        carry_ref[...] = jnp.zeros_like(carry_ref)

    vals = lg_ref[...]
    tm = vals.shape[0]
    lane = lax.broadcasted_iota(I32, vals.shape, 1).astype(F32)
    jm = jnp.zeros(vals.shape, F32)
    tops = []
    for j in range(TOP_K):
        m = jnp.max(vals, axis=-1, keepdims=True)
        idx = jnp.min(jnp.where(vals == m, lane, float(LANES)), axis=-1, keepdims=True)
        sel = lane == idx
        jm = jnp.where(sel, float(j + 1), jm)
        vals = jnp.where(sel, -jnp.inf, vals)
        tops.append(m)
    es = [jnp.exp(m - tops[0]) for m in tops]
    den = es[0] + es[1] + es[2] + es[3]
    wm = jnp.zeros(vals.shape, F32)
    for j in range(TOP_K):
        wm = jnp.where(jm == float(j + 1), es[j] / den, wm)
    sel_any = jnp.where(jm > 0.0, 1.0, 0.0)
    r = lax.broadcasted_iota(I32, (tm, tm), 0)
    c = lax.broadcasted_iota(I32, (tm, tm), 1)
    tri = jnp.where(c < r, 1.0, 0.0).astype(BF16)
    carry = carry_ref[...]
    rank_ref[...] = jnp.dot(tri, sel_any.astype(BF16), preferred_element_type=F32) + carry
    jm_ref[...] = jm
    wm_ref[...] = wm
    carry = carry + jnp.sum(sel_any, axis=0, keepdims=True)
    carry_ref[...] = carry
    cnt_ref[...] = carry


def _router_call(logits, tm=256):
    n = logits.shape[0]
    tm = min(tm, n)
    tile = pl.BlockSpec((tm, LANES), lambda i: (i, 0))
    row = pl.BlockSpec((1, LANES), lambda i: (0, 0))
    return pl.pallas_call(
        _router_body,
        grid=(n // tm,),
        in_specs=[tile],
        out_specs=[tile, tile, tile, row],
        out_shape=[jax.ShapeDtypeStruct((n, LANES), F32)] * 3 + [jax.ShapeDtypeStruct((1, LANES), F32)],
        scratch_shapes=[pltpu.VMEM((1, LANES), F32)],
        compiler_params=_cparams(("arbitrary",)),
        name="router",
    )(logits)


def _dest_body(rank_ref, jm_ref, wm_ref, ps_ref, dest_ref, w4_ref):
    slot = rank_ref[...] + ps_ref[...]
    jm = jm_ref[...]
    wm = wm_ref[...]
    lane = lax.broadcasted_iota(I32, jm.shape, 1)
    dest = jnp.zeros(jm.shape, F32)
    w4 = jnp.zeros(jm.shape, F32)
    for j in range(TOP_K):
        sel = jm == float(j + 1)
        dj = jnp.sum(jnp.where(sel, slot, 0.0), axis=-1, keepdims=True)
        wj = jnp.sum(jnp.where(sel, wm, 0.0), axis=-1, keepdims=True)
        dest = jnp.where(lane == j, dj, dest)
        w4 = jnp.where(lane == j, wj, w4)
    dest_ref[...] = dest.astype(I32)
    w4_ref[...] = w4


def _dest_call(rank, jm, wm, ps_row, tm=512):
    n = rank.shape[0]
    tm = min(tm, n)
    tile = pl.BlockSpec((tm, LANES), lambda i: (i, 0))
    return pl.pallas_call(
        _dest_body,
        grid=(n // tm,),
        in_specs=[tile, tile, tile, pl.BlockSpec((1, LANES), lambda i: (0, 0))],
        out_specs=[tile, tile],
        out_shape=[jax.ShapeDtypeStruct((n, LANES), I32), jax.ShapeDtypeStruct((n, LANES), F32)],
        compiler_params=_cparams(("parallel",)),
        name="dest",
    )(rank, jm, wm, ps_row)


ROW_UNROLL = 8
SLOT_UNROLL = 16


def _build_slot_table(dest_ref, cnt_ref, pst_ref, slot_ref, n_tok):
    def pads(e, carry):
        base = pst_ref[e] + cnt_ref[e]

        def zeros(i, c2):
            for u in range(SLOT_UNROLL):
                slot_ref[base + i * SLOT_UNROLL + u] = 0
            return c2
        lax.fori_loop(0, SLOT_BLOCK // SLOT_UNROLL, zeros, 0)
        return carry
    lax.fori_loop(0, N_EXPERTS, pads, 0)

    per_trip = SLOT_UNROLL // TOP_K
    assert n_tok % per_trip == 0

    def toks(i, carry):
        for u in range(per_trip):
            t = i * per_trip + u
            for j in range(TOP_K):
                slot_ref[dest_ref[t * TOP_K + j]] = t
        return carry
    lax.fori_loop(0, n_tok // per_trip, toks, 0)


def _expert_body(gexp_ref, gsub0_ref, gns_ref, dest_ref, cnt_ref, pst_ref,
                 hp_hbm, w1g_ref, w1l_ref, b1g_ref, b1l_ref, w2_ref, b2_ref,
                 outs_hbm,
                 stage, acc, slot_ref, sem_in, sem_out, *, n_chunks, n_groups, n_tok):
    g = pl.program_id(0)
    c = pl.program_id(1)
    ns = gns_ref[g]
    s0 = gsub0_ref[g]
    sub = SLOT_BLOCK
    half = w1g_ref.shape[0] // 2
    last = n_chunks - 1
    cur = g % 2
    pieces = half // LANES

    def row_copy(tok, buf, r):
        return pltpu.make_async_copy(hp_hbm.at[pl.ds(pl.multiple_of(tok * pieces, pieces), pieces), :],
                                     stage.at[buf, pl.ds(pl.multiple_of(r * pieces, pieces), pieces), :],
                                     sem_in.at[buf])

    def issue_rows(gi, buf, blk0, n_blk):
        base = gsub0_ref[gi] * sub

        def trip(i, carry):
            for u in range(ROW_UNROLL):
                r = blk0 * sub + i * ROW_UNROLL + u
                row_copy(slot_ref[base + r], buf, r).start()
            return carry
        lax.fori_loop(0, n_blk * (sub // ROW_UNROLL), trip, 0)

    nxt = jnp.minimum(g + 1, n_groups - 1)
    need = jnp.where(jnp.logical_and(g + 1 < n_groups, gns_ref[nxt] > 0), gns_ref[nxt], 0)
    nxt_base = gsub0_ref[nxt] * sub

    def issue_block(blk):
        for u in range(sub):
            r = blk * sub + u
            row_copy(slot_ref[nxt_base + r], 1 - cur, r).start()

    def out_copy(r0, rows):
        row0 = s0 * sub + r0
        return pltpu.make_async_copy(acc.at[pl.ds(r0, rows), :], outs_hbm.at[pl.ds(row0, rows), :], sem_out)

    def compute(r0, rows, issue_blk=None):
        los, his = [], []
        for j in range(pieces):
            w = stage[cur, pl.ds(r0 * pieces + j, rows, stride=pieces), :]
            los.append(lax.bitcast_convert_type(w << 16, F32).astype(BF16))
            his.append(lax.bitcast_convert_type(w & jnp.uint32(0xFFFF0000), F32).astype(BF16))
        xl = jnp.concatenate(los, axis=1)
        xh = jnp.concatenate(his, axis=1)
        hg = (jnp.dot(xl, w1g_ref[:half, :].astype(BF16), preferred_element_type=F32)
              + jnp.dot(xh, w1g_ref[half:, :].astype(BF16), preferred_element_type=F32) + b1g_ref[...])
        hl = (jnp.dot(xl, w1l_ref[:half, :].astype(BF16), preferred_element_type=F32)
              + jnp.dot(xh, w1l_ref[half:, :].astype(BF16), preferred_element_type=F32) + b1l_ref[...])
        gate = jnp.minimum(hg, SWIGLU_LIMIT)
        lin = jnp.clip(hl, -SWIGLU_LIMIT, SWIGLU_LIMIT)
        act = gate * jax.nn.sigmoid(SWIGLU_ALPHA * gate) * (lin + 1.0)
        acc[pl.ds(r0, rows), :] += jnp.dot(act.astype(BF16), w2_ref[...].astype(BF16),
                                           preferred_element_type=F32)
        if issue_blk is not None:
            issue_block(issue_blk)

        @pl.when(c == last)
        def _():
            out_copy(r0, rows).start()

    @pl.when(ns > 0)
    def _():
        @pl.when(c == 0)
        def _load():
            @pl.when(g == 0)
            def _():
                _build_slot_table(dest_ref, cnt_ref, pst_ref, slot_ref, n_tok)
                issue_rows(0, 0, 0, ns)

            def wt(i, carry):
                for u in range(ROW_UNROLL):
                    row_copy(0, cur, i * ROW_UNROLL + u).wait()
                return carry
            lax.fori_loop(0, ns * (sub // ROW_UNROLL), wt, 0)

            def init(s, carry):
                acc[pl.ds(pl.multiple_of(s * sub, sub), sub), :] = jnp.broadcast_to(
                    b2_ref[...], (sub, acc.shape[1]))
                return carry
            lax.fori_loop(0, ns, init, 0)

        full = TRIP_SUBS * sub
        n_full = ns // TRIP_SUBS

        def for_each_trip(fn, fn_full=None):
            def trip(i, carry):
                fn(pl.multiple_of(i * full, full), full)
                return carry
            if fn_full is None:
                lax.fori_loop(0, n_full, trip, 0)
            else:
                fn_full()
            r0 = n_full * full
            width = full // 2
            while width >= sub:
                has = (ns * sub) & width

                @pl.when(has != 0)
                def _(r0=r0, width=width):
                    fn(pl.multiple_of(r0, width), width)
                r0 = r0 + has
                width //= 2

        def full_trips():
            n_issue = jnp.clip(need - c * n_full, 0, n_full)

            def with_issue(i, carry):
                compute(pl.multiple_of(i * full, full), full, issue_blk=c * n_full + i)
                return carry

            def plain(i, carry):
                compute(pl.multiple_of(i * full, full), full)
                return carry
            lax.fori_loop(0, n_issue, with_issue, 0)
            lax.fori_loop(n_issue, n_full, plain, 0)

        for_each_trip(compute, full_trips)

        @pl.when(c == last)
        def _drain():
            done = jnp.minimum(need, n_chunks * n_full)
            issue_rows(nxt, 1 - cur, done, need - done)
            for_each_trip(lambda r0, rows: out_copy(r0, rows).wait())

            @pl.when(jnp.logical_or(g + 1 >= n_groups, gns_ref[nxt] == 0))
            def _tail():
                acc[pl.ds(0, sub), :] = jnp.zeros((sub, acc.shape[1]), F32)
                first = s0 + ns
                n_tail = outs_hbm.shape[0] // sub - first

                def tail_copy(i):
                    row0 = pl.multiple_of((first + i) * sub, sub)
                    return pltpu.make_async_copy(acc.at[pl.ds(0, sub), :], outs_hbm.at[pl.ds(row0, sub), :],
                                                 sem_out)

                def st(i, carry):
                    tail_copy(i).start()
                    return carry

                def wt(i, carry):
                    tail_copy(i).wait()
                    return carry
                lax.fori_loop(0, n_tail, st, 0)
                lax.fori_loop(0, n_tail, wt, 0)


def _expert_call(gexp, gsub0, gns, dest_flat, counts, pstart, n_slots, hp, w1, b1, w2, b2):
    half = w1.shape[1] // 2
    n_exp, d, ff2 = w1.shape
    ff = ff2 // 2
    tc = FF_CHUNK
    n_chunks = ff // tc
    n_groups = gexp.shape[0]
    rows = GROUP_SUBS * SLOT_BLOCK

    def chunk(c, gn, g):
        return jnp.where(gn[g] > 0, c, n_chunks - 1)

    grid_spec = pltpu.PrefetchScalarGridSpec(
        num_scalar_prefetch=6,
        grid=(n_groups, n_chunks),
        in_specs=[
            pl.BlockSpec(memory_space=pl.ANY),
            pl.BlockSpec((None, d, tc), lambda g, c, ge, gs, gn, *_: (ge[g], 0, chunk(c, gn, g))),
            pl.BlockSpec((None, d, tc), lambda g, c, ge, gs, gn, *_: (ge[g], 0, n_chunks + chunk(c, gn, g))),
            pl.BlockSpec((None, 1, tc), lambda g, c, ge, gs, gn, *_: (ge[g], 0, chunk(c, gn, g))),
            pl.BlockSpec((None, 1, tc), lambda g, c, ge, gs, gn, *_: (ge[g], 0, n_chunks + chunk(c, gn, g))),
            pl.BlockSpec((None, tc, d), lambda g, c, ge, gs, gn, *_: (ge[g], chunk(c, gn, g), 0)),
            pl.BlockSpec((None, 1, d), lambda g, c, ge, gs, gn, *_: (ge[g], 0, 0)),
        ],
        out_specs=pl.BlockSpec(memory_space=pl.ANY),
        scratch_shapes=[
            pltpu.VMEM((2, rows * (half // LANES), LANES), U32),
            pltpu.VMEM((rows, d), F32),
            pltpu.SMEM((n_slots + SLOT_BLOCK,), I32),
            pltpu.SemaphoreType.DMA((2,)), pltpu.SemaphoreType.DMA(()),
        ],
    )
    return pl.pallas_call(
        functools.partial(_expert_body, n_chunks=n_chunks, n_groups=n_groups,
                          n_tok=dest_flat.shape[0] // TOP_K),
        grid_spec=grid_spec,
        out_shape=jax.ShapeDtypeStruct((n_slots, d), F32),
        compiler_params=_cparams(("arbitrary", "arbitrary")),
        name="expert",
    )(gexp, gsub0, gns, dest_flat, counts, pstart, hp, w1, w1, b1.reshape(n_exp, 1, ff2),
      b1.reshape(n_exp, 1, ff2), w2, b2.reshape(n_exp, 1, d))


def _final_body(dest_ref, x1_ref, w4_ref, p_ref, wple_ref, wpg_ref, bpg_ref, npl_ref, outs_hbm,
                o_ref, gbuf, sems, *, tm, n_steps):
    i = pl.program_id(0)
    cur = i % 2

    def row_copy(slot, buf, j, q, u):
        return pltpu.make_async_copy(outs_hbm.at[pl.ds(slot, 1), :], gbuf.at[buf, j, q, pl.ds(u, 1), :],
                                     sems.at[buf])

    def issue(step, buf):
        def rows(q, carry):
            for u in range(SUBLANES):
                t = step * tm + q * SUBLANES + u
                for j in range(TOP_K):
                    row_copy(dest_ref[t * TOP_K + j], buf, j, q, u).start(priority=j % 2)
            return carry
        lax.fori_loop(0, tm // SUBLANES, rows, 0)

    @pl.when(i == 0)
    def _():
        issue(0, 0)

    pe = jnp.dot(p_ref[...].astype(BF16), wple_ref[...], preferred_element_type=F32)

    def wt(q, carry):
        for u in range(SUBLANES):
            for j in range(TOP_K):
                row_copy(0, cur, j, q, u).wait()
        return carry
    lax.fori_loop(0, tm // SUBLANES, wt, 0)

    w4 = w4_ref[...]
    d = o_ref.shape[1]
    y = gbuf[cur, 0].reshape(tm, d) * w4[:, 0:1]
    for j in range(1, TOP_K):
        y = y + gbuf[cur, j].reshape(tm, d) * w4[:, j:j + 1]
    x2 = x1_ref[...] + y
    hn = _rmsnorm_rows(x2, npl_ref[...])
    gate = jax.nn.sigmoid(jnp.dot(hn.astype(BF16), wpg_ref[...], preferred_element_type=F32) + bpg_ref[...])
    o_ref[...] = x2 + gate * pe

    nxt = jnp.minimum(i + 1, n_steps - 1)
    for q in range(tm // SUBLANES):
        for u in range(SUBLANES):
            t = nxt * tm + q * SUBLANES + u
            for j in range(TOP_K):
                row_copy(dest_ref[t * TOP_K + j], 1 - cur, j, q, u).start(priority=j % 2)

    @pl.when(i == n_steps - 1)
    def _():
        def wt_last(q, carry):
            for u in range(SUBLANES):
                for j in range(TOP_K):
                    row_copy(0, 1 - cur, j, q, u).wait()
            return carry
        lax.fori_loop(0, tm // SUBLANES, wt_last, 0)


def _final_call(dest_flat, x1, w4, p2d, wple, wpg, bpg_row, npl_row, outs, tm=256):
    n, d = x1.shape
    tm = min(tm, n)
    pd = p2d.shape[1]
    res = lambda shape: pl.BlockSpec(shape, lambda i, dst: (0,) * len(shape), pipeline_mode=pl.Buffered(1))
    grid_spec = pltpu.PrefetchScalarGridSpec(
        num_scalar_prefetch=1,
        grid=(n // tm,),
        in_specs=[pl.BlockSpec((tm, d), lambda i, dst: (i, 0)),
                  pl.BlockSpec((tm, LANES), lambda i, dst: (i, 0)),
                  pl.BlockSpec((tm, pd), lambda i, dst: (i, 0)),
                  res((pd, d)), res((d, d)), res((1, d)), res((1, d)),
                  pl.BlockSpec(memory_space=pl.ANY)],
        out_specs=pl.BlockSpec((tm, d), lambda i, dst: (i, 0)),
        scratch_shapes=[pltpu.VMEM((2, TOP_K, tm // SUBLANES, SUBLANES, d), F32),
                        pltpu.SemaphoreType.DMA((2,))],
    )
    return pl.pallas_call(
        functools.partial(_final_body, tm=tm, n_steps=n // tm),
        grid_spec=grid_spec,
        out_shape=jax.ShapeDtypeStruct((n, d), F32),
        compiler_params=_cparams(("arbitrary",)),
        name="final",
    )(dest_flat, x1, w4, p2d, wple, wpg, bpg_row, npl_row, outs)


def _t5_bucket(rel):
    n = -rel
    nb = REL_BUCKETS // 2
    ret = jnp.where(n < 0, nb, 0)
    n = jnp.abs(n)
    max_exact = nb // 2
    large = max_exact + (jnp.log(jnp.maximum(n, 1).astype(jnp.float32) / max_exact)
                         / math.log(REL_MAX_DIST / max_exact) * (nb - max_exact)).astype(jnp.int32)
    large = jnp.minimum(large, nb - 1)
    return ret + jnp.where(n < max_exact, n, large)


def _bias_body(rv_ref, o_ref, *, t):
    x = jnp.broadcast_to(rv_ref[...], (t, rv_ref.shape[1]))
    o_ref[...] = pltpu.roll(x, 1, 1, stride=1, stride_axis=0)[:, t:]


def _diff_bias_table(rel_bias, seq, t):
    rel = jnp.arange(seq + t) - (seq - 1)
    rv = (jnp.transpose(rel_bias[_t5_bucket(rel)], (1, 0)).astype(F32) * LOG2E)[:, None, :]
    return pl.pallas_call(
        functools.partial(_bias_body, t=t),
        grid=(N_HEADS,),
        in_specs=[pl.BlockSpec((None, 1, seq + t), lambda h: (h, 0, 0))],
        out_specs=pl.BlockSpec((None, t, seq), lambda h: (h, 0, 0)),
        out_shape=jax.ShapeDtypeStruct((N_HEADS, t, seq), F32),
        compiler_params=_cparams(("parallel",)),
        name="bias_table",
    )(rv)


def _pad_cols(a, width):
    return jnp.pad(a, ((0, 0), (0, width - a.shape[1])))


def _group_table(counts):
    n_sub = (counts + SLOT_BLOCK - 1) // SLOT_BLOCK
    sub_start = jnp.cumsum(n_sub) - n_sub
    n_grp = (n_sub + GROUP_SUBS - 1) // GROUP_SUBS
    grp_end = jnp.cumsum(n_grp)
    total = grp_end[-1]
    return n_sub, sub_start, n_grp, grp_end, total


def _layer(x, p_l, w_in, b_gate, b_forget, dq_norm, dk_norm, fq_norm, fk_norm, lq1, lk1, lq2, lk2,
           lambda_init, subln, w_up_a, w_up_b, w_out, rel_bias, norm_mix, norm_moe, w_router,
           b_router, w1, b1, w2, b2, norm_ple, w_ple_gate, b_ple_gate, w_ple):
    b, s, d = x.shape
    n = b * s
    width = N_HEADS * HEAD_DIM
    t = min(ATTN_T, s)
    x2d = x.reshape(n, d)
    row = lambda v: v.reshape(1, -1).astype(F32)

    f_row = 6 * width
    wt_in = jnp.swapaxes(w_in, 0, 1)
    h, f_logit = _norm_call(x2d, row(norm_mix), wt_in, f_row)
    cum = _cum_call(f_logit, _pad_cols(row(b_forget), LANES), b, s)
    cum_bhs = jnp.transpose(cum[:, :N_HEADS].reshape(b, s, N_HEADS), (0, 2, 1))
    cum_row = cum_bhs[:, :, None, :]

    diff_scale = DIFF_QK_DIM ** -0.5 * LOG2E
    fox_scale = HEAD_DIM ** -0.5 * LOG2E
    gain_d = jnp.concatenate([jnp.tile(dq_norm * diff_scale, 2 * N_HEADS), jnp.tile(dk_norm, 2 * N_HEADS)])
    gain_f = jnp.concatenate([jnp.tile(fq_norm * fox_scale, N_HEADS), jnp.tile(fk_norm, N_HEADS)])
    zeros_w = jnp.zeros((1, width), F32)
    dqk = _proj_call(h, wt_in, 0, 2 * width, row(gain_d), "norm64", "proj_dqk")
    dv = _proj_call(h, wt_in, 2 * width, width, zeros_w, "plain", "proj_dv")
    fqk = _proj_call(h, wt_in, 3 * width, 2 * width, row(gain_f), "norm128", "proj_fqk")
    fv = _proj_call(h, wt_in, 5 * width, width, zeros_w, "plain", "proj_fv")
    gates = _proj_call(h, wt_in, f_row + N_HEADS, 2 * d, row(b_gate), "gate", "proj_gate")

    lam_rows = [row(v) for v in (lq1, lk1, lq2, lk2)]
    bias = _diff_bias_table(rel_bias, s, t)
    od = _diff_attn_call(dqk.reshape(b, s, 2 * width), dv.reshape(b, s, width), bias, lam_rows,
                         row(subln), lambda_init, t)
    of = _fox_attn_call(fqk.reshape(b, s, 2 * width), fv.reshape(b, s, width), cum.reshape(b, s, LANES),
                        cum_row, t)

    br_pad = jnp.full((1, LANES), NEG, F32).at[0, :N_EXPERTS].set(b_router.astype(F32))
    wr_pad = jnp.pad(jnp.swapaxes(w_router.astype(F32), 0, 1), ((0, LANES - N_EXPERTS), (0, 0)))
    wr_hi = wr_pad.astype(BF16)
    wr_lo = (wr_pad - wr_hi.astype(F32)).astype(BF16)
    x1, hp, logits = _mix_call(od.reshape(n, width), of.reshape(n, width), gates, x2d,
                               w_up_a.astype(BF16), w_up_b.astype(BF16), w_out.astype(BF16),
                               row(norm_moe), wr_hi, wr_lo, br_pad)

    rank, jm, wm, cnt = _router_call(logits)
    counts = cnt[0, :N_EXPERTS].astype(I32)
    padded = ((counts + SLOT_BLOCK - 1) // SLOT_BLOCK) * SLOT_BLOCK
    pstart = jnp.cumsum(padded) - padded
    ps_row = _pad_cols(pstart.astype(F32).reshape(1, -1), LANES)
    dest128, w4 = _dest_call(rank, jm, wm, ps_row)
    dest_flat = dest128[:, :TOP_K].reshape(-1)

    n_slots = n * TOP_K + N_EXPERTS * SLOT_BLOCK

    n_sub, sub_start, n_grp, grp_end, total = _group_table(counts)
    max_groups = N_EXPERTS + (n_slots // SLOT_BLOCK) // GROUP_SUBS
    gidx = jnp.arange(max_groups, dtype=I32)
    gvalid = gidx < total
    gsafe = jnp.minimum(gidx, total - 1)
    gexp = jnp.searchsorted(grp_end, gsafe, side="right").astype(I32)
    kth = gsafe - (grp_end - n_grp)[gexp]
    gsub0 = (sub_start[gexp] + kth * GROUP_SUBS).astype(I32)
    gns = jnp.where(gvalid, jnp.minimum(GROUP_SUBS, n_sub[gexp] - kth * GROUP_SUBS), 0).astype(I32)
    outs = _expert_call(gexp, gsub0, gns, dest_flat, counts, pstart, n_slots, hp, w1, b1, w2, b2)

    out = _final_call(dest_flat, x1, w4, p_l.reshape(n, -1), w_ple.astype(BF16),
                      w_ple_gate.astype(BF16), row(b_ple_gate), row(norm_ple), outs)
    return out.reshape(b, s, d)


def kernel(x, p, w_in, b_gate, b_forget, dq_norm, dk_norm, fq_norm, fk_norm, lambda_q1, lambda_k1,
           lambda_q2, lambda_k2, subln, w_up_a, w_up_b, w_out, rel_bias, norm_mix, norm_moe,
           w_router, b_router, w1, b1, w2, b2, norm_ple, w_ple_gate, b_ple_gate, w_ple):
    for i in range(w_in.shape[0]):
        lambda_init = 0.8 - 0.6 * math.exp(-0.3 * i)
        x = _layer(x, p[i], w_in[i], b_gate[i], b_forget[i], dq_norm[i], dk_norm[i], fq_norm[i],
                   fk_norm[i], lambda_q1[i], lambda_k1[i], lambda_q2[i], lambda_k2[i], lambda_init,
                   subln[i], w_up_a[i], w_up_b[i], w_out[i], rel_bias, norm_mix[i], norm_moe[i],
                   w_router[i], b_router[i], w1[i], b1[i], w2[i], b2[i], norm_ple[i],
                   w_ple_gate[i], b_ple_gate[i], w_ple[i])
    return x
```

```python
import functools
import math

import jax
import jax.numpy as jnp
from jax import lax
from jax.experimental import pallas as pl
from jax.experimental.pallas import tpu as pltpu

F32 = jnp.float32
BF16 = jnp.bfloat16
U32 = jnp.uint32
I32 = jnp.int32
HIGHEST = lax.Precision.HIGHEST

N_HEADS = 8
HEAD_DIM = 128
DIFF_QK_DIM = 64
CHUNK = 64
REL_BUCKETS = 32
REL_MAX_DIST = 128
N_EXPERTS = 32
TOP_K = 4
SWIGLU_LIMIT = 7.0
SWIGLU_ALPHA = 1.702
RMS_EPS = 1e-6
NEG = -1e30
LOG2E = math.log2(math.e)

LANES = 128
SUBLANES = 8
SLOT_BLOCK = 128
GROUP_SUBS = 12
TRIP_SUBS = 4
FF_CHUNK = 512
ATTN_T = 256
VMEM_LIMIT = 56 * 1024 * 1024


_NT = (((1,), (1,)), ((), ()))


def _cparams(sem):
    return pltpu.CompilerParams(dimension_semantics=sem, vmem_limit_bytes=VMEM_LIMIT)


def _rmsnorm_rows(x, g):
    ms = jnp.mean(x * x, axis=-1, keepdims=True)
    return x * lax.rsqrt(ms + RMS_EPS) * g


def _norm_body(x_ref, g_ref, wf_ref, h_ref, fl_ref):
    y = _rmsnorm_rows(x_ref[...], g_ref[...])
    y_hi = y.astype(BF16)
    h_ref[...] = y_hi
    rowi = lax.broadcasted_iota(I32, (LANES, 1), 0)
    wf = jnp.where(rowi < N_HEADS, wf_ref[...], 0.0)
    y_lo = (y - y_hi.astype(F32)).astype(BF16)
    wf_hi = wf.astype(BF16)
    wf_lo = (wf - wf_hi.astype(F32)).astype(BF16)
    fl_ref[...] = (lax.dot_general(y_hi, wf_hi, _NT, preferred_element_type=F32)
                   + lax.dot_general(y_lo, wf_hi, _NT, preferred_element_type=F32)
                   + lax.dot_general(y_hi, wf_lo, _NT, preferred_element_type=F32))


def _norm_call(x2d, g_row, wt_in, f_row, tm=512):
    n, d = x2d.shape
    tm = min(tm, n)
    assert f_row % LANES == 0
    return pl.pallas_call(
        _norm_body,
        grid=(n // tm,),
        in_specs=[pl.BlockSpec((tm, d), lambda i: (i, 0)),
                  pl.BlockSpec((1, d), lambda i: (0, 0)),
                  pl.BlockSpec((LANES, d), lambda i: (f_row // LANES, 0))],
        out_specs=[pl.BlockSpec((tm, d), lambda i: (i, 0)),
                   pl.BlockSpec((tm, LANES), lambda i: (i, 0))],
        out_shape=[jax.ShapeDtypeStruct((n, d), BF16),
                   jax.ShapeDtypeStruct((n, LANES), F32)],
        compiler_params=_cparams(("parallel",)),
        name="norm",
    )(x2d, g_row, wt_in)


def _cum_body(fl_ref, bf_ref, cum_ref):
    s = fl_ref.shape[0]
    r = lax.broadcasted_iota(I32, (LANES, LANES), 0)
    c = lax.broadcasted_iota(I32, (LANES, LANES), 1)
    tri = jnp.where(r >= c, 1.0, 0.0).astype(F32)
    carry = jnp.zeros((1, LANES), F32)
    for blk in range(s // LANES):
        rows = slice(blk * LANES, (blk + 1) * LANES)
        lf = jax.nn.log_sigmoid(fl_ref[rows, :] + bf_ref[...])
        cb = jnp.dot(tri, lf, precision=HIGHEST, preferred_element_type=F32) + carry
        cum_ref[rows, :] = cb * LOG2E
        carry = cb[LANES - 1:LANES, :]


def _cum_call(fl, bf_row, batch, seq):
    return pl.pallas_call(
        _cum_body,
        grid=(batch,),
        in_specs=[pl.BlockSpec((seq, LANES), lambda b: (b, 0)),
                  pl.BlockSpec((1, LANES), lambda b: (0, 0))],
        out_specs=pl.BlockSpec((seq, LANES), lambda b: (b, 0)),
        out_shape=jax.ShapeDtypeStruct(fl.shape, F32),
        compiler_params=_cparams(("parallel",)),
        name="cum",
    )(fl, bf_row)


def _proj_epilogue(acc, aux_ref, o_ref, kind):
    tn = acc.shape[1]
    if kind == "plain":
        o_ref[...] = acc.astype(o_ref.dtype)
    elif kind == "gate":
        o_ref[...] = jax.nn.sigmoid(acc + aux_ref[...]).astype(o_ref.dtype)
    else:
        lo = lax.broadcasted_iota(I32, (1, LANES), 1) < DIFF_QK_DIM
        for s in range(tn // LANES):
            cols = slice(s * LANES, (s + 1) * LANES)
            blk = acc[:, cols]
            sq = blk * blk
            if kind == "norm128":
                ms = jnp.mean(sq, axis=-1, keepdims=True)
            else:
                s_lo = jnp.sum(jnp.where(lo, sq, 0.0), axis=-1, keepdims=True)
                s_hi = jnp.sum(jnp.where(lo, 0.0, sq), axis=-1, keepdims=True)
                ms = jnp.where(lo, s_lo, s_hi) * (1.0 / DIFF_QK_DIM)
            o_ref[:, cols] = (blk * lax.rsqrt(ms + RMS_EPS) * aux_ref[:, cols]).astype(o_ref.dtype)


def _proj_body(h_ref, wt_ref, aux_ref, o_ref, *, kind):
    acc = lax.dot_general(h_ref[...], wt_ref[...].astype(BF16), _NT, preferred_element_type=F32)
    _proj_epilogue(acc, aux_ref, o_ref, kind)


def _proj_shift_body(h_ref, wa_ref, wb_ref, aux_ref, o_ref, *, kind, shift):
    wt = jnp.concatenate([wa_ref[shift:, :], wb_ref[...]], axis=0)
    acc = lax.dot_general(h_ref[...], wt.astype(BF16), _NT, preferred_element_type=F32)
    _proj_epilogue(acc, aux_ref, o_ref, kind)


def _proj_call(h, wt, row0, nrows, aux_row, kind, name, tm=1024, tn=1024):
    m, k = h.shape
    tm = min(tm, m)
    assert nrows % tn == 0 and m % tm == 0
    shift = row0 % tn
    base = row0 - shift
    h_spec = pl.BlockSpec((tm, k), lambda i, j: (i, 0))
    w_spec = pl.BlockSpec((tn, k), lambda i, j: (base // tn + j, 0))
    aux_spec = pl.BlockSpec((1, tn), lambda i, j: (0, j))
    if shift == 0:
        body, w_specs, ws = functools.partial(_proj_body, kind=kind), [w_spec], [wt]
    else:
        assert shift % SUBLANES == 0 and tn % shift == 0
        body = functools.partial(_proj_shift_body, kind=kind, shift=shift)
        w_specs = [w_spec, pl.BlockSpec((shift, k), lambda i, j: ((base + (j + 1) * tn) // shift, 0))]
        ws = [wt, wt]
    return pl.pallas_call(
        body,
        grid=(m // tm, nrows // tn),
        in_specs=[h_spec] + w_specs + [aux_spec],
        out_specs=pl.BlockSpec((tm, tn), lambda i, j: (i, j)),
        out_shape=jax.ShapeDtypeStruct((m, nrows), BF16),
        compiler_params=_cparams(("parallel", "parallel")),
        name=name,
    )(h, *ws, aux_row)


DIFF_HEADS_PER_STEP = 4
FOX_HEADS_PER_STEP = 8


def _softmax_step(s, m_ref, acc_ref, v1):
    m_old = m_ref[...]
    mn = jnp.maximum(m_old, jnp.max(s, axis=-1, keepdims=True))
    alpha = jnp.exp2(m_old - mn)
    p = jnp.exp2(s - jnp.tile(mn, (1, s.shape[1] // LANES))).astype(BF16)
    acc_ref[...] = (jnp.tile(alpha, (1, acc_ref.shape[1] // LANES)) * acc_ref[...]
                    + jnp.dot(p, v1, preferred_element_type=F32))
    m_ref[...] = mn


def _causal_sweep(step, qi, t):
    def pair(i, carry):
        step(2 * i, 2 * t, False)
        return carry
    lax.fori_loop(0, qi // 2, pair, 0)

    @pl.when(qi % 2 == 1)
    def _():
        step(qi - 1, t, False)
    step(qi, t, True)


def _with_ones(v):
    return jnp.concatenate([v, jnp.ones_like(v)], axis=1)


def _normalised(acc):
    return acc[:, :HEAD_DIM] / acc[:, HEAD_DIM:]


def _diff_attn_body(lq1_ref, lk1_ref, lq2_ref, lk2_ref, g_ref, q_ref, k_ref, v_ref, bias_ref,
                    o_ref, m_sc, acc_sc, *, t, nh, lambda_init):
    qi = pl.program_id(2)
    s_len = k_ref.shape[0]
    m_sc[...] = jnp.full(m_sc.shape, NEG, F32)
    acc_sc[...] = jnp.zeros(acc_sc.shape, F32)
    lo = lax.broadcasted_iota(I32, (t, HEAD_DIM), 1) < DIFF_QK_DIM

    def step(kb, w, masked):
        k0 = pl.multiple_of(kb * t, t)
        off = pl.multiple_of((kb - qi) * t + (s_len - t), LANES)
        if masked:
            r = lax.broadcasted_iota(I32, (t, w), 0)
            c = lax.broadcasted_iota(I32, (t, w), 1)
            ok = (c // CHUNK) <= (r // CHUNK)
        for h in range(nh):
            cols = slice(h * HEAD_DIM, (h + 1) * HEAD_DIM)
            q = q_ref[:, cols]
            zero = jnp.zeros_like(q)
            k = k_ref[pl.ds(k0, w), cols]
            v1 = _with_ones(v_ref[pl.ds(k0, w), cols])
            b = bias_ref[h, :, pl.ds(off, w)]
            for mp, qm in enumerate((jnp.where(lo, q, zero), jnp.where(lo, zero, q))):
                s = lax.dot_general(qm, k, _NT, preferred_element_type=F32) + b
                if masked:
                    s = jnp.where(ok, s, NEG)
                _softmax_step(s, m_sc.at[2 * h + mp], acc_sc.at[2 * h + mp], v1)

    _causal_sweep(step, qi, t)

    lam = (jnp.exp(jnp.sum(lq1_ref[...] * lk1_ref[...], axis=-1, keepdims=True))
           - jnp.exp(jnp.sum(lq2_ref[...] * lk2_ref[...], axis=-1, keepdims=True)) + lambda_init)
    for h in range(nh):
        o = _normalised(acc_sc[2 * h]) - lam * _normalised(acc_sc[2 * h + 1])
        o = _rmsnorm_rows(o, g_ref[...]) * (1.0 - lambda_init)
        o_ref[:, h * HEAD_DIM:(h + 1) * HEAD_DIM] = o.astype(o_ref.dtype)


def _diff_attn_call(qk, v, bias, lam_rows, subln_row, lambda_init, t):
    b, s, _ = v.shape
    nh = DIFF_HEADS_PER_STEP
    wb = nh * HEAD_DIM
    n_hg = N_HEADS // nh
    vec = lambda w: pl.BlockSpec((1, w), lambda hg, bb, qi: (0, 0))
    return pl.pallas_call(
        functools.partial(_diff_attn_body, t=t, nh=nh, lambda_init=lambda_init),
        grid=(n_hg, b, s // t),
        in_specs=[vec(DIFF_QK_DIM)] * 4 + [
            vec(HEAD_DIM),
            pl.BlockSpec((None, t, wb), lambda hg, bb, qi: (bb, qi, hg)),
            pl.BlockSpec((None, s, wb), lambda hg, bb, qi: (bb, 0, n_hg + hg)),
            pl.BlockSpec((None, s, wb), lambda hg, bb, qi: (bb, 0, hg)),
            pl.BlockSpec((nh, t, s), lambda hg, bb, qi: (hg, 0, 0))],
        out_specs=pl.BlockSpec((None, t, wb), lambda hg, bb, qi: (bb, qi, hg)),
        out_shape=jax.ShapeDtypeStruct(v.shape, BF16),
        scratch_shapes=[pltpu.VMEM((2 * nh, t, LANES), F32), pltpu.VMEM((2 * nh, t, 2 * HEAD_DIM), F32)],
        compiler_params=_cparams(("parallel", "parallel", "parallel")),
        name="diff_attn",
    )(*lam_rows, subln_row, qk, qk, v, bias)


def _fox_attn_body(q_ref, k_ref, v_ref, cq_ref, ck_ref, o_ref, m_sc, acc_sc, cq_sc, *, t, nh):
    qi = pl.program_id(2)
    m_sc[...] = jnp.full(m_sc.shape, NEG, F32)
    acc_sc[...] = jnp.zeros(acc_sc.shape, F32)
    hg = pl.program_id(0)
    lane = lax.broadcasted_iota(I32, (t, LANES), 1)
    cq_all = cq_ref[...]
    for h in range(nh):
        col = jnp.sum(jnp.where(lane == hg * nh + h, cq_all, 0.0), axis=-1, keepdims=True)
        cq_sc[h] = jnp.broadcast_to(col, (t, LANES))

    def step(kb, w, masked):
        k0 = pl.multiple_of(kb * t, t)
        if masked:
            r = lax.broadcasted_iota(I32, (t, w), 0)
            c = lax.broadcasted_iota(I32, (t, w), 1)
            ok = c <= r
        for h in range(nh):
            cols = slice(h * HEAD_DIM, (h + 1) * HEAD_DIM)
            k = k_ref[pl.ds(k0, w), cols]
            v1 = _with_ones(v_ref[pl.ds(k0, w), cols])
            s = lax.dot_general(q_ref[:, cols], k, _NT, preferred_element_type=F32)
            s = s + jnp.tile(cq_sc[h], (1, w // LANES)) - ck_ref[h, :, pl.ds(k0, w)]
            if masked:
                s = jnp.where(ok, s, NEG)
            _softmax_step(s, m_sc.at[h], acc_sc.at[h], v1)

    _causal_sweep(step, qi, t)
    for h in range(nh):
        o_ref[:, h * HEAD_DIM:(h + 1) * HEAD_DIM] = _normalised(acc_sc[h]).astype(o_ref.dtype)


def _fox_attn_call(qk, v, cum, cum_row, t):
    b, s, _ = v.shape
    nh = FOX_HEADS_PER_STEP
    wb = nh * HEAD_DIM
    n_hg = N_HEADS // nh
    return pl.pallas_call(
        functools.partial(_fox_attn_body, t=t, nh=nh),
        grid=(n_hg, b, s // t),
        in_specs=[pl.BlockSpec((None, t, wb), lambda hg, bb, qi: (bb, qi, hg)),
                  pl.BlockSpec((None, s, wb), lambda hg, bb, qi: (bb, 0, n_hg + hg)),
                  pl.BlockSpec((None, s, wb), lambda hg, bb, qi: (bb, 0, hg)),
                  pl.BlockSpec((None, t, LANES), lambda hg, bb, qi: (bb, qi, 0)),
                  pl.BlockSpec((None, nh, 1, s), lambda hg, bb, qi: (bb, hg, 0, 0))],
        out_specs=pl.BlockSpec((None, t, wb), lambda hg, bb, qi: (bb, qi, hg)),
        out_shape=jax.ShapeDtypeStruct(v.shape, BF16),
        scratch_shapes=[pltpu.VMEM((nh, t, LANES), F32), pltpu.VMEM((nh, t, 2 * HEAD_DIM), F32),
                        pltpu.VMEM((nh, t, LANES), F32)],
        compiler_params=_cparams(("parallel", "parallel", "parallel")),
        name="fox_attn",
    )(qk, qk, v, cum, cum_row)


def _mix_body(od_ref, of_ref, g_ref, x_ref, wa_ref, wb_ref, wo_ref, nm_ref, wrh_ref, wrl_ref, br_ref,
              x1_ref, hp_ref, lg_ref):
    d = x_ref.shape[1]
    ua = jnp.dot(od_ref[...], wa_ref[...], preferred_element_type=F32)
    ub = jnp.dot(of_ref[...], wb_ref[...], preferred_element_type=F32)
    mixed = g_ref[:, :d].astype(F32) * ua + g_ref[:, d:].astype(F32) * ub
    x1 = x_ref[...] + jnp.dot(mixed.astype(BF16), wo_ref[...], preferred_element_type=F32)
    x1_ref[...] = x1
    hm = _rmsnorm_rows(x1, nm_ref[...])
    h_hi = hm.astype(BF16)
    h_hi32 = h_hi.astype(F32)
    h_lo = (hm - h_hi32).astype(BF16)
    lg_ref[...] = (lax.dot_general(h_hi, wrh_ref[...], _NT, preferred_element_type=F32)
                   + lax.dot_general(h_lo, wrh_ref[...], _NT, preferred_element_type=F32)
                   + lax.dot_general(h_hi, wrl_ref[...], _NT, preferred_element_type=F32) + br_ref[...])
    bits = lax.bitcast_convert_type(h_hi32, U32)
    packed = (bits[:, :d // 2] >> 16) | (bits[:, d // 2:] & jnp.uint32(0xFFFF0000))
    tm = packed.shape[0]
    pieces = packed.shape[1] // LANES
    for j in range(pieces):
        hp_ref[pl.ds(j, tm, stride=pieces), :] = packed[:, j * LANES:(j + 1) * LANES]


def _resident(shape):
    return pl.BlockSpec(shape, lambda i: (0,) * len(shape), pipeline_mode=pl.Buffered(1))


def _mix_call(od, of, gates, x2d, wa, wb, wo, nm_row, wr_hi, wr_lo, br_pad, tm=256):
    n, d = x2d.shape
    tm = min(tm, n)
    wdt = od.shape[1]
    return pl.pallas_call(
        _mix_body,
        grid=(n // tm,),
        in_specs=[pl.BlockSpec((tm, wdt), lambda i: (i, 0)),
                  pl.BlockSpec((tm, wdt), lambda i: (i, 0)),
                  pl.BlockSpec((tm, 2 * d), lambda i: (i, 0)),
                  pl.BlockSpec((tm, d), lambda i: (i, 0)),
                  _resident((wdt, d)), _resident((wdt, d)), _resident((d, d)),
                  _resident((1, d)), _resident((LANES, d)), _resident((LANES, d)), _resident((1, LANES))],
        out_specs=[pl.BlockSpec((tm, d), lambda i: (i, 0)),
                   pl.BlockSpec((tm * (d // 2 // LANES), LANES), lambda i: (i, 0)),
                   pl.BlockSpec((tm, LANES), lambda i: (i, 0))],
        out_shape=[jax.ShapeDtypeStruct((n, d), F32),
                   jax.ShapeDtypeStruct((n * (d // 2 // LANES), LANES), U32),
                   jax.ShapeDtypeStruct((n, LANES), F32)],
        compiler_params=_cparams(("parallel",)),
        name="mix",
    )(od, of, gates, x2d, wa, wb, wo, nm_row, wr_hi, wr_lo, br_pad)


def _router_body(lg_ref, rank_ref, jm_ref, wm_ref, cnt_ref, carry_ref):
    i = pl.program_id(0)

    @pl.when(i == 0)
    def _():
        carry_ref[...] = jnp.zeros_like(carry_ref)

    vals = lg_ref[...]
    tm = vals.shape[0]
    lane = lax.broadcasted_iota(I32, vals.shape, 1).astype(F32)
    jm = jnp.zeros(vals.shape, F32)
    tops = []
    for j in range(TOP_K):
        m = jnp.max(vals, axis=-1, keepdims=True)
        idx = jnp.min(jnp.where(vals == m, lane, float(LANES)), axis=-1, keepdims=True)
        sel = lane == idx
        jm = jnp.where(sel, float(j + 1), jm)
        vals = jnp.where(sel, -jnp.inf, vals)
        tops.append(m)
    es = [jnp.exp(m - tops[0]) for m in tops]
    den = es[0] + es[1] + es[2] + es[3]
    wm = jnp.zeros(vals.shape, F32)
    for j in range(TOP_K):
        wm = jnp.where(jm == float(j + 1), es[j] / den, wm)
    sel_any = jnp.where(jm > 0.0, 1.0, 0.0)
    r = lax.broadcasted_iota(I32, (tm, tm), 0)
    c = lax.broadcasted_iota(I32, (tm, tm), 1)
    tri = jnp.where(c < r, 1.0, 0.0).astype(BF16)
    carry = carry_ref[...]
    rank_ref[...] = jnp.dot(tri, sel_any.astype(BF16), preferred_element_type=F32) + carry
    jm_ref[...] = jm
    wm_ref[...] = wm
    carry = carry + jnp.sum(sel_any, axis=0, keepdims=True)
    carry_ref[...] = carry
    cnt_ref[...] = carry


def _router_call(logits, tm=256):
    n = logits.shape[0]
    tm = min(tm, n)
    tile = pl.BlockSpec((tm, LANES), lambda i: (i, 0))
    row = pl.BlockSpec((1, LANES), lambda i: (0, 0))
    return pl.pallas_call(
        _router_body,
        grid=(n // tm,),
        in_specs=[tile],
        out_specs=[tile, tile, tile, row],
        out_shape=[jax.ShapeDtypeStruct((n, LANES), F32)] * 3 + [jax.ShapeDtypeStruct((1, LANES), F32)],
        scratch_shapes=[pltpu.VMEM((1, LANES), F32)],
        compiler_params=_cparams(("arbitrary",)),
        name="router",
    )(logits)


def _dest_body(rank_ref, jm_ref, wm_ref, ps_ref, dest_ref, w4_ref):
    slot = rank_ref[...] + ps_ref[...]
    jm = jm_ref[...]
    wm = wm_ref[...]
    lane = lax.broadcasted_iota(I32, jm.shape, 1)
    dest = jnp.zeros(jm.shape, F32)
    w4 = jnp.zeros(jm.shape, F32)
    for j in range(TOP_K):
        sel = jm == float(j + 1)
        dj = jnp.sum(jnp.where(sel, slot, 0.0), axis=-1, keepdims=True)
        wj = jnp.sum(jnp.where(sel, wm, 0.0), axis=-1, keepdims=True)
        dest = jnp.where(lane == j, dj, dest)
        w4 = jnp.where(lane == j, wj, w4)
    dest_ref[...] = dest.astype(I32)
    w4_ref[...] = w4


def _dest_call(rank, jm, wm, ps_row, tm=512):
    n = rank.shape[0]
    tm = min(tm, n)
    tile = pl.BlockSpec((tm, LANES), lambda i: (i, 0))
    return pl.pallas_call(
        _dest_body,
        grid=(n // tm,),
        in_specs=[tile, tile, tile, pl.BlockSpec((1, LANES), lambda i: (0, 0))],
        out_specs=[tile, tile],
        out_shape=[jax.ShapeDtypeStruct((n, LANES), I32), jax.ShapeDtypeStruct((n, LANES), F32)],
        compiler_params=_cparams(("parallel",)),
        name="dest",
    )(rank, jm, wm, ps_row)


ROW_UNROLL = 8
GATHER_PRIORITY = 1
SLOT_UNROLL = 16


def _build_slot_table(dest_ref, cnt_ref, pst_ref, slot_ref, n_tok):
    def pads(e, carry):
        base = pst_ref[e] + cnt_ref[e]

        def zeros(i, c2):
            for u in range(SLOT_UNROLL):
                slot_ref[base + i * SLOT_UNROLL + u] = 0
            return c2
        lax.fori_loop(0, SLOT_BLOCK // SLOT_UNROLL, zeros, 0)
        return carry
    lax.fori_loop(0, N_EXPERTS, pads, 0)

    per_trip = SLOT_UNROLL // TOP_K
    assert n_tok % per_trip == 0

    def toks(i, carry):
        for u in range(per_trip):
            t = i * per_trip + u
            for j in range(TOP_K):
                slot_ref[dest_ref[t * TOP_K + j]] = t
        return carry
    lax.fori_loop(0, n_tok // per_trip, toks, 0)


def _expert_body(gexp_ref, gsub0_ref, gns_ref, dest_ref, cnt_ref, pst_ref,
                 hp_hbm, w1g_ref, w1l_ref, b1g_ref, b1l_ref, w2_ref, b2_ref,
                 outs_hbm,
                 stage, acc, slot_ref, sem_in, sem_out, *, n_chunks, n_groups, n_tok):
    g = pl.program_id(0)
    c = pl.program_id(1)
    ns = gns_ref[g]
    s0 = gsub0_ref[g]
    sub = SLOT_BLOCK
    half = w1g_ref.shape[0] // 2
    last = n_chunks - 1
    cur = g % 2
    pieces = half // LANES

    def row_copy(tok, buf, r):
        return pltpu.make_async_copy(hp_hbm.at[pl.ds(pl.multiple_of(tok * pieces, pieces), pieces), :],
                                     stage.at[buf, pl.ds(pl.multiple_of(r * pieces, pieces), pieces), :],
                                     sem_in.at[buf])

    def issue_rows(gi, buf, blk0, n_blk):
        base = gsub0_ref[gi] * sub

        def trip(i, carry):
            for u in range(ROW_UNROLL):
                r = blk0 * sub + i * ROW_UNROLL + u
                row_copy(slot_ref[base + r], buf, r).start(priority=GATHER_PRIORITY)
            return carry
        lax.fori_loop(0, n_blk * (sub // ROW_UNROLL), trip, 0)

    nxt = jnp.minimum(g + 1, n_groups - 1)
    need = jnp.where(jnp.logical_and(g + 1 < n_groups, gns_ref[nxt] > 0), gns_ref[nxt], 0)
    nxt_base = gsub0_ref[nxt] * sub

    def issue_block(blk):
        for u in range(sub):
            r = blk * sub + u
            row_copy(slot_ref[nxt_base + r], 1 - cur, r).start(priority=GATHER_PRIORITY)

    def out_copy(r0, rows):
        row0 = s0 * sub + r0
        return pltpu.make_async_copy(acc.at[pl.ds(r0, rows), :], outs_hbm.at[pl.ds(row0, rows), :], sem_out)

    def compute(r0, rows, issue_blk=None):
        los, his = [], []
        for j in range(pieces):
            w = stage[cur, pl.ds(r0 * pieces + j, rows, stride=pieces), :]
            los.append(lax.bitcast_convert_type(w << 16, F32).astype(BF16))
            his.append(lax.bitcast_convert_type(w & jnp.uint32(0xFFFF0000), F32).astype(BF16))
        xl = jnp.concatenate(los, axis=1)
        xh = jnp.concatenate(his, axis=1)
        hg = (jnp.dot(xl, w1g_ref[:half, :].astype(BF16), preferred_element_type=F32)
              + jnp.dot(xh, w1g_ref[half:, :].astype(BF16), preferred_element_type=F32) + b1g_ref[...])
        hl = (jnp.dot(xl, w1l_ref[:half, :].astype(BF16), preferred_element_type=F32)
              + jnp.dot(xh, w1l_ref[half:, :].astype(BF16), preferred_element_type=F32) + b1l_ref[...])
        gate = jnp.minimum(hg, SWIGLU_LIMIT)
        lin = jnp.clip(hl, -SWIGLU_LIMIT, SWIGLU_LIMIT)
        act = gate * jax.nn.sigmoid(SWIGLU_ALPHA * gate) * (lin + 1.0)
        acc[pl.ds(r0, rows), :] += jnp.dot(act.astype(BF16), w2_ref[...].astype(BF16),
                                           preferred_element_type=F32)
        if issue_blk is not None:
            issue_block(issue_blk)

        @pl.when(c == last)
        def _():
            out_copy(r0, rows).start()

    @pl.when(ns > 0)
    def _():
        @pl.when(c == 0)
        def _load():
            @pl.when(g == 0)
            def _():
                _build_slot_table(dest_ref, cnt_ref, pst_ref, slot_ref, n_tok)
                issue_rows(0, 0, 0, ns)

            def wt(i, carry):
                for u in range(ROW_UNROLL):
                    row_copy(0, cur, i * ROW_UNROLL + u).wait()
                return carry
            lax.fori_loop(0, ns * (sub // ROW_UNROLL), wt, 0)

            def init(s, carry):
                acc[pl.ds(pl.multiple_of(s * sub, sub), sub), :] = jnp.broadcast_to(
                    b2_ref[...], (sub, acc.shape[1]))
                return carry
            lax.fori_loop(0, ns, init, 0)

        full = TRIP_SUBS * sub
        n_full = ns // TRIP_SUBS

        def for_each_trip(fn, fn_full=None):
            def trip(i, carry):
                fn(pl.multiple_of(i * full, full), full)
                return carry
            if fn_full is None:
                lax.fori_loop(0, n_full, trip, 0)
            else:
                fn_full()
            r0 = n_full * full
            width = full // 2
            while width >= sub:
                has = (ns * sub) & width

                @pl.when(has != 0)
                def _(r0=r0, width=width):
                    fn(pl.multiple_of(r0, width), width)
                r0 = r0 + has
                width //= 2

        def full_trips():
            n_issue = jnp.clip(need - c * n_full, 0, n_full)

            def with_issue(i, carry):
                compute(pl.multiple_of(i * full, full), full, issue_blk=c * n_full + i)
                return carry

            def plain(i, carry):
                compute(pl.multiple_of(i * full, full), full)
                return carry
            lax.fori_loop(0, n_issue, with_issue, 0)
            lax.fori_loop(n_issue, n_full, plain, 0)

        for_each_trip(compute, full_trips)

        @pl.when(c == last)
        def _drain():
            done = jnp.minimum(need, n_chunks * n_full)
            issue_rows(nxt, 1 - cur, done, need - done)
            for_each_trip(lambda r0, rows: out_copy(r0, rows).wait())

            @pl.when(jnp.logical_or(g + 1 >= n_groups, gns_ref[nxt] == 0))
            def _tail():
                acc[pl.ds(0, sub), :] = jnp.zeros((sub, acc.shape[1]), F32)
                first = s0 + ns
                n_tail = outs_hbm.shape[0] // sub - first

                def tail_copy(i):
                    row0 = pl.multiple_of((first + i) * sub, sub)
                    return pltpu.make_async_copy(acc.at[pl.ds(0, sub), :], outs_hbm.at[pl.ds(row0, sub), :],
                                                 sem_out)

                def st(i, carry):
                    tail_copy(i).start()
                    return carry

                def wt(i, carry):
                    tail_copy(i).wait()
                    return carry
                lax.fori_loop(0, n_tail, st, 0)
                lax.fori_loop(0, n_tail, wt, 0)


def _expert_call(gexp, gsub0, gns, dest_flat, counts, pstart, n_slots, hp, w1, b1, w2, b2):
    half = w1.shape[1] // 2
    n_exp, d, ff2 = w1.shape
    ff = ff2 // 2
    tc = FF_CHUNK
    n_chunks = ff // tc
    n_groups = gexp.shape[0]
    rows = GROUP_SUBS * SLOT_BLOCK

    def chunk(c, gn, g):
        return jnp.where(gn[g] > 0, c, n_chunks - 1)

    grid_spec = pltpu.PrefetchScalarGridSpec(
        num_scalar_prefetch=6,
        grid=(n_groups, n_chunks),
        in_specs=[
            pl.BlockSpec(memory_space=pl.ANY),
            pl.BlockSpec((None, d, tc), lambda g, c, ge, gs, gn, *_: (ge[g], 0, chunk(c, gn, g))),
            pl.BlockSpec((None, d, tc), lambda g, c, ge, gs, gn, *_: (ge[g], 0, n_chunks + chunk(c, gn, g))),
            pl.BlockSpec((None, 1, tc), lambda g, c, ge, gs, gn, *_: (ge[g], 0, chunk(c, gn, g))),
            pl.BlockSpec((None, 1, tc), lambda g, c, ge, gs, gn, *_: (ge[g], 0, n_chunks + chunk(c, gn, g))),
            pl.BlockSpec((None, tc, d), lambda g, c, ge, gs, gn, *_: (ge[g], chunk(c, gn, g), 0)),
            pl.BlockSpec((None, 1, d), lambda g, c, ge, gs, gn, *_: (ge[g], 0, 0)),
        ],
        out_specs=pl.BlockSpec(memory_space=pl.ANY),
        scratch_shapes=[
            pltpu.VMEM((2, rows * (half // LANES), LANES), U32),
            pltpu.VMEM((rows, d), F32),
            pltpu.SMEM((n_slots + SLOT_BLOCK,), I32),
            pltpu.SemaphoreType.DMA((2,)), pltpu.SemaphoreType.DMA(()),
        ],
    )
    return pl.pallas_call(
        functools.partial(_expert_body, n_chunks=n_chunks, n_groups=n_groups,
                          n_tok=dest_flat.shape[0] // TOP_K),
        grid_spec=grid_spec,
        out_shape=jax.ShapeDtypeStruct((n_slots, d), F32),
        compiler_params=_cparams(("arbitrary", "arbitrary")),
        name="expert",
    )(gexp, gsub0, gns, dest_flat, counts, pstart, hp, w1, w1, b1.reshape(n_exp, 1, ff2),
      b1.reshape(n_exp, 1, ff2), w2, b2.reshape(n_exp, 1, d))


def _final_body(dest_ref, x1_ref, w4_ref, p_ref, wple_ref, wpg_ref, bpg_ref, npl_ref, outs_hbm,
                o_ref, gbuf, sems, *, tm, n_steps):
    i = pl.program_id(0)
    cur = i % 2

    def row_copy(slot, buf, j, q, u):
        return pltpu.make_async_copy(outs_hbm.at[pl.ds(slot, 1), :], gbuf.at[buf, j, q, pl.ds(u, 1), :],
                                     sems.at[buf])

    def issue(step, buf):
        def rows(q, carry):
            for u in range(SUBLANES):
                t = step * tm + q * SUBLANES + u
                for j in range(TOP_K):
                    row_copy(dest_ref[t * TOP_K + j], buf, j, q, u).start(priority=j % 2)
            return carry
        lax.fori_loop(0, tm // SUBLANES, rows, 0)

    @pl.when(i == 0)
    def _():
        issue(0, 0)

    pe = jnp.dot(p_ref[...].astype(BF16), wple_ref[...], preferred_element_type=F32)

    def wt(q, carry):
        for u in range(SUBLANES):
            for j in range(TOP_K):
                row_copy(0, cur, j, q, u).wait()
        return carry
    lax.fori_loop(0, tm // SUBLANES, wt, 0)

    w4 = w4_ref[...]
    d = o_ref.shape[1]
    y = gbuf[cur, 0].reshape(tm, d) * w4[:, 0:1]
    for j in range(1, TOP_K):
        y = y + gbuf[cur, j].reshape(tm, d) * w4[:, j:j + 1]
    x2 = x1_ref[...] + y
    hn = _rmsnorm_rows(x2, npl_ref[...])
    gate = jax.nn.sigmoid(jnp.dot(hn.astype(BF16), wpg_ref[...], preferred_element_type=F32) + bpg_ref[...])
    o_ref[...] = x2 + gate * pe

    nxt = jnp.minimum(i + 1, n_steps - 1)
    for q in range(tm // SUBLANES):
        for u in range(SUBLANES):
            t = nxt * tm + q * SUBLANES + u
            for j in range(TOP_K):
                row_copy(dest_ref[t * TOP_K + j], 1 - cur, j, q, u).start(priority=j % 2)

    @pl.when(i == n_steps - 1)
    def _():
        def wt_last(q, carry):
            for u in range(SUBLANES):
                for j in range(TOP_K):
                    row_copy(0, 1 - cur, j, q, u).wait()
            return carry
        lax.fori_loop(0, tm // SUBLANES, wt_last, 0)


def _final_call(dest_flat, x1, w4, p2d, wple, wpg, bpg_row, npl_row, outs, tm=256):
    n, d = x1.shape
    tm = min(tm, n)
    pd = p2d.shape[1]
    res = lambda shape: pl.BlockSpec(shape, lambda i, dst: (0,) * len(shape), pipeline_mode=pl.Buffered(1))
    grid_spec = pltpu.PrefetchScalarGridSpec(
        num_scalar_prefetch=1,
        grid=(n // tm,),
        in_specs=[pl.BlockSpec((tm, d), lambda i, dst: (i, 0)),
                  pl.BlockSpec((tm, LANES), lambda i, dst: (i, 0)),
                  pl.BlockSpec((tm, pd), lambda i, dst: (i, 0)),
                  res((pd, d)), res((d, d)), res((1, d)), res((1, d)),
                  pl.BlockSpec(memory_space=pl.ANY)],
        out_specs=pl.BlockSpec((tm, d), lambda i, dst: (i, 0)),
        scratch_shapes=[pltpu.VMEM((2, TOP_K, tm // SUBLANES, SUBLANES, d), F32),
                        pltpu.SemaphoreType.DMA((2,))],
    )
    return pl.pallas_call(
        functools.partial(_final_body, tm=tm, n_steps=n // tm),
        grid_spec=grid_spec,
        out_shape=jax.ShapeDtypeStruct((n, d), F32),
        compiler_params=_cparams(("arbitrary",)),
        name="final",
    )(dest_flat, x1, w4, p2d, wple, wpg, bpg_row, npl_row, outs)


def _t5_bucket(rel):
    n = -rel
    nb = REL_BUCKETS // 2
    ret = jnp.where(n < 0, nb, 0)
    n = jnp.abs(n)
    max_exact = nb // 2
    large = max_exact + (jnp.log(jnp.maximum(n, 1).astype(jnp.float32) / max_exact)
                         / math.log(REL_MAX_DIST / max_exact) * (nb - max_exact)).astype(jnp.int32)
    large = jnp.minimum(large, nb - 1)
    return ret + jnp.where(n < max_exact, n, large)


def _bias_body(rv_ref, o_ref, *, t):
    x = jnp.broadcast_to(rv_ref[...], (t, rv_ref.shape[1]))
    o_ref[...] = pltpu.roll(x, 1, 1, stride=1, stride_axis=0)[:, t:]


def _diff_bias_table(rel_bias, seq, t):
    rel = jnp.arange(seq + t) - (seq - 1)
    rv = (jnp.transpose(rel_bias[_t5_bucket(rel)], (1, 0)).astype(F32) * LOG2E)[:, None, :]
    return pl.pallas_call(
        functools.partial(_bias_body, t=t),
        grid=(N_HEADS,),
        in_specs=[pl.BlockSpec((None, 1, seq + t), lambda h: (h, 0, 0))],
        out_specs=pl.BlockSpec((None, t, seq), lambda h: (h, 0, 0)),
        out_shape=jax.ShapeDtypeStruct((N_HEADS, t, seq), F32),
        compiler_params=_cparams(("parallel",)),
        name="bias_table",
    )(rv)


def _pad_cols(a, width):
    return jnp.pad(a, ((0, 0), (0, width - a.shape[1])))


def _group_table(counts):
    n_sub = (counts + SLOT_BLOCK - 1) // SLOT_BLOCK
    sub_start = jnp.cumsum(n_sub) - n_sub
    n_grp = (n_sub + GROUP_SUBS - 1) // GROUP_SUBS
    grp_end = jnp.cumsum(n_grp)
    total = grp_end[-1]
    return n_sub, sub_start, n_grp, grp_end, total


def _layer(x, p_l, w_in, b_gate, b_forget, dq_norm, dk_norm, fq_norm, fk_norm, lq1, lk1, lq2, lk2,
           lambda_init, subln, w_up_a, w_up_b, w_out, rel_bias, norm_mix, norm_moe, w_router,
           b_router, w1, b1, w2, b2, norm_ple, w_ple_gate, b_ple_gate, w_ple):
    b, s, d = x.shape
    n = b * s
    width = N_HEADS * HEAD_DIM
    t = min(ATTN_T, s)
    x2d = x.reshape(n, d)
    row = lambda v: v.reshape(1, -1).astype(F32)

    f_row = 6 * width
    wt_in = jnp.swapaxes(w_in, 0, 1)
    h, f_logit = _norm_call(x2d, row(norm_mix), wt_in, f_row)
    cum = _cum_call(f_logit, _pad_cols(row(b_forget), LANES), b, s)
    cum_bhs = jnp.transpose(cum[:, :N_HEADS].reshape(b, s, N_HEADS), (0, 2, 1))
    cum_row = cum_bhs[:, :, None, :]

    diff_scale = DIFF_QK_DIM ** -0.5 * LOG2E
    fox_scale = HEAD_DIM ** -0.5 * LOG2E
    gain_d = jnp.concatenate([jnp.tile(dq_norm * diff_scale, 2 * N_HEADS), jnp.tile(dk_norm, 2 * N_HEADS)])
    gain_f = jnp.concatenate([jnp.tile(fq_norm * fox_scale, N_HEADS), jnp.tile(fk_norm, N_HEADS)])
    zeros_w = jnp.zeros((1, width), F32)
    dqk = _proj_call(h, wt_in, 0, 2 * width, row(gain_d), "norm64", "proj_dqk")
    dv = _proj_call(h, wt_in, 2 * width, width, zeros_w, "plain", "proj_dv")
    fqk = _proj_call(h, wt_in, 3 * width, 2 * width, row(gain_f), "norm128", "proj_fqk")
    fv = _proj_call(h, wt_in, 5 * width, width, zeros_w, "plain", "proj_fv")
    gates = _proj_call(h, wt_in, f_row + N_HEADS, 2 * d, row(b_gate), "gate", "proj_gate")

    lam_rows = [row(v) for v in (lq1, lk1, lq2, lk2)]
    bias = _diff_bias_table(rel_bias, s, t)
    od = _diff_attn_call(dqk.reshape(b, s, 2 * width), dv.reshape(b, s, width), bias, lam_rows,
                         row(subln), lambda_init, t)
    of = _fox_attn_call(fqk.reshape(b, s, 2 * width), fv.reshape(b, s, width), cum.reshape(b, s, LANES),
                        cum_row, t)

    br_pad = jnp.full((1, LANES), NEG, F32).at[0, :N_EXPERTS].set(b_router.astype(F32))
    wr_pad = jnp.pad(jnp.swapaxes(w_router.astype(F32), 0, 1), ((0, LANES - N_EXPERTS), (0, 0)))
    wr_hi = wr_pad.astype(BF16)
    wr_lo = (wr_pad - wr_hi.astype(F32)).astype(BF16)
    x1, hp, logits = _mix_call(od.reshape(n, width), of.reshape(n, width), gates, x2d,
                               w_up_a.astype(BF16), w_up_b.astype(BF16), w_out.astype(BF16),
                               row(norm_moe), wr_hi, wr_lo, br_pad)

    rank, jm, wm, cnt = _router_call(logits)
    counts = cnt[0, :N_EXPERTS].astype(I32)
    padded = ((counts + SLOT_BLOCK - 1) // SLOT_BLOCK) * SLOT_BLOCK
    pstart = jnp.cumsum(padded) - padded
    ps_row = _pad_cols(pstart.astype(F32).reshape(1, -1), LANES)
    dest128, w4 = _dest_call(rank, jm, wm, ps_row)
    dest_flat = dest128[:, :TOP_K].reshape(-1)

    n_slots = n * TOP_K + N_EXPERTS * SLOT_BLOCK

    n_sub, sub_start, n_grp, grp_end, total = _group_table(counts)
    max_groups = N_EXPERTS + (n_slots // SLOT_BLOCK) // GROUP_SUBS
    gidx = jnp.arange(max_groups, dtype=I32)
    gvalid = gidx < total
    gsafe = jnp.minimum(gidx, total - 1)
    gexp = jnp.searchsorted(grp_end, gsafe, side="right").astype(I32)
    kth = gsafe - (grp_end - n_grp)[gexp]
    gsub0 = (sub_start[gexp] + kth * GROUP_SUBS).astype(I32)
    gns = jnp.where(gvalid, jnp.minimum(GROUP_SUBS, n_sub[gexp] - kth * GROUP_SUBS), 0).astype(I32)
    outs = _expert_call(gexp, gsub0, gns, dest_flat, counts, pstart, n_slots, hp, w1, b1, w2, b2)

    out = _final_call(dest_flat, x1, w4, p_l.reshape(n, -1), w_ple.astype(BF16),
                      w_ple_gate.astype(BF16), row(b_ple_gate), row(norm_ple), outs)
    return out.reshape(b, s, d)


def kernel(x, p, w_in, b_gate, b_forget, dq_norm, dk_norm, fq_norm, fk_norm, lambda_q1, lambda_k1,
           lambda_q2, lambda_k2, subln, w_up_a, w_up_b, w_out, rel_bias, norm_mix, norm_moe,
           w_router, b_router, w1, b1, w2, b2, norm_ple, w_ple_gate, b_ple_gate, w_ple):
    for i in range(w_in.shape[0]):
        lambda_init = 0.8 - 0.6 * math.exp(-0.3 * i)
        x = _layer(x, p[i], w_in[i], b_gate[i], b_forget[i], dq_norm[i], dk_norm[i], fq_norm[i],
                   fk_norm[i], lambda_q1[i], lambda_k1[i], lambda_q2[i], lambda_k2[i], lambda_init,
                   subln[i], w_up_a[i], w_up_b[i], w_out[i], rel_bias, norm_mix[i], norm_moe[i],
                   w_router[i], b_router[i], w1[i], b1[i], w2[i], b2[i], norm_ple[i],
                   w_ple_gate[i], b_ple_gate[i], w_ple[i])
    return x
```

```python
import functools
import math

import jax
import jax.numpy as jnp
from jax import lax
from jax.experimental import pallas as pl
from jax.experimental.pallas import tpu as pltpu

F32 = jnp.float32
BF16 = jnp.bfloat16
U32 = jnp.uint32
I32 = jnp.int32
HIGHEST = lax.Precision.HIGHEST

N_HEADS = 8
HEAD_DIM = 128
DIFF_QK_DIM = 64
CHUNK = 64
REL_BUCKETS = 32
REL_MAX_DIST = 128
N_EXPERTS = 32
TOP_K = 4
SWIGLU_LIMIT = 7.0
SWIGLU_ALPHA = 1.702
RMS_EPS = 1e-6
NEG = -1e30
LOG2E = math.log2(math.e)

LANES = 128
SUBLANES = 8
SLOT_BLOCK = 128
GROUP_SUBS = 12
TRIP_SUBS = 4
FF_CHUNK = 512
ATTN_T = 256
VMEM_LIMIT = 56 * 1024 * 1024


_NT = (((1,), (1,)), ((), ()))


def _cparams(sem):
    return pltpu.CompilerParams(dimension_semantics=sem, vmem_limit_bytes=VMEM_LIMIT)


def _rmsnorm_rows(x, g):
    ms = jnp.mean(x * x, axis=-1, keepdims=True)
    return x * lax.rsqrt(ms + RMS_EPS) * g


def _norm_body(x_ref, g_ref, wf_ref, h_ref, fl_ref):
    y = _rmsnorm_rows(x_ref[...], g_ref[...])
    y_hi = y.astype(BF16)
    h_ref[...] = y_hi
    rowi = lax.broadcasted_iota(I32, (LANES, 1), 0)
    wf = jnp.where(rowi < N_HEADS, wf_ref[...], 0.0)
    y_lo = (y - y_hi.astype(F32)).astype(BF16)
    wf_hi = wf.astype(BF16)
    wf_lo = (wf - wf_hi.astype(F32)).astype(BF16)
    fl_ref[...] = (lax.dot_general(y_hi, wf_hi, _NT, preferred_element_type=F32)
                   + lax.dot_general(y_lo, wf_hi, _NT, preferred_element_type=F32)
                   + lax.dot_general(y_hi, wf_lo, _NT, preferred_element_type=F32))


def _norm_call(x2d, g_row, wt_in, f_row, tm=512):
    n, d = x2d.shape
    tm = min(tm, n)
    assert f_row % LANES == 0
    return pl.pallas_call(
        _norm_body,
        grid=(n // tm,),
        in_specs=[pl.BlockSpec((tm, d), lambda i: (i, 0)),
                  pl.BlockSpec((1, d), lambda i: (0, 0)),
                  pl.BlockSpec((LANES, d), lambda i: (f_row // LANES, 0))],
        out_specs=[pl.BlockSpec((tm, d), lambda i: (i, 0)),
                   pl.BlockSpec((tm, LANES), lambda i: (i, 0))],
        out_shape=[jax.ShapeDtypeStruct((n, d), BF16),
                   jax.ShapeDtypeStruct((n, LANES), F32)],
        compiler_params=_cparams(("parallel",)),
        name="norm",
    )(x2d, g_row, wt_in)


def _cum_body(fl_ref, bf_ref, cum_ref):
    s = fl_ref.shape[0]
    r = lax.broadcasted_iota(I32, (LANES, LANES), 0)
    c = lax.broadcasted_iota(I32, (LANES, LANES), 1)
    tri = jnp.where(r >= c, 1.0, 0.0).astype(F32)
    carry = jnp.zeros((1, LANES), F32)
    for blk in range(s // LANES):
        rows = slice(blk * LANES, (blk + 1) * LANES)
        lf = jax.nn.log_sigmoid(fl_ref[rows, :] + bf_ref[...])
        cb = jnp.dot(tri, lf, precision=HIGHEST, preferred_element_type=F32) + carry
        cum_ref[rows, :] = cb * LOG2E
        carry = cb[LANES - 1:LANES, :]


def _cum_call(fl, bf_row, batch, seq):
    return pl.pallas_call(
        _cum_body,
        grid=(batch,),
        in_specs=[pl.BlockSpec((seq, LANES), lambda b: (b, 0)),
                  pl.BlockSpec((1, LANES), lambda b: (0, 0))],
        out_specs=pl.BlockSpec((seq, LANES), lambda b: (b, 0)),
        out_shape=jax.ShapeDtypeStruct(fl.shape, F32),
        compiler_params=_cparams(("parallel",)),
        name="cum",
    )(fl, bf_row)


def _proj_epilogue(acc, aux_ref, o_ref, kind):
    tn = acc.shape[1]
    if kind == "plain":
        o_ref[...] = acc.astype(o_ref.dtype)
    elif kind == "gate":
        o_ref[...] = jax.nn.sigmoid(acc + aux_ref[...]).astype(o_ref.dtype)
    else:
        lo = lax.broadcasted_iota(I32, (1, LANES), 1) < DIFF_QK_DIM
        for s in range(tn // LANES):
            cols = slice(s * LANES, (s + 1) * LANES)
            blk = acc[:, cols]
            sq = blk * blk
            if kind == "norm128":
                ms = jnp.mean(sq, axis=-1, keepdims=True)
            else:
                s_lo = jnp.sum(jnp.where(lo, sq, 0.0), axis=-1, keepdims=True)
                s_hi = jnp.sum(jnp.where(lo, 0.0, sq), axis=-1, keepdims=True)
                ms = jnp.where(lo, s_lo, s_hi) * (1.0 / DIFF_QK_DIM)
            o_ref[:, cols] = (blk * lax.rsqrt(ms + RMS_EPS) * aux_ref[:, cols]).astype(o_ref.dtype)


def _proj_body(h_ref, wt_ref, aux_ref, o_ref, *, kind):
    acc = lax.dot_general(h_ref[...], wt_ref[...].astype(BF16), _NT, preferred_element_type=F32)
    _proj_epilogue(acc, aux_ref, o_ref, kind)


def _proj_shift_body(h_ref, wa_ref, wb_ref, aux_ref, o_ref, *, kind, shift):
    wt = jnp.concatenate([wa_ref[shift:, :], wb_ref[...]], axis=0)
    acc = lax.dot_general(h_ref[...], wt.astype(BF16), _NT, preferred_element_type=F32)
    _proj_epilogue(acc, aux_ref, o_ref, kind)


def _proj_call(h, wt, row0, nrows, aux_row, kind, name, tm=1024, tn=1024):
    m, k = h.shape
    tm = min(tm, m)
    assert nrows % tn == 0 and m % tm == 0
    shift = row0 % tn
    base = row0 - shift
    h_spec = pl.BlockSpec((tm, k), lambda i, j: (i, 0))
    w_spec = pl.BlockSpec((tn, k), lambda i, j: (base // tn + j, 0))
    aux_spec = pl.BlockSpec((1, tn), lambda i, j: (0, j))
    if shift == 0:
        body, w_specs, ws = functools.partial(_proj_body, kind=kind), [w_spec], [wt]
    else:
        assert shift % SUBLANES == 0 and tn % shift == 0
        body = functools.partial(_proj_shift_body, kind=kind, shift=shift)
        w_specs = [w_spec, pl.BlockSpec((shift, k), lambda i, j: ((base + (j + 1) * tn) // shift, 0))]
        ws = [wt, wt]
    return pl.pallas_call(
        body,
        grid=(m // tm, nrows // tn),
        in_specs=[h_spec] + w_specs + [aux_spec],
        out_specs=pl.BlockSpec((tm, tn), lambda i, j: (i, j)),
        out_shape=jax.ShapeDtypeStruct((m, nrows), BF16),
        compiler_params=_cparams(("parallel", "parallel")),
        name=name,
    )(h, *ws, aux_row)


DIFF_HEADS_PER_STEP = 4
FOX_HEADS_PER_STEP = 8


def _softmax_step(s, m_ref, acc_ref, v1):
    m_old = m_ref[...]
    mn = jnp.maximum(m_old, jnp.max(s, axis=-1, keepdims=True))
    alpha = jnp.exp2(m_old - mn)
    p = jnp.exp2(s - jnp.tile(mn, (1, s.shape[1] // LANES))).astype(BF16)
    acc_ref[...] = (jnp.tile(alpha, (1, acc_ref.shape[1] // LANES)) * acc_ref[...]
                    + jnp.dot(p, v1, preferred_element_type=F32))
    m_ref[...] = mn


def _causal_sweep(step, qi, t):
    def pair(i, carry):
        step(2 * i, 2 * t, None)
        return carry
    lax.fori_loop(0, qi // 2, pair, 0)

    @pl.when(qi % 2 == 1)
    def _():
        step(qi - 1, 2 * t, -t)

    @pl.when(qi % 2 == 0)
    def _():
        step(qi, t, 0)


def _with_ones(v):
    return jnp.concatenate([v, jnp.ones_like(v)], axis=1)


def _normalised(acc):
    return acc[:, :HEAD_DIM] / acc[:, HEAD_DIM:]


def _diff_attn_body(lq1_ref, lk1_ref, lq2_ref, lk2_ref, g_ref, q_ref, k_ref, v_ref, bias_ref,
                    o_ref, m_sc, acc_sc, *, t, nh, lambda_init):
    qi = pl.program_id(2)
    s_len = k_ref.shape[0]
    m_sc[...] = jnp.full(m_sc.shape, NEG, F32)
    acc_sc[...] = jnp.zeros(acc_sc.shape, F32)
    lo = lax.broadcasted_iota(I32, (t, HEAD_DIM), 1) < DIFF_QK_DIM

    def step(kb, w, rel):
        k0 = pl.multiple_of(kb * t, t)
        off = pl.multiple_of((kb - qi) * t + (s_len - t), LANES)
        masked = rel is not None
        if masked:
            r = lax.broadcasted_iota(I32, (t, w), 0)
            c = lax.broadcasted_iota(I32, (t, w), 1) + rel
            chunk_of = lambda pos: lax.shift_right_arithmetic(pos, jnp.int32(CHUNK.bit_length() - 1))
            ok = chunk_of(c) <= chunk_of(r)
        for h in range(nh):
            cols = slice(h * HEAD_DIM, (h + 1) * HEAD_DIM)
            q = q_ref[:, cols]
            zero = jnp.zeros_like(q)
            k = k_ref[pl.ds(k0, w), cols]
            v1 = _with_ones(v_ref[pl.ds(k0, w), cols])
            b = bias_ref[h, :, pl.ds(off, w)]
            for mp, qm in enumerate((jnp.where(lo, q, zero), jnp.where(lo, zero, q))):
                s = lax.dot_general(qm, k, _NT, preferred_element_type=F32) + b
                if masked:
                    s = jnp.where(ok, s, NEG)
                _softmax_step(s, m_sc.at[2 * h + mp], acc_sc.at[2 * h + mp], v1)

    _causal_sweep(step, qi, t)

    lam = (jnp.exp(jnp.sum(lq1_ref[...] * lk1_ref[...], axis=-1, keepdims=True))
           - jnp.exp(jnp.sum(lq2_ref[...] * lk2_ref[...], axis=-1, keepdims=True)) + lambda_init)
    for h in range(nh):
        o = _normalised(acc_sc[2 * h]) - lam * _normalised(acc_sc[2 * h + 1])
        o = _rmsnorm_rows(o, g_ref[...]) * (1.0 - lambda_init)
        o_ref[:, h * HEAD_DIM:(h + 1) * HEAD_DIM] = o.astype(o_ref.dtype)


def _diff_attn_call(qk, v, bias, lam_rows, subln_row, lambda_init, t):
    b, s, _ = v.shape
    nh = DIFF_HEADS_PER_STEP
    wb = nh * HEAD_DIM
    n_hg = N_HEADS // nh
    vec = lambda w: pl.BlockSpec((1, w), lambda hg, bb, qi: (0, 0))
    return pl.pallas_call(
        functools.partial(_diff_attn_body, t=t, nh=nh, lambda_init=lambda_init),
        grid=(n_hg, b, s // t),
        in_specs=[vec(DIFF_QK_DIM)] * 4 + [
            vec(HEAD_DIM),
            pl.BlockSpec((None, t, wb), lambda hg, bb, qi: (bb, qi, hg)),
            pl.BlockSpec((None, s, wb), lambda hg, bb, qi: (bb, 0, n_hg + hg)),
            pl.BlockSpec((None, s, wb), lambda hg, bb, qi: (bb, 0, hg)),
            pl.BlockSpec((nh, t, s), lambda hg, bb, qi: (hg, 0, 0))],
        out_specs=pl.BlockSpec((None, t, wb), lambda hg, bb, qi: (bb, qi, hg)),
        out_shape=jax.ShapeDtypeStruct(v.shape, BF16),
        scratch_shapes=[pltpu.VMEM((2 * nh, t, LANES), F32), pltpu.VMEM((2 * nh, t, 2 * HEAD_DIM), F32)],
        compiler_params=_cparams(("parallel", "parallel", "parallel")),
        name="diff_attn",
    )(*lam_rows, subln_row, qk, qk, v, bias)


def _fox_attn_body(q_ref, k_ref, v_ref, cq_ref, ck_ref, o_ref, m_sc, acc_sc, cq_sc, *, t, nh):
    qi = pl.program_id(2)
    m_sc[...] = jnp.full(m_sc.shape, NEG, F32)
    acc_sc[...] = jnp.zeros(acc_sc.shape, F32)
    hg = pl.program_id(0)
    lane = lax.broadcasted_iota(I32, (t, LANES), 1)
    cq_all = cq_ref[...]
    for h in range(nh):
        col = jnp.sum(jnp.where(lane == hg * nh + h, cq_all, 0.0), axis=-1, keepdims=True)
        cq_sc[h] = jnp.broadcast_to(col, (t, LANES))

    def step(kb, w, rel):
        k0 = pl.multiple_of(kb * t, t)
        masked = rel is not None
        if masked:
            r = lax.broadcasted_iota(I32, (t, w), 0)
            c = lax.broadcasted_iota(I32, (t, w), 1) + rel
            ok = c <= r
        for h in range(nh):
            cols = slice(h * HEAD_DIM, (h + 1) * HEAD_DIM)
            k = k_ref[pl.ds(k0, w), cols]
            v1 = _with_ones(v_ref[pl.ds(k0, w), cols])
            s = lax.dot_general(q_ref[:, cols], k, _NT, preferred_element_type=F32)
            s = s + jnp.tile(cq_sc[h], (1, w // LANES)) - ck_ref[h, :, pl.ds(k0, w)]
            if masked:
                s = jnp.where(ok, s, NEG)
            _softmax_step(s, m_sc.at[h], acc_sc.at[h], v1)

    _causal_sweep(step, qi, t)
    for h in range(nh):
        o_ref[:, h * HEAD_DIM:(h + 1) * HEAD_DIM] = _normalised(acc_sc[h]).astype(o_ref.dtype)


def _fox_attn_call(qk, v, cum, cum_row, t):
    b, s, _ = v.shape
    nh = FOX_HEADS_PER_STEP
    wb = nh * HEAD_DIM
    n_hg = N_HEADS // nh
    return pl.pallas_call(
        functools.partial(_fox_attn_body, t=t, nh=nh),
        grid=(n_hg, b, s // t),
        in_specs=[pl.BlockSpec((None, t, wb), lambda hg, bb, qi: (bb, qi, hg)),
                  pl.BlockSpec((None, s, wb), lambda hg, bb, qi: (bb, 0, n_hg + hg)),
                  pl.BlockSpec((None, s, wb), lambda hg, bb, qi: (bb, 0, hg)),
                  pl.BlockSpec((None, t, LANES), lambda hg, bb, qi: (bb, qi, 0)),
                  pl.BlockSpec((None, nh, 1, s), lambda hg, bb, qi: (bb, hg, 0, 0))],
        out_specs=pl.BlockSpec((None, t, wb), lambda hg, bb, qi: (bb, qi, hg)),
        out_shape=jax.ShapeDtypeStruct(v.shape, BF16),
        scratch_shapes=[pltpu.VMEM((nh, t, LANES), F32), pltpu.VMEM((nh, t, 2 * HEAD_DIM), F32),
                        pltpu.VMEM((nh, t, LANES), F32)],
        compiler_params=_cparams(("parallel", "parallel", "parallel")),
        name="fox_attn",
    )(qk, qk, v, cum, cum_row)


def _mix_body(od_ref, of_ref, g_ref, x_ref, wa_ref, wb_ref, wo_ref, nm_ref, wrh_ref, wrl_ref, br_ref,
              x1_ref, hp_ref, lg_ref):
    d = x_ref.shape[1]
    ua = jnp.dot(od_ref[...], wa_ref[...], preferred_element_type=F32)
    ub = jnp.dot(of_ref[...], wb_ref[...], preferred_element_type=F32)
    mixed = g_ref[:, :d].astype(F32) * ua + g_ref[:, d:].astype(F32) * ub
    x1 = x_ref[...] + jnp.dot(mixed.astype(BF16), wo_ref[...], preferred_element_type=F32)
    x1_ref[...] = x1
    hm = _rmsnorm_rows(x1, nm_ref[...])
    h_hi = hm.astype(BF16)
    h_hi32 = h_hi.astype(F32)
    h_lo = (hm - h_hi32).astype(BF16)
    lg_ref[...] = (lax.dot_general(h_hi, wrh_ref[...], _NT, preferred_element_type=F32)
                   + lax.dot_general(h_lo, wrh_ref[...], _NT, preferred_element_type=F32)
                   + lax.dot_general(h_hi, wrl_ref[...], _NT, preferred_element_type=F32) + br_ref[...])
    bits = lax.bitcast_convert_type(h_hi32, U32)
    packed = (bits[:, :d // 2] >> 16) | (bits[:, d // 2:] & jnp.uint32(0xFFFF0000))
    tm = packed.shape[0]
    pieces = packed.shape[1] // LANES
    for j in range(pieces):
        hp_ref[pl.ds(j, tm, stride=pieces), :] = packed[:, j * LANES:(j + 1) * LANES]


def _resident(shape):
    return pl.BlockSpec(shape, lambda i: (0,) * len(shape), pipeline_mode=pl.Buffered(1))


def _mix_call(od, of, gates, x2d, wa, wb, wo, nm_row, wr_hi, wr_lo, br_pad, tm=256):
    n, d = x2d.shape
    tm = min(tm, n)
    wdt = od.shape[1]
    return pl.pallas_call(
        _mix_body,
        grid=(n // tm,),
        in_specs=[pl.BlockSpec((tm, wdt), lambda i: (i, 0)),
                  pl.BlockSpec((tm, wdt), lambda i: (i, 0)),
                  pl.BlockSpec((tm, 2 * d), lambda i: (i, 0)),
                  pl.BlockSpec((tm, d), lambda i: (i, 0)),
                  _resident((wdt, d)), _resident((wdt, d)), _resident((d, d)),
                  _resident((1, d)), _resident((LANES, d)), _resident((LANES, d)), _resident((1, LANES))],
        out_specs=[pl.BlockSpec((tm, d), lambda i: (i, 0)),
                   pl.BlockSpec((tm * (d // 2 // LANES), LANES), lambda i: (i, 0)),
                   pl.BlockSpec((tm, LANES), lambda i: (i, 0))],
        out_shape=[jax.ShapeDtypeStruct((n, d), F32),
                   jax.ShapeDtypeStruct((n * (d // 2 // LANES), LANES), U32),
                   jax.ShapeDtypeStruct((n, LANES), F32)],
        compiler_params=_cparams(("parallel",)),
        name="mix",
    )(od, of, gates, x2d, wa, wb, wo, nm_row, wr_hi, wr_lo, br_pad)


def _route_body(lg_ref, dest_ref, w4_ref, cnt_ref, ps_ref, rank_sc, jm_sc, wm_sc, carry_sc, ps_sc, *, tm):
    phase = pl.program_id(0)
    i = pl.program_id(1)
    rows = pl.ds(pl.multiple_of(i * tm, tm), tm)

    @pl.when(phase == 0)
    def _select():
        @pl.when(i == 0)
        def _():
            carry_sc[...] = jnp.zeros_like(carry_sc)

        vals = lg_ref[...]
        lane = lax.broadcasted_iota(I32, vals.shape, 1).astype(F32)
        jm = jnp.zeros(vals.shape, F32)
        tops = []
        for j in range(TOP_K):
            m = jnp.max(vals, axis=-1, keepdims=True)
            idx = jnp.min(jnp.where(vals == m, lane, float(LANES)), axis=-1, keepdims=True)
            sel = lane == idx
            jm = jnp.where(sel, float(j + 1), jm)
            vals = jnp.where(sel, -jnp.inf, vals)
            tops.append(m)
        es = [jnp.exp(m - tops[0]) for m in tops]
        den = es[0] + es[1] + es[2] + es[3]
        wm = jnp.zeros(vals.shape, F32)
        for j in range(TOP_K):
            wm = jnp.where(jm == float(j + 1), es[j] / den, wm)
        sel_any = jnp.where(jm > 0.0, 1.0, 0.0)
        r = lax.broadcasted_iota(I32, (tm, tm), 0)
        c = lax.broadcasted_iota(I32, (tm, tm), 1)
        tri = jnp.where(c < r, 1.0, 0.0).astype(BF16)
        carry = carry_sc[...]
        rank_sc[rows, :] = jnp.dot(tri, sel_any.astype(BF16), preferred_element_type=F32) + carry
        jm_sc[rows, :] = jm
        wm_sc[rows, :] = wm
        carry_sc[...] = carry + jnp.sum(sel_any, axis=0, keepdims=True)

    @pl.when(phase == 1)
    def _place():
        @pl.when(i == 0)
        def _():
            cnt = carry_sc[...]
            padded = jnp.ceil(cnt * (1.0 / SLOT_BLOCK)) * float(SLOT_BLOCK)
            r = lax.broadcasted_iota(I32, (LANES, LANES), 0)
            c = lax.broadcasted_iota(I32, (LANES, LANES), 1)
            before = jnp.where(r < c, 1.0, 0.0).astype(F32)
            ps_sc[...] = jnp.dot(jnp.broadcast_to(padded, (SUBLANES, LANES)), before, precision=HIGHEST,
                                 preferred_element_type=F32)[:1]
            cnt_ref[...] = cnt
            ps_ref[...] = ps_sc[...]

        slot = rank_sc[rows, :] + ps_sc[...]
        jm = jm_sc[rows, :]
        wm = wm_sc[rows, :]
        lane = lax.broadcasted_iota(I32, jm.shape, 1)
        dest = jnp.zeros(jm.shape, F32)
        w4 = jnp.zeros(jm.shape, F32)
        for j in range(TOP_K):
            sel = jm == float(j + 1)
            dj = jnp.sum(jnp.where(sel, slot, 0.0), axis=-1, keepdims=True)
            wj = jnp.sum(jnp.where(sel, wm, 0.0), axis=-1, keepdims=True)
            dest = jnp.where(lane == j, dj, dest)
            w4 = jnp.where(lane == j, wj, w4)
        dest_ref[...] = dest.astype(I32)
        w4_ref[...] = w4


def _route_call(logits, tm=256):
    n = logits.shape[0]
    tm = min(tm, n)
    nt = n // tm
    row = pl.BlockSpec((1, LANES), lambda p, i: (0, 0))
    out_tile = pl.BlockSpec((tm, LANES), lambda p, i: (p * i, 0))
    return pl.pallas_call(
        functools.partial(_route_body, tm=tm),
        grid=(2, nt),
        in_specs=[pl.BlockSpec((tm, LANES), lambda p, i: (i * (1 - p) + (nt - 1) * p, 0))],
        out_specs=[out_tile, out_tile, row, row],
        out_shape=[jax.ShapeDtypeStruct((n, LANES), I32), jax.ShapeDtypeStruct((n, LANES), F32),
                   jax.ShapeDtypeStruct((1, LANES), F32), jax.ShapeDtypeStruct((1, LANES), F32)],
        scratch_shapes=[pltpu.VMEM((n, LANES), F32)] * 3 + [pltpu.VMEM((1, LANES), F32)] * 2,
        compiler_params=_cparams(("arbitrary", "arbitrary")),
        name="route",
    )(logits)


ROW_UNROLL = 8
SLOT_UNROLL = 16


def _build_slot_table(dest_ref, cnt_ref, pst_ref, slot_ref, n_tok):
    def pads(e, carry):
        base = pst_ref[e] + cnt_ref[e]

        def zeros(i, c2):
            for u in range(SLOT_UNROLL):
                slot_ref[base + i * SLOT_UNROLL + u] = 0
            return c2
        lax.fori_loop(0, SLOT_BLOCK // SLOT_UNROLL, zeros, 0)
        return carry
    lax.fori_loop(0, N_EXPERTS, pads, 0)

    per_trip = SLOT_UNROLL // TOP_K
    assert n_tok % per_trip == 0

    def toks(i, carry):
        for u in range(per_trip):
            t = i * per_trip + u
            for j in range(TOP_K):
                slot_ref[dest_ref[t * TOP_K + j]] = t
        return carry
    lax.fori_loop(0, n_tok // per_trip, toks, 0)


def _expert_body(gexp_ref, gsub0_ref, gns_ref, dest_ref, cnt_ref, pst_ref,
                 hp_hbm, w1g_ref, w1l_ref, b1g_ref, b1l_ref, w2_ref, b2_ref,
                 outs_hbm,
                 stage, acc, slot_ref, sem_in, sem_out, *, n_chunks, n_groups, n_tok):
    g = pl.program_id(0)
    c = pl.program_id(1)
    ns = gns_ref[g]
    s0 = gsub0_ref[g]
    sub = SLOT_BLOCK
    half = w1g_ref.shape[0] // 2
    last = n_chunks - 1
    cur = g % 2
    pieces = half // LANES

    def row_copy(tok, buf, r):
        return pltpu.make_async_copy(hp_hbm.at[pl.ds(pl.multiple_of(tok * pieces, pieces), pieces), :],
                                     stage.at[buf, pl.ds(pl.multiple_of(r * pieces, pieces), pieces), :],
                                     sem_in.at[buf])

    def issue_rows(gi, buf, blk0, n_blk):
        base = gsub0_ref[gi] * sub

        def trip(i, carry):
            for u in range(ROW_UNROLL):
                r = blk0 * sub + i * ROW_UNROLL + u
                row_copy(slot_ref[base + r], buf, r).start()
            return carry
        lax.fori_loop(0, n_blk * (sub // ROW_UNROLL), trip, 0)

    nxt = jnp.minimum(g + 1, n_groups - 1)
    need = jnp.where(jnp.logical_and(g + 1 < n_groups, gns_ref[nxt] > 0), gns_ref[nxt], 0)
    nxt_base = gsub0_ref[nxt] * sub

    def issue_block(blk):
        for u in range(sub):
            r = blk * sub + u
            row_copy(slot_ref[nxt_base + r], 1 - cur, r).start()

    def out_copy(r0, rows):
        row0 = s0 * sub + r0
        return pltpu.make_async_copy(acc.at[pl.ds(r0, rows), :], outs_hbm.at[pl.ds(row0, rows), :], sem_out)

    def compute(r0, rows, issue_blk=None):
        los, his = [], []
        for j in range(pieces):
            w = stage[cur, pl.ds(r0 * pieces + j, rows, stride=pieces), :]
            los.append(lax.bitcast_convert_type(w << 16, F32).astype(BF16))
            his.append(lax.bitcast_convert_type(w & jnp.uint32(0xFFFF0000), F32).astype(BF16))
        xl = jnp.concatenate(los, axis=1)
        xh = jnp.concatenate(his, axis=1)
        hg = (jnp.dot(xl, w1g_ref[:half, :].astype(BF16), preferred_element_type=F32)
              + jnp.dot(xh, w1g_ref[half:, :].astype(BF16), preferred_element_type=F32) + b1g_ref[...])
        hl = (jnp.dot(xl, w1l_ref[:half, :].astype(BF16), preferred_element_type=F32)
              + jnp.dot(xh, w1l_ref[half:, :].astype(BF16), preferred_element_type=F32) + b1l_ref[...])
        gate = jnp.minimum(hg, SWIGLU_LIMIT)
        lin = jnp.clip(hl, -SWIGLU_LIMIT, SWIGLU_LIMIT)
        act = gate * jax.nn.sigmoid(SWIGLU_ALPHA * gate) * (lin + 1.0)
        acc[pl.ds(r0, rows), :] += jnp.dot(act.astype(BF16), w2_ref[...].astype(BF16),
                                           preferred_element_type=F32)
        if issue_blk is not None:
            issue_block(issue_blk)

        @pl.when(c == last)
        def _():
            out_copy(r0, rows).start()

    @pl.when(ns > 0)
    def _():
        @pl.when(c == 0)
        def _load():
            @pl.when(g == 0)
            def _():
                _build_slot_table(dest_ref, cnt_ref, pst_ref, slot_ref, n_tok)
                issue_rows(0, 0, 0, ns)

            def wt(i, carry):
                for u in range(ROW_UNROLL):
                    row_copy(0, cur, i * ROW_UNROLL + u).wait()
                return carry
            lax.fori_loop(0, ns * (sub // ROW_UNROLL), wt, 0)

            def init(s, carry):
                acc[pl.ds(pl.multiple_of(s * sub, sub), sub), :] = jnp.broadcast_to(
                    b2_ref[...], (sub, acc.shape[1]))
                return carry
            lax.fori_loop(0, ns, init, 0)

        full = TRIP_SUBS * sub
        n_full = ns // TRIP_SUBS

        def for_each_trip(fn, fn_full=None):
            def trip(i, carry):
                fn(pl.multiple_of(i * full, full), full)
                return carry
            if fn_full is None:
                lax.fori_loop(0, n_full, trip, 0)
            else:
                fn_full()
            r0 = n_full * full
            width = full // 2
            while width >= sub:
                has = (ns * sub) & width

                @pl.when(has != 0)
                def _(r0=r0, width=width):
                    fn(pl.multiple_of(r0, width), width)
                r0 = r0 + has
                width //= 2

        def full_trips():
            n_issue = jnp.clip(need - c * n_full, 0, n_full)

            def with_issue(i, carry):
                compute(pl.multiple_of(i * full, full), full, issue_blk=c * n_full + i)
                return carry

            def plain(i, carry):
                compute(pl.multiple_of(i * full, full), full)
                return carry
            lax.fori_loop(0, n_issue, with_issue, 0)
            lax.fori_loop(n_issue, n_full, plain, 0)

        for_each_trip(compute, full_trips)

        @pl.when(c == last)
        def _drain():
            done = jnp.minimum(need, n_chunks * n_full)
            issue_rows(nxt, 1 - cur, done, need - done)
            for_each_trip(lambda r0, rows: out_copy(r0, rows).wait())

            @pl.when(jnp.logical_or(g + 1 >= n_groups, gns_ref[nxt] == 0))
            def _tail():
                acc[pl.ds(0, sub), :] = jnp.zeros((sub, acc.shape[1]), F32)
                first = s0 + ns
                n_tail = outs_hbm.shape[0] // sub - first

                def tail_copy(i):
                    row0 = pl.multiple_of((first + i) * sub, sub)
                    return pltpu.make_async_copy(acc.at[pl.ds(0, sub), :], outs_hbm.at[pl.ds(row0, sub), :],
                                                 sem_out)

                def st(i, carry):
                    tail_copy(i).start()
                    return carry

                def wt(i, carry):
                    tail_copy(i).wait()
                    return carry
                lax.fori_loop(0, n_tail, st, 0)
                lax.fori_loop(0, n_tail, wt, 0)


def _expert_call(gexp, gsub0, gns, dest_flat, counts, pstart, n_slots, hp, w1, b1, w2, b2):
    half = w1.shape[1] // 2
    n_exp, d, ff2 = w1.shape
    ff = ff2 // 2
    tc = FF_CHUNK
    n_chunks = ff // tc
    n_groups = gexp.shape[0]
    rows = GROUP_SUBS * SLOT_BLOCK

    def chunk(c, gn, g):
        return jnp.where(gn[g] > 0, c, n_chunks - 1)

    grid_spec = pltpu.PrefetchScalarGridSpec(
        num_scalar_prefetch=6,
        grid=(n_groups, n_chunks),
        in_specs=[
            pl.BlockSpec(memory_space=pl.ANY),
            pl.BlockSpec((None, d, tc), lambda g, c, ge, gs, gn, *_: (ge[g], 0, chunk(c, gn, g))),
            pl.BlockSpec((None, d, tc), lambda g, c, ge, gs, gn, *_: (ge[g], 0, n_chunks + chunk(c, gn, g))),
            pl.BlockSpec((None, 1, tc), lambda g, c, ge, gs, gn, *_: (ge[g], 0, chunk(c, gn, g))),
            pl.BlockSpec((None, 1, tc), lambda g, c, ge, gs, gn, *_: (ge[g], 0, n_chunks + chunk(c, gn, g))),
            pl.BlockSpec((None, tc, d), lambda g, c, ge, gs, gn, *_: (ge[g], chunk(c, gn, g), 0)),
            pl.BlockSpec((None, 1, d), lambda g, c, ge, gs, gn, *_: (ge[g], 0, 0)),
        ],
        out_specs=pl.BlockSpec(memory_space=pl.ANY),
        scratch_shapes=[
            pltpu.VMEM((2, rows * (half // LANES), LANES), U32),
            pltpu.VMEM((rows, d), F32),
            pltpu.SMEM((n_slots + SLOT_BLOCK,), I32),
            pltpu.SemaphoreType.DMA((2,)), pltpu.SemaphoreType.DMA(()),
        ],
    )
    return pl.pallas_call(
        functools.partial(_expert_body, n_chunks=n_chunks, n_groups=n_groups,
                          n_tok=dest_flat.shape[0] // TOP_K),
        grid_spec=grid_spec,
        out_shape=jax.ShapeDtypeStruct((n_slots, d), F32),
        compiler_params=_cparams(("arbitrary", "arbitrary")),
        name="expert",
    )(gexp, gsub0, gns, dest_flat, counts, pstart, hp, w1, w1, b1.reshape(n_exp, 1, ff2),
      b1.reshape(n_exp, 1, ff2), w2, b2.reshape(n_exp, 1, d))


def _final_body(dest_ref, x1_ref, w4_ref, p_ref, wple_ref, wpg_ref, bpg_ref, npl_ref, outs_hbm,
                o_ref, gbuf, sems, *, tm, n_steps):
    i = pl.program_id(0)
    cur = i % 2

    def row_copy(slot, buf, j, q, u):
        return pltpu.make_async_copy(outs_hbm.at[pl.ds(slot, 1), :], gbuf.at[buf, j, q, pl.ds(u, 1), :],
                                     sems.at[buf])

    def issue(step, buf):
        def rows(q, carry):
            for u in range(SUBLANES):
                t = step * tm + q * SUBLANES + u
                for j in range(TOP_K):
                    row_copy(dest_ref[t * TOP_K + j], buf, j, q, u).start(priority=j % 2)
            return carry
        lax.fori_loop(0, tm // SUBLANES, rows, 0)

    @pl.when(i == 0)
    def _():
        issue(0, 0)

    pe = jnp.dot(p_ref[...].astype(BF16), wple_ref[...], preferred_element_type=F32)

    def wt(q, carry):
        for u in range(SUBLANES):
            for j in range(TOP_K):
                row_copy(0, cur, j, q, u).wait()
        return carry
    lax.fori_loop(0, tm // SUBLANES, wt, 0)

    w4 = w4_ref[...]
    d = o_ref.shape[1]
    y = gbuf[cur, 0].reshape(tm, d) * w4[:, 0:1]
    for j in range(1, TOP_K):
        y = y + gbuf[cur, j].reshape(tm, d) * w4[:, j:j + 1]
    x2 = x1_ref[...] + y
    hn = _rmsnorm_rows(x2, npl_ref[...])
    gate = jax.nn.sigmoid(jnp.dot(hn.astype(BF16), wpg_ref[...], preferred_element_type=F32) + bpg_ref[...])
    o_ref[...] = x2 + gate * pe

    nxt = jnp.minimum(i + 1, n_steps - 1)
    for q in range(tm // SUBLANES):
        for u in range(SUBLANES):
            t = nxt * tm + q * SUBLANES + u
            for j in range(TOP_K):
                row_copy(dest_ref[t * TOP_K + j], 1 - cur, j, q, u).start(priority=j % 2)

    @pl.when(i == n_steps - 1)
    def _():
        def wt_last(q, carry):
            for u in range(SUBLANES):
                for j in range(TOP_K):
                    row_copy(0, 1 - cur, j, q, u).wait()
            return carry
        lax.fori_loop(0, tm // SUBLANES, wt_last, 0)


def _final_call(dest_flat, x1, w4, p2d, wple, wpg, bpg_row, npl_row, outs, tm=256):
    n, d = x1.shape
    tm = min(tm, n)
    pd = p2d.shape[1]
    res = lambda shape: pl.BlockSpec(shape, lambda i, dst: (0,) * len(shape), pipeline_mode=pl.Buffered(1))
    grid_spec = pltpu.PrefetchScalarGridSpec(
        num_scalar_prefetch=1,
        grid=(n // tm,),
        in_specs=[pl.BlockSpec((tm, d), lambda i, dst: (i, 0)),
                  pl.BlockSpec((tm, LANES), lambda i, dst: (i, 0)),
                  pl.BlockSpec((tm, pd), lambda i, dst: (i, 0)),
                  res((pd, d)), res((d, d)), res((1, d)), res((1, d)),
                  pl.BlockSpec(memory_space=pl.ANY)],
        out_specs=pl.BlockSpec((tm, d), lambda i, dst: (i, 0)),
        scratch_shapes=[pltpu.VMEM((2, TOP_K, tm // SUBLANES, SUBLANES, d), F32),
                        pltpu.SemaphoreType.DMA((2,))],
    )
    return pl.pallas_call(
        functools.partial(_final_body, tm=tm, n_steps=n // tm),
        grid_spec=grid_spec,
        out_shape=jax.ShapeDtypeStruct((n, d), F32),
        compiler_params=_cparams(("arbitrary",)),
        name="final",
    )(dest_flat, x1, w4, p2d, wple, wpg, bpg_row, npl_row, outs)


def _t5_bucket(rel):
    n = -rel
    nb = REL_BUCKETS // 2
    ret = jnp.where(n < 0, nb, 0)
    n = jnp.abs(n)
    max_exact = nb // 2
    large = max_exact + (jnp.log(jnp.maximum(n, 1).astype(jnp.float32) / max_exact)
                         / math.log(REL_MAX_DIST / max_exact) * (nb - max_exact)).astype(jnp.int32)
    large = jnp.minimum(large, nb - 1)
    return ret + jnp.where(n < max_exact, n, large)


def _bias_body(rv_ref, o_ref, *, t):
    x = jnp.broadcast_to(rv_ref[...], (t, rv_ref.shape[1]))
    o_ref[...] = pltpu.roll(x, 1, 1, stride=1, stride_axis=0)[:, t:]


def _diff_bias_table(rel_bias, seq, t):
    rel = jnp.arange(seq + t) - (seq - 1)
    rv = (jnp.transpose(rel_bias[_t5_bucket(rel)], (1, 0)).astype(F32) * LOG2E)[:, None, :]
    return pl.pallas_call(
        functools.partial(_bias_body, t=t),
        grid=(N_HEADS,),
        in_specs=[pl.BlockSpec((None, 1, seq + t), lambda h: (h, 0, 0))],
        out_specs=pl.BlockSpec((None, t, seq), lambda h: (h, 0, 0)),
        out_shape=jax.ShapeDtypeStruct((N_HEADS, t, seq), F32),
        compiler_params=_cparams(("parallel",)),
        name="bias_table",
    )(rv)


def _pad_cols(a, width):
    return jnp.pad(a, ((0, 0), (0, width - a.shape[1])))


def _group_table(counts):
    n_sub = (counts + SLOT_BLOCK - 1) // SLOT_BLOCK
    sub_start = jnp.cumsum(n_sub) - n_sub
    n_grp = (n_sub + GROUP_SUBS - 1) // GROUP_SUBS
    grp_end = jnp.cumsum(n_grp)
    total = grp_end[-1]
    return n_sub, sub_start, n_grp, grp_end, total


def _layer(x, p_l, w_in, b_gate, b_forget, dq_norm, dk_norm, fq_norm, fk_norm, lq1, lk1, lq2, lk2,
           lambda_init, subln, w_up_a, w_up_b, w_out, rel_bias, norm_mix, norm_moe, w_router,
           b_router, w1, b1, w2, b2, norm_ple, w_ple_gate, b_ple_gate, w_ple):
    b, s, d = x.shape
    n = b * s
    width = N_HEADS * HEAD_DIM
    t = min(ATTN_T, s)
    x2d = x.reshape(n, d)
    row = lambda v: v.reshape(1, -1).astype(F32)

    f_row = 6 * width
    wt_in = jnp.swapaxes(w_in, 0, 1)
    h, f_logit = _norm_call(x2d, row(norm_mix), wt_in, f_row)
    cum = _cum_call(f_logit, _pad_cols(row(b_forget), LANES), b, s)
    cum_bhs = jnp.transpose(cum[:, :N_HEADS].reshape(b, s, N_HEADS), (0, 2, 1))
    cum_row = cum_bhs[:, :, None, :]

    diff_scale = DIFF_QK_DIM ** -0.5 * LOG2E
    fox_scale = HEAD_DIM ** -0.5 * LOG2E
    gain_d = jnp.concatenate([jnp.tile(dq_norm * diff_scale, 2 * N_HEADS), jnp.tile(dk_norm, 2 * N_HEADS)])
    gain_f = jnp.concatenate([jnp.tile(fq_norm * fox_scale, N_HEADS), jnp.tile(fk_norm, N_HEADS)])
    zeros_w = jnp.zeros((1, width), F32)
    dqk = _proj_call(h, wt_in, 0, 2 * width, row(gain_d), "norm64", "proj_dqk")
    dv = _proj_call(h, wt_in, 2 * width, width, zeros_w, "plain", "proj_dv")
    fqk = _proj_call(h, wt_in, 3 * width, 2 * width, row(gain_f), "norm128", "proj_fqk")
    fv = _proj_call(h, wt_in, 5 * width, width, zeros_w, "plain", "proj_fv")
    gates = _proj_call(h, wt_in, f_row + N_HEADS, 2 * d, row(b_gate), "gate", "proj_gate")

    lam_rows = [row(v) for v in (lq1, lk1, lq2, lk2)]
    bias = _diff_bias_table(rel_bias, s, t)
    od = _diff_attn_call(dqk.reshape(b, s, 2 * width), dv.reshape(b, s, width), bias, lam_rows,
                         row(subln), lambda_init, t)
    of = _fox_attn_call(fqk.reshape(b, s, 2 * width), fv.reshape(b, s, width), cum.reshape(b, s, LANES),
                        cum_row, t)

    br_pad = jnp.full((1, LANES), NEG, F32).at[0, :N_EXPERTS].set(b_router.astype(F32))
    wr_pad = jnp.pad(jnp.swapaxes(w_router.astype(F32), 0, 1), ((0, LANES - N_EXPERTS), (0, 0)))
    wr_hi = wr_pad.astype(BF16)
    wr_lo = (wr_pad - wr_hi.astype(F32)).astype(BF16)
    x1, hp, logits = _mix_call(od.reshape(n, width), of.reshape(n, width), gates, x2d,
                               w_up_a.astype(BF16), w_up_b.astype(BF16), w_out.astype(BF16),
                               row(norm_moe), wr_hi, wr_lo, br_pad)

    dest128, w4, cnt, ps_row = _route_call(logits)
    counts = cnt[0, :N_EXPERTS].astype(I32)
    pstart = ps_row[0, :N_EXPERTS].astype(I32)
    dest_flat = dest128[:, :TOP_K].reshape(-1)

    n_slots = n * TOP_K + N_EXPERTS * SLOT_BLOCK

    n_sub, sub_start, n_grp, grp_end, total = _group_table(counts)
    max_groups = N_EXPERTS + (n_slots // SLOT_BLOCK) // GROUP_SUBS
    gidx = jnp.arange(max_groups, dtype=I32)
    gvalid = gidx < total
    gsafe = jnp.minimum(gidx, total - 1)
    gexp = jnp.searchsorted(grp_end, gsafe, side="right").astype(I32)
    kth = gsafe - (grp_end - n_grp)[gexp]
    gsub0 = (sub_start[gexp] + kth * GROUP_SUBS).astype(I32)
    gns = jnp.where(gvalid, jnp.minimum(GROUP_SUBS, n_sub[gexp] - kth * GROUP_SUBS), 0).astype(I32)
    outs = _expert_call(gexp, gsub0, gns, dest_flat, counts, pstart, n_slots, hp, w1, b1, w2, b2)

    out = _final_call(dest_flat, x1, w4, p_l.reshape(n, -1), w_ple.astype(BF16),
                      w_ple_gate.astype(BF16), row(b_ple_gate), row(norm_ple), outs)
    return out.reshape(b, s, d)


def kernel(x, p, w_in, b_gate, b_forget, dq_norm, dk_norm, fq_norm, fk_norm, lambda_q1, lambda_k1,
           lambda_q2, lambda_k2, subln, w_up_a, w_up_b, w_out, rel_bias, norm_mix, norm_moe,
           w_router, b_router, w1, b1, w2, b2, norm_ple, w_ple_gate, b_ple_gate, w_ple):
    for i in range(w_in.shape[0]):
        lambda_init = 0.8 - 0.6 * math.exp(-0.3 * i)
        x = _layer(x, p[i], w_in[i], b_gate[i], b_forget[i], dq_norm[i], dk_norm[i], fq_norm[i],
                   fk_norm[i], lambda_q1[i], lambda_k1[i], lambda_q2[i], lambda_k2[i], lambda_init,
                   subln[i], w_up_a[i], w_up_b[i], w_out[i], rel_bias, norm_mix[i], norm_moe[i],
                   w_router[i], b_router[i], w1[i], b1[i], w2[i], b2[i], norm_ple[i],
                   w_ple_gate[i], b_ple_gate[i], w_ple[i])
    return x
```

```python
import functools
import math

import jax
import jax.numpy as jnp
from jax import lax
from jax.experimental import pallas as pl
from jax.experimental.pallas import tpu as pltpu

F32 = jnp.float32
BF16 = jnp.bfloat16
U32 = jnp.uint32
I32 = jnp.int32
HIGHEST = lax.Precision.HIGHEST

N_HEADS = 8
HEAD_DIM = 128
DIFF_QK_DIM = 64
CHUNK = 64
REL_BUCKETS = 32
REL_MAX_DIST = 128
N_EXPERTS = 32
TOP_K = 4
SWIGLU_LIMIT = 7.0
SWIGLU_ALPHA = 1.702
RMS_EPS = 1e-6
NEG = -1e30
LOG2E = math.log2(math.e)

LANES = 128
SUBLANES = 8
SLOT_BLOCK = 128
GROUP_SUBS = 10
TRIP_SUBS = 4
FF_CHUNK = 512
ATTN_T = 256
VMEM_LIMIT = 56 * 1024 * 1024


_NT = (((1,), (1,)), ((), ()))


def _cparams(sem):
    return pltpu.CompilerParams(dimension_semantics=sem, vmem_limit_bytes=VMEM_LIMIT)


def _rmsnorm_rows(x, g):
    ms = jnp.mean(x * x, axis=-1, keepdims=True)
    return x * lax.rsqrt(ms + RMS_EPS) * g


def _norm_body(x_ref, g_ref, wf_ref, h_ref, fl_ref):
    y = _rmsnorm_rows(x_ref[...], g_ref[...])
    y_hi = y.astype(BF16)
    h_ref[...] = y_hi
    rowi = lax.broadcasted_iota(I32, (LANES, 1), 0)
    wf = jnp.where(rowi < N_HEADS, wf_ref[...], 0.0)
    y_lo = (y - y_hi.astype(F32)).astype(BF16)
    wf_hi = wf.astype(BF16)
    wf_lo = (wf - wf_hi.astype(F32)).astype(BF16)
    fl_ref[...] = (lax.dot_general(y_hi, wf_hi, _NT, preferred_element_type=F32)
                   + lax.dot_general(y_lo, wf_hi, _NT, preferred_element_type=F32)
                   + lax.dot_general(y_hi, wf_lo, _NT, preferred_element_type=F32))


def _norm_call(x2d, g_row, wt_in, f_row, tm=512):
    n, d = x2d.shape
    tm = min(tm, n)
    assert f_row % LANES == 0
    return pl.pallas_call(
        _norm_body,
        grid=(n // tm,),
        in_specs=[pl.BlockSpec((tm, d), lambda i: (i, 0)),
                  pl.BlockSpec((1, d), lambda i: (0, 0)),
                  pl.BlockSpec((LANES, d), lambda i: (f_row // LANES, 0))],
        out_specs=[pl.BlockSpec((tm, d), lambda i: (i, 0)),
                   pl.BlockSpec((tm, LANES), lambda i: (i, 0))],
        out_shape=[jax.ShapeDtypeStruct((n, d), BF16),
                   jax.ShapeDtypeStruct((n, LANES), F32)],
        compiler_params=_cparams(("parallel",)),
        name="norm",
    )(x2d, g_row, wt_in)


def _cum_body(fl_ref, bf_ref, cum_ref):
    s = fl_ref.shape[0]
    r = lax.broadcasted_iota(I32, (LANES, LANES), 0)
    c = lax.broadcasted_iota(I32, (LANES, LANES), 1)
    tri = jnp.where(r >= c, 1.0, 0.0).astype(F32)
    carry = jnp.zeros((1, LANES), F32)
    for blk in range(s // LANES):
        rows = slice(blk * LANES, (blk + 1) * LANES)
        lf = jax.nn.log_sigmoid(fl_ref[rows, :] + bf_ref[...])
        cb = jnp.dot(tri, lf, precision=HIGHEST, preferred_element_type=F32) + carry
        cum_ref[rows, :] = cb * LOG2E
        carry = cb[LANES - 1:LANES, :]


def _cum_call(fl, bf_row, batch, seq):
    return pl.pallas_call(
        _cum_body,
        grid=(batch,),
        in_specs=[pl.BlockSpec((seq, LANES), lambda b: (b, 0)),
                  pl.BlockSpec((1, LANES), lambda b: (0, 0))],
        out_specs=pl.BlockSpec((seq, LANES), lambda b: (b, 0)),
        out_shape=jax.ShapeDtypeStruct(fl.shape, F32),
        compiler_params=_cparams(("parallel",)),
        name="cum",
    )(fl, bf_row)


def _proj_epilogue(acc, aux_ref, o_ref, kind):
    tn = acc.shape[1]
    if kind == "plain":
        o_ref[...] = acc.astype(o_ref.dtype)
    elif kind == "gate":
        o_ref[...] = jax.nn.sigmoid(acc + aux_ref[...]).astype(o_ref.dtype)
    else:
        lo = lax.broadcasted_iota(I32, (1, LANES), 1) < DIFF_QK_DIM
        for s in range(tn // LANES):
            cols = slice(s * LANES, (s + 1) * LANES)
            blk = acc[:, cols]
            sq = blk * blk
            if kind == "norm128":
                ms = jnp.mean(sq, axis=-1, keepdims=True)
            else:
                s_lo = jnp.sum(jnp.where(lo, sq, 0.0), axis=-1, keepdims=True)
                s_hi = jnp.sum(jnp.where(lo, 0.0, sq), axis=-1, keepdims=True)
                ms = jnp.where(lo, s_lo, s_hi) * (1.0 / DIFF_QK_DIM)
            o_ref[:, cols] = (blk * lax.rsqrt(ms + RMS_EPS) * aux_ref[:, cols]).astype(o_ref.dtype)


def _proj_body(h_ref, wt_ref, aux_ref, o_ref, *, kind):
    acc = lax.dot_general(h_ref[...], wt_ref[...].astype(BF16), _NT, preferred_element_type=F32)
    _proj_epilogue(acc, aux_ref, o_ref, kind)


def _proj_shift_body(h_ref, wa_ref, wb_ref, aux_ref, o_ref, *, kind, shift):
    wt = jnp.concatenate([wa_ref[shift:, :], wb_ref[...]], axis=0)
    acc = lax.dot_general(h_ref[...], wt.astype(BF16), _NT, preferred_element_type=F32)
    _proj_epilogue(acc, aux_ref, o_ref, kind)


def _proj_call(h, wt, row0, nrows, aux_row, kind, name, tm=1024, tn=1024):
    m, k = h.shape
    tm = min(tm, m)
    assert nrows % tn == 0 and m % tm == 0
    shift = row0 % tn
    base = row0 - shift
    h_spec = pl.BlockSpec((tm, k), lambda i, j: (i, 0))
    w_spec = pl.BlockSpec((tn, k), lambda i, j: (base // tn + j, 0))
    aux_spec = pl.BlockSpec((1, tn), lambda i, j: (0, j))
    if shift == 0:
        body, w_specs, ws = functools.partial(_proj_body, kind=kind), [w_spec], [wt]
    else:
        assert shift % SUBLANES == 0 and tn % shift == 0
        body = functools.partial(_proj_shift_body, kind=kind, shift=shift)
        w_specs = [w_spec, pl.BlockSpec((shift, k), lambda i, j: ((base + (j + 1) * tn) // shift, 0))]
        ws = [wt, wt]
    return pl.pallas_call(
        body,
        grid=(m // tm, nrows // tn),
        in_specs=[h_spec] + w_specs + [aux_spec],
        out_specs=pl.BlockSpec((tm, tn), lambda i, j: (i, j)),
        out_shape=jax.ShapeDtypeStruct((m, nrows), BF16),
        compiler_params=_cparams(("parallel", "parallel")),
        name=name,
    )(h, *ws, aux_row)


DIFF_HEADS_PER_STEP = 4
FOX_HEADS_PER_STEP = 8


def _softmax_step(s, m_ref, acc_ref, v1):
    m_old = m_ref[...]
    mn = jnp.maximum(m_old, jnp.max(s, axis=-1, keepdims=True))
    alpha = jnp.exp2(m_old - mn)
    p = jnp.exp2(s - jnp.tile(mn, (1, s.shape[1] // LANES))).astype(BF16)
    acc_ref[...] = (jnp.tile(alpha, (1, acc_ref.shape[1] // LANES)) * acc_ref[...]
                    + jnp.dot(p, v1, preferred_element_type=F32))
    m_ref[...] = mn


def _causal_sweep(step, qi, t):
    def pair(i, carry):
        step(2 * i, 2 * t, None)
        return carry
    lax.fori_loop(0, qi // 2, pair, 0)

    @pl.when(qi % 2 == 1)
    def _():
        step(qi - 1, t, None)
    step(qi, t, 0)


def _with_ones(v):
    return jnp.concatenate([v, jnp.ones_like(v)], axis=1)


def _normalised(acc):
    return acc[:, :HEAD_DIM] / acc[:, HEAD_DIM:]


def _diff_attn_body(lq1_ref, lk1_ref, lq2_ref, lk2_ref, g_ref, q_ref, k_ref, v_ref, bias_ref,
                    o_ref, m_sc, acc_sc, *, t, nh, lambda_init):
    qi = pl.program_id(2)
    s_len = k_ref.shape[0]
    m_sc[...] = jnp.full(m_sc.shape, NEG, F32)
    acc_sc[...] = jnp.zeros(acc_sc.shape, F32)
    lo = lax.broadcasted_iota(I32, (t, HEAD_DIM), 1) < DIFF_QK_DIM

    def step(kb, w, rel):
        k0 = pl.multiple_of(kb * t, t)
        off = pl.multiple_of((kb - qi) * t + (s_len - t), LANES)
        masked = rel is not None
        if masked:
            r = lax.broadcasted_iota(I32, (t, w), 0)
            c = lax.broadcasted_iota(I32, (t, w), 1) + rel
            chunk_of = lambda pos: lax.shift_right_arithmetic(pos, jnp.int32(CHUNK.bit_length() - 1))
            ok = chunk_of(c) <= chunk_of(r)
        for h in range(nh):
            cols = slice(h * HEAD_DIM, (h + 1) * HEAD_DIM)
            q = q_ref[:, cols]
            zero = jnp.zeros_like(q)
            k = k_ref[pl.ds(k0, w), cols]
            v1 = _with_ones(v_ref[pl.ds(k0, w), cols])
            b = bias_ref[h, :, pl.ds(off, w)]
            for mp, qm in enumerate((jnp.where(lo, q, zero), jnp.where(lo, zero, q))):
                s = lax.dot_general(qm, k, _NT, preferred_element_type=F32) + b
                if masked:
                    s = jnp.where(ok, s, NEG)
                _softmax_step(s, m_sc.at[2 * h + mp], acc_sc.at[2 * h + mp], v1)

    _causal_sweep(step, qi, t)

    lam = (jnp.exp(jnp.sum(lq1_ref[...] * lk1_ref[...], axis=-1, keepdims=True))
           - jnp.exp(jnp.sum(lq2_ref[...] * lk2_ref[...], axis=-1, keepdims=True)) + lambda_init)
    for h in range(nh):
        o = _normalised(acc_sc[2 * h]) - lam * _normalised(acc_sc[2 * h + 1])
        o = _rmsnorm_rows(o, g_ref[...]) * (1.0 - lambda_init)
        o_ref[:, h * HEAD_DIM:(h + 1) * HEAD_DIM] = o.astype(o_ref.dtype)


def _diff_attn_call(qk, v, bias, lam_rows, subln_row, lambda_init, t):
    b, s, _ = v.shape
    nh = DIFF_HEADS_PER_STEP
    wb = nh * HEAD_DIM
    n_hg = N_HEADS // nh
    vec = lambda w: pl.BlockSpec((1, w), lambda hg, bb, qi: (0, 0))
    return pl.pallas_call(
        functools.partial(_diff_attn_body, t=t, nh=nh, lambda_init=lambda_init),
        grid=(n_hg, b, s // t),
        in_specs=[vec(DIFF_QK_DIM)] * 4 + [
            vec(HEAD_DIM),
            pl.BlockSpec((None, t, wb), lambda hg, bb, qi: (bb, qi, hg)),
            pl.BlockSpec((None, s, wb), lambda hg, bb, qi: (bb, 0, n_hg + hg)),
            pl.BlockSpec((None, s, wb), lambda hg, bb, qi: (bb, 0, hg)),
            pl.BlockSpec((nh, t, s), lambda hg, bb, qi: (hg, 0, 0))],
        out_specs=pl.BlockSpec((None, t, wb), lambda hg, bb, qi: (bb, qi, hg)),
        out_shape=jax.ShapeDtypeStruct(v.shape, BF16),
        scratch_shapes=[pltpu.VMEM((2 * nh, t, LANES), F32), pltpu.VMEM((2 * nh, t, 2 * HEAD_DIM), F32)],
        compiler_params=_cparams(("parallel", "parallel", "parallel")),
        name="diff_attn",
    )(*lam_rows, subln_row, qk, qk, v, bias)


def _fox_attn_body(q_ref, k_ref, v_ref, cq_ref, ck_ref, o_ref, m_sc, acc_sc, cq_sc, *, t, nh):
    qi = pl.program_id(2)
    m_sc[...] = jnp.full(m_sc.shape, NEG, F32)
    acc_sc[...] = jnp.zeros(acc_sc.shape, F32)
    hg = pl.program_id(0)
    lane = lax.broadcasted_iota(I32, (t, LANES), 1)
    cq_all = cq_ref[...]
    for h in range(nh):
        col = jnp.sum(jnp.where(lane == hg * nh + h, cq_all, 0.0), axis=-1, keepdims=True)
        cq_sc[h] = jnp.broadcast_to(col, (t, LANES))

    def step(kb, w, rel):
        k0 = pl.multiple_of(kb * t, t)
        masked = rel is not None
        if masked:
            r = lax.broadcasted_iota(I32, (t, w), 0)
            c = lax.broadcasted_iota(I32, (t, w), 1) + rel
            ok = c <= r
        for h in range(nh):
            cols = slice(h * HEAD_DIM, (h + 1) * HEAD_DIM)
            k = k_ref[pl.ds(k0, w), cols]
            v1 = _with_ones(v_ref[pl.ds(k0, w), cols])
            s = lax.dot_general(q_ref[:, cols], k, _NT, preferred_element_type=F32)
            s = s + jnp.tile(cq_sc[h], (1, w // LANES)) - ck_ref[h, :, pl.ds(k0, w)]
            if masked:
                s = jnp.where(ok, s, NEG)
            _softmax_step(s, m_sc.at[h], acc_sc.at[h], v1)

    _causal_sweep(step, qi, t)
    for h in range(nh):
        o_ref[:, h * HEAD_DIM:(h + 1) * HEAD_DIM] = _normalised(acc_sc[h]).astype(o_ref.dtype)


def _fox_attn_call(qk, v, cum, cum_row, t):
    b, s, _ = v.shape
    nh = FOX_HEADS_PER_STEP
    wb = nh * HEAD_DIM
    n_hg = N_HEADS // nh
    return pl.pallas_call(
        functools.partial(_fox_attn_body, t=t, nh=nh),
        grid=(n_hg, b, s // t),
        in_specs=[pl.BlockSpec((None, t, wb), lambda hg, bb, qi: (bb, qi, hg)),
                  pl.BlockSpec((None, s, wb), lambda hg, bb, qi: (bb, 0, n_hg + hg)),
                  pl.BlockSpec((None, s, wb), lambda hg, bb, qi: (bb, 0, hg)),
                  pl.BlockSpec((None, t, LANES), lambda hg, bb, qi: (bb, qi, 0)),
                  pl.BlockSpec((None, nh, 1, s), lambda hg, bb, qi: (bb, hg, 0, 0))],
        out_specs=pl.BlockSpec((None, t, wb), lambda hg, bb, qi: (bb, qi, hg)),
        out_shape=jax.ShapeDtypeStruct(v.shape, BF16),
        scratch_shapes=[pltpu.VMEM((nh, t, LANES), F32), pltpu.VMEM((nh, t, 2 * HEAD_DIM), F32),
                        pltpu.VMEM((nh, t, LANES), F32)],
        compiler_params=_cparams(("parallel", "parallel", "parallel")),
        name="fox_attn",
    )(qk, qk, v, cum, cum_row)


def _mix_body(od_ref, of_ref, g_ref, x_ref, wa_ref, wb_ref, wo_ref, nm_ref, wrh_ref, wrl_ref, br_ref,
              x1_ref, hp_ref, lg_ref):
    d = x_ref.shape[1]
    ua = jnp.dot(od_ref[...], wa_ref[...], preferred_element_type=F32)
    ub = jnp.dot(of_ref[...], wb_ref[...], preferred_element_type=F32)
    mixed = g_ref[:, :d].astype(F32) * ua + g_ref[:, d:].astype(F32) * ub
    x1 = x_ref[...] + jnp.dot(mixed.astype(BF16), wo_ref[...], preferred_element_type=F32)
    x1_ref[...] = x1
    hm = _rmsnorm_rows(x1, nm_ref[...])
    h_hi = hm.astype(BF16)
    h_hi32 = h_hi.astype(F32)
    h_lo = (hm - h_hi32).astype(BF16)
    lg_ref[...] = (lax.dot_general(h_hi, wrh_ref[...], _NT, preferred_element_type=F32)
                   + lax.dot_general(h_lo, wrh_ref[...], _NT, preferred_element_type=F32)
                   + lax.dot_general(h_hi, wrl_ref[...], _NT, preferred_element_type=F32) + br_ref[...])
    bits = lax.bitcast_convert_type(h_hi32, U32)
    packed = (bits[:, :d // 2] >> 16) | (bits[:, d // 2:] & jnp.uint32(0xFFFF0000))
    tm = packed.shape[0]
    pieces = packed.shape[1] // LANES
    for j in range(pieces):
        hp_ref[pl.ds(j, tm, stride=pieces), :] = packed[:, j * LANES:(j + 1) * LANES]


def _resident(shape):
    return pl.BlockSpec(shape, lambda i: (0,) * len(shape), pipeline_mode=pl.Buffered(1))


def _mix_call(od, of, gates, x2d, wa, wb, wo, nm_row, wr_hi, wr_lo, br_pad, tm=256):
    n, d = x2d.shape
    tm = min(tm, n)
    wdt = od.shape[1]
    return pl.pallas_call(
        _mix_body,
        grid=(n // tm,),
        in_specs=[pl.BlockSpec((tm, wdt), lambda i: (i, 0)),
                  pl.BlockSpec((tm, wdt), lambda i: (i, 0)),
                  pl.BlockSpec((tm, 2 * d), lambda i: (i, 0)),
                  pl.BlockSpec((tm, d), lambda i: (i, 0)),
                  _resident((wdt, d)), _resident((wdt, d)), _resident((d, d)),
                  _resident((1, d)), _resident((LANES, d)), _resident((LANES, d)), _resident((1, LANES))],
        out_specs=[pl.BlockSpec((tm, d), lambda i: (i, 0)),
                   pl.BlockSpec((tm * (d // 2 // LANES), LANES), lambda i: (i, 0)),
                   pl.BlockSpec((tm, LANES), lambda i: (i, 0))],
        out_shape=[jax.ShapeDtypeStruct((n, d), F32),
                   jax.ShapeDtypeStruct((n * (d // 2 // LANES), LANES), U32),
                   jax.ShapeDtypeStruct((n, LANES), F32)],
        compiler_params=_cparams(("parallel",)),
        name="mix",
    )(od, of, gates, x2d, wa, wb, wo, nm_row, wr_hi, wr_lo, br_pad)


def _route_body(lg_ref, dest_ref, w4_ref, cnt_ref, ps_ref, rank_sc, jm_sc, wm_sc, carry_sc, ps_sc, *, tm):
    phase = pl.program_id(0)
    i = pl.program_id(1)
    rows = pl.ds(pl.multiple_of(i * tm, tm), tm)

    @pl.when(phase == 0)
    def _select():
        @pl.when(i == 0)
        def _():
            carry_sc[...] = jnp.zeros_like(carry_sc)

        vals = lg_ref[...]
        lane = lax.broadcasted_iota(I32, vals.shape, 1).astype(F32)
        jm = jnp.zeros(vals.shape, F32)
        tops = []
        for j in range(TOP_K):
            m = jnp.max(vals, axis=-1, keepdims=True)
            idx = jnp.min(jnp.where(vals == m, lane, float(LANES)), axis=-1, keepdims=True)
            sel = lane == idx
            jm = jnp.where(sel, float(j + 1), jm)
            vals = jnp.where(sel, -jnp.inf, vals)
            tops.append(m)
        es = [jnp.exp(m - tops[0]) for m in tops]
        den = es[0] + es[1] + es[2] + es[3]
        wm = jnp.zeros(vals.shape, F32)
        for j in range(TOP_K):
            wm = jnp.where(jm == float(j + 1), es[j] / den, wm)
        sel_any = jnp.where(jm > 0.0, 1.0, 0.0)
        r = lax.broadcasted_iota(I32, (tm, tm), 0)
        c = lax.broadcasted_iota(I32, (tm, tm), 1)
        tri = jnp.where(c < r, 1.0, 0.0).astype(BF16)
        carry = carry_sc[...]
        rank_sc[rows, :] = jnp.dot(tri, sel_any.astype(BF16), preferred_element_type=F32) + carry
        jm_sc[rows, :] = jm
        wm_sc[rows, :] = wm
        carry_sc[...] = carry + jnp.sum(sel_any, axis=0, keepdims=True)

    @pl.when(phase == 1)
    def _place():
        @pl.when(i == 0)
        def _():
            cnt = carry_sc[...]
            padded = jnp.ceil(cnt * (1.0 / SLOT_BLOCK)) * float(SLOT_BLOCK)
            r = lax.broadcasted_iota(I32, (LANES, LANES), 0)
            c = lax.broadcasted_iota(I32, (LANES, LANES), 1)
            before = jnp.where(r < c, 1.0, 0.0).astype(F32)
            ps_sc[...] = jnp.dot(jnp.broadcast_to(padded, (SUBLANES, LANES)), before, precision=HIGHEST,
                                 preferred_element_type=F32)[:1]
            cnt_ref[...] = cnt
            ps_ref[...] = ps_sc[...]

        slot = rank_sc[rows, :] + ps_sc[...]
        jm = jm_sc[rows, :]
        wm = wm_sc[rows, :]
        lane = lax.broadcasted_iota(I32, jm.shape, 1)
        dest = jnp.zeros(jm.shape, F32)
        w4 = jnp.zeros(jm.shape, F32)
        for j in range(TOP_K):
            sel = jm == float(j + 1)
            dj = jnp.sum(jnp.where(sel, slot, 0.0), axis=-1, keepdims=True)
            wj = jnp.sum(jnp.where(sel, wm, 0.0), axis=-1, keepdims=True)
            dest = jnp.where(lane == j, dj, dest)
            w4 = jnp.where(lane == j, wj, w4)
        dest_ref[...] = dest.astype(I32)
        w4_ref[...] = w4


def _route_call(logits, tm=256):
    n = logits.shape[0]
    tm = min(tm, n)
    nt = n // tm
    row = pl.BlockSpec((1, LANES), lambda p, i: (0, 0))
    out_tile = pl.BlockSpec((tm, LANES), lambda p, i: (p * i, 0))
    return pl.pallas_call(
        functools.partial(_route_body, tm=tm),
        grid=(2, nt),
        in_specs=[pl.BlockSpec((tm, LANES), lambda p, i: (i * (1 - p) + (nt - 1) * p, 0))],
        out_specs=[out_tile, out_tile, row, row],
        out_shape=[jax.ShapeDtypeStruct((n, LANES), I32), jax.ShapeDtypeStruct((n, LANES), F32),
                   jax.ShapeDtypeStruct((1, LANES), F32), jax.ShapeDtypeStruct((1, LANES), F32)],
        scratch_shapes=[pltpu.VMEM((n, LANES), F32)] * 3 + [pltpu.VMEM((1, LANES), F32)] * 2,
        compiler_params=_cparams(("arbitrary", "arbitrary")),
        name="route",
    )(logits)


ROW_UNROLL = 8
SLOT_UNROLL = 16
W1_SLOTS = 3


def _build_slot_table(dest_ref, cnt_ref, pst_ref, slot_ref, n_tok):
    def pads(e, carry):
        base = pst_ref[e] + cnt_ref[e]

        def zeros(i, c2):
            for u in range(SLOT_UNROLL):
                slot_ref[base + i * SLOT_UNROLL + u] = 0
            return c2
        lax.fori_loop(0, SLOT_BLOCK // SLOT_UNROLL, zeros, 0)
        return carry
    lax.fori_loop(0, N_EXPERTS, pads, 0)

    per_trip = SLOT_UNROLL // TOP_K
    assert n_tok % per_trip == 0

    def toks(i, carry):
        for u in range(per_trip):
            t = i * per_trip + u
            for j in range(TOP_K):
                slot_ref[dest_ref[t * TOP_K + j]] = t
        return carry
    lax.fori_loop(0, n_tok // per_trip, toks, 0)


def _expert_body(gexp_ref, gsub0_ref, gns_ref, dest_ref, cnt_ref, pst_ref,
                 hp_hbm, w1_hbm, b1g_ref, b1l_ref, w2_ref, b2_ref,
                 outs_hbm,
                 stage, acc, wg_ring, wl_ring, slot_ref, sem_in, sem_out, sem_w, *, n_chunks, n_groups, n_tok):
    g = pl.program_id(0)
    c = pl.program_id(1)
    ns = gns_ref[g]
    s0 = gsub0_ref[g]
    sub = SLOT_BLOCK
    half = wg_ring.shape[1] // 2
    tc = wg_ring.shape[2]
    ff = n_chunks * tc
    last = n_chunks - 1
    step = g * n_chunks + c
    wslot = step % W1_SLOTS

    def w1_copies(grp, chunk, slot_):
        e = gexp_ref[grp]
        col = pl.multiple_of(chunk * tc, tc)
        return (pltpu.make_async_copy(w1_hbm.at[e, :, pl.ds(col, tc)], wg_ring.at[slot_], sem_w.at[slot_, 0]),
                pltpu.make_async_copy(w1_hbm.at[e, :, pl.ds(ff + col, tc)], wl_ring.at[slot_],
                                      sem_w.at[slot_, 1]))

    def w1_fetch(ahead):
        c2 = (c + ahead) % n_chunks
        g2 = g + (c + ahead) // n_chunks
        g2c = jnp.minimum(g2, n_groups - 1)

        @pl.when(jnp.logical_and(g2 < n_groups, gns_ref[g2c] > 0))
        def _():
            for cp in w1_copies(g2c, c2, (step + ahead) % W1_SLOTS):
                cp.start()
    cur = g % 2
    pieces = half // LANES

    def row_copy(tok, buf, r):
        return pltpu.make_async_copy(hp_hbm.at[pl.ds(pl.multiple_of(tok * pieces, pieces), pieces), :],
                                     stage.at[buf, pl.ds(pl.multiple_of(r * pieces, pieces), pieces), :],
                                     sem_in.at[buf])

    def issue_rows(gi, buf, blk0, n_blk):
        base = gsub0_ref[gi] * sub

        def trip(i, carry):
            for u in range(ROW_UNROLL):
                r = blk0 * sub + i * ROW_UNROLL + u
                row_copy(slot_ref[base + r], buf, r).start()
            return carry
        lax.fori_loop(0, n_blk * (sub // ROW_UNROLL), trip, 0)

    nxt = jnp.minimum(g + 1, n_groups - 1)
    need = jnp.where(jnp.logical_and(g + 1 < n_groups, gns_ref[nxt] > 0), gns_ref[nxt], 0)
    nxt_base = gsub0_ref[nxt] * sub

    def issue_block(blk):
        for u in range(sub):
            r = blk * sub + u
            row_copy(slot_ref[nxt_base + r], 1 - cur, r).start()

    def out_copy(r0, rows):
        row0 = s0 * sub + r0
        return pltpu.make_async_copy(acc.at[pl.ds(r0, rows), :], outs_hbm.at[pl.ds(row0, rows), :], sem_out)

    def compute(r0, rows, issue_blk=None):
        los, his = [], []
        for j in range(pieces):
            w = stage[cur, pl.ds(r0 * pieces + j, rows, stride=pieces), :]
            los.append(lax.bitcast_convert_type(w << 16, F32).astype(BF16))
            his.append(lax.bitcast_convert_type(w & jnp.uint32(0xFFFF0000), F32).astype(BF16))
        xl = jnp.concatenate(los, axis=1)
        xh = jnp.concatenate(his, axis=1)
        hg = (jnp.dot(xl, wg_ring[wslot, :half, :].astype(BF16), preferred_element_type=F32)
              + jnp.dot(xh, wg_ring[wslot, half:, :].astype(BF16), preferred_element_type=F32) + b1g_ref[...])
        hl = (jnp.dot(xl, wl_ring[wslot, :half, :].astype(BF16), preferred_element_type=F32)
              + jnp.dot(xh, wl_ring[wslot, half:, :].astype(BF16), preferred_element_type=F32) + b1l_ref[...])
        gate = jnp.minimum(hg, SWIGLU_LIMIT)
        lin = jnp.clip(hl, -SWIGLU_LIMIT, SWIGLU_LIMIT)
        act = gate * jax.nn.sigmoid(SWIGLU_ALPHA * gate) * (lin + 1.0)
        acc[pl.ds(r0, rows), :] += jnp.dot(act.astype(BF16), w2_ref[...].astype(BF16),
                                           preferred_element_type=F32)
        if issue_blk is not None:
            issue_block(issue_blk)

        @pl.when(c == last)
        def _():
            out_copy(r0, rows).start()

    @pl.when(ns > 0)
    def _():
        @pl.when(step == 0)
        def _():
            for ahead in range(W1_SLOTS - 1):
                w1_fetch(ahead)
        w1_fetch(W1_SLOTS - 1)
        for cp in w1_copies(g, c, wslot):
            cp.wait()

        @pl.when(c == 0)
        def _load():
            @pl.when(g == 0)
            def _():
                _build_slot_table(dest_ref, cnt_ref, pst_ref, slot_ref, n_tok)
                issue_rows(0, 0, 0, ns)

            def wt(i, carry):
                for u in range(ROW_UNROLL):
                    row_copy(0, cur, i * ROW_UNROLL + u).wait()
                return carry
            lax.fori_loop(0, ns * (sub // ROW_UNROLL), wt, 0)

            def init(s, carry):
                acc[pl.ds(pl.multiple_of(s * sub, sub), sub), :] = jnp.broadcast_to(
                    b2_ref[...], (sub, acc.shape[1]))
                return carry
            lax.fori_loop(0, ns, init, 0)

        full = TRIP_SUBS * sub
        n_full = ns // TRIP_SUBS

        def for_each_trip(fn, fn_full=None):
            def trip(i, carry):
                fn(pl.multiple_of(i * full, full), full)
                return carry
            if fn_full is None:
                lax.fori_loop(0, n_full, trip, 0)
            else:
                fn_full()
            r0 = n_full * full
            width = full // 2
            while width >= sub:
                has = (ns * sub) & width

                @pl.when(has != 0)
                def _(r0=r0, width=width):
                    fn(pl.multiple_of(r0, width), width)
                r0 = r0 + has
                width //= 2

        def full_trips():
            n_issue = jnp.clip(need - c * n_full, 0, n_full)

            def with_issue(i, carry):
                compute(pl.multiple_of(i * full, full), full, issue_blk=c * n_full + i)
                return carry

            def plain(i, carry):
                compute(pl.multiple_of(i * full, full), full)
                return carry
            lax.fori_loop(0, n_issue, with_issue, 0)
            lax.fori_loop(n_issue, n_full, plain, 0)

        for_each_trip(compute, full_trips)

        @pl.when(c == last)
        def _drain():
            done = jnp.minimum(need, n_chunks * n_full)
            issue_rows(nxt, 1 - cur, done, need - done)
            for_each_trip(lambda r0, rows: out_copy(r0, rows).wait())

            @pl.when(jnp.logical_or(g + 1 >= n_groups, gns_ref[nxt] == 0))
            def _tail():
                acc[pl.ds(0, sub), :] = jnp.zeros((sub, acc.shape[1]), F32)
                first = s0 + ns
                n_tail = outs_hbm.shape[0] // sub - first

                def tail_copy(i):
                    row0 = pl.multiple_of((first + i) * sub, sub)
                    return pltpu.make_async_copy(acc.at[pl.ds(0, sub), :], outs_hbm.at[pl.ds(row0, sub), :],
                                                 sem_out)

                def st(i, carry):
                    tail_copy(i).start()
                    return carry

                def wt(i, carry):
                    tail_copy(i).wait()
                    return carry
                lax.fori_loop(0, n_tail, st, 0)
                lax.fori_loop(0, n_tail, wt, 0)


def _expert_call(gexp, gsub0, gns, dest_flat, counts, pstart, n_slots, hp, w1, b1, w2, b2):
    half = w1.shape[1] // 2
    n_exp, d, ff2 = w1.shape
    ff = ff2 // 2
    tc = FF_CHUNK
    n_chunks = ff // tc
    n_groups = gexp.shape[0]
    rows = GROUP_SUBS * SLOT_BLOCK

    def chunk(c, gn, g):
        return jnp.where(gn[g] > 0, c, n_chunks - 1)

    grid_spec = pltpu.PrefetchScalarGridSpec(
        num_scalar_prefetch=6,
        grid=(n_groups, n_chunks),
        in_specs=[
            pl.BlockSpec(memory_space=pl.ANY),
            pl.BlockSpec(memory_space=pl.ANY),
            pl.BlockSpec((None, 1, tc), lambda g, c, ge, gs, gn, *_: (ge[g], 0, chunk(c, gn, g))),
            pl.BlockSpec((None, 1, tc), lambda g, c, ge, gs, gn, *_: (ge[g], 0, n_chunks + chunk(c, gn, g))),
            pl.BlockSpec((None, tc, d), lambda g, c, ge, gs, gn, *_: (ge[g], chunk(c, gn, g), 0)),
            pl.BlockSpec((None, 1, d), lambda g, c, ge, gs, gn, *_: (ge[g], 0, 0)),
        ],
        out_specs=pl.BlockSpec(memory_space=pl.ANY),
        scratch_shapes=[
            pltpu.VMEM((2, rows * (half // LANES), LANES), U32),
            pltpu.VMEM((rows, d), F32),
            pltpu.VMEM((W1_SLOTS, d, tc), F32), pltpu.VMEM((W1_SLOTS, d, tc), F32),
            pltpu.SMEM((n_slots + SLOT_BLOCK,), I32),
            pltpu.SemaphoreType.DMA((2,)), pltpu.SemaphoreType.DMA(()),
            pltpu.SemaphoreType.DMA((W1_SLOTS, 2)),
        ],
    )
    return pl.pallas_call(
        functools.partial(_expert_body, n_chunks=n_chunks, n_groups=n_groups,
                          n_tok=dest_flat.shape[0] // TOP_K),
        grid_spec=grid_spec,
        out_shape=jax.ShapeDtypeStruct((n_slots, d), F32),
        compiler_params=_cparams(("arbitrary", "arbitrary")),
        name="expert",
    )(gexp, gsub0, gns, dest_flat, counts, pstart, hp, w1, b1.reshape(n_exp, 1, ff2),
      b1.reshape(n_exp, 1, ff2), w2, b2.reshape(n_exp, 1, d))


def _final_body(dest_ref, x1_ref, w4_ref, p_ref, wple_ref, wpg_ref, bpg_ref, npl_ref, outs_hbm,
                o_ref, gbuf, sems, *, tm, n_steps):
    i = pl.program_id(0)
    cur = i % 2

    def row_copy(slot, buf, j, q, u):
        return pltpu.make_async_copy(outs_hbm.at[pl.ds(slot, 1), :], gbuf.at[buf, j, q, pl.ds(u, 1), :],
                                     sems.at[buf])

    def issue(step, buf):
        def rows(q, carry):
            for u in range(SUBLANES):
                t = step * tm + q * SUBLANES + u
                for j in range(TOP_K):
                    row_copy(dest_ref[t * TOP_K + j], buf, j, q, u).start(priority=j % 2)
            return carry
        lax.fori_loop(0, tm // SUBLANES, rows, 0)

    @pl.when(i == 0)
    def _():
        issue(0, 0)

    pe = jnp.dot(p_ref[...].astype(BF16), wple_ref[...], preferred_element_type=F32)

    def wt(q, carry):
        for u in range(SUBLANES):
            for j in range(TOP_K):
                row_copy(0, cur, j, q, u).wait()
        return carry
    lax.fori_loop(0, tm // SUBLANES, wt, 0)

    w4 = w4_ref[...]
    d = o_ref.shape[1]
    y = gbuf[cur, 0].reshape(tm, d) * w4[:, 0:1]
    for j in range(1, TOP_K):
        y = y + gbuf[cur, j].reshape(tm, d) * w4[:, j:j + 1]
    x2 = x1_ref[...] + y
    hn = _rmsnorm_rows(x2, npl_ref[...])
    gate = jax.nn.sigmoid(jnp.dot(hn.astype(BF16), wpg_ref[...], preferred_element_type=F32) + bpg_ref[...])
    o_ref[...] = x2 + gate * pe

    nxt = jnp.minimum(i + 1, n_steps - 1)
    for q in range(tm // SUBLANES):
        for u in range(SUBLANES):
            t = nxt * tm + q * SUBLANES + u
            for j in range(TOP_K):
                row_copy(dest_ref[t * TOP_K + j], 1 - cur, j, q, u).start(priority=j % 2)

    @pl.when(i == n_steps - 1)
    def _():
        def wt_last(q, carry):
            for u in range(SUBLANES):
                for j in range(TOP_K):
                    row_copy(0, 1 - cur, j, q, u).wait()
            return carry
        lax.fori_loop(0, tm // SUBLANES, wt_last, 0)


def _final_call(dest_flat, x1, w4, p2d, wple, wpg, bpg_row, npl_row, outs, tm=256):
    n, d = x1.shape
    tm = min(tm, n)
    pd = p2d.shape[1]
    res = lambda shape: pl.BlockSpec(shape, lambda i, dst: (0,) * len(shape), pipeline_mode=pl.Buffered(1))
    grid_spec = pltpu.PrefetchScalarGridSpec(
        num_scalar_prefetch=1,
        grid=(n // tm,),
        in_specs=[pl.BlockSpec((tm, d), lambda i, dst: (i, 0)),
                  pl.BlockSpec((tm, LANES), lambda i, dst: (i, 0)),
                  pl.BlockSpec((tm, pd), lambda i, dst: (i, 0)),
                  res((pd, d)), res((d, d)), res((1, d)), res((1, d)),
                  pl.BlockSpec(memory_space=pl.ANY)],
        out_specs=pl.BlockSpec((tm, d), lambda i, dst: (i, 0)),
        scratch_shapes=[pltpu.VMEM((2, TOP_K, tm // SUBLANES, SUBLANES, d), F32),
                        pltpu.SemaphoreType.DMA((2,))],
    )
    return pl.pallas_call(
        functools.partial(_final_body, tm=tm, n_steps=n // tm),
        grid_spec=grid_spec,
        out_shape=jax.ShapeDtypeStruct((n, d), F32),
        compiler_params=_cparams(("arbitrary",)),
        name="final",
    )(dest_flat, x1, w4, p2d, wple, wpg, bpg_row, npl_row, outs)


def _t5_bucket(rel):
    n = -rel
    nb = REL_BUCKETS // 2
    ret = jnp.where(n < 0, nb, 0)
    n = jnp.abs(n)
    max_exact = nb // 2
    large = max_exact + (jnp.log(jnp.maximum(n, 1).astype(jnp.float32) / max_exact)
                         / math.log(REL_MAX_DIST / max_exact) * (nb - max_exact)).astype(jnp.int32)
    large = jnp.minimum(large, nb - 1)
    return ret + jnp.where(n < max_exact, n, large)


def _bias_body(rv_ref, o_ref, *, t):
    x = jnp.broadcast_to(rv_ref[...], (t, rv_ref.shape[1]))
    o_ref[...] = pltpu.roll(x, 1, 1, stride=1, stride_axis=0)[:, t:]


def _diff_bias_table(rel_bias, seq, t):
    rel = jnp.arange(seq + t) - (seq - 1)
    rv = (jnp.transpose(rel_bias[_t5_bucket(rel)], (1, 0)).astype(F32) * LOG2E)[:, None, :]
    return pl.pallas_call(
        functools.partial(_bias_body, t=t),
        grid=(N_HEADS,),
        in_specs=[pl.BlockSpec((None, 1, seq + t), lambda h: (h, 0, 0))],
        out_specs=pl.BlockSpec((None, t, seq), lambda h: (h, 0, 0)),
        out_shape=jax.ShapeDtypeStruct((N_HEADS, t, seq), F32),
        compiler_params=_cparams(("parallel",)),
        name="bias_table",
    )(rv)


def _pad_cols(a, width):
    return jnp.pad(a, ((0, 0), (0, width - a.shape[1])))


def _group_table(counts):
    n_sub = (counts + SLOT_BLOCK - 1) // SLOT_BLOCK
    sub_start = jnp.cumsum(n_sub) - n_sub
    n_grp = (n_sub + GROUP_SUBS - 1) // GROUP_SUBS
    grp_end = jnp.cumsum(n_grp)
    total = grp_end[-1]
    return n_sub, sub_start, n_grp, grp_end, total


def _layer(x, p_l, w_in, b_gate, b_forget, dq_norm, dk_norm, fq_norm, fk_norm, lq1, lk1, lq2, lk2,
           lambda_init, subln, w_up_a, w_up_b, w_out, rel_bias, norm_mix, norm_moe, w_router,
           b_router, w1, b1, w2, b2, norm_ple, w_ple_gate, b_ple_gate, w_ple):
    b, s, d = x.shape
    n = b * s
    width = N_HEADS * HEAD_DIM
    t = min(ATTN_T, s)
    x2d = x.reshape(n, d)
    row = lambda v: v.reshape(1, -1).astype(F32)

    f_row = 6 * width
    wt_in = jnp.swapaxes(w_in, 0, 1)
    h, f_logit = _norm_call(x2d, row(norm_mix), wt_in, f_row)
    cum = _cum_call(f_logit, _pad_cols(row(b_forget), LANES), b, s)
    cum_bhs = jnp.transpose(cum[:, :N_HEADS].reshape(b, s, N_HEADS), (0, 2, 1))
    cum_row = cum_bhs[:, :, None, :]

    diff_scale = DIFF_QK_DIM ** -0.5 * LOG2E
    fox_scale = HEAD_DIM ** -0.5 * LOG2E
    gain_d = jnp.concatenate([jnp.tile(dq_norm * diff_scale, 2 * N_HEADS), jnp.tile(dk_norm, 2 * N_HEADS)])
    gain_f = jnp.concatenate([jnp.tile(fq_norm * fox_scale, N_HEADS), jnp.tile(fk_norm, N_HEADS)])
    zeros_w = jnp.zeros((1, width), F32)
    dqk = _proj_call(h, wt_in, 0, 2 * width, row(gain_d), "norm64", "proj_dqk")
    dv = _proj_call(h, wt_in, 2 * width, width, zeros_w, "plain", "proj_dv")
    fqk = _proj_call(h, wt_in, 3 * width, 2 * width, row(gain_f), "norm128", "proj_fqk")
    fv = _proj_call(h, wt_in, 5 * width, width, zeros_w, "plain", "proj_fv")
    gates = _proj_call(h, wt_in, f_row + N_HEADS, 2 * d, row(b_gate), "gate", "proj_gate")

    lam_rows = [row(v) for v in (lq1, lk1, lq2, lk2)]
    bias = _diff_bias_table(rel_bias, s, t)
    od = _diff_attn_call(dqk.reshape(b, s, 2 * width), dv.reshape(b, s, width), bias, lam_rows,
                         row(subln), lambda_init, t)
    of = _fox_attn_call(fqk.reshape(b, s, 2 * width), fv.reshape(b, s, width), cum.reshape(b, s, LANES),
                        cum_row, t)

    br_pad = jnp.full((1, LANES), NEG, F32).at[0, :N_EXPERTS].set(b_router.astype(F32))
    wr_pad = jnp.pad(jnp.swapaxes(w_router.astype(F32), 0, 1), ((0, LANES - N_EXPERTS), (0, 0)))
    wr_hi = wr_pad.astype(BF16)
    wr_lo = (wr_pad - wr_hi.astype(F32)).astype(BF16)
    x1, hp, logits = _mix_call(od.reshape(n, width), of.reshape(n, width), gates, x2d,
                               w_up_a.astype(BF16), w_up_b.astype(BF16), w_out.astype(BF16),
                               row(norm_moe), wr_hi, wr_lo, br_pad)

    dest128, w4, cnt, ps_row = _route_call(logits)
    counts = cnt[0, :N_EXPERTS].astype(I32)
    pstart = ps_row[0, :N_EXPERTS].astype(I32)
    dest_flat = dest128[:, :TOP_K].reshape(-1)

    n_slots = n * TOP_K + N_EXPERTS * SLOT_BLOCK

    n_sub, sub_start, n_grp, grp_end, total = _group_table(counts)
    max_groups = N_EXPERTS + (n_slots // SLOT_BLOCK) // GROUP_SUBS
    gidx = jnp.arange(max_groups, dtype=I32)
    gvalid = gidx < total
    gsafe = jnp.minimum(gidx, total - 1)
    gexp = jnp.searchsorted(grp_end, gsafe, side="right").astype(I32)
    kth = gsafe - (grp_end - n_grp)[gexp]
    gsub0 = (sub_start[gexp] + kth * GROUP_SUBS).astype(I32)
    gns = jnp.where(gvalid, jnp.minimum(GROUP_SUBS, n_sub[gexp] - kth * GROUP_SUBS), 0).astype(I32)
    outs = _expert_call(gexp, gsub0, gns, dest_flat, counts, pstart, n_slots, hp, w1, b1, w2, b2)

    out = _final_call(dest_flat, x1, w4, p_l.reshape(n, -1), w_ple.astype(BF16),
                      w_ple_gate.astype(BF16), row(b_ple_gate), row(norm_ple), outs)
    return out.reshape(b, s, d)


def kernel(x, p, w_in, b_gate, b_forget, dq_norm, dk_norm, fq_norm, fk_norm, lambda_q1, lambda_k1,
           lambda_q2, lambda_k2, subln, w_up_a, w_up_b, w_out, rel_bias, norm_mix, norm_moe,
           w_router, b_router, w1, b1, w2, b2, norm_ple, w_ple_gate, b_ple_gate, w_ple):
    for i in range(w_in.shape[0]):
        lambda_init = 0.8 - 0.6 * math.exp(-0.3 * i)
        x = _layer(x, p[i], w_in[i], b_gate[i], b_forget[i], dq_norm[i], dk_norm[i], fq_norm[i],
                   fk_norm[i], lambda_q1[i], lambda_k1[i], lambda_q2[i], lambda_k2[i], lambda_init,
                   subln[i], w_up_a[i], w_up_b[i], w_out[i], rel_bias, norm_mix[i], norm_moe[i],
                   w_router[i], b_router[i], w1[i], b1[i], w2[i], b2[i], norm_ple[i],
                   w_ple_gate[i], b_ple_gate[i], w_ple[i])
    return x
```

```python
import functools
import math

import jax
import jax.numpy as jnp
from jax import lax
from jax.experimental import pallas as pl
from jax.experimental.pallas import tpu as pltpu

F32 = jnp.float32
BF16 = jnp.bfloat16
U32 = jnp.uint32
I32 = jnp.int32
HIGHEST = lax.Precision.HIGHEST

N_HEADS = 8
HEAD_DIM = 128
DIFF_QK_DIM = 64
CHUNK = 64
REL_BUCKETS = 32
REL_MAX_DIST = 128
N_EXPERTS = 32
TOP_K = 4
SWIGLU_LIMIT = 7.0
SWIGLU_ALPHA = 1.702
RMS_EPS = 1e-6
NEG = -1e30
LOG2E = math.log2(math.e)

LANES = 128
SUBLANES = 8
SLOT_BLOCK = 128
GROUP_SUBS = 12
TRIP_SUBS = 4
FF_CHUNK = 512
ATTN_T = 256
VMEM_LIMIT = 56 * 1024 * 1024


_NT = (((1,), (1,)), ((), ()))


def _cparams(sem):
    return pltpu.CompilerParams(dimension_semantics=sem, vmem_limit_bytes=VMEM_LIMIT)


def _rmsnorm_rows(x, g):
    ms = jnp.mean(x * x, axis=-1, keepdims=True)
    return x * lax.rsqrt(ms + RMS_EPS) * g


def _norm_body(x_ref, g_ref, wf_ref, h_ref, fl_ref):
    y = _rmsnorm_rows(x_ref[...], g_ref[...])
    y_hi = y.astype(BF16)
    h_ref[...] = y_hi
    rowi = lax.broadcasted_iota(I32, (LANES, 1), 0)
    wf = jnp.where(rowi < N_HEADS, wf_ref[...], 0.0)
    y_lo = (y - y_hi.astype(F32)).astype(BF16)
    wf_hi = wf.astype(BF16)
    wf_lo = (wf - wf_hi.astype(F32)).astype(BF16)
    fl_ref[...] = (lax.dot_general(y_hi, wf_hi, _NT, preferred_element_type=F32)
                   + lax.dot_general(y_lo, wf_hi, _NT, preferred_element_type=F32)
                   + lax.dot_general(y_hi, wf_lo, _NT, preferred_element_type=F32))


def _norm_call(x2d, g_row, wt_in, f_row, tm=512):
    n, d = x2d.shape
    tm = min(tm, n)
    assert f_row % LANES == 0
    return pl.pallas_call(
        _norm_body,
        grid=(n // tm,),
        in_specs=[pl.BlockSpec((tm, d), lambda i: (i, 0)),
                  pl.BlockSpec((1, d), lambda i: (0, 0)),
                  pl.BlockSpec((LANES, d), lambda i: (f_row // LANES, 0))],
        out_specs=[pl.BlockSpec((tm, d), lambda i: (i, 0)),
                   pl.BlockSpec((tm, LANES), lambda i: (i, 0))],
        out_shape=[jax.ShapeDtypeStruct((n, d), BF16),
                   jax.ShapeDtypeStruct((n, LANES), F32)],
        compiler_params=_cparams(("parallel",)),
        name="norm",
    )(x2d, g_row, wt_in)


def _cum_body(fl_ref, bf_ref, cum_ref):
    s = fl_ref.shape[0]
    r = lax.broadcasted_iota(I32, (LANES, LANES), 0)
    c = lax.broadcasted_iota(I32, (LANES, LANES), 1)
    tri = jnp.where(r >= c, 1.0, 0.0).astype(F32)
    carry = jnp.zeros((1, LANES), F32)
    for blk in range(s // LANES):
        rows = slice(blk * LANES, (blk + 1) * LANES)
        lf = jax.nn.log_sigmoid(fl_ref[rows, :] + bf_ref[...])
        cb = jnp.dot(tri, lf, precision=HIGHEST, preferred_element_type=F32) + carry
        cum_ref[rows, :] = cb * LOG2E
        carry = cb[LANES - 1:LANES, :]


def _cum_call(fl, bf_row, batch, seq):
    return pl.pallas_call(
        _cum_body,
        grid=(batch,),
        in_specs=[pl.BlockSpec((seq, LANES), lambda b: (b, 0)),
                  pl.BlockSpec((1, LANES), lambda b: (0, 0))],
        out_specs=pl.BlockSpec((seq, LANES), lambda b: (b, 0)),
        out_shape=jax.ShapeDtypeStruct(fl.shape, F32),
        compiler_params=_cparams(("parallel",)),
        name="cum",
    )(fl, bf_row)


def _proj_epilogue(acc, aux_ref, o_ref, kind):
    tn = acc.shape[1]
    if kind == "plain":
        o_ref[...] = acc.astype(o_ref.dtype)
    elif kind == "gate":
        o_ref[...] = jax.nn.sigmoid(acc + aux_ref[...]).astype(o_ref.dtype)
    else:
        lo = lax.broadcasted_iota(I32, (1, LANES), 1) < DIFF_QK_DIM
        for s in range(tn // LANES):
            cols = slice(s * LANES, (s + 1) * LANES)
            blk = acc[:, cols]
            sq = blk * blk
            if kind == "norm128":
                ms = jnp.mean(sq, axis=-1, keepdims=True)
            else:
                s_lo = jnp.sum(jnp.where(lo, sq, 0.0), axis=-1, keepdims=True)
                s_hi = jnp.sum(jnp.where(lo, 0.0, sq), axis=-1, keepdims=True)
                ms = jnp.where(lo, s_lo, s_hi) * (1.0 / DIFF_QK_DIM)
            o_ref[:, cols] = (blk * lax.rsqrt(ms + RMS_EPS) * aux_ref[:, cols]).astype(o_ref.dtype)


def _proj_body(h_ref, wt_ref, aux_ref, o_ref, *, kind):
    acc = lax.dot_general(h_ref[...], wt_ref[...].astype(BF16), _NT, preferred_element_type=F32)
    _proj_epilogue(acc, aux_ref, o_ref, kind)


def _proj_shift_body(h_ref, wa_ref, wb_ref, aux_ref, o_ref, *, kind, shift):
    wt = jnp.concatenate([wa_ref[shift:, :], wb_ref[...]], axis=0)
    acc = lax.dot_general(h_ref[...], wt.astype(BF16), _NT, preferred_element_type=F32)
    _proj_epilogue(acc, aux_ref, o_ref, kind)


def _proj_call(h, wt, row0, nrows, aux_row, kind, name, tm=1024, tn=1024):
    m, k = h.shape
    tm = min(tm, m)
    assert nrows % tn == 0 and m % tm == 0
    shift = row0 % tn
    base = row0 - shift
    h_spec = pl.BlockSpec((tm, k), lambda i, j: (i, 0))
    w_spec = pl.BlockSpec((tn, k), lambda i, j: (base // tn + j, 0))
    aux_spec = pl.BlockSpec((1, tn), lambda i, j: (0, j))
    if shift == 0:
        body, w_specs, ws = functools.partial(_proj_body, kind=kind), [w_spec], [wt]
    else:
        assert shift % SUBLANES == 0 and tn % shift == 0
        body = functools.partial(_proj_shift_body, kind=kind, shift=shift)
        w_specs = [w_spec, pl.BlockSpec((shift, k), lambda i, j: ((base + (j + 1) * tn) // shift, 0))]
        ws = [wt, wt]
    return pl.pallas_call(
        body,
        grid=(m // tm, nrows // tn),
        in_specs=[h_spec] + w_specs + [aux_spec],
        out_specs=pl.BlockSpec((tm, tn), lambda i, j: (i, j)),
        out_shape=jax.ShapeDtypeStruct((m, nrows), BF16),
        compiler_params=_cparams(("parallel", "parallel")),
        name=name,
    )(h, *ws, aux_row)


DIFF_HEADS_PER_STEP = 8
FOX_HEADS_PER_STEP = 8


def _softmax_step(s, m_ref, acc_ref, v1):
    m_old = m_ref[...]
    mn = jnp.maximum(m_old, jnp.max(s, axis=-1, keepdims=True))
    alpha = jnp.exp2(m_old - mn)
    p = jnp.exp2(s - jnp.tile(mn, (1, s.shape[1] // LANES))).astype(BF16)
    acc_ref[...] = (jnp.tile(alpha, (1, acc_ref.shape[1] // LANES)) * acc_ref[...]
                    + jnp.dot(p, v1, preferred_element_type=F32))
    m_ref[...] = mn


def _causal_sweep(step, qi, t):
    def pair(i, carry):
        step(2 * i, 2 * t, None)
        return carry
    lax.fori_loop(0, qi // 2, pair, 0)

    @pl.when(qi % 2 == 1)
    def _():
        step(qi - 1, t, None)
    step(qi, t, 0)


def _with_ones(v):
    return jnp.concatenate([v, jnp.ones_like(v)], axis=1)


def _normalised(acc):
    return acc[:, :HEAD_DIM] / acc[:, HEAD_DIM:]


def _diff_attn_body(lq1_ref, lk1_ref, lq2_ref, lk2_ref, g_ref, q_ref, k_ref, v_ref, bias_ref,
                    o_ref, m_sc, acc_sc, *, t, nh, lambda_init):
    qi = pl.program_id(2)
    s_len = k_ref.shape[0]
    m_sc[...] = jnp.full(m_sc.shape, NEG, F32)
    acc_sc[...] = jnp.zeros(acc_sc.shape, F32)
    lo = lax.broadcasted_iota(I32, (t, HEAD_DIM), 1) < DIFF_QK_DIM

    def step(kb, w, rel):
        k0 = pl.multiple_of(kb * t, t)
        off = pl.multiple_of((kb - qi) * t + (s_len - t), LANES)
        masked = rel is not None
        if masked:
            r = lax.broadcasted_iota(I32, (t, w), 0)
            c = lax.broadcasted_iota(I32, (t, w), 1) + rel
            chunk_of = lambda pos: lax.shift_right_arithmetic(pos, jnp.int32(CHUNK.bit_length() - 1))
            ok = chunk_of(c) <= chunk_of(r)
        for h in range(nh):
            cols = slice(h * HEAD_DIM, (h + 1) * HEAD_DIM)
            q = q_ref[:, cols]
            zero = jnp.zeros_like(q)
            k = k_ref[pl.ds(k0, w), cols]
            v1 = _with_ones(v_ref[pl.ds(k0, w), cols])
            b = bias_ref[h, :, pl.ds(off, w)]
            for mp, qm in enumerate((jnp.where(lo, q, zero), jnp.where(lo, zero, q))):
                s = lax.dot_general(qm, k, _NT, preferred_element_type=F32) + b
                if masked:
                    s = jnp.where(ok, s, NEG)
                _softmax_step(s, m_sc.at[2 * h + mp], acc_sc.at[2 * h + mp], v1)

    _causal_sweep(step, qi, t)

    lam = (jnp.exp(jnp.sum(lq1_ref[...] * lk1_ref[...], axis=-1, keepdims=True))
           - jnp.exp(jnp.sum(lq2_ref[...] * lk2_ref[...], axis=-1, keepdims=True)) + lambda_init)
    for h in range(nh):
        o = _normalised(acc_sc[2 * h]) - lam * _normalised(acc_sc[2 * h + 1])
        o = _rmsnorm_rows(o, g_ref[...]) * (1.0 - lambda_init)
        o_ref[:, h * HEAD_DIM:(h + 1) * HEAD_DIM] = o.astype(o_ref.dtype)


def _diff_attn_call(qk, v, bias, lam_rows, subln_row, lambda_init, t):
    b, s, _ = v.shape
    nh = DIFF_HEADS_PER_STEP
    wb = nh * HEAD_DIM
    n_hg = N_HEADS // nh
    vec = lambda w: pl.BlockSpec((1, w), lambda hg, bb, qi: (0, 0))
    return pl.pallas_call(
        functools.partial(_diff_attn_body, t=t, nh=nh, lambda_init=lambda_init),
        grid=(n_hg, b, s // t),
        in_specs=[vec(DIFF_QK_DIM)] * 4 + [
            vec(HEAD_DIM),
            pl.BlockSpec((None, t, wb), lambda hg, bb, qi: (bb, qi, hg)),
            pl.BlockSpec((None, s, wb), lambda hg, bb, qi: (bb, 0, n_hg + hg)),
            pl.BlockSpec((None, s, wb), lambda hg, bb, qi: (bb, 0, hg)),
            pl.BlockSpec((nh, t, s), lambda hg, bb, qi: (hg, 0, 0),
                         pipeline_mode=pl.Buffered(1 if n_hg == 1 else 2))],
        out_specs=pl.BlockSpec((None, t, wb), lambda hg, bb, qi: (bb, qi, hg)),
        out_shape=jax.ShapeDtypeStruct(v.shape, BF16),
        scratch_shapes=[pltpu.VMEM((2 * nh, t, LANES), F32), pltpu.VMEM((2 * nh, t, 2 * HEAD_DIM), F32)],
        compiler_params=_cparams(("parallel", "parallel", "parallel")),
        name="diff_attn",
    )(*lam_rows, subln_row, qk, qk, v, bias)


def _fox_attn_body(q_ref, k_ref, v_ref, cq_ref, ck_ref, o_ref, m_sc, acc_sc, cq_sc, *, t, nh):
    qi = pl.program_id(2)
    m_sc[...] = jnp.full(m_sc.shape, NEG, F32)
    acc_sc[...] = jnp.zeros(acc_sc.shape, F32)
    hg = pl.program_id(0)
    lane = lax.broadcasted_iota(I32, (t, LANES), 1)
    cq_all = cq_ref[...]
    for h in range(nh):
        col = jnp.sum(jnp.where(lane == hg * nh + h, cq_all, 0.0), axis=-1, keepdims=True)
        cq_sc[h] = jnp.broadcast_to(col, (t, LANES))

    def step(kb, w, rel):
        k0 = pl.multiple_of(kb * t, t)
        masked = rel is not None
        if masked:
            r = lax.broadcasted_iota(I32, (t, w), 0)
            c = lax.broadcasted_iota(I32, (t, w), 1) + rel
            ok = c <= r
        for h in range(nh):
            cols = slice(h * HEAD_DIM, (h + 1) * HEAD_DIM)
            k = k_ref[pl.ds(k0, w), cols]
            v1 = _with_ones(v_ref[pl.ds(k0, w), cols])
            s = lax.dot_general(q_ref[:, cols], k, _NT, preferred_element_type=F32)
            s = s + jnp.tile(cq_sc[h], (1, w // LANES)) - ck_ref[h, :, pl.ds(k0, w)]
            if masked:
                s = jnp.where(ok, s, NEG)
            _softmax_step(s, m_sc.at[h], acc_sc.at[h], v1)

    _causal_sweep(step, qi, t)
    for h in range(nh):
        o_ref[:, h * HEAD_DIM:(h + 1) * HEAD_DIM] = _normalised(acc_sc[h]).astype(o_ref.dtype)


def _fox_attn_call(qk, v, cum, cum_row, t):
    b, s, _ = v.shape
    nh = FOX_HEADS_PER_STEP
    wb = nh * HEAD_DIM
    n_hg = N_HEADS // nh
    return pl.pallas_call(
        functools.partial(_fox_attn_body, t=t, nh=nh),
        grid=(n_hg, b, s // t),
        in_specs=[pl.BlockSpec((None, t, wb), lambda hg, bb, qi: (bb, qi, hg)),
                  pl.BlockSpec((None, s, wb), lambda hg, bb, qi: (bb, 0, n_hg + hg)),
                  pl.BlockSpec((None, s, wb), lambda hg, bb, qi: (bb, 0, hg)),
                  pl.BlockSpec((None, t, LANES), lambda hg, bb, qi: (bb, qi, 0)),
                  pl.BlockSpec((None, nh, 1, s), lambda hg, bb, qi: (bb, hg, 0, 0))],
        out_specs=pl.BlockSpec((None, t, wb), lambda hg, bb, qi: (bb, qi, hg)),
        out_shape=jax.ShapeDtypeStruct(v.shape, BF16),
        scratch_shapes=[pltpu.VMEM((nh, t, LANES), F32), pltpu.VMEM((nh, t, 2 * HEAD_DIM), F32),
                        pltpu.VMEM((nh, t, LANES), F32)],
        compiler_params=_cparams(("parallel", "parallel", "parallel")),
        name="fox_attn",
    )(qk, qk, v, cum, cum_row)


def _mix_body(od_ref, of_ref, g_ref, x_ref, wa_ref, wb_ref, wo_ref, nm_ref, wrh_ref, wrl_ref, br_ref,
              x1_ref, hp_ref, lg_ref):
    d = x_ref.shape[1]
    ua = jnp.dot(od_ref[...], wa_ref[...], preferred_element_type=F32)
    ub = jnp.dot(of_ref[...], wb_ref[...], preferred_element_type=F32)
    mixed = g_ref[:, :d].astype(F32) * ua + g_ref[:, d:].astype(F32) * ub
    x1 = x_ref[...] + jnp.dot(mixed.astype(BF16), wo_ref[...], preferred_element_type=F32)
    x1_ref[...] = x1
    hm = _rmsnorm_rows(x1, nm_ref[...])
    h_hi = hm.astype(BF16)
    h_hi32 = h_hi.astype(F32)
    h_lo = (hm - h_hi32).astype(BF16)
    lg_ref[...] = (lax.dot_general(h_hi, wrh_ref[...], _NT, preferred_element_type=F32)
                   + lax.dot_general(h_lo, wrh_ref[...], _NT, preferred_element_type=F32)
                   + lax.dot_general(h_hi, wrl_ref[...], _NT, preferred_element_type=F32) + br_ref[...])
    bits = lax.bitcast_convert_type(h_hi32, U32)
    packed = (bits[:, :d // 2] >> 16) | (bits[:, d // 2:] & jnp.uint32(0xFFFF0000))
    tm = packed.shape[0]
    pieces = packed.shape[1] // LANES
    for j in range(pieces):
        hp_ref[pl.ds(j, tm, stride=pieces), :] = packed[:, j * LANES:(j + 1) * LANES]


def _resident(shape):
    return pl.BlockSpec(shape, lambda i: (0,) * len(shape), pipeline_mode=pl.Buffered(1))


def _mix_call(od, of, gates, x2d, wa, wb, wo, nm_row, wr_hi, wr_lo, br_pad, tm=256):
    n, d = x2d.shape
    tm = min(tm, n)
    wdt = od.shape[1]
    return pl.pallas_call(
        _mix_body,
        grid=(n // tm,),
        in_specs=[pl.BlockSpec((tm, wdt), lambda i: (i, 0)),
                  pl.BlockSpec((tm, wdt), lambda i: (i, 0)),
                  pl.BlockSpec((tm, 2 * d), lambda i: (i, 0)),
                  pl.BlockSpec((tm, d), lambda i: (i, 0)),
                  _resident((wdt, d)), _resident((wdt, d)), _resident((d, d)),
                  _resident((1, d)), _resident((LANES, d)), _resident((LANES, d)), _resident((1, LANES))],
        out_specs=[pl.BlockSpec((tm, d), lambda i: (i, 0)),
                   pl.BlockSpec((tm * (d // 2 // LANES), LANES), lambda i: (i, 0)),
                   pl.BlockSpec((tm, LANES), lambda i: (i, 0))],
        out_shape=[jax.ShapeDtypeStruct((n, d), F32),
                   jax.ShapeDtypeStruct((n * (d // 2 // LANES), LANES), U32),
                   jax.ShapeDtypeStruct((n, LANES), F32)],
        compiler_params=_cparams(("parallel",)),
        name="mix",
    )(od, of, gates, x2d, wa, wb, wo, nm_row, wr_hi, wr_lo, br_pad)


def _route_body(lg_ref, dest_ref, w4_ref, cnt_ref, ps_ref, rank_sc, jm_sc, wm_sc, carry_sc, ps_sc, *, tm):
    phase = pl.program_id(0)
    i = pl.program_id(1)
    rows = pl.ds(pl.multiple_of(i * tm, tm), tm)

    @pl.when(phase == 0)
    def _select():
        @pl.when(i == 0)
        def _():
            carry_sc[...] = jnp.zeros_like(carry_sc)

        vals = lg_ref[...]
        lane = lax.broadcasted_iota(I32, vals.shape, 1).astype(F32)
        jm = jnp.zeros(vals.shape, F32)
        tops = []
        for j in range(TOP_K):
            m = jnp.max(vals, axis=-1, keepdims=True)
            idx = jnp.min(jnp.where(vals == m, lane, float(LANES)), axis=-1, keepdims=True)
            sel = lane == idx
            jm = jnp.where(sel, float(j + 1), jm)
            vals = jnp.where(sel, -jnp.inf, vals)
            tops.append(m)
        es = [jnp.exp(m - tops[0]) for m in tops]
        den = es[0] + es[1] + es[2] + es[3]
        wm = jnp.zeros(vals.shape, F32)
        for j in range(TOP_K):
            wm = jnp.where(jm == float(j + 1), es[j] / den, wm)
        sel_any = jnp.where(jm > 0.0, 1.0, 0.0)
        r = lax.broadcasted_iota(I32, (tm, tm), 0)
        c = lax.broadcasted_iota(I32, (tm, tm), 1)
        tri = jnp.where(c < r, 1.0, 0.0).astype(BF16)
        carry = carry_sc[...]
        rank_sc[rows, :] = jnp.dot(tri, sel_any.astype(BF16), preferred_element_type=F32) + carry
        jm_sc[rows, :] = jm
        wm_sc[rows, :] = wm
        carry_sc[...] = carry + jnp.sum(sel_any, axis=0, keepdims=True)

    @pl.when(phase == 1)
    def _place():
        @pl.when(i == 0)
        def _():
            cnt = carry_sc[...]
            padded = jnp.ceil(cnt * (1.0 / SLOT_BLOCK)) * float(SLOT_BLOCK)
            r = lax.broadcasted_iota(I32, (LANES, LANES), 0)
            c = lax.broadcasted_iota(I32, (LANES, LANES), 1)
            before = jnp.where(r < c, 1.0, 0.0).astype(F32)
            ps_sc[...] = jnp.dot(jnp.broadcast_to(padded, (SUBLANES, LANES)), before, precision=HIGHEST,
                                 preferred_element_type=F32)[:1]
            cnt_ref[...] = cnt
            ps_ref[...] = ps_sc[...]

        slot = rank_sc[rows, :] + ps_sc[...]
        jm = jm_sc[rows, :]
        wm = wm_sc[rows, :]
        lane = lax.broadcasted_iota(I32, jm.shape, 1)
        dest = jnp.zeros(jm.shape, F32)
        w4 = jnp.zeros(jm.shape, F32)
        for j in range(TOP_K):
            sel = jm == float(j + 1)
            dj = jnp.sum(jnp.where(sel, slot, 0.0), axis=-1, keepdims=True)
            wj = jnp.sum(jnp.where(sel, wm, 0.0), axis=-1, keepdims=True)
            dest = jnp.where(lane == j, dj, dest)
            w4 = jnp.where(lane == j, wj, w4)
        dest_ref[...] = dest.astype(I32)
        w4_ref[...] = w4


def _route_call(logits, tm=256):
    n = logits.shape[0]
    tm = min(tm, n)
    nt = n // tm
    row = pl.BlockSpec((1, LANES), lambda p, i: (0, 0))
    out_tile = pl.BlockSpec((tm, LANES), lambda p, i: (p * i, 0))
    return pl.pallas_call(
        functools.partial(_route_body, tm=tm),
        grid=(2, nt),
        in_specs=[pl.BlockSpec((tm, LANES), lambda p, i: (i * (1 - p) + (nt - 1) * p, 0))],
        out_specs=[out_tile, out_tile, row, row],
        out_shape=[jax.ShapeDtypeStruct((n, LANES), I32), jax.ShapeDtypeStruct((n, LANES), F32),
                   jax.ShapeDtypeStruct((1, LANES), F32), jax.ShapeDtypeStruct((1, LANES), F32)],
        scratch_shapes=[pltpu.VMEM((n, LANES), F32)] * 3 + [pltpu.VMEM((1, LANES), F32)] * 2,
        compiler_params=_cparams(("arbitrary", "arbitrary")),
        name="route",
    )(logits)


ROW_UNROLL = 8
SLOT_UNROLL = 16


def _build_slot_table(dest_ref, cnt_ref, pst_ref, slot_ref, n_tok):
    def pads(e, carry):
        base = pst_ref[e] + cnt_ref[e]

        def zeros(i, c2):
            for u in range(SLOT_UNROLL):
                slot_ref[base + i * SLOT_UNROLL + u] = 0
            return c2
        lax.fori_loop(0, SLOT_BLOCK // SLOT_UNROLL, zeros, 0)
        return carry
    lax.fori_loop(0, N_EXPERTS, pads, 0)

    per_trip = SLOT_UNROLL // TOP_K
    assert n_tok % per_trip == 0

    def toks(i, carry):
        for u in range(per_trip):
            t = i * per_trip + u
            for j in range(TOP_K):
                slot_ref[dest_ref[t * TOP_K + j]] = t
        return carry
    lax.fori_loop(0, n_tok // per_trip, toks, 0)


def _expert_body(gexp_ref, gsub0_ref, gns_ref, dest_ref, cnt_ref, pst_ref,
                 hp_hbm, w1g_ref, w1l_ref, b1g_ref, b1l_ref, w2_ref, b2_ref,
                 outs_hbm,
                 stage, acc, slot_ref, sem_in, sem_out, *, n_chunks, n_groups, n_tok):
    g = pl.program_id(0)
    c = pl.program_id(1)
    ns = gns_ref[g]
    s0 = gsub0_ref[g]
    sub = SLOT_BLOCK
    half = w1g_ref.shape[0] // 2
    last = n_chunks - 1
    cur = g % 2
    pieces = half // LANES

    def row_copy(tok, buf, r):
        return pltpu.make_async_copy(hp_hbm.at[pl.ds(pl.multiple_of(tok * pieces, pieces), pieces), :],
                                     stage.at[buf, pl.ds(pl.multiple_of(r * pieces, pieces), pieces), :],
                                     sem_in.at[buf])

    def issue_rows(gi, buf, blk0, n_blk):
        base = gsub0_ref[gi] * sub

        def trip(i, carry):
            for u in range(ROW_UNROLL):
                r = blk0 * sub + i * ROW_UNROLL + u
                row_copy(slot_ref[base + r], buf, r).start()
            return carry
        lax.fori_loop(0, n_blk * (sub // ROW_UNROLL), trip, 0)

    nxt = jnp.minimum(g + 1, n_groups - 1)
    need = jnp.where(jnp.logical_and(g + 1 < n_groups, gns_ref[nxt] > 0), gns_ref[nxt], 0)
    nxt_base = gsub0_ref[nxt] * sub

    def issue_block(blk):
        for u in range(sub):
            r = blk * sub + u
            row_copy(slot_ref[nxt_base + r], 1 - cur, r).start()

    def out_copy(r0, rows):
        row0 = s0 * sub + r0
        return pltpu.make_async_copy(acc.at[pl.ds(r0, rows), :], outs_hbm.at[pl.ds(row0, rows), :], sem_out)

    def compute(r0, rows, issue_blk=None):
        los, his = [], []
        for j in range(pieces):
            w = stage[cur, pl.ds(r0 * pieces + j, rows, stride=pieces), :]
            los.append(lax.bitcast_convert_type(w << 16, F32).astype(BF16))
            his.append(lax.bitcast_convert_type(w & jnp.uint32(0xFFFF0000), F32).astype(BF16))
        xl = jnp.concatenate(los, axis=1)
        xh = jnp.concatenate(his, axis=1)
        hg = (jnp.dot(xl, w1g_ref[:half, :].astype(BF16), preferred_element_type=F32)
              + jnp.dot(xh, w1g_ref[half:, :].astype(BF16), preferred_element_type=F32) + b1g_ref[...])
        hl = (jnp.dot(xl, w1l_ref[:half, :].astype(BF16), preferred_element_type=F32)
              + jnp.dot(xh, w1l_ref[half:, :].astype(BF16), preferred_element_type=F32) + b1l_ref[...])
        gate = jnp.minimum(hg, SWIGLU_LIMIT)
        lin = jnp.clip(hl, -SWIGLU_LIMIT, SWIGLU_LIMIT)
        act = gate * jax.nn.sigmoid(SWIGLU_ALPHA * gate) * (lin + 1.0)
        acc[pl.ds(r0, rows), :] += jnp.dot(act.astype(BF16), w2_ref[...].astype(BF16),
                                           preferred_element_type=F32)
        if issue_blk is not None:
            issue_block(issue_blk)

        @pl.when(c == last)
        def _():
            out_copy(r0, rows).start()

    @pl.when(ns > 0)
    def _():
        @pl.when(c == 0)
        def _load():
            @pl.when(g == 0)
            def _():
                _build_slot_table(dest_ref, cnt_ref, pst_ref, slot_ref, n_tok)
                issue_rows(0, 0, 0, ns)

            def wt(i, carry):
                for u in range(ROW_UNROLL):
                    row_copy(0, cur, i * ROW_UNROLL + u).wait()
                return carry
            lax.fori_loop(0, ns * (sub // ROW_UNROLL), wt, 0)

            def init(s, carry):
                acc[pl.ds(pl.multiple_of(s * sub, sub), sub), :] = jnp.broadcast_to(
                    b2_ref[...], (sub, acc.shape[1]))
                return carry
            lax.fori_loop(0, ns, init, 0)

        full = TRIP_SUBS * sub
        n_full = ns // TRIP_SUBS

        def for_each_trip(fn, fn_full=None):
            def trip(i, carry):
                fn(pl.multiple_of(i * full, full), full)
                return carry
            if fn_full is None:
                lax.fori_loop(0, n_full, trip, 0)
            else:
                fn_full()
            r0 = n_full * full
            width = full // 2
            while width >= sub:
                has = (ns * sub) & width

                @pl.when(has != 0)
                def _(r0=r0, width=width):
                    fn(pl.multiple_of(r0, width), width)
                r0 = r0 + has
                width //= 2

        def full_trips():
            n_issue = jnp.clip(need - c * n_full, 0, n_full)

            def with_issue(i, carry):
                compute(pl.multiple_of(i * full, full), full, issue_blk=c * n_full + i)
                return carry

            def plain(i, carry):
                compute(pl.multiple_of(i * full, full), full)
                return carry
            lax.fori_loop(0, n_issue, with_issue, 0)
            lax.fori_loop(n_issue, n_full, plain, 0)

        for_each_trip(compute, full_trips)

        @pl.when(c == last)
        def _drain():
            done = jnp.minimum(need, n_chunks * n_full)
            issue_rows(nxt, 1 - cur, done, need - done)
            for_each_trip(lambda r0, rows: out_copy(r0, rows).wait())

            @pl.when(jnp.logical_or(g + 1 >= n_groups, gns_ref[nxt] == 0))
            def _tail():
                acc[pl.ds(0, sub), :] = jnp.zeros((sub, acc.shape[1]), F32)
                first = s0 + ns
                n_tail = outs_hbm.shape[0] // sub - first

                def tail_copy(i):
                    row0 = pl.multiple_of((first + i) * sub, sub)
                    return pltpu.make_async_copy(acc.at[pl.ds(0, sub), :], outs_hbm.at[pl.ds(row0, sub), :],
                                                 sem_out)

                def st(i, carry):
                    tail_copy(i).start()
                    return carry

                def wt(i, carry):
                    tail_copy(i).wait()
                    return carry
                lax.fori_loop(0, n_tail, st, 0)
                lax.fori_loop(0, n_tail, wt, 0)


def _expert_call(gexp, gsub0, gns, dest_flat, counts, pstart, n_slots, hp, w1, b1, w2, b2):
    half = w1.shape[1] // 2
    n_exp, d, ff2 = w1.shape
    ff = ff2 // 2
    tc = FF_CHUNK
    n_chunks = ff // tc
    n_groups = gexp.shape[0]
    rows = GROUP_SUBS * SLOT_BLOCK

    def chunk(c, gn, g):
        return jnp.where(gn[g] > 0, c, n_chunks - 1)

    grid_spec = pltpu.PrefetchScalarGridSpec(
        num_scalar_prefetch=6,
        grid=(n_groups, n_chunks),
        in_specs=[
            pl.BlockSpec(memory_space=pl.ANY),
            pl.BlockSpec((None, d, tc), lambda g, c, ge, gs, gn, *_: (ge[g], 0, chunk(c, gn, g))),
            pl.BlockSpec((None, d, tc), lambda g, c, ge, gs, gn, *_: (ge[g], 0, n_chunks + chunk(c, gn, g))),
            pl.BlockSpec((None, 1, tc), lambda g, c, ge, gs, gn, *_: (ge[g], 0, chunk(c, gn, g))),
            pl.BlockSpec((None, 1, tc), lambda g, c, ge, gs, gn, *_: (ge[g], 0, n_chunks + chunk(c, gn, g))),
            pl.BlockSpec((None, tc, d), lambda g, c, ge, gs, gn, *_: (ge[g], chunk(c, gn, g), 0)),
            pl.BlockSpec((None, 1, d), lambda g, c, ge, gs, gn, *_: (ge[g], 0, 0)),
        ],
        out_specs=pl.BlockSpec(memory_space=pl.ANY),
        scratch_shapes=[
            pltpu.VMEM((2, rows * (half // LANES), LANES), U32),
            pltpu.VMEM((rows, d), F32),
            pltpu.SMEM((n_slots + SLOT_BLOCK,), I32),
            pltpu.SemaphoreType.DMA((2,)), pltpu.SemaphoreType.DMA(()),
        ],
    )
    return pl.pallas_call(
        functools.partial(_expert_body, n_chunks=n_chunks, n_groups=n_groups,
                          n_tok=dest_flat.shape[0] // TOP_K),
        grid_spec=grid_spec,
        out_shape=jax.ShapeDtypeStruct((n_slots, d), F32),
        compiler_params=_cparams(("arbitrary", "arbitrary")),
        name="expert",
    )(gexp, gsub0, gns, dest_flat, counts, pstart, hp, w1, w1, b1.reshape(n_exp, 1, ff2),
      b1.reshape(n_exp, 1, ff2), w2, b2.reshape(n_exp, 1, d))


def _final_body(dest_ref, x1_ref, w4_ref, p_ref, wple_ref, wpg_ref, bpg_ref, npl_ref, outs_hbm,
                o_ref, gbuf, sems, *, tm, n_steps):
    i = pl.program_id(0)
    cur = i % 2

    def row_copy(slot, buf, j, q, u):
        return pltpu.make_async_copy(outs_hbm.at[pl.ds(slot, 1), :], gbuf.at[buf, j, q, pl.ds(u, 1), :],
                                     sems.at[buf])

    def issue(step, buf):
        def rows(q, carry):
            for u in range(SUBLANES):
                t = step * tm + q * SUBLANES + u
                for j in range(TOP_K):
                    row_copy(dest_ref[t * TOP_K + j], buf, j, q, u).start(priority=j % 2)
            return carry
        lax.fori_loop(0, tm // SUBLANES, rows, 0)

    @pl.when(i == 0)
    def _():
        issue(0, 0)

    pe = jnp.dot(p_ref[...].astype(BF16), wple_ref[...], preferred_element_type=F32)

    def wt(q, carry):
        for u in range(SUBLANES):
            for j in range(TOP_K):
                row_copy(0, cur, j, q, u).wait()
        return carry
    lax.fori_loop(0, tm // SUBLANES, wt, 0)

    w4 = w4_ref[...]
    d = o_ref.shape[1]
    y = gbuf[cur, 0].reshape(tm, d) * w4[:, 0:1]
    for j in range(1, TOP_K):
        y = y + gbuf[cur, j].reshape(tm, d) * w4[:, j:j + 1]
    x2 = x1_ref[...] + y
    hn = _rmsnorm_rows(x2, npl_ref[...])
    gate = jax.nn.sigmoid(jnp.dot(hn.astype(BF16), wpg_ref[...], preferred_element_type=F32) + bpg_ref[...])
    o_ref[...] = x2 + gate * pe

    nxt = jnp.minimum(i + 1, n_steps - 1)
    for q in range(tm // SUBLANES):
        for u in range(SUBLANES):
            t = nxt * tm + q * SUBLANES + u
            for j in range(TOP_K):
                row_copy(dest_ref[t * TOP_K + j], 1 - cur, j, q, u).start(priority=j % 2)

    @pl.when(i == n_steps - 1)
    def _():
        def wt_last(q, carry):
            for u in range(SUBLANES):
                for j in range(TOP_K):
                    row_copy(0, 1 - cur, j, q, u).wait()
            return carry
        lax.fori_loop(0, tm // SUBLANES, wt_last, 0)


def _final_call(dest_flat, x1, w4, p2d, wple, wpg, bpg_row, npl_row, outs, tm=256):
    n, d = x1.shape
    tm = min(tm, n)
    pd = p2d.shape[1]
    res = lambda shape: pl.BlockSpec(shape, lambda i, dst: (0,) * len(shape), pipeline_mode=pl.Buffered(1))
    grid_spec = pltpu.PrefetchScalarGridSpec(
        num_scalar_prefetch=1,
        grid=(n // tm,),
        in_specs=[pl.BlockSpec((tm, d), lambda i, dst: (i, 0)),
                  pl.BlockSpec((tm, LANES), lambda i, dst: (i, 0)),
                  pl.BlockSpec((tm, pd), lambda i, dst: (i, 0)),
                  res((pd, d)), res((d, d)), res((1, d)), res((1, d)),
                  pl.BlockSpec(memory_space=pl.ANY)],
        out_specs=pl.BlockSpec((tm, d), lambda i, dst: (i, 0)),
        scratch_shapes=[pltpu.VMEM((2, TOP_K, tm // SUBLANES, SUBLANES, d), F32),
                        pltpu.SemaphoreType.DMA((2,))],
    )
    return pl.pallas_call(
        functools.partial(_final_body, tm=tm, n_steps=n // tm),
        grid_spec=grid_spec,
        out_shape=jax.ShapeDtypeStruct((n, d), F32),
        compiler_params=_cparams(("arbitrary",)),
        name="final",
    )(dest_flat, x1, w4, p2d, wple, wpg, bpg_row, npl_row, outs)


def _t5_bucket(rel):
    n = -rel
    nb = REL_BUCKETS // 2
    ret = jnp.where(n < 0, nb, 0)
    n = jnp.abs(n)
    max_exact = nb // 2
    large = max_exact + (jnp.log(jnp.maximum(n, 1).astype(jnp.float32) / max_exact)
                         / math.log(REL_MAX_DIST / max_exact) * (nb - max_exact)).astype(jnp.int32)
    large = jnp.minimum(large, nb - 1)
    return ret + jnp.where(n < max_exact, n, large)


def _bias_body(rv_ref, o_ref, *, t):
    x = jnp.broadcast_to(rv_ref[...], (t, rv_ref.shape[1]))
    o_ref[...] = pltpu.roll(x, 1, 1, stride=1, stride_axis=0)[:, t:]


def _diff_bias_table(rel_bias, seq, t):
    rel = jnp.arange(seq + t) - (seq - 1)
    rv = (jnp.transpose(rel_bias[_t5_bucket(rel)], (1, 0)).astype(F32) * LOG2E)[:, None, :]
    return pl.pallas_call(
        functools.partial(_bias_body, t=t),
        grid=(N_HEADS,),
        in_specs=[pl.BlockSpec((None, 1, seq + t), lambda h: (h, 0, 0))],
        out_specs=pl.BlockSpec((None, t, seq), lambda h: (h, 0, 0)),
        out_shape=jax.ShapeDtypeStruct((N_HEADS, t, seq), F32),
        compiler_params=_cparams(("parallel",)),
        name="bias_table",
    )(rv)


def _pad_cols(a, width):
    return jnp.pad(a, ((0, 0), (0, width - a.shape[1])))


def _group_table(counts):
    n_sub = (counts + SLOT_BLOCK - 1) // SLOT_BLOCK
    sub_start = jnp.cumsum(n_sub) - n_sub
    n_grp = (n_sub + GROUP_SUBS - 1) // GROUP_SUBS
    grp_end = jnp.cumsum(n_grp)
    total = grp_end[-1]
    return n_sub, sub_start, n_grp, grp_end, total


def _layer(x, p_l, w_in, b_gate, b_forget, dq_norm, dk_norm, fq_norm, fk_norm, lq1, lk1, lq2, lk2,
           lambda_init, subln, w_up_a, w_up_b, w_out, rel_bias, norm_mix, norm_moe, w_router,
           b_router, w1, b1, w2, b2, norm_ple, w_ple_gate, b_ple_gate, w_ple):
    b, s, d = x.shape
    n = b * s
    width = N_HEADS * HEAD_DIM
    t = min(ATTN_T, s)
    x2d = x.reshape(n, d)
    row = lambda v: v.reshape(1, -1).astype(F32)

    f_row = 6 * width
    wt_in = jnp.swapaxes(w_in, 0, 1)
    h, f_logit = _norm_call(x2d, row(norm_mix), wt_in, f_row)
    cum = _cum_call(f_logit, _pad_cols(row(b_forget), LANES), b, s)
    cum_bhs = jnp.transpose(cum[:, :N_HEADS].reshape(b, s, N_HEADS), (0, 2, 1))
    cum_row = cum_bhs[:, :, None, :]

    diff_scale = DIFF_QK_DIM ** -0.5 * LOG2E
    fox_scale = HEAD_DIM ** -0.5 * LOG2E
    gain_d = jnp.concatenate([jnp.tile(dq_norm * diff_scale, 2 * N_HEADS), jnp.tile(dk_norm, 2 * N_HEADS)])
    gain_f = jnp.concatenate([jnp.tile(fq_norm * fox_scale, N_HEADS), jnp.tile(fk_norm, N_HEADS)])
    zeros_w = jnp.zeros((1, width), F32)
    dqk = _proj_call(h, wt_in, 0, 2 * width, row(gain_d), "norm64", "proj_dqk")
    dv = _proj_call(h, wt_in, 2 * width, width, zeros_w, "plain", "proj_dv")
    fqk = _proj_call(h, wt_in, 3 * width, 2 * width, row(gain_f), "norm128", "proj_fqk")
    fv = _proj_call(h, wt_in, 5 * width, width, zeros_w, "plain", "proj_fv")
    gates = _proj_call(h, wt_in, f_row + N_HEADS, 2 * d, row(b_gate), "gate", "proj_gate")

    lam_rows = [row(v) for v in (lq1, lk1, lq2, lk2)]
    bias = _diff_bias_table(rel_bias, s, t)
    od = _diff_attn_call(dqk.reshape(b, s, 2 * width), dv.reshape(b, s, width), bias, lam_rows,
                         row(subln), lambda_init, t)
    of = _fox_attn_call(fqk.reshape(b, s, 2 * width), fv.reshape(b, s, width), cum.reshape(b, s, LANES),
                        cum_row, t)

    br_pad = jnp.full((1, LANES), NEG, F32).at[0, :N_EXPERTS].set(b_router.astype(F32))
    wr_pad = jnp.pad(jnp.swapaxes(w_router.astype(F32), 0, 1), ((0, LANES - N_EXPERTS), (0, 0)))
    wr_hi = wr_pad.astype(BF16)
    wr_lo = (wr_pad - wr_hi.astype(F32)).astype(BF16)
    x1, hp, logits = _mix_call(od.reshape(n, width), of.reshape(n, width), gates, x2d,
                               w_up_a.astype(BF16), w_up_b.astype(BF16), w_out.astype(BF16),
                               row(norm_moe), wr_hi, wr_lo, br_pad)

    dest128, w4, cnt, ps_row = _route_call(logits)
    counts = cnt[0, :N_EXPERTS].astype(I32)
    pstart = ps_row[0, :N_EXPERTS].astype(I32)
    dest_flat = dest128[:, :TOP_K].reshape(-1)

    n_slots = n * TOP_K + N_EXPERTS * SLOT_BLOCK

    n_sub, sub_start, n_grp, grp_end, total = _group_table(counts)
    max_groups = N_EXPERTS + (n_slots // SLOT_BLOCK) // GROUP_SUBS
    gidx = jnp.arange(max_groups, dtype=I32)
    gvalid = gidx < total
    gsafe = jnp.minimum(gidx, total - 1)
    gexp = jnp.searchsorted(grp_end, gsafe, side="right").astype(I32)
    kth = gsafe - (grp_end - n_grp)[gexp]
    gsub0 = (sub_start[gexp] + kth * GROUP_SUBS).astype(I32)
    gns = jnp.where(gvalid, jnp.minimum(GROUP_SUBS, n_sub[gexp] - kth * GROUP_SUBS), 0).astype(I32)
    def run_experts(ng):
        return _expert_call(gexp[:ng], gsub0[:ng], gns[:ng], dest_flat, counts, pstart, n_slots, hp,
                            w1, b1, w2, b2)

    few = min(N_EXPERTS + 2, max_groups)
    outs = lax.cond(total <= few, lambda: run_experts(few), lambda: run_experts(max_groups))

    out = _final_call(dest_flat, x1, w4, p_l.reshape(n, -1), w_ple.astype(BF16),
                      w_ple_gate.astype(BF16), row(b_ple_gate), row(norm_ple), outs)
    return out.reshape(b, s, d)


def kernel(x, p, w_in, b_gate, b_forget, dq_norm, dk_norm, fq_norm, fk_norm, lambda_q1, lambda_k1,
           lambda_q2, lambda_k2, subln, w_up_a, w_up_b, w_out, rel_bias, norm_mix, norm_moe,
           w_router, b_router, w1, b1, w2, b2, norm_ple, w_ple_gate, b_ple_gate, w_ple):
    for i in range(w_in.shape[0]):
        lambda_init = 0.8 - 0.6 * math.exp(-0.3 * i)
        x = _layer(x, p[i], w_in[i], b_gate[i], b_forget[i], dq_norm[i], dk_norm[i], fq_norm[i],
                   fk_norm[i], lambda_q1[i], lambda_k1[i], lambda_q2[i], lambda_k2[i], lambda_init,
                   subln[i], w_up_a[i], w_up_b[i], w_out[i], rel_bias, norm_mix[i], norm_moe[i],
                   w_router[i], b_router[i], w1[i], b1[i], w2[i], b2[i], norm_ple[i],
                   w_ple_gate[i], b_ple_gate[i], w_ple[i])
    return x
```

```python
import functools
import math

import jax
import jax.numpy as jnp
from jax import lax
from jax.experimental import pallas as pl
from jax.experimental.pallas import tpu as pltpu

F32 = jnp.float32
BF16 = jnp.bfloat16
U32 = jnp.uint32
I32 = jnp.int32
HIGHEST = lax.Precision.HIGHEST

N_HEADS = 8
HEAD_DIM = 128
DIFF_QK_DIM = 64
CHUNK = 64
REL_BUCKETS = 32
REL_MAX_DIST = 128
N_EXPERTS = 32
TOP_K = 4
SWIGLU_LIMIT = 7.0
SWIGLU_ALPHA = 1.702
RMS_EPS = 1e-6
NEG = -1e30
LOG2E = math.log2(math.e)

LANES = 128
SUBLANES = 8
SLOT_BLOCK = 128
GROUP_SUBS = 12
TRIP_SUBS = 4
FF_CHUNK = 512
ATTN_T = 256
VMEM_LIMIT = 56 * 1024 * 1024


_NT = (((1,), (1,)), ((), ()))


def _cparams(sem):
    return pltpu.CompilerParams(dimension_semantics=sem, vmem_limit_bytes=VMEM_LIMIT)


def _rmsnorm_rows(x, g):
    ms = jnp.mean(x * x, axis=-1, keepdims=True)
    return x * lax.rsqrt(ms + RMS_EPS) * g


def _norm_body(x_ref, g_ref, wf_ref, h_ref, fl_ref):
    y = _rmsnorm_rows(x_ref[...], g_ref[...])
    y_hi = y.astype(BF16)
    h_ref[...] = y_hi
    rowi = lax.broadcasted_iota(I32, (LANES, 1), 0)
    wf = jnp.where(rowi < N_HEADS, wf_ref[...], 0.0)
    y_lo = (y - y_hi.astype(F32)).astype(BF16)
    wf_hi = wf.astype(BF16)
    wf_lo = (wf - wf_hi.astype(F32)).astype(BF16)
    fl_ref[...] = (lax.dot_general(y_hi, wf_hi, _NT, preferred_element_type=F32)
                   + lax.dot_general(y_lo, wf_hi, _NT, preferred_element_type=F32)
                   + lax.dot_general(y_hi, wf_lo, _NT, preferred_element_type=F32))


def _norm_call(x2d, g_row, wt_in, f_row, tm=512):
    n, d = x2d.shape
    tm = min(tm, n)
    assert f_row % LANES == 0
    return pl.pallas_call(
        _norm_body,
        grid=(n // tm,),
        in_specs=[pl.BlockSpec((tm, d), lambda i: (i, 0)),
                  pl.BlockSpec((1, d), lambda i: (0, 0)),
                  pl.BlockSpec((LANES, d), lambda i: (f_row // LANES, 0))],
        out_specs=[pl.BlockSpec((tm, d), lambda i: (i, 0)),
                   pl.BlockSpec((tm, LANES), lambda i: (i, 0))],
        out_shape=[jax.ShapeDtypeStruct((n, d), BF16),
                   jax.ShapeDtypeStruct((n, LANES), F32)],
        compiler_params=_cparams(("parallel",)),
        name="norm",
    )(x2d, g_row, wt_in)


def _cum_body(fl_ref, bf_ref, cum_ref):
    s = fl_ref.shape[0]
    r = lax.broadcasted_iota(I32, (LANES, LANES), 0)
    c = lax.broadcasted_iota(I32, (LANES, LANES), 1)
    tri = jnp.where(r >= c, 1.0, 0.0).astype(F32)
    carry = jnp.zeros((1, LANES), F32)
    for blk in range(s // LANES):
        rows = slice(blk * LANES, (blk + 1) * LANES)
        lf = jax.nn.log_sigmoid(fl_ref[rows, :] + bf_ref[...])
        cb = jnp.dot(tri, lf, precision=HIGHEST, preferred_element_type=F32) + carry
        cum_ref[rows, :] = cb * LOG2E
        carry = cb[LANES - 1:LANES, :]


def _cum_call(fl, bf_row, batch, seq):
    return pl.pallas_call(
        _cum_body,
        grid=(batch,),
        in_specs=[pl.BlockSpec((seq, LANES), lambda b: (b, 0)),
                  pl.BlockSpec((1, LANES), lambda b: (0, 0))],
        out_specs=pl.BlockSpec((seq, LANES), lambda b: (b, 0)),
        out_shape=jax.ShapeDtypeStruct(fl.shape, F32),
        compiler_params=_cparams(("parallel",)),
        name="cum",
    )(fl, bf_row)


def _proj_epilogue(acc, aux_ref, o_ref, kind):
    tn = acc.shape[1]
    if kind == "plain":
        o_ref[...] = acc.astype(o_ref.dtype)
    elif kind == "gate":
        o_ref[...] = jax.nn.sigmoid(acc + aux_ref[...]).astype(o_ref.dtype)
    else:
        lo = lax.broadcasted_iota(I32, (1, LANES), 1) < DIFF_QK_DIM
        for s in range(tn // LANES):
            cols = slice(s * LANES, (s + 1) * LANES)
            blk = acc[:, cols]
            sq = blk * blk
            if kind == "norm128":
                ms = jnp.mean(sq, axis=-1, keepdims=True)
            else:
                s_lo = jnp.sum(jnp.where(lo, sq, 0.0), axis=-1, keepdims=True)
                s_hi = jnp.sum(jnp.where(lo, 0.0, sq), axis=-1, keepdims=True)
                ms = jnp.where(lo, s_lo, s_hi) * (1.0 / DIFF_QK_DIM)
            o_ref[:, cols] = (blk * lax.rsqrt(ms + RMS_EPS) * aux_ref[:, cols]).astype(o_ref.dtype)


def _proj_body(h_ref, wt_ref, aux_ref, o_ref, *, kind):
    acc = lax.dot_general(h_ref[...], wt_ref[...].astype(BF16), _NT, preferred_element_type=F32)
    _proj_epilogue(acc, aux_ref, o_ref, kind)


def _proj_shift_body(h_ref, wa_ref, wb_ref, aux_ref, o_ref, *, kind, shift):
    wt = jnp.concatenate([wa_ref[shift:, :], wb_ref[...]], axis=0)
    acc = lax.dot_general(h_ref[...], wt.astype(BF16), _NT, preferred_element_type=F32)
    _proj_epilogue(acc, aux_ref, o_ref, kind)


def _proj_call(h, wt, row0, nrows, aux_row, kind, name, tm=1024, tn=1024):
    m, k = h.shape
    tm = min(tm, m)
    assert nrows % tn == 0 and m % tm == 0
    shift = row0 % tn
    base = row0 - shift
    h_spec = pl.BlockSpec((tm, k), lambda i, j: (i, 0))
    w_spec = pl.BlockSpec((tn, k), lambda i, j: (base // tn + j, 0))
    aux_spec = pl.BlockSpec((1, tn), lambda i, j: (0, j))
    if shift == 0:
        body, w_specs, ws = functools.partial(_proj_body, kind=kind), [w_spec], [wt]
    else:
        assert shift % SUBLANES == 0 and tn % shift == 0
        body = functools.partial(_proj_shift_body, kind=kind, shift=shift)
        w_specs = [w_spec, pl.BlockSpec((shift, k), lambda i, j: ((base + (j + 1) * tn) // shift, 0))]
        ws = [wt, wt]
    return pl.pallas_call(
        body,
        grid=(m // tm, nrows // tn),
        in_specs=[h_spec] + w_specs + [aux_spec],
        out_specs=pl.BlockSpec((tm, tn), lambda i, j: (i, j)),
        out_shape=jax.ShapeDtypeStruct((m, nrows), BF16),
        compiler_params=_cparams(("parallel", "parallel")),
        name=name,
    )(h, *ws, aux_row)


INPROJ_TN = 1024
_INPROJ_PARTS = ((0, 2, "norm64", 0), (2, 1, "plain", 0), (3, 2, "norm128", 0), (5, 1, "plain", 0),
                 (6, 4, "gate", N_HEADS))
INPROJ_GATE_BLOCKS = 4


def _inproj_body(x_ref, g_ref, wa_ref, wb_ref, wf_ref, aux_ref, o_ref, fl_ref, h_sc):
    j = pl.program_id(1)

    @pl.when(j == 0)
    def _norm():
        y = _rmsnorm_rows(x_ref[...], g_ref[...])
        y_hi = y.astype(BF16)
        h_sc[...] = y_hi
        rowi = lax.broadcasted_iota(I32, (LANES, 1), 0)
        wf = jnp.where(rowi < N_HEADS, wf_ref[...], 0.0)
        y_lo = (y - y_hi.astype(F32)).astype(BF16)
        wf_hi = wf.astype(BF16)
        wf_lo = (wf - wf_hi.astype(F32)).astype(BF16)
        fl_ref[...] = (lax.dot_general(y_hi, wf_hi, _NT, preferred_element_type=F32)
                       + lax.dot_general(y_lo, wf_hi, _NT, preferred_element_type=F32)
                       + lax.dot_general(y_hi, wf_lo, _NT, preferred_element_type=F32))

    for first, count, kind, shift in _INPROJ_PARTS:
        @pl.when(jnp.logical_and(j >= first, j < first + count))
        def _(kind=kind, shift=shift):
            if shift == 0:
                wt = wa_ref[...]
            else:
                wt = jnp.concatenate([wa_ref[shift:, :], wb_ref[...]], axis=0)
            acc = lax.dot_general(h_sc[...], wt.astype(BF16), _NT, preferred_element_type=F32)
            _proj_epilogue(acc, aux_ref, o_ref, kind)


def _inproj_call(x2d, g_row, wt_in, aux_row, tm=1024):
    n, d = x2d.shape
    tm = min(tm, n)
    tn = INPROJ_TN
    nblk = sum(p[1] for p in _INPROJ_PARTS)
    first_qkv = _INPROJ_PARTS[-1][0]
    assert _INPROJ_PARTS[-1][3] == SUBLANES and first_qkv * tn % LANES == 0

    def out_block(j):
        return jnp.where(j >= first_qkv, j - first_qkv, j + INPROJ_GATE_BLOCKS)

    return pl.pallas_call(
        _inproj_body,
        grid=(n // tm, nblk),
        in_specs=[pl.BlockSpec((tm, d), lambda i, j: (i, 0)),
                  pl.BlockSpec((1, d), lambda i, j: (0, 0)),
                  pl.BlockSpec((tn, d), lambda i, j: (j, 0)),
                  pl.BlockSpec((SUBLANES, d), lambda i, j: ((j + 1) * (tn // SUBLANES), 0)),
                  pl.BlockSpec((LANES, d), lambda i, j: (first_qkv * tn // LANES, 0)),
                  pl.BlockSpec((1, tn), lambda i, j: (0, j))],
        out_specs=[pl.BlockSpec((tm, tn), lambda i, j: (i, out_block(j))),
                   pl.BlockSpec((tm, LANES), lambda i, j: (i, 0))],
        out_shape=[jax.ShapeDtypeStruct((n, nblk * tn), BF16), jax.ShapeDtypeStruct((n, LANES), F32)],
        scratch_shapes=[pltpu.VMEM((tm, d), BF16)],
        compiler_params=_cparams(("parallel", "arbitrary")),
        name="inproj",
    )(x2d, g_row, wt_in, wt_in, wt_in, aux_row)


DIFF_HEADS_PER_STEP = 8
FOX_HEADS_PER_STEP = 8


def _softmax_step(s, m_ref, acc_ref, v1):
    m_old = m_ref[...]
    mn = jnp.maximum(m_old, jnp.max(s, axis=-1, keepdims=True))
    alpha = jnp.exp2(m_old - mn)
    p = jnp.exp2(s - jnp.tile(mn, (1, s.shape[1] // LANES))).astype(BF16)
    acc_ref[...] = (jnp.tile(alpha, (1, acc_ref.shape[1] // LANES)) * acc_ref[...]
                    + jnp.dot(p, v1, preferred_element_type=F32))
    m_ref[...] = mn


def _causal_sweep(step, qi, t):
    def pair(i, carry):
        step(2 * i, 2 * t, None)
        return carry
    lax.fori_loop(0, qi // 2, pair, 0)

    @pl.when(qi % 2 == 1)
    def _():
        step(qi - 1, t, None)
    step(qi, t, 0)


def _with_ones(v):
    return jnp.concatenate([v, jnp.ones_like(v)], axis=1)


def _normalised(acc):
    return acc[:, :HEAD_DIM] / acc[:, HEAD_DIM:]


def _diff_attn_body(lq1_ref, lk1_ref, lq2_ref, lk2_ref, g_ref, q_ref, k_ref, v_ref, bias_ref,
                    o_ref, m_sc, acc_sc, *, t, nh, lambda_init):
    qi = pl.program_id(2)
    s_len = k_ref.shape[0]
    m_sc[...] = jnp.full(m_sc.shape, NEG, F32)
    acc_sc[...] = jnp.zeros(acc_sc.shape, F32)
    lo = lax.broadcasted_iota(I32, (t, HEAD_DIM), 1) < DIFF_QK_DIM

    def step(kb, w, rel):
        k0 = pl.multiple_of(kb * t, t)
        off = pl.multiple_of((kb - qi) * t + (s_len - t), LANES)
        masked = rel is not None
        if masked:
            r = lax.broadcasted_iota(I32, (t, w), 0)
            c = lax.broadcasted_iota(I32, (t, w), 1) + rel
            chunk_of = lambda pos: lax.shift_right_arithmetic(pos, jnp.int32(CHUNK.bit_length() - 1))
            ok = chunk_of(c) <= chunk_of(r)
        for h in range(nh):
            cols = slice(h * HEAD_DIM, (h + 1) * HEAD_DIM)
            q = q_ref[:, cols]
            zero = jnp.zeros_like(q)
            k = k_ref[pl.ds(k0, w), cols]
            v1 = _with_ones(v_ref[pl.ds(k0, w), cols])
            b = bias_ref[h, :, pl.ds(off, w)]
            for mp, qm in enumerate((jnp.where(lo, q, zero), jnp.where(lo, zero, q))):
                s = lax.dot_general(qm, k, _NT, preferred_element_type=F32) + b
                if masked:
                    s = jnp.where(ok, s, NEG)
                _softmax_step(s, m_sc.at[2 * h + mp], acc_sc.at[2 * h + mp], v1)

    _causal_sweep(step, qi, t)

    lam = (jnp.exp(jnp.sum(lq1_ref[...] * lk1_ref[...], axis=-1, keepdims=True))
           - jnp.exp(jnp.sum(lq2_ref[...] * lk2_ref[...], axis=-1, keepdims=True)) + lambda_init)
    for h in range(nh):
        o = _normalised(acc_sc[2 * h]) - lam * _normalised(acc_sc[2 * h + 1])
        o = _rmsnorm_rows(o, g_ref[...]) * (1.0 - lambda_init)
        o_ref[:, h * HEAD_DIM:(h + 1) * HEAD_DIM] = o.astype(o_ref.dtype)


def _diff_attn_call(qkv, blocks, bias, lam_rows, subln_row, lambda_init, t):
    b, s, _ = qkv.shape
    nh = DIFF_HEADS_PER_STEP
    wb = nh * HEAD_DIM
    n_hg = N_HEADS // nh
    qb, kb_, vb = (blk * n_hg for blk in blocks)
    vec = lambda w: pl.BlockSpec((1, w), lambda hg, bb, qi: (0, 0))
    return pl.pallas_call(
        functools.partial(_diff_attn_body, t=t, nh=nh, lambda_init=lambda_init),
        grid=(n_hg, b, s // t),
        in_specs=[vec(DIFF_QK_DIM)] * 4 + [
            vec(HEAD_DIM),
            pl.BlockSpec((None, t, wb), lambda hg, bb, qi: (bb, qi, qb + hg)),
            pl.BlockSpec((None, s, wb), lambda hg, bb, qi: (bb, 0, kb_ + hg)),
            pl.BlockSpec((None, s, wb), lambda hg, bb, qi: (bb, 0, vb + hg)),
            pl.BlockSpec((nh, t, s), lambda hg, bb, qi: (hg, 0, 0),
                         pipeline_mode=pl.Buffered(1 if n_hg == 1 else 2))],
        out_specs=pl.BlockSpec((None, t, wb), lambda hg, bb, qi: (bb, qi, hg)),
        out_shape=jax.ShapeDtypeStruct((b, s, N_HEADS * HEAD_DIM), BF16),
        scratch_shapes=[pltpu.VMEM((2 * nh, t, LANES), F32), pltpu.VMEM((2 * nh, t, 2 * HEAD_DIM), F32)],
        compiler_params=_cparams(("parallel", "parallel", "parallel")),
        name="diff_attn",
    )(*lam_rows, subln_row, qkv, qkv, qkv, bias)


def _fox_attn_body(q_ref, k_ref, v_ref, cq_ref, ck_ref, o_ref, m_sc, acc_sc, cq_sc, *, t, nh):
    qi = pl.program_id(2)
    m_sc[...] = jnp.full(m_sc.shape, NEG, F32)
    acc_sc[...] = jnp.zeros(acc_sc.shape, F32)
    hg = pl.program_id(0)
    lane = lax.broadcasted_iota(I32, (t, LANES), 1)
    cq_all = cq_ref[...]
    for h in range(nh):
        col = jnp.sum(jnp.where(lane == hg * nh + h, cq_all, 0.0), axis=-1, keepdims=True)
        cq_sc[h] = jnp.broadcast_to(col, (t, LANES))

    def step(kb, w, rel):
        k0 = pl.multiple_of(kb * t, t)
        masked = rel is not None
        if masked:
            r = lax.broadcasted_iota(I32, (t, w), 0)
            c = lax.broadcasted_iota(I32, (t, w), 1) + rel
            ok = c <= r
        for h in range(nh):
            cols = slice(h * HEAD_DIM, (h + 1) * HEAD_DIM)
            k = k_ref[pl.ds(k0, w), cols]
            v1 = _with_ones(v_ref[pl.ds(k0, w), cols])
            s = lax.dot_general(q_ref[:, cols], k, _NT, preferred_element_type=F32)
            s = s + jnp.tile(cq_sc[h], (1, w // LANES)) - ck_ref[h, :, pl.ds(k0, w)]
            if masked:
                s = jnp.where(ok, s, NEG)
            _softmax_step(s, m_sc.at[h], acc_sc.at[h], v1)

    _causal_sweep(step, qi, t)
    for h in range(nh):
        o_ref[:, h * HEAD_DIM:(h + 1) * HEAD_DIM] = _normalised(acc_sc[h]).astype(o_ref.dtype)


def _fox_attn_call(qkv, blocks, cum, cum_row, t):
    b, s, _ = qkv.shape
    nh = FOX_HEADS_PER_STEP
    wb = nh * HEAD_DIM
    n_hg = N_HEADS // nh
    qb, kb_, vb = (blk * n_hg for blk in blocks)
    return pl.pallas_call(
        functools.partial(_fox_attn_body, t=t, nh=nh),
        grid=(n_hg, b, s // t),
        in_specs=[pl.BlockSpec((None, t, wb), lambda hg, bb, qi: (bb, qi, qb + hg)),
                  pl.BlockSpec((None, s, wb), lambda hg, bb, qi: (bb, 0, kb_ + hg)),
                  pl.BlockSpec((None, s, wb), lambda hg, bb, qi: (bb, 0, vb + hg)),
                  pl.BlockSpec((None, t, LANES), lambda hg, bb, qi: (bb, qi, 0)),
                  pl.BlockSpec((None, nh, 1, s), lambda hg, bb, qi: (bb, hg, 0, 0))],
        out_specs=pl.BlockSpec((None, t, wb), lambda hg, bb, qi: (bb, qi, hg)),
        out_shape=jax.ShapeDtypeStruct((b, s, N_HEADS * HEAD_DIM), BF16),
        scratch_shapes=[pltpu.VMEM((nh, t, LANES), F32), pltpu.VMEM((nh, t, 2 * HEAD_DIM), F32),
                        pltpu.VMEM((nh, t, LANES), F32)],
        compiler_params=_cparams(("parallel", "parallel", "parallel")),
        name="fox_attn",
    )(qkv, qkv, qkv, cum, cum_row)


def _mix_body(od_ref, of_ref, g_ref, x_ref, wa_ref, wb_ref, wo_ref, nm_ref, wrh_ref, wrl_ref, br_ref,
              x1_ref, hp_ref, lg_ref):
    d = x_ref.shape[1]
    ua = jnp.dot(od_ref[...], wa_ref[...], preferred_element_type=F32)
    ub = jnp.dot(of_ref[...], wb_ref[...], preferred_element_type=F32)
    mixed = g_ref[:, :d].astype(F32) * ua + g_ref[:, d:].astype(F32) * ub
    x1 = x_ref[...] + jnp.dot(mixed.astype(BF16), wo_ref[...], preferred_element_type=F32)
    x1_ref[...] = x1
    hm = _rmsnorm_rows(x1, nm_ref[...])
    h_hi = hm.astype(BF16)
    h_hi32 = h_hi.astype(F32)
    h_lo = (hm - h_hi32).astype(BF16)
    lg_ref[...] = (lax.dot_general(h_hi, wrh_ref[...], _NT, preferred_element_type=F32)
                   + lax.dot_general(h_lo, wrh_ref[...], _NT, preferred_element_type=F32)
                   + lax.dot_general(h_hi, wrl_ref[...], _NT, preferred_element_type=F32) + br_ref[...])
    bits = lax.bitcast_convert_type(h_hi32, U32)
    packed = (bits[:, :d // 2] >> 16) | (bits[:, d // 2:] & jnp.uint32(0xFFFF0000))
    tm = packed.shape[0]
    pieces = packed.shape[1] // LANES
    for j in range(pieces):
        hp_ref[pl.ds(j, tm, stride=pieces), :] = packed[:, j * LANES:(j + 1) * LANES]


def _resident(shape):
    return pl.BlockSpec(shape, lambda i: (0,) * len(shape), pipeline_mode=pl.Buffered(1))


def _mix_call(od, of, gates, x2d, wa, wb, wo, nm_row, wr_hi, wr_lo, br_pad, tm=256):
    n, d = x2d.shape
    tm = min(tm, n)
    wdt = od.shape[1]
    return pl.pallas_call(
        _mix_body,
        grid=(n // tm,),
        in_specs=[pl.BlockSpec((tm, wdt), lambda i: (i, 0)),
                  pl.BlockSpec((tm, wdt), lambda i: (i, 0)),
                  pl.BlockSpec((tm, 2 * d), lambda i: (i, 0)),
                  pl.BlockSpec((tm, d), lambda i: (i, 0)),
                  _resident((wdt, d)), _resident((wdt, d)), _resident((d, d)),
                  _resident((1, d)), _resident((LANES, d)), _resident((LANES, d)), _resident((1, LANES))],
        out_specs=[pl.BlockSpec((tm, d), lambda i: (i, 0)),
                   pl.BlockSpec((tm * (d // 2 // LANES), LANES), lambda i: (i, 0)),
                   pl.BlockSpec((tm, LANES), lambda i: (i, 0))],
        out_shape=[jax.ShapeDtypeStruct((n, d), F32),
                   jax.ShapeDtypeStruct((n * (d // 2 // LANES), LANES), U32),
                   jax.ShapeDtypeStruct((n, LANES), F32)],
        compiler_params=_cparams(("parallel",)),
        name="mix",
    )(od, of, gates, x2d, wa, wb, wo, nm_row, wr_hi, wr_lo, br_pad)


def _route_body(lg_ref, dest_ref, w4_ref, cnt_ref, ps_ref, rank_sc, jm_sc, wm_sc, carry_sc, ps_sc, *, tm):
    phase = pl.program_id(0)
    i = pl.program_id(1)
    rows = pl.ds(pl.multiple_of(i * tm, tm), tm)

    @pl.when(phase == 0)
    def _select():
        @pl.when(i == 0)
        def _():
            carry_sc[...] = jnp.zeros_like(carry_sc)

        vals = lg_ref[...]
        lane = lax.broadcasted_iota(I32, vals.shape, 1).astype(F32)
        jm = jnp.zeros(vals.shape, F32)
        tops = []
        for j in range(TOP_K):
            m = jnp.max(vals, axis=-1, keepdims=True)
            idx = jnp.min(jnp.where(vals == m, lane, float(LANES)), axis=-1, keepdims=True)
            sel = lane == idx
            jm = jnp.where(sel, float(j + 1), jm)
            vals = jnp.where(sel, -jnp.inf, vals)
            tops.append(m)
        es = [jnp.exp(m - tops[0]) for m in tops]
        den = es[0] + es[1] + es[2] + es[3]
        wm = jnp.zeros(vals.shape, F32)
        for j in range(TOP_K):
            wm = jnp.where(jm == float(j + 1), es[j] / den, wm)
        sel_any = jnp.where(jm > 0.0, 1.0, 0.0)
        r = lax.broadcasted_iota(I32, (tm, tm), 0)
        c = lax.broadcasted_iota(I32, (tm, tm), 1)
        tri = jnp.where(c < r, 1.0, 0.0).astype(BF16)
        carry = carry_sc[...]
        rank_sc[rows, :] = jnp.dot(tri, sel_any.astype(BF16), preferred_element_type=F32) + carry
        jm_sc[rows, :] = jm
        wm_sc[rows, :] = wm
        carry_sc[...] = carry + jnp.sum(sel_any, axis=0, keepdims=True)

    @pl.when(phase == 1)
    def _place():
        @pl.when(i == 0)
        def _():
            cnt = carry_sc[...]
            padded = jnp.ceil(cnt * (1.0 / SLOT_BLOCK)) * float(SLOT_BLOCK)
            r = lax.broadcasted_iota(I32, (LANES, LANES), 0)
            c = lax.broadcasted_iota(I32, (LANES, LANES), 1)
            before = jnp.where(r < c, 1.0, 0.0).astype(F32)
            ps_sc[...] = jnp.dot(jnp.broadcast_to(padded, (SUBLANES, LANES)), before, precision=HIGHEST,
                                 preferred_element_type=F32)[:1]
            cnt_ref[...] = cnt
            ps_ref[...] = ps_sc[...]

        slot = rank_sc[rows, :] + ps_sc[...]
        jm = jm_sc[rows, :]
        wm = wm_sc[rows, :]
        lane = lax.broadcasted_iota(I32, jm.shape, 1)
        dest = jnp.zeros(jm.shape, F32)
        w4 = jnp.zeros(jm.shape, F32)
        for j in range(TOP_K):
            sel = jm == float(j + 1)
            dj = jnp.sum(jnp.where(sel, slot, 0.0), axis=-1, keepdims=True)
            wj = jnp.sum(jnp.where(sel, wm, 0.0), axis=-1, keepdims=True)
            dest = jnp.where(lane == j, dj, dest)
            w4 = jnp.where(lane == j, wj, w4)
        dest_ref[...] = dest.astype(I32)
        w4_ref[...] = w4


def _route_call(logits, tm=256):
    n = logits.shape[0]
    tm = min(tm, n)
    nt = n // tm
    row = pl.BlockSpec((1, LANES), lambda p, i: (0, 0))
    out_tile = pl.BlockSpec((tm, LANES), lambda p, i: (p * i, 0))
    return pl.pallas_call(
        functools.partial(_route_body, tm=tm),
        grid=(2, nt),
        in_specs=[pl.BlockSpec((tm, LANES), lambda p, i: (i * (1 - p) + (nt - 1) * p, 0))],
        out_specs=[out_tile, out_tile, row, row],
        out_shape=[jax.ShapeDtypeStruct((n, LANES), I32), jax.ShapeDtypeStruct((n, LANES), F32),
                   jax.ShapeDtypeStruct((1, LANES), F32), jax.ShapeDtypeStruct((1, LANES), F32)],
        scratch_shapes=[pltpu.VMEM((n, LANES), F32)] * 3 + [pltpu.VMEM((1, LANES), F32)] * 2,
        compiler_params=_cparams(("arbitrary", "arbitrary")),
        name="route",
    )(logits)


ROW_UNROLL = 8
SLOT_UNROLL = 16


def _build_slot_table(dest_ref, cnt_ref, pst_ref, slot_ref, n_tok):
    def pads(e, carry):
        base = pst_ref[e] + cnt_ref[e]

        def zeros(i, c2):
            for u in range(SLOT_UNROLL):
                slot_ref[base + i * SLOT_UNROLL + u] = 0
            return c2
        lax.fori_loop(0, SLOT_BLOCK // SLOT_UNROLL, zeros, 0)
        return carry
    lax.fori_loop(0, N_EXPERTS, pads, 0)

    per_trip = SLOT_UNROLL // TOP_K
    assert n_tok % per_trip == 0

    def toks(i, carry):
        for u in range(per_trip):
            t = i * per_trip + u
            for j in range(TOP_K):
                slot_ref[dest_ref[t * TOP_K + j]] = t
        return carry
    lax.fori_loop(0, n_tok // per_trip, toks, 0)


def _expert_body(gexp_ref, gsub0_ref, gns_ref, dest_ref, cnt_ref, pst_ref,
                 hp_hbm, w1g_ref, w1l_ref, b1g_ref, b1l_ref, w2_ref, b2_ref,
                 outs_hbm,
                 stage, acc, slot_ref, sem_in, sem_out, *, n_chunks, n_groups, n_tok):
    g = pl.program_id(0)
    c = pl.program_id(1)
    ns = gns_ref[g]
    s0 = gsub0_ref[g]
    sub = SLOT_BLOCK
    half = w1g_ref.shape[0] // 2
    last = n_chunks - 1
    cur = g % 2
    pieces = half // LANES

    def row_copy(tok, buf, r):
        return pltpu.make_async_copy(hp_hbm.at[pl.ds(pl.multiple_of(tok * pieces, pieces), pieces), :],
                                     stage.at[buf, pl.ds(pl.multiple_of(r * pieces, pieces), pieces), :],
                                     sem_in.at[buf])

    def issue_rows(gi, buf, blk0, n_blk):
        base = gsub0_ref[gi] * sub

        def trip(i, carry):
            for u in range(ROW_UNROLL):
                r = blk0 * sub + i * ROW_UNROLL + u
                row_copy(slot_ref[base + r], buf, r).start()
            return carry
        lax.fori_loop(0, n_blk * (sub // ROW_UNROLL), trip, 0)

    nxt = jnp.minimum(g + 1, n_groups - 1)
    need = jnp.where(jnp.logical_and(g + 1 < n_groups, gns_ref[nxt] > 0), gns_ref[nxt], 0)
    nxt_base = gsub0_ref[nxt] * sub

    def issue_block(blk):
        for u in range(sub):
            r = blk * sub + u
            row_copy(slot_ref[nxt_base + r], 1 - cur, r).start()

    def out_copy(r0, rows):
        row0 = s0 * sub + r0
        return pltpu.make_async_copy(acc.at[pl.ds(r0, rows), :], outs_hbm.at[pl.ds(row0, rows), :], sem_out)

    def compute(r0, rows, issue_blk=None):
        los, his = [], []
        for j in range(pieces):
            w = stage[cur, pl.ds(r0 * pieces + j, rows, stride=pieces), :]
            los.append(lax.bitcast_convert_type(w << 16, F32).astype(BF16))
            his.append(lax.bitcast_convert_type(w & jnp.uint32(0xFFFF0000), F32).astype(BF16))
        xl = jnp.concatenate(los, axis=1)
        xh = jnp.concatenate(his, axis=1)
        hg = (jnp.dot(xl, w1g_ref[:half, :].astype(BF16), preferred_element_type=F32)
              + jnp.dot(xh, w1g_ref[half:, :].astype(BF16), preferred_element_type=F32) + b1g_ref[...])
        hl = (jnp.dot(xl, w1l_ref[:half, :].astype(BF16), preferred_element_type=F32)
              + jnp.dot(xh, w1l_ref[half:, :].astype(BF16), preferred_element_type=F32) + b1l_ref[...])
        gate = jnp.minimum(hg, SWIGLU_LIMIT)
        lin = jnp.clip(hl, -SWIGLU_LIMIT, SWIGLU_LIMIT)
        act = gate * jax.nn.sigmoid(SWIGLU_ALPHA * gate) * (lin + 1.0)
        acc[pl.ds(r0, rows), :] += jnp.dot(act.astype(BF16), w2_ref[...].astype(BF16),
                                           preferred_element_type=F32)
        if issue_blk is not None:
            issue_block(issue_blk)

        @pl.when(c == last)
        def _():
            out_copy(r0, rows).start()

    @pl.when(ns > 0)
    def _():
        @pl.when(c == 0)
        def _load():
            @pl.when(g == 0)
            def _():
                _build_slot_table(dest_ref, cnt_ref, pst_ref, slot_ref, n_tok)
                issue_rows(0, 0, 0, ns)

            def wt(i, carry):
                for u in range(ROW_UNROLL):
                    row_copy(0, cur, i * ROW_UNROLL + u).wait()
                return carry
            lax.fori_loop(0, ns * (sub // ROW_UNROLL), wt, 0)

            def init(s, carry):
                acc[pl.ds(pl.multiple_of(s * sub, sub), sub), :] = jnp.broadcast_to(
                    b2_ref[...], (sub, acc.shape[1]))
                return carry
            lax.fori_loop(0, ns, init, 0)

        full = TRIP_SUBS * sub
        n_full = ns // TRIP_SUBS

        def for_each_trip(fn, fn_full=None):
            def trip(i, carry):
                fn(pl.multiple_of(i * full, full), full)
                return carry
            if fn_full is None:
                lax.fori_loop(0, n_full, trip, 0)
            else:
                fn_full()
            r0 = n_full * full
            width = full // 2
            while width >= sub:
                has = (ns * sub) & width

                @pl.when(has != 0)
                def _(r0=r0, width=width):
                    fn(pl.multiple_of(r0, width), width)
                r0 = r0 + has
                width //= 2

        def full_trips():
            n_issue = jnp.clip(need - c * n_full, 0, n_full)

            def with_issue(i, carry):
                compute(pl.multiple_of(i * full, full), full, issue_blk=c * n_full + i)
                return carry

            def plain(i, carry):
                compute(pl.multiple_of(i * full, full), full)
                return carry
            lax.fori_loop(0, n_issue, with_issue, 0)
            lax.fori_loop(n_issue, n_full, plain, 0)

        for_each_trip(compute, full_trips)

        @pl.when(c == last)
        def _drain():
            done = jnp.minimum(need, n_chunks * n_full)
            issue_rows(nxt, 1 - cur, done, need - done)
            for_each_trip(lambda r0, rows: out_copy(r0, rows).wait())

            @pl.when(jnp.logical_or(g + 1 >= n_groups, gns_ref[nxt] == 0))
            def _tail():
                acc[pl.ds(0, sub), :] = jnp.zeros((sub, acc.shape[1]), F32)
                first = s0 + ns
                n_tail = outs_hbm.shape[0] // sub - first

                def tail_copy(i):
                    row0 = pl.multiple_of((first + i) * sub, sub)
                    return pltpu.make_async_copy(acc.at[pl.ds(0, sub), :], outs_hbm.at[pl.ds(row0, sub), :],
                                                 sem_out)

                def st(i, carry):
                    tail_copy(i).start()
                    return carry

                def wt(i, carry):
                    tail_copy(i).wait()
                    return carry
                lax.fori_loop(0, n_tail, st, 0)
                lax.fori_loop(0, n_tail, wt, 0)


def _expert_call(gexp, gsub0, gns, dest_flat, counts, pstart, n_slots, hp, w1, b1, w2, b2):
    half = w1.shape[1] // 2
    n_exp, d, ff2 = w1.shape
    ff = ff2 // 2
    tc = FF_CHUNK
    n_chunks = ff // tc
    n_groups = gexp.shape[0]
    rows = GROUP_SUBS * SLOT_BLOCK

    def chunk(c, gn, g):
        return jnp.where(gn[g] > 0, c, n_chunks - 1)

    grid_spec = pltpu.PrefetchScalarGridSpec(
        num_scalar_prefetch=6,
        grid=(n_groups, n_chunks),
        in_specs=[
            pl.BlockSpec(memory_space=pl.ANY),
            pl.BlockSpec((None, d, tc), lambda g, c, ge, gs, gn, *_: (ge[g], 0, chunk(c, gn, g))),
            pl.BlockSpec((None, d, tc), lambda g, c, ge, gs, gn, *_: (ge[g], 0, n_chunks + chunk(c, gn, g))),
            pl.BlockSpec((None, 1, tc), lambda g, c, ge, gs, gn, *_: (ge[g], 0, chunk(c, gn, g))),
            pl.BlockSpec((None, 1, tc), lambda g, c, ge, gs, gn, *_: (ge[g], 0, n_chunks + chunk(c, gn, g))),
            pl.BlockSpec((None, tc, d), lambda g, c, ge, gs, gn, *_: (ge[g], chunk(c, gn, g), 0)),
            pl.BlockSpec((None, 1, d), lambda g, c, ge, gs, gn, *_: (ge[g], 0, 0)),
        ],
        out_specs=pl.BlockSpec(memory_space=pl.ANY),
        scratch_shapes=[
            pltpu.VMEM((2, rows * (half // LANES), LANES), U32),
            pltpu.VMEM((rows, d), F32),
            pltpu.SMEM((n_slots + SLOT_BLOCK,), I32),
            pltpu.SemaphoreType.DMA((2,)), pltpu.SemaphoreType.DMA(()),
        ],
    )
    return pl.pallas_call(
        functools.partial(_expert_body, n_chunks=n_chunks, n_groups=n_groups,
                          n_tok=dest_flat.shape[0] // TOP_K),
        grid_spec=grid_spec,
        out_shape=jax.ShapeDtypeStruct((n_slots, d), F32),
        compiler_params=_cparams(("arbitrary", "arbitrary")),
        name="expert",
    )(gexp, gsub0, gns, dest_flat, counts, pstart, hp, w1, w1, b1.reshape(n_exp, 1, ff2),
      b1.reshape(n_exp, 1, ff2), w2, b2.reshape(n_exp, 1, d))


def _final_body(dest_ref, x1_ref, w4_ref, p_ref, wple_ref, wpg_ref, bpg_ref, npl_ref, outs_hbm,
                o_ref, gbuf, sems, *, tm, n_steps):
    i = pl.program_id(0)
    cur = i % 2

    def row_copy(slot, buf, j, q, u):
        return pltpu.make_async_copy(outs_hbm.at[pl.ds(slot, 1), :], gbuf.at[buf, j, q, pl.ds(u, 1), :],
                                     sems.at[buf])

    def issue(step, buf):
        def rows(q, carry):
            for u in range(SUBLANES):
                t = step * tm + q * SUBLANES + u
                for j in range(TOP_K):
                    row_copy(dest_ref[t * TOP_K + j], buf, j, q, u).start(priority=j % 2)
            return carry
        lax.fori_loop(0, tm // SUBLANES, rows, 0)

    @pl.when(i == 0)
    def _():
        issue(0, 0)

    pe = jnp.dot(p_ref[...].astype(BF16), wple_ref[...], preferred_element_type=F32)

    def wt(q, carry):
        for u in range(SUBLANES):
            for j in range(TOP_K):
                row_copy(0, cur, j, q, u).wait()
        return carry
    lax.fori_loop(0, tm // SUBLANES, wt, 0)

    w4 = w4_ref[...]
    d = o_ref.shape[1]
    y = gbuf[cur, 0].reshape(tm, d) * w4[:, 0:1]
    for j in range(1, TOP_K):
        y = y + gbuf[cur, j].reshape(tm, d) * w4[:, j:j + 1]
    x2 = x1_ref[...] + y
    hn = _rmsnorm_rows(x2, npl_ref[...])
    gate = jax.nn.sigmoid(jnp.dot(hn.astype(BF16), wpg_ref[...], preferred_element_type=F32) + bpg_ref[...])
    o_ref[...] = x2 + gate * pe

    nxt = jnp.minimum(i + 1, n_steps - 1)
    for q in range(tm // SUBLANES):
        for u in range(SUBLANES):
            t = nxt * tm + q * SUBLANES + u
            for j in range(TOP_K):
                row_copy(dest_ref[t * TOP_K + j], 1 - cur, j, q, u).start(priority=j % 2)

    @pl.when(i == n_steps - 1)
    def _():
        def wt_last(q, carry):
            for u in range(SUBLANES):
                for j in range(TOP_K):
                    row_copy(0, 1 - cur, j, q, u).wait()
            return carry
        lax.fori_loop(0, tm // SUBLANES, wt_last, 0)


def _final_call(dest_flat, x1, w4, p2d, wple, wpg, bpg_row, npl_row, outs, tm=256):
    n, d = x1.shape
    tm = min(tm, n)
    pd = p2d.shape[1]
    res = lambda shape: pl.BlockSpec(shape, lambda i, dst: (0,) * len(shape), pipeline_mode=pl.Buffered(1))
    grid_spec = pltpu.PrefetchScalarGridSpec(
        num_scalar_prefetch=1,
        grid=(n // tm,),
        in_specs=[pl.BlockSpec((tm, d), lambda i, dst: (i, 0)),
                  pl.BlockSpec((tm, LANES), lambda i, dst: (i, 0)),
                  pl.BlockSpec((tm, pd), lambda i, dst: (i, 0)),
                  res((pd, d)), res((d, d)), res((1, d)), res((1, d)),
                  pl.BlockSpec(memory_space=pl.ANY)],
        out_specs=pl.BlockSpec((tm, d), lambda i, dst: (i, 0)),
        scratch_shapes=[pltpu.VMEM((2, TOP_K, tm // SUBLANES, SUBLANES, d), F32),
                        pltpu.SemaphoreType.DMA((2,))],
    )
    return pl.pallas_call(
        functools.partial(_final_body, tm=tm, n_steps=n // tm),
        grid_spec=grid_spec,
        out_shape=jax.ShapeDtypeStruct((n, d), F32),
        compiler_params=_cparams(("arbitrary",)),
        name="final",
    )(dest_flat, x1, w4, p2d, wple, wpg, bpg_row, npl_row, outs)


def _t5_bucket(rel):
    n = -rel
    nb = REL_BUCKETS // 2
    ret = jnp.where(n < 0, nb, 0)
    n = jnp.abs(n)
    max_exact = nb // 2
    large = max_exact + (jnp.log(jnp.maximum(n, 1).astype(jnp.float32) / max_exact)
                         / math.log(REL_MAX_DIST / max_exact) * (nb - max_exact)).astype(jnp.int32)
    large = jnp.minimum(large, nb - 1)
    return ret + jnp.where(n < max_exact, n, large)


def _bias_body(rv_ref, o_ref, *, t):
    x = jnp.broadcast_to(rv_ref[...], (t, rv_ref.shape[1]))
    o_ref[...] = pltpu.roll(x, 1, 1, stride=1, stride_axis=0)[:, t:]


def _diff_bias_table(rel_bias, seq, t):
    rel = jnp.arange(seq + t) - (seq - 1)
    rv = (jnp.transpose(rel_bias[_t5_bucket(rel)], (1, 0)).astype(F32) * LOG2E)[:, None, :]
    return pl.pallas_call(
        functools.partial(_bias_body, t=t),
        grid=(N_HEADS,),
        in_specs=[pl.BlockSpec((None, 1, seq + t), lambda h: (h, 0, 0))],
        out_specs=pl.BlockSpec((None, t, seq), lambda h: (h, 0, 0)),
        out_shape=jax.ShapeDtypeStruct((N_HEADS, t, seq), F32),
        compiler_params=_cparams(("parallel",)),
        name="bias_table",
    )(rv)


def _pad_cols(a, width):
    return jnp.pad(a, ((0, 0), (0, width - a.shape[1])))


def _group_table(counts):
    n_sub = (counts + SLOT_BLOCK - 1) // SLOT_BLOCK
    sub_start = jnp.cumsum(n_sub) - n_sub
    n_grp = (n_sub + GROUP_SUBS - 1) // GROUP_SUBS
    grp_end = jnp.cumsum(n_grp)
    total = grp_end[-1]
    return n_sub, sub_start, n_grp, grp_end, total


def _layer(x, p_l, w_in, b_gate, b_forget, dq_norm, dk_norm, fq_norm, fk_norm, lq1, lk1, lq2, lk2,
           lambda_init, subln, w_up_a, w_up_b, w_out, rel_bias, norm_mix, norm_moe, w_router,
           b_router, w1, b1, w2, b2, norm_ple, w_ple_gate, b_ple_gate, w_ple):
    b, s, d = x.shape
    n = b * s
    width = N_HEADS * HEAD_DIM
    t = min(ATTN_T, s)
    x2d = x.reshape(n, d)
    row = lambda v: v.reshape(1, -1).astype(F32)

    wt_in = jnp.swapaxes(w_in, 0, 1)
    diff_scale = DIFF_QK_DIM ** -0.5 * LOG2E
    fox_scale = HEAD_DIM ** -0.5 * LOG2E
    aux = jnp.concatenate([jnp.tile(dq_norm * diff_scale, 2 * N_HEADS), jnp.tile(dk_norm, 2 * N_HEADS),
                           jnp.zeros((width,), F32),
                           jnp.tile(fq_norm * fox_scale, N_HEADS), jnp.tile(fk_norm, N_HEADS),
                           jnp.zeros((width,), F32), b_gate.reshape(-1)])
    proj, f_logit = _inproj_call(x2d, row(norm_mix), wt_in, row(aux))
    gates = proj
    qkv = proj.reshape(b, s, -1)
    first = INPROJ_GATE_BLOCKS * INPROJ_TN // width

    cum = _cum_call(f_logit, _pad_cols(row(b_forget), LANES), b, s)
    cum_bhs = jnp.transpose(cum[:, :N_HEADS].reshape(b, s, N_HEADS), (0, 2, 1))
    cum_row = cum_bhs[:, :, None, :]

    lam_rows = [row(v) for v in (lq1, lk1, lq2, lk2)]
    bias = _diff_bias_table(rel_bias, s, t)
    od = _diff_attn_call(qkv, (first, first + 1, first + 2), bias, lam_rows, row(subln), lambda_init, t)
    of = _fox_attn_call(qkv, (first + 3, first + 4, first + 5), cum.reshape(b, s, LANES), cum_row, t)

    br_pad = jnp.full((1, LANES), NEG, F32).at[0, :N_EXPERTS].set(b_router.astype(F32))
    wr_pad = jnp.pad(jnp.swapaxes(w_router.astype(F32), 0, 1), ((0, LANES - N_EXPERTS), (0, 0)))
    wr_hi = wr_pad.astype(BF16)
    wr_lo = (wr_pad - wr_hi.astype(F32)).astype(BF16)
    x1, hp, logits = _mix_call(od.reshape(n, width), of.reshape(n, width), gates, x2d,
                               w_up_a.astype(BF16), w_up_b.astype(BF16), w_out.astype(BF16),
                               row(norm_moe), wr_hi, wr_lo, br_pad)

    dest128, w4, cnt, ps_row = _route_call(logits)
    counts = cnt[0, :N_EXPERTS].astype(I32)
    pstart = ps_row[0, :N_EXPERTS].astype(I32)
    dest_flat = dest128[:, :TOP_K].reshape(-1)

    n_slots = n * TOP_K + N_EXPERTS * SLOT_BLOCK

    n_sub, sub_start, n_grp, grp_end, total = _group_table(counts)
    max_groups = N_EXPERTS + (n_slots // SLOT_BLOCK) // GROUP_SUBS
    gidx = jnp.arange(max_groups, dtype=I32)
    gvalid = gidx < total
    gsafe = jnp.minimum(gidx, total - 1)
    gexp = jnp.searchsorted(grp_end, gsafe, side="right").astype(I32)
    kth = gsafe - (grp_end - n_grp)[gexp]
    gsub0 = (sub_start[gexp] + kth * GROUP_SUBS).astype(I32)
    gns = jnp.where(gvalid, jnp.minimum(GROUP_SUBS, n_sub[gexp] - kth * GROUP_SUBS), 0).astype(I32)
    def run_experts(ng):
        return _expert_call(gexp[:ng], gsub0[:ng], gns[:ng], dest_flat, counts, pstart, n_slots, hp,
                            w1, b1, w2, b2)

    few = min(N_EXPERTS + 2, max_groups)
    outs = lax.cond(total <= few, lambda: run_experts(few), lambda: run_experts(max_groups))

    out = _final_call(dest_flat, x1, w4, p_l.reshape(n, -1), w_ple.astype(BF16),
                      w_ple_gate.astype(BF16), row(b_ple_gate), row(norm_ple), outs)
    return out.reshape(b, s, d)


def kernel(x, p, w_in, b_gate, b_forget, dq_norm, dk_norm, fq_norm, fk_norm, lambda_q1, lambda_k1,
           lambda_q2, lambda_k2, subln, w_up_a, w_up_b, w_out, rel_bias, norm_mix, norm_moe,
           w_router, b_router, w1, b1, w2, b2, norm_ple, w_ple_gate, b_ple_gate, w_ple):
    for i in range(w_in.shape[0]):
        lambda_init = 0.8 - 0.6 * math.exp(-0.3 * i)
        x = _layer(x, p[i], w_in[i], b_gate[i], b_forget[i], dq_norm[i], dk_norm[i], fq_norm[i],
                   fk_norm[i], lambda_q1[i], lambda_k1[i], lambda_q2[i], lambda_k2[i], lambda_init,
                   subln[i], w_up_a[i], w_up_b[i], w_out[i], rel_bias, norm_mix[i], norm_moe[i],
                   w_router[i], b_router[i], w1[i], b1[i], w2[i], b2[i], norm_ple[i],
                   w_ple_gate[i], b_ple_gate[i], w_ple[i])
    return x
```

```python
import functools
import math

import jax
import jax.numpy as jnp
from jax import lax
from jax.experimental import pallas as pl
from jax.experimental.pallas import tpu as pltpu

F32 = jnp.float32
BF16 = jnp.bfloat16
U32 = jnp.uint32
I32 = jnp.int32
HIGHEST = lax.Precision.HIGHEST

N_HEADS = 8
HEAD_DIM = 128
DIFF_QK_DIM = 64
CHUNK = 64
REL_BUCKETS = 32
REL_MAX_DIST = 128
N_EXPERTS = 32
TOP_K = 4
SWIGLU_LIMIT = 7.0
SWIGLU_ALPHA = 1.702
RMS_EPS = 1e-6
NEG = -1e30
LOG2E = math.log2(math.e)

LANES = 128
SUBLANES = 8
SLOT_BLOCK = 128
GROUP_SUBS = 12
TRIP_SUBS = 4
FF_CHUNK = 512
ATTN_T = 256
VMEM_LIMIT = 56 * 1024 * 1024


_NT = (((1,), (1,)), ((), ()))


def _cparams(sem):
    return pltpu.CompilerParams(dimension_semantics=sem, vmem_limit_bytes=VMEM_LIMIT)


def _rmsnorm_rows(x, g):
    ms = jnp.mean(x * x, axis=-1, keepdims=True)
    return x * lax.rsqrt(ms + RMS_EPS) * g


def _cum_body(fl_ref, bf_ref, cum_ref):
    s = fl_ref.shape[0]
    r = lax.broadcasted_iota(I32, (LANES, LANES), 0)
    c = lax.broadcasted_iota(I32, (LANES, LANES), 1)
    tri = jnp.where(r >= c, 1.0, 0.0).astype(F32)
    carry = jnp.zeros((1, LANES), F32)
    for blk in range(s // LANES):
        rows = slice(blk * LANES, (blk + 1) * LANES)
        lf = jax.nn.log_sigmoid(fl_ref[rows, :] + bf_ref[...])
        cb = jnp.dot(tri, lf, precision=HIGHEST, preferred_element_type=F32) + carry
        cum_ref[rows, :] = cb * LOG2E
        carry = cb[LANES - 1:LANES, :]


def _cum_call(fl, bf_row, batch, seq):
    return pl.pallas_call(
        _cum_body,
        grid=(batch,),
        in_specs=[pl.BlockSpec((seq, LANES), lambda b: (b, 0)),
                  pl.BlockSpec((1, LANES), lambda b: (0, 0))],
        out_specs=pl.BlockSpec((seq, LANES), lambda b: (b, 0)),
        out_shape=jax.ShapeDtypeStruct(fl.shape, F32),
        compiler_params=_cparams(("parallel",)),
        name="cum",
    )(fl, bf_row)


def _proj_epilogue(acc, aux_ref, o_ref, kind):
    tn = acc.shape[1]
    if kind == "plain":
        o_ref[...] = acc.astype(o_ref.dtype)
    elif kind == "gate":
        o_ref[...] = jax.nn.sigmoid(acc + aux_ref[...]).astype(o_ref.dtype)
    else:
        lo = lax.broadcasted_iota(I32, (1, LANES), 1) < DIFF_QK_DIM
        for s in range(tn // LANES):
            cols = slice(s * LANES, (s + 1) * LANES)
            blk = acc[:, cols]
            sq = blk * blk
            if kind == "norm128":
                ms = jnp.mean(sq, axis=-1, keepdims=True)
            else:
                s_lo = jnp.sum(jnp.where(lo, sq, 0.0), axis=-1, keepdims=True)
                s_hi = jnp.sum(jnp.where(lo, 0.0, sq), axis=-1, keepdims=True)
                ms = jnp.where(lo, s_lo, s_hi) * (1.0 / DIFF_QK_DIM)
            o_ref[:, cols] = (blk * lax.rsqrt(ms + RMS_EPS) * aux_ref[:, cols]).astype(o_ref.dtype)


INPROJ_TN = 1024
_INPROJ_PARTS = ((0, 2, "norm64", 0), (2, 1, "plain", 0), (3, 2, "norm128", 0), (5, 1, "plain", 0),
                 (6, 4, "gate", N_HEADS))
INPROJ_GATE_BLOCKS = 4


def _inproj_body(x_ref, g_ref, wa_ref, wb_ref, wf_ref, aux_ref, o_ref, fl_ref, h_sc):
    j = pl.program_id(1)

    @pl.when(j == 0)
    def _norm():
        y = _rmsnorm_rows(x_ref[...], g_ref[...])
        y_hi = y.astype(BF16)
        h_sc[...] = y_hi
        rowi = lax.broadcasted_iota(I32, (LANES, 1), 0)
        wf = jnp.where(rowi < N_HEADS, wf_ref[...], 0.0)
        y_lo = (y - y_hi.astype(F32)).astype(BF16)
        wf_hi = wf.astype(BF16)
        wf_lo = (wf - wf_hi.astype(F32)).astype(BF16)
        fl_ref[...] = (lax.dot_general(y_hi, wf_hi, _NT, preferred_element_type=F32)
                       + lax.dot_general(y_lo, wf_hi, _NT, preferred_element_type=F32)
                       + lax.dot_general(y_hi, wf_lo, _NT, preferred_element_type=F32))

    for first, count, kind, shift in _INPROJ_PARTS:
        @pl.when(jnp.logical_and(j >= first, j < first + count))
        def _(kind=kind, shift=shift):
            if shift == 0:
                wt = wa_ref[...]
            else:
                wt = jnp.concatenate([wa_ref[shift:, :], wb_ref[...]], axis=0)
            acc = lax.dot_general(h_sc[...], wt.astype(BF16), _NT, preferred_element_type=F32)
            _proj_epilogue(acc, aux_ref, o_ref, kind)


def _inproj_call(x2d, g_row, wt_in, aux_row, tm=1024):
    n, d = x2d.shape
    tm = min(tm, n)
    tn = INPROJ_TN
    nblk = sum(p[1] for p in _INPROJ_PARTS)
    first_qkv = _INPROJ_PARTS[-1][0]
    assert _INPROJ_PARTS[-1][3] == SUBLANES and first_qkv * tn % LANES == 0

    def out_block(j):
        return jnp.where(j >= first_qkv, j - first_qkv, j + INPROJ_GATE_BLOCKS)

    return pl.pallas_call(
        _inproj_body,
        grid=(n // tm, nblk),
        in_specs=[pl.BlockSpec((tm, d), lambda i, j: (i, 0)),
                  pl.BlockSpec((1, d), lambda i, j: (0, 0)),
                  pl.BlockSpec((tn, d), lambda i, j: (j, 0)),
                  pl.BlockSpec((SUBLANES, d), lambda i, j: ((j + 1) * (tn // SUBLANES), 0)),
                  pl.BlockSpec((LANES, d), lambda i, j: (first_qkv * tn // LANES, 0)),
                  pl.BlockSpec((1, tn), lambda i, j: (0, j))],
        out_specs=[pl.BlockSpec((tm, tn), lambda i, j: (i, out_block(j))),
                   pl.BlockSpec((tm, LANES), lambda i, j: (i, 0))],
        out_shape=[jax.ShapeDtypeStruct((n, nblk * tn), BF16), jax.ShapeDtypeStruct((n, LANES), F32)],
        scratch_shapes=[pltpu.VMEM((tm, d), BF16)],
        compiler_params=_cparams(("parallel", "arbitrary")),
        name="inproj",
    )(x2d, g_row, wt_in, wt_in, wt_in, aux_row)


DIFF_HEADS_PER_STEP = 8
FOX_HEADS_PER_STEP = 8


def _softmax_step(s, m_ref, acc_ref, v1):
    m_old = m_ref[...]
    mn = jnp.maximum(m_old, jnp.max(s, axis=-1, keepdims=True))
    alpha = jnp.exp2(m_old - mn)
    p = jnp.exp2(s - jnp.tile(mn, (1, s.shape[1] // LANES))).astype(BF16)
    acc_ref[...] = (jnp.tile(alpha, (1, acc_ref.shape[1] // LANES)) * acc_ref[...]
                    + jnp.dot(p, v1, preferred_element_type=F32))
    m_ref[...] = mn


def _causal_sweep(step, qi, t):
    def pair(i, carry):
        step(2 * i, 2 * t, None)
        return carry
    lax.fori_loop(0, qi // 2, pair, 0)

    @pl.when(qi % 2 == 1)
    def _():
        step(qi - 1, t, None)
    step(qi, t, 0)


def _with_ones(v):
    return jnp.concatenate([v, jnp.ones_like(v)], axis=1)


def _normalised(acc):
    return acc[:, :HEAD_DIM] / acc[:, HEAD_DIM:]


def _diff_attn_body(lq1_ref, lk1_ref, lq2_ref, lk2_ref, g_ref, q_ref, k_ref, v_ref, bias_ref,
                    o_ref, m_sc, acc_sc, *, t, nh, lambda_init):
    qi = pl.program_id(2)
    s_len = k_ref.shape[0]
    m_sc[...] = jnp.full(m_sc.shape, NEG, F32)
    acc_sc[...] = jnp.zeros(acc_sc.shape, F32)
    lo = lax.broadcasted_iota(I32, (t, HEAD_DIM), 1) < DIFF_QK_DIM

    def step(kb, w, rel):
        k0 = pl.multiple_of(kb * t, t)
        off = pl.multiple_of((kb - qi) * t + (s_len - t), LANES)
        masked = rel is not None
        if masked:
            r = lax.broadcasted_iota(I32, (t, w), 0)
            c = lax.broadcasted_iota(I32, (t, w), 1) + rel
            chunk_of = lambda pos: lax.shift_right_arithmetic(pos, jnp.int32(CHUNK.bit_length() - 1))
            ok = chunk_of(c) <= chunk_of(r)
        for h in range(nh):
            cols = slice(h * HEAD_DIM, (h + 1) * HEAD_DIM)
            q = q_ref[:, cols]
            zero = jnp.zeros_like(q)
            k = k_ref[pl.ds(k0, w), cols]
            v1 = _with_ones(v_ref[pl.ds(k0, w), cols])
            b = bias_ref[h, :, pl.ds(off, w)]
            for mp, qm in enumerate((jnp.where(lo, q, zero), jnp.where(lo, zero, q))):
                s = lax.dot_general(qm, k, _NT, preferred_element_type=F32) + b
                if masked:
                    s = jnp.where(ok, s, NEG)
                _softmax_step(s, m_sc.at[2 * h + mp], acc_sc.at[2 * h + mp], v1)

    _causal_sweep(step, qi, t)

    lam = (jnp.exp(jnp.sum(lq1_ref[...] * lk1_ref[...], axis=-1, keepdims=True))
           - jnp.exp(jnp.sum(lq2_ref[...] * lk2_ref[...], axis=-1, keepdims=True)) + lambda_init)
    for h in range(nh):
        o = _normalised(acc_sc[2 * h]) - lam * _normalised(acc_sc[2 * h + 1])
        o = _rmsnorm_rows(o, g_ref[...]) * (1.0 - lambda_init)
        o_ref[:, h * HEAD_DIM:(h + 1) * HEAD_DIM] = o.astype(o_ref.dtype)


def _diff_attn_call(qkv, blocks, bias, lam_rows, subln_row, lambda_init, t):
    b, s, _ = qkv.shape
    nh = DIFF_HEADS_PER_STEP
    wb = nh * HEAD_DIM
    n_hg = N_HEADS // nh
    qb, kb_, vb = (blk * n_hg for blk in blocks)
    vec = lambda w: pl.BlockSpec((1, w), lambda hg, bb, qi: (0, 0))
    return pl.pallas_call(
        functools.partial(_diff_attn_body, t=t, nh=nh, lambda_init=lambda_init),
        grid=(n_hg, b, s // t),
        in_specs=[vec(DIFF_QK_DIM)] * 4 + [
            vec(HEAD_DIM),
            pl.BlockSpec((None, t, wb), lambda hg, bb, qi: (bb, qi, qb + hg)),
            pl.BlockSpec((None, s, wb), lambda hg, bb, qi: (bb, 0, kb_ + hg)),
            pl.BlockSpec((None, s, wb), lambda hg, bb, qi: (bb, 0, vb + hg)),
            pl.BlockSpec((nh, t, s), lambda hg, bb, qi: (hg, 0, 0),
                         pipeline_mode=pl.Buffered(1 if n_hg == 1 else 2))],
        out_specs=pl.BlockSpec((None, t, wb), lambda hg, bb, qi: (bb, qi, hg)),
        out_shape=jax.ShapeDtypeStruct((b, s, N_HEADS * HEAD_DIM), BF16),
        scratch_shapes=[pltpu.VMEM((2 * nh, t, LANES), F32), pltpu.VMEM((2 * nh, t, 2 * HEAD_DIM), F32)],
        compiler_params=_cparams(("parallel", "parallel", "parallel")),
        name="diff_attn",
    )(*lam_rows, subln_row, qkv, qkv, qkv, bias)


def _fox_attn_body(q_ref, k_ref, v_ref, cq_ref, ck_ref, o_ref, m_sc, acc_sc, cq_sc, *, t, nh):
    qi = pl.program_id(2)
    m_sc[...] = jnp.full(m_sc.shape, NEG, F32)
    acc_sc[...] = jnp.zeros(acc_sc.shape, F32)
    hg = pl.program_id(0)
    lane = lax.broadcasted_iota(I32, (t, LANES), 1)
    cq_all = cq_ref[...]
    for h in range(nh):
        col = jnp.sum(jnp.where(lane == hg * nh + h, cq_all, 0.0), axis=-1, keepdims=True)
        cq_sc[h] = jnp.broadcast_to(col, (t, LANES))

    def step(kb, w, rel):
        k0 = pl.multiple_of(kb * t, t)
        masked = rel is not None
        if masked:
            r = lax.broadcasted_iota(I32, (t, w), 0)
            c = lax.broadcasted_iota(I32, (t, w), 1) + rel
            ok = c <= r
        for h in range(nh):
            cols = slice(h * HEAD_DIM, (h + 1) * HEAD_DIM)
            k = k_ref[pl.ds(k0, w), cols]
            v1 = _with_ones(v_ref[pl.ds(k0, w), cols])
            s = lax.dot_general(q_ref[:, cols], k, _NT, preferred_element_type=F32)
            s = s + jnp.tile(cq_sc[h], (1, w // LANES)) - ck_ref[h, :, pl.ds(k0, w)]
            if masked:
                s = jnp.where(ok, s, NEG)
            _softmax_step(s, m_sc.at[h], acc_sc.at[h], v1)

    _causal_sweep(step, qi, t)
    for h in range(nh):
        o_ref[:, h * HEAD_DIM:(h + 1) * HEAD_DIM] = _normalised(acc_sc[h]).astype(o_ref.dtype)


def _fox_attn_call(qkv, blocks, cum, cum_row, t):
    b, s, _ = qkv.shape
    nh = FOX_HEADS_PER_STEP
    wb = nh * HEAD_DIM
    n_hg = N_HEADS // nh
    qb, kb_, vb = (blk * n_hg for blk in blocks)
    return pl.pallas_call(
        functools.partial(_fox_attn_body, t=t, nh=nh),
        grid=(n_hg, b, s // t),
        in_specs=[pl.BlockSpec((None, t, wb), lambda hg, bb, qi: (bb, qi, qb + hg)),
                  pl.BlockSpec((None, s, wb), lambda hg, bb, qi: (bb, 0, kb_ + hg)),
                  pl.BlockSpec((None, s, wb), lambda hg, bb, qi: (bb, 0, vb + hg)),
                  pl.BlockSpec((None, t, LANES), lambda hg, bb, qi: (bb, qi, 0)),
                  pl.BlockSpec((None, nh, 1, s), lambda hg, bb, qi: (bb, hg, 0, 0))],
        out_specs=pl.BlockSpec((None, t, wb), lambda hg, bb, qi: (bb, qi, hg)),
        out_shape=jax.ShapeDtypeStruct((b, s, N_HEADS * HEAD_DIM), BF16),
        scratch_shapes=[pltpu.VMEM((nh, t, LANES), F32), pltpu.VMEM((nh, t, 2 * HEAD_DIM), F32),
                        pltpu.VMEM((nh, t, LANES), F32)],
        compiler_params=_cparams(("parallel", "parallel", "parallel")),
        name="fox_attn",
    )(qkv, qkv, qkv, cum, cum_row)


def _mix_body(od_ref, of_ref, g_ref, x_ref, wa_ref, wb_ref, wo_ref, nm_ref, wrh_ref, wrl_ref, br_ref,
              x1_ref, hp_ref, lg_ref):
    d = x_ref.shape[1]
    ua = jnp.dot(od_ref[...], wa_ref[...], preferred_element_type=F32)
    ub = jnp.dot(of_ref[...], wb_ref[...], preferred_element_type=F32)
    mixed = g_ref[:, :d].astype(F32) * ua + g_ref[:, d:].astype(F32) * ub
    x1 = x_ref[...] + jnp.dot(mixed.astype(BF16), wo_ref[...], preferred_element_type=F32)
    x1_ref[...] = x1
    hm = _rmsnorm_rows(x1, nm_ref[...])
    h_hi = hm.astype(BF16)
    h_hi32 = h_hi.astype(F32)
    h_lo = (hm - h_hi32).astype(BF16)
    lg_ref[...] = (lax.dot_general(h_hi, wrh_ref[...], _NT, preferred_element_type=F32)
                   + lax.dot_general(h_lo, wrh_ref[...], _NT, preferred_element_type=F32)
                   + lax.dot_general(h_hi, wrl_ref[...], _NT, preferred_element_type=F32) + br_ref[...])
    bits = lax.bitcast_convert_type(h_hi32, U32)
    packed = (bits[:, :d // 2] >> 16) | (bits[:, d // 2:] & jnp.uint32(0xFFFF0000))
    tm = packed.shape[0]
    pieces = packed.shape[1] // LANES
    for j in range(pieces):
        hp_ref[pl.ds(j, tm, stride=pieces), :] = packed[:, j * LANES:(j + 1) * LANES]


def _resident(shape):
    return pl.BlockSpec(shape, lambda i: (0,) * len(shape), pipeline_mode=pl.Buffered(1))


def _mix_call(od, of, gates, x2d, wa, wb, wo, nm_row, wr_hi, wr_lo, br_pad, tm=256):
    n, d = x2d.shape
    tm = min(tm, n)
    wdt = od.shape[1]
    return pl.pallas_call(
        _mix_body,
        grid=(n // tm,),
        in_specs=[pl.BlockSpec((tm, wdt), lambda i: (i, 0)),
                  pl.BlockSpec((tm, wdt), lambda i: (i, 0)),
                  pl.BlockSpec((tm, 2 * d), lambda i: (i, 0)),
                  pl.BlockSpec((tm, d), lambda i: (i, 0)),
                  _resident((wdt, d)), _resident((wdt, d)), _resident((d, d)),
                  _resident((1, d)), _resident((LANES, d)), _resident((LANES, d)), _resident((1, LANES))],
        out_specs=[pl.BlockSpec((tm, d), lambda i: (i, 0)),
                   pl.BlockSpec((tm * (d // 2 // LANES), LANES), lambda i: (i, 0)),
                   pl.BlockSpec((tm, LANES), lambda i: (i, 0))],
        out_shape=[jax.ShapeDtypeStruct((n, d), F32),
                   jax.ShapeDtypeStruct((n * (d // 2 // LANES), LANES), U32),
                   jax.ShapeDtypeStruct((n, LANES), F32)],
        compiler_params=_cparams(("parallel",)),
        name="mix",
    )(od, of, gates, x2d, wa, wb, wo, nm_row, wr_hi, wr_lo, br_pad)


def _route_body(lg_ref, dest_ref, w4_ref, cnt_ref, ps_ref, rank_sc, jm_sc, wm_sc, carry_sc, ps_sc, *, tm):
    phase = pl.program_id(0)
    i = pl.program_id(1)
    rows = pl.ds(pl.multiple_of(i * tm, tm), tm)

    @pl.when(phase == 0)
    def _select():
        @pl.when(i == 0)
        def _():
            carry_sc[...] = jnp.zeros_like(carry_sc)

        vals = lg_ref[...]
        lane = lax.broadcasted_iota(I32, vals.shape, 1).astype(F32)
        jm = jnp.zeros(vals.shape, F32)
        tops = []
        for j in range(TOP_K):
            m = jnp.max(vals, axis=-1, keepdims=True)
            idx = jnp.min(jnp.where(vals == m, lane, float(LANES)), axis=-1, keepdims=True)
            sel = lane == idx
            jm = jnp.where(sel, float(j + 1), jm)
            vals = jnp.where(sel, -jnp.inf, vals)
            tops.append(m)
        es = [jnp.exp(m - tops[0]) for m in tops]
        den = es[0] + es[1] + es[2] + es[3]
        wm = jnp.zeros(vals.shape, F32)
        for j in range(TOP_K):
            wm = jnp.where(jm == float(j + 1), es[j] / den, wm)
        sel_any = jnp.where(jm > 0.0, 1.0, 0.0)
        r = lax.broadcasted_iota(I32, (tm, tm), 0)
        c = lax.broadcasted_iota(I32, (tm, tm), 1)
        tri = jnp.where(c < r, 1.0, 0.0).astype(BF16)
        carry = carry_sc[...]
        rank_sc[rows, :] = jnp.dot(tri, sel_any.astype(BF16), preferred_element_type=F32) + carry
        jm_sc[rows, :] = jm
        wm_sc[rows, :] = wm
        carry_sc[...] = carry + jnp.sum(sel_any, axis=0, keepdims=True)

    @pl.when(phase == 1)
    def _place():
        @pl.when(i == 0)
        def _():
            cnt = carry_sc[...]
            padded = jnp.ceil(cnt * (1.0 / SLOT_BLOCK)) * float(SLOT_BLOCK)
            r = lax.broadcasted_iota(I32, (LANES, LANES), 0)
            c = lax.broadcasted_iota(I32, (LANES, LANES), 1)
            before = jnp.where(r < c, 1.0, 0.0).astype(F32)
            ps_sc[...] = jnp.dot(jnp.broadcast_to(padded, (SUBLANES, LANES)), before, precision=HIGHEST,
                                 preferred_element_type=F32)[:1]
            cnt_ref[...] = cnt
            ps_ref[...] = ps_sc[...]

        slot = rank_sc[rows, :] + ps_sc[...]
        jm = jm_sc[rows, :]
        wm = wm_sc[rows, :]
        lane = lax.broadcasted_iota(I32, jm.shape, 1)
        dest = jnp.zeros(jm.shape, F32)
        w4 = jnp.zeros(jm.shape, F32)
        for j in range(TOP_K):
            sel = jm == float(j + 1)
            dj = jnp.sum(jnp.where(sel, slot, 0.0), axis=-1, keepdims=True)
            wj = jnp.sum(jnp.where(sel, wm, 0.0), axis=-1, keepdims=True)
            dest = jnp.where(lane == j, dj, dest)
            w4 = jnp.where(lane == j, wj, w4)
        dest_ref[...] = dest.astype(I32)
        w4_ref[...] = w4


def _route_call(logits, tm=1024):
    n = logits.shape[0]
    tm = min(tm, n)
    nt = n // tm
    row = pl.BlockSpec((1, LANES), lambda p, i: (0, 0))
    out_tile = pl.BlockSpec((tm, LANES), lambda p, i: (p * i, 0))
    return pl.pallas_call(
        functools.partial(_route_body, tm=tm),
        grid=(2, nt),
        in_specs=[pl.BlockSpec((tm, LANES), lambda p, i: (i * (1 - p) + (nt - 1) * p, 0))],
        out_specs=[out_tile, out_tile, row, row],
        out_shape=[jax.ShapeDtypeStruct((n, LANES), I32), jax.ShapeDtypeStruct((n, LANES), F32),
                   jax.ShapeDtypeStruct((1, LANES), F32), jax.ShapeDtypeStruct((1, LANES), F32)],
        scratch_shapes=[pltpu.VMEM((n, LANES), F32)] * 3 + [pltpu.VMEM((1, LANES), F32)] * 2,
        compiler_params=_cparams(("arbitrary", "arbitrary")),
        name="route",
    )(logits)


ROW_UNROLL = 8
SLOT_UNROLL = 16


def _build_slot_table(dest_ref, cnt_ref, pst_ref, slot_ref, n_tok):
    def pads(e, carry):
        base = pst_ref[e] + cnt_ref[e]

        def zeros(i, c2):
            for u in range(SLOT_UNROLL):
                slot_ref[base + i * SLOT_UNROLL + u] = 0
            return c2
        lax.fori_loop(0, SLOT_BLOCK // SLOT_UNROLL, zeros, 0)
        return carry
    lax.fori_loop(0, N_EXPERTS, pads, 0)

    per_trip = SLOT_UNROLL // TOP_K
    assert n_tok % per_trip == 0

    def toks(i, carry):
        for u in range(per_trip):
            t = i * per_trip + u
            for j in range(TOP_K):
                slot_ref[dest_ref[t * TOP_K + j]] = t
        return carry
    lax.fori_loop(0, n_tok // per_trip, toks, 0)


def _expert_body(gexp_ref, gsub0_ref, gns_ref, dest_ref, cnt_ref, pst_ref,
                 hp_hbm, w1g_ref, w1l_ref, b1g_ref, b1l_ref, w2_ref, b2_ref,
                 outs_hbm,
                 stage, acc, slot_ref, sem_in, sem_out, *, n_chunks, n_groups, n_tok):
    g = pl.program_id(0)
    c = pl.program_id(1)
    ns = gns_ref[g]
    s0 = gsub0_ref[g]
    sub = SLOT_BLOCK
    half = w1g_ref.shape[0] // 2
    last = n_chunks - 1
    cur = g % 2
    pieces = half // LANES

    def row_copy(tok, buf, r):
        return pltpu.make_async_copy(hp_hbm.at[pl.ds(pl.multiple_of(tok * pieces, pieces), pieces), :],
                                     stage.at[buf, pl.ds(pl.multiple_of(r * pieces, pieces), pieces), :],
                                     sem_in.at[buf])

    def issue_rows(gi, buf, blk0, n_blk):
        base = gsub0_ref[gi] * sub

        def trip(i, carry):
            for u in range(ROW_UNROLL):
                r = blk0 * sub + i * ROW_UNROLL + u
                row_copy(slot_ref[base + r], buf, r).start()
            return carry
        lax.fori_loop(0, n_blk * (sub // ROW_UNROLL), trip, 0)

    nxt = jnp.minimum(g + 1, n_groups - 1)
    need = jnp.where(jnp.logical_and(g + 1 < n_groups, gns_ref[nxt] > 0), gns_ref[nxt], 0)
    nxt_base = gsub0_ref[nxt] * sub

    def issue_block(blk):
        for u in range(sub):
            r = blk * sub + u
            row_copy(slot_ref[nxt_base + r], 1 - cur, r).start()

    def out_copy(r0, rows):
        row0 = s0 * sub + r0
        return pltpu.make_async_copy(acc.at[pl.ds(r0, rows), :], outs_hbm.at[pl.ds(row0, rows), :], sem_out)

    def compute(r0, rows, issue_blk=None):
        los, his = [], []
        for j in range(pieces):
            w = stage[cur, pl.ds(r0 * pieces + j, rows, stride=pieces), :]
            los.append(lax.bitcast_convert_type(w << 16, F32).astype(BF16))
            his.append(lax.bitcast_convert_type(w & jnp.uint32(0xFFFF0000), F32).astype(BF16))
        xl = jnp.concatenate(los, axis=1)
        xh = jnp.concatenate(his, axis=1)
        hg = (jnp.dot(xl, w1g_ref[:half, :].astype(BF16), preferred_element_type=F32)
              + jnp.dot(xh, w1g_ref[half:, :].astype(BF16), preferred_element_type=F32) + b1g_ref[...])
        hl = (jnp.dot(xl, w1l_ref[:half, :].astype(BF16), preferred_element_type=F32)
              + jnp.dot(xh, w1l_ref[half:, :].astype(BF16), preferred_element_type=F32) + b1l_ref[...])
        gate = jnp.minimum(hg, SWIGLU_LIMIT)
        lin = jnp.clip(hl, -SWIGLU_LIMIT, SWIGLU_LIMIT)
        act = gate * jax.nn.sigmoid(SWIGLU_ALPHA * gate) * (lin + 1.0)
        acc[pl.ds(r0, rows), :] += jnp.dot(act.astype(BF16), w2_ref[...].astype(BF16),
                                           preferred_element_type=F32)
        if issue_blk is not None:
            issue_block(issue_blk)

        @pl.when(c == last)
        def _():
            out_copy(r0, rows).start()

    @pl.when(ns > 0)
    def _():
        @pl.when(c == 0)
        def _load():
            @pl.when(g == 0)
            def _():
                _build_slot_table(dest_ref, cnt_ref, pst_ref, slot_ref, n_tok)
                issue_rows(0, 0, 0, ns)

            def wt(i, carry):
                for u in range(ROW_UNROLL):
                    row_copy(0, cur, i * ROW_UNROLL + u).wait()
                return carry
            lax.fori_loop(0, ns * (sub // ROW_UNROLL), wt, 0)

            def init(s, carry):
                acc[pl.ds(pl.multiple_of(s * sub, sub), sub), :] = jnp.broadcast_to(
                    b2_ref[...], (sub, acc.shape[1]))
                return carry
            lax.fori_loop(0, ns, init, 0)

        full = TRIP_SUBS * sub
        n_full = ns // TRIP_SUBS

        def for_each_trip(fn, fn_full=None):
            def trip(i, carry):
                fn(pl.multiple_of(i * full, full), full)
                return carry
            if fn_full is None:
                lax.fori_loop(0, n_full, trip, 0)
            else:
                fn_full()
            r0 = n_full * full
            width = full // 2
            while width >= sub:
                has = (ns * sub) & width

                @pl.when(has != 0)
                def _(r0=r0, width=width):
                    fn(pl.multiple_of(r0, width), width)
                r0 = r0 + has
                width //= 2

        def full_trips():
            n_issue = jnp.clip(need - c * n_full, 0, n_full)

            def with_issue(i, carry):
                compute(pl.multiple_of(i * full, full), full, issue_blk=c * n_full + i)
                return carry

            def plain(i, carry):
                compute(pl.multiple_of(i * full, full), full)
                return carry
            lax.fori_loop(0, n_issue, with_issue, 0)
            lax.fori_loop(n_issue, n_full, plain, 0)

        for_each_trip(compute, full_trips)

        @pl.when(c == last)
        def _drain():
            done = jnp.minimum(need, n_chunks * n_full)
            issue_rows(nxt, 1 - cur, done, need - done)
            for_each_trip(lambda r0, rows: out_copy(r0, rows).wait())

            @pl.when(jnp.logical_or(g + 1 >= n_groups, gns_ref[nxt] == 0))
            def _tail():
                acc[pl.ds(0, sub), :] = jnp.zeros((sub, acc.shape[1]), F32)
                first = s0 + ns
                n_tail = outs_hbm.shape[0] // sub - first

                def tail_copy(i):
                    row0 = pl.multiple_of((first + i) * sub, sub)
                    return pltpu.make_async_copy(acc.at[pl.ds(0, sub), :], outs_hbm.at[pl.ds(row0, sub), :],
                                                 sem_out)

                def st(i, carry):
                    tail_copy(i).start()
                    return carry

                def wt(i, carry):
                    tail_copy(i).wait()
                    return carry
                lax.fori_loop(0, n_tail, st, 0)
                lax.fori_loop(0, n_tail, wt, 0)


def _expert_call(gexp, gsub0, gns, dest_flat, counts, pstart, n_slots, hp, w1, b1, w2, b2):
    half = w1.shape[1] // 2
    n_exp, d, ff2 = w1.shape
    ff = ff2 // 2
    tc = FF_CHUNK
    n_chunks = ff // tc
    n_groups = gexp.shape[0]
    rows = GROUP_SUBS * SLOT_BLOCK

    def chunk(c, gn, g):
        return jnp.where(gn[g] > 0, c, n_chunks - 1)

    grid_spec = pltpu.PrefetchScalarGridSpec(
        num_scalar_prefetch=6,
        grid=(n_groups, n_chunks),
        in_specs=[
            pl.BlockSpec(memory_space=pl.ANY),
            pl.BlockSpec((None, d, tc), lambda g, c, ge, gs, gn, *_: (ge[g], 0, chunk(c, gn, g))),
            pl.BlockSpec((None, d, tc), lambda g, c, ge, gs, gn, *_: (ge[g], 0, n_chunks + chunk(c, gn, g))),
            pl.BlockSpec((None, 1, tc), lambda g, c, ge, gs, gn, *_: (ge[g], 0, chunk(c, gn, g))),
            pl.BlockSpec((None, 1, tc), lambda g, c, ge, gs, gn, *_: (ge[g], 0, n_chunks + chunk(c, gn, g))),
            pl.BlockSpec((None, tc, d), lambda g, c, ge, gs, gn, *_: (ge[g], chunk(c, gn, g), 0)),
            pl.BlockSpec((None, 1, d), lambda g, c, ge, gs, gn, *_: (ge[g], 0, 0)),
        ],
        out_specs=pl.BlockSpec(memory_space=pl.ANY),
        scratch_shapes=[
            pltpu.VMEM((2, rows * (half // LANES), LANES), U32),
            pltpu.VMEM((rows, d), F32),
            pltpu.SMEM((n_slots + SLOT_BLOCK,), I32),
            pltpu.SemaphoreType.DMA((2,)), pltpu.SemaphoreType.DMA(()),
        ],
    )
    return pl.pallas_call(
        functools.partial(_expert_body, n_chunks=n_chunks, n_groups=n_groups,
                          n_tok=dest_flat.shape[0] // TOP_K),
        grid_spec=grid_spec,
        out_shape=jax.ShapeDtypeStruct((n_slots, d), F32),
        compiler_params=_cparams(("arbitrary", "arbitrary")),
        name="expert",
    )(gexp, gsub0, gns, dest_flat, counts, pstart, hp, w1, w1, b1.reshape(n_exp, 1, ff2),
      b1.reshape(n_exp, 1, ff2), w2, b2.reshape(n_exp, 1, d))


def _final_body(dest_ref, x1_ref, w4_ref, p_ref, wple_ref, wpg_ref, bpg_ref, npl_ref, outs_hbm,
                o_ref, gbuf, sems, *, tm, n_steps):
    i = pl.program_id(0)
    cur = i % 2

    def row_copy(slot, buf, j, q, u):
        return pltpu.make_async_copy(outs_hbm.at[pl.ds(slot, 1), :], gbuf.at[buf, j, q, pl.ds(u, 1), :],
                                     sems.at[buf])

    def issue(step, buf):
        def rows(q, carry):
            for u in range(SUBLANES):
                t = step * tm + q * SUBLANES + u
                for j in range(TOP_K):
                    row_copy(dest_ref[t * TOP_K + j], buf, j, q, u).start(priority=j % 2)
            return carry
        lax.fori_loop(0, tm // SUBLANES, rows, 0)

    @pl.when(i == 0)
    def _():
        issue(0, 0)

    pe = jnp.dot(p_ref[...].astype(BF16), wple_ref[...], preferred_element_type=F32)

    def wt(q, carry):
        for u in range(SUBLANES):
            for j in range(TOP_K):
                row_copy(0, cur, j, q, u).wait()
        return carry
    lax.fori_loop(0, tm // SUBLANES, wt, 0)

    w4 = w4_ref[...]
    d = o_ref.shape[1]
    y = gbuf[cur, 0].reshape(tm, d) * w4[:, 0:1]
    for j in range(1, TOP_K):
        y = y + gbuf[cur, j].reshape(tm, d) * w4[:, j:j + 1]
    x2 = x1_ref[...] + y
    hn = _rmsnorm_rows(x2, npl_ref[...])
    gate = jax.nn.sigmoid(jnp.dot(hn.astype(BF16), wpg_ref[...], preferred_element_type=F32) + bpg_ref[...])
    o_ref[...] = x2 + gate * pe

    nxt = jnp.minimum(i + 1, n_steps - 1)
    for q in range(tm // SUBLANES):
        for u in range(SUBLANES):
            t = nxt * tm + q * SUBLANES + u
            for j in range(TOP_K):
                row_copy(dest_ref[t * TOP_K + j], 1 - cur, j, q, u).start(priority=j % 2)

    @pl.when(i == n_steps - 1)
    def _():
        def wt_last(q, carry):
            for u in range(SUBLANES):
                for j in range(TOP_K):
                    row_copy(0, 1 - cur, j, q, u).wait()
            return carry
        lax.fori_loop(0, tm // SUBLANES, wt_last, 0)


def _final_call(dest_flat, x1, w4, p2d, wple, wpg, bpg_row, npl_row, outs, tm=256):
    n, d = x1.shape
    tm = min(tm, n)
    pd = p2d.shape[1]
    res = lambda shape: pl.BlockSpec(shape, lambda i, dst: (0,) * len(shape), pipeline_mode=pl.Buffered(1))
    grid_spec = pltpu.PrefetchScalarGridSpec(
        num_scalar_prefetch=1,
        grid=(n // tm,),
        in_specs=[pl.BlockSpec((tm, d), lambda i, dst: (i, 0)),
                  pl.BlockSpec((tm, LANES), lambda i, dst: (i, 0)),
                  pl.BlockSpec((tm, pd), lambda i, dst: (i, 0)),
                  res((pd, d)), res((d, d)), res((1, d)), res((1, d)),
                  pl.BlockSpec(memory_space=pl.ANY)],
        out_specs=pl.BlockSpec((tm, d), lambda i, dst: (i, 0)),
        scratch_shapes=[pltpu.VMEM((2, TOP_K, tm // SUBLANES, SUBLANES, d), F32),
                        pltpu.SemaphoreType.DMA((2,))],
    )
    return pl.pallas_call(
        functools.partial(_final_body, tm=tm, n_steps=n // tm),
        grid_spec=grid_spec,
        out_shape=jax.ShapeDtypeStruct((n, d), F32),
        compiler_params=_cparams(("arbitrary",)),
        name="final",
    )(dest_flat, x1, w4, p2d, wple, wpg, bpg_row, npl_row, outs)


def _t5_bucket(rel):
    n = -rel
    nb = REL_BUCKETS // 2
    ret = jnp.where(n < 0, nb, 0)
    n = jnp.abs(n)
    max_exact = nb // 2
    large = max_exact + (jnp.log(jnp.maximum(n, 1).astype(jnp.float32) / max_exact)
                         / math.log(REL_MAX_DIST / max_exact) * (nb - max_exact)).astype(jnp.int32)
    large = jnp.minimum(large, nb - 1)
    return ret + jnp.where(n < max_exact, n, large)


def _bias_body(rv_ref, o_ref, *, t):
    x = jnp.broadcast_to(rv_ref[...], (t, rv_ref.shape[1]))
    o_ref[...] = pltpu.roll(x, 1, 1, stride=1, stride_axis=0)[:, t:]


def _diff_bias_table(rel_bias, seq, t):
    rel = jnp.arange(seq + t) - (seq - 1)
    rv = (jnp.transpose(rel_bias[_t5_bucket(rel)], (1, 0)).astype(F32) * LOG2E)[:, None, :]
    return pl.pallas_call(
        functools.partial(_bias_body, t=t),
        grid=(N_HEADS,),
        in_specs=[pl.BlockSpec((None, 1, seq + t), lambda h: (h, 0, 0))],
        out_specs=pl.BlockSpec((None, t, seq), lambda h: (h, 0, 0)),
        out_shape=jax.ShapeDtypeStruct((N_HEADS, t, seq), F32),
        compiler_params=_cparams(("parallel",)),
        name="bias_table",
    )(rv)


def _pad_cols(a, width):
    return jnp.pad(a, ((0, 0), (0, width - a.shape[1])))


def _group_table(counts):
    n_sub = (counts + SLOT_BLOCK - 1) // SLOT_BLOCK
    sub_start = jnp.cumsum(n_sub) - n_sub
    n_grp = (n_sub + GROUP_SUBS - 1) // GROUP_SUBS
    grp_end = jnp.cumsum(n_grp)
    total = grp_end[-1]
    return n_sub, sub_start, n_grp, grp_end, total


def _layer(x, p_l, w_in, b_gate, b_forget, dq_norm, dk_norm, fq_norm, fk_norm, lq1, lk1, lq2, lk2,
           lambda_init, subln, w_up_a, w_up_b, w_out, rel_bias, norm_mix, norm_moe, w_router,
           b_router, w1, b1, w2, b2, norm_ple, w_ple_gate, b_ple_gate, w_ple):
    b, s, d = x.shape
    n = b * s
    width = N_HEADS * HEAD_DIM
    t = min(ATTN_T, s)
    x2d = x.reshape(n, d)
    row = lambda v: v.reshape(1, -1).astype(F32)

    wt_in = jnp.swapaxes(w_in, 0, 1)
    diff_scale = DIFF_QK_DIM ** -0.5 * LOG2E
    fox_scale = HEAD_DIM ** -0.5 * LOG2E
    aux = jnp.concatenate([jnp.tile(dq_norm * diff_scale, 2 * N_HEADS), jnp.tile(dk_norm, 2 * N_HEADS),
                           jnp.zeros((width,), F32),
                           jnp.tile(fq_norm * fox_scale, N_HEADS), jnp.tile(fk_norm, N_HEADS),
                           jnp.zeros((width,), F32), b_gate.reshape(-1)])
    proj, f_logit = _inproj_call(x2d, row(norm_mix), wt_in, row(aux))
    gates = proj
    qkv = proj.reshape(b, s, -1)
    first = INPROJ_GATE_BLOCKS * INPROJ_TN // width

    cum = _cum_call(f_logit, _pad_cols(row(b_forget), LANES), b, s)
    cum_bhs = jnp.transpose(cum[:, :N_HEADS].reshape(b, s, N_HEADS), (0, 2, 1))
    cum_row = cum_bhs[:, :, None, :]

    lam_rows = [row(v) for v in (lq1, lk1, lq2, lk2)]
    bias = _diff_bias_table(rel_bias, s, t)
    od = _diff_attn_call(qkv, (first, first + 1, first + 2), bias, lam_rows, row(subln), lambda_init, t)
    of = _fox_attn_call(qkv, (first + 3, first + 4, first + 5), cum.reshape(b, s, LANES), cum_row, t)

    br_pad = jnp.full((1, LANES), NEG, F32).at[0, :N_EXPERTS].set(b_router.astype(F32))
    wr_pad = jnp.pad(jnp.swapaxes(w_router.astype(F32), 0, 1), ((0, LANES - N_EXPERTS), (0, 0)))
    wr_hi = wr_pad.astype(BF16)
    wr_lo = (wr_pad - wr_hi.astype(F32)).astype(BF16)
    x1, hp, logits = _mix_call(od.reshape(n, width), of.reshape(n, width), gates, x2d,
                               w_up_a.astype(BF16), w_up_b.astype(BF16), w_out.astype(BF16),
                               row(norm_moe), wr_hi, wr_lo, br_pad)

    dest128, w4, cnt, ps_row = _route_call(logits)
    counts = cnt[0, :N_EXPERTS].astype(I32)
    pstart = ps_row[0, :N_EXPERTS].astype(I32)
    dest_flat = dest128[:, :TOP_K].reshape(-1)

    n_slots = n * TOP_K + N_EXPERTS * SLOT_BLOCK

    n_sub, sub_start, n_grp, grp_end, total = _group_table(counts)
    max_groups = N_EXPERTS + (n_slots // SLOT_BLOCK) // GROUP_SUBS
    gidx = jnp.arange(max_groups, dtype=I32)
    gvalid = gidx < total
    gsafe = jnp.minimum(gidx, total - 1)
    gexp = jnp.searchsorted(grp_end, gsafe, side="right").astype(I32)
    kth = gsafe - (grp_end - n_grp)[gexp]
    gsub0 = (sub_start[gexp] + kth * GROUP_SUBS).astype(I32)
    gns = jnp.where(gvalid, jnp.minimum(GROUP_SUBS, n_sub[gexp] - kth * GROUP_SUBS), 0).astype(I32)
    def run_experts(ng):
        return _expert_call(gexp[:ng], gsub0[:ng], gns[:ng], dest_flat, counts, pstart, n_slots, hp,
                            w1, b1, w2, b2)

    few = min(N_EXPERTS + 2, max_groups)
    outs = lax.cond(total <= few, lambda: run_experts(few), lambda: run_experts(max_groups))

    out = _final_call(dest_flat, x1, w4, p_l.reshape(n, -1), w_ple.astype(BF16),
                      w_ple_gate.astype(BF16), row(b_ple_gate), row(norm_ple), outs)
    return out.reshape(b, s, d)


def kernel(x, p, w_in, b_gate, b_forget, dq_norm, dk_norm, fq_norm, fk_norm, lambda_q1, lambda_k1,
           lambda_q2, lambda_k2, subln, w_up_a, w_up_b, w_out, rel_bias, norm_mix, norm_moe,
           w_router, b_router, w1, b1, w2, b2, norm_ple, w_ple_gate, b_ple_gate, w_ple):
    for i in range(w_in.shape[0]):
        lambda_init = 0.8 - 0.6 * math.exp(-0.3 * i)
        x = _layer(x, p[i], w_in[i], b_gate[i], b_forget[i], dq_norm[i], dk_norm[i], fq_norm[i],
                   fk_norm[i], lambda_q1[i], lambda_k1[i], lambda_q2[i], lambda_k2[i], lambda_init,
                   subln[i], w_up_a[i], w_up_b[i], w_out[i], rel_bias, norm_mix[i], norm_moe[i],
                   w_router[i], b_router[i], w1[i], b1[i], w2[i], b2[i], norm_ple[i],
                   w_ple_gate[i], b_ple_gate[i], w_ple[i])
    return x
```

```python
import functools
import math

import jax
import jax.numpy as jnp
from jax import lax
from jax.experimental import pallas as pl
from jax.experimental.pallas import tpu as pltpu

F32 = jnp.float32
BF16 = jnp.bfloat16
U32 = jnp.uint32
I32 = jnp.int32
HIGHEST = lax.Precision.HIGHEST

N_HEADS = 8
HEAD_DIM = 128
DIFF_QK_DIM = 64
CHUNK = 64
REL_BUCKETS = 32
REL_MAX_DIST = 128
N_EXPERTS = 32
TOP_K = 4
SWIGLU_LIMIT = 7.0
SWIGLU_ALPHA = 1.702
RMS_EPS = 1e-6
NEG = -1e30
LOG2E = math.log2(math.e)

LANES = 128
SUBLANES = 8
SLOT_BLOCK = 128
GROUP_SUBS = 12
TRIP_SUBS = 4
FF_CHUNK = 512
ATTN_T = 256
VMEM_LIMIT = 56 * 1024 * 1024


_NT = (((1,), (1,)), ((), ()))


def _cparams(sem):
    return pltpu.CompilerParams(dimension_semantics=sem, vmem_limit_bytes=VMEM_LIMIT)


def _rmsnorm_rows(x, g):
    ms = jnp.mean(x * x, axis=-1, keepdims=True)
    return x * lax.rsqrt(ms + RMS_EPS) * g


def _cum_body(fl_ref, bf_ref, cum_ref):
    s = fl_ref.shape[0]
    r = lax.broadcasted_iota(I32, (LANES, LANES), 0)
    c = lax.broadcasted_iota(I32, (LANES, LANES), 1)
    tri = jnp.where(r >= c, 1.0, 0.0).astype(F32)
    carry = jnp.zeros((1, LANES), F32)
    for blk in range(s // LANES):
        rows = slice(blk * LANES, (blk + 1) * LANES)
        lf = jax.nn.log_sigmoid(fl_ref[rows, :] + bf_ref[...])
        cb = jnp.dot(tri, lf, precision=HIGHEST, preferred_element_type=F32) + carry
        cum_ref[rows, :] = cb * LOG2E
        carry = cb[LANES - 1:LANES, :]


def _cum_call(fl, bf_row, batch, seq):
    return pl.pallas_call(
        _cum_body,
        grid=(batch,),
        in_specs=[pl.BlockSpec((seq, LANES), lambda b: (b, 0)),
                  pl.BlockSpec((1, LANES), lambda b: (0, 0))],
        out_specs=pl.BlockSpec((seq, LANES), lambda b: (b, 0)),
        out_shape=jax.ShapeDtypeStruct(fl.shape, F32),
        compiler_params=_cparams(("parallel",)),
        name="cum",
    )(fl, bf_row)


def _proj_epilogue(acc, aux_ref, o_ref, kind):
    tn = acc.shape[1]
    if kind == "plain":
        o_ref[...] = acc.astype(o_ref.dtype)
    elif kind == "gate":
        o_ref[...] = jax.nn.sigmoid(acc + aux_ref[...]).astype(o_ref.dtype)
    else:
        lo = lax.broadcasted_iota(I32, (1, LANES), 1) < DIFF_QK_DIM
        for s in range(tn // LANES):
            cols = slice(s * LANES, (s + 1) * LANES)
            blk = acc[:, cols]
            sq = blk * blk
            if kind == "norm128":
                ms = jnp.mean(sq, axis=-1, keepdims=True)
            else:
                s_lo = jnp.sum(jnp.where(lo, sq, 0.0), axis=-1, keepdims=True)
                s_hi = jnp.sum(jnp.where(lo, 0.0, sq), axis=-1, keepdims=True)
                ms = jnp.where(lo, s_lo, s_hi) * (1.0 / DIFF_QK_DIM)
            o_ref[:, cols] = (blk * lax.rsqrt(ms + RMS_EPS) * aux_ref[:, cols]).astype(o_ref.dtype)


INPROJ_TN = 1024
_INPROJ_PARTS = ((0, 2, "norm64", 0), (2, 1, "plain", 0), (3, 2, "norm128", 0), (5, 1, "plain", 0),
                 (6, 4, "gate", N_HEADS))
INPROJ_GATE_BLOCKS = 4


def _inproj_body(x_ref, g_ref, wa_ref, wb_ref, wf_ref, aux_ref, o_ref, fl_ref, h_sc):
    j = pl.program_id(1)

    @pl.when(j == 0)
    def _norm():
        y = _rmsnorm_rows(x_ref[...], g_ref[...])
        y_hi = y.astype(BF16)
        h_sc[...] = y_hi
        rowi = lax.broadcasted_iota(I32, (LANES, 1), 0)
        wf = jnp.where(rowi < N_HEADS, wf_ref[...], 0.0)
        y_lo = (y - y_hi.astype(F32)).astype(BF16)
        wf_hi = wf.astype(BF16)
        wf_lo = (wf - wf_hi.astype(F32)).astype(BF16)
        fl_ref[...] = (lax.dot_general(y_hi, wf_hi, _NT, preferred_element_type=F32)
                       + lax.dot_general(y_lo, wf_hi, _NT, preferred_element_type=F32)
                       + lax.dot_general(y_hi, wf_lo, _NT, preferred_element_type=F32))

    for first, count, kind, shift in _INPROJ_PARTS:
        @pl.when(jnp.logical_and(j >= first, j < first + count))
        def _(kind=kind, shift=shift):
            if shift == 0:
                wt = wa_ref[...]
            else:
                wt = jnp.concatenate([wa_ref[shift:, :], wb_ref[...]], axis=0)
            acc = lax.dot_general(h_sc[...], wt.astype(BF16), _NT, preferred_element_type=F32)
            _proj_epilogue(acc, aux_ref, o_ref, kind)


def _inproj_call(x2d, g_row, wt_in, aux_row, tm=1024):
    n, d = x2d.shape
    tm = min(tm, n)
    tn = INPROJ_TN
    nblk = sum(p[1] for p in _INPROJ_PARTS)
    first_qkv = _INPROJ_PARTS[-1][0]
    assert _INPROJ_PARTS[-1][3] == SUBLANES and first_qkv * tn % LANES == 0

    def out_block(j):
        return jnp.where(j >= first_qkv, j - first_qkv, j + INPROJ_GATE_BLOCKS)

    return pl.pallas_call(
        _inproj_body,
        grid=(n // tm, nblk),
        in_specs=[pl.BlockSpec((tm, d), lambda i, j: (i, 0)),
                  pl.BlockSpec((1, d), lambda i, j: (0, 0)),
                  pl.BlockSpec((tn, d), lambda i, j: (j, 0)),
                  pl.BlockSpec((SUBLANES, d), lambda i, j: ((j + 1) * (tn // SUBLANES), 0)),
                  pl.BlockSpec((LANES, d), lambda i, j: (first_qkv * tn // LANES, 0)),
                  pl.BlockSpec((1, tn), lambda i, j: (0, j))],
        out_specs=[pl.BlockSpec((tm, tn), lambda i, j: (i, out_block(j))),
                   pl.BlockSpec((tm, LANES), lambda i, j: (i, 0))],
        out_shape=[jax.ShapeDtypeStruct((n, nblk * tn), BF16), jax.ShapeDtypeStruct((n, LANES), F32)],
        scratch_shapes=[pltpu.VMEM((tm, d), BF16)],
        compiler_params=_cparams(("parallel", "arbitrary")),
        name="inproj",
    )(x2d, g_row, wt_in, wt_in, wt_in, aux_row)


DIFF_HEADS_PER_STEP = 8
FOX_HEADS_PER_STEP = 8


def _softmax_step(s, m_ref, acc_ref, v1):
    m_old = m_ref[...]
    mn = jnp.maximum(m_old, jnp.max(s, axis=-1, keepdims=True))
    alpha = jnp.exp2(m_old - mn)
    p = jnp.exp2(s - jnp.tile(mn, (1, s.shape[1] // LANES))).astype(BF16)
    acc_ref[...] = (jnp.tile(alpha, (1, acc_ref.shape[1] // LANES)) * acc_ref[...]
                    + jnp.dot(p, v1, preferred_element_type=F32))
    m_ref[...] = mn


def _causal_sweep(step, qi, t):
    def pair(i, carry):
        step(2 * i, 2 * t, None)
        return carry
    lax.fori_loop(0, qi // 2, pair, 0)

    @pl.when(qi % 2 == 1)
    def _():
        step(qi - 1, t, None)
    step(qi, t, 0)


def _with_ones(v):
    return jnp.concatenate([v, jnp.ones_like(v)], axis=1)


def _normalised(acc):
    return acc[:, :HEAD_DIM] / acc[:, HEAD_DIM:]


def _diff_attn_body(lq1_ref, lk1_ref, lq2_ref, lk2_ref, g_ref, q_ref, k_ref, v_ref, bias_ref,
                    o_ref, m_sc, acc_sc, *, t, nh, lambda_init):
    qi = pl.program_id(2)
    s_len = k_ref.shape[0]
    m_sc[...] = jnp.full(m_sc.shape, NEG, F32)
    acc_sc[...] = jnp.zeros(acc_sc.shape, F32)
    lo = lax.broadcasted_iota(I32, (t, HEAD_DIM), 1) < DIFF_QK_DIM

    def step(kb, w, rel):
        k0 = pl.multiple_of(kb * t, t)
        off = pl.multiple_of((kb - qi) * t + (s_len - t), LANES)
        masked = rel is not None
        if masked:
            r = lax.broadcasted_iota(I32, (t, w), 0)
            c = lax.broadcasted_iota(I32, (t, w), 1) + rel
            chunk_of = lambda pos: lax.shift_right_arithmetic(pos, jnp.int32(CHUNK.bit_length() - 1))
            ok = chunk_of(c) <= chunk_of(r)
        for h in range(nh):
            cols = slice(h * HEAD_DIM, (h + 1) * HEAD_DIM)
            q = q_ref[:, cols]
            zero = jnp.zeros_like(q)
            k = k_ref[pl.ds(k0, w), cols]
            v1 = _with_ones(v_ref[pl.ds(k0, w), cols])
            b = bias_ref[h, :, pl.ds(off, w)]
            for mp, qm in enumerate((jnp.where(lo, q, zero), jnp.where(lo, zero, q))):
                s = lax.dot_general(qm, k, _NT, preferred_element_type=F32) + b
                if masked:
                    s = jnp.where(ok, s, NEG)
                _softmax_step(s, m_sc.at[2 * h + mp], acc_sc.at[2 * h + mp], v1)

    _causal_sweep(step, qi, t)

    lam = (jnp.exp(jnp.sum(lq1_ref[...] * lk1_ref[...], axis=-1, keepdims=True))
           - jnp.exp(jnp.sum(lq2_ref[...] * lk2_ref[...], axis=-1, keepdims=True)) + lambda_init)
    for h in range(nh):
        o = _normalised(acc_sc[2 * h]) - lam * _normalised(acc_sc[2 * h + 1])
        o = _rmsnorm_rows(o, g_ref[...]) * (1.0 - lambda_init)
        o_ref[:, h * HEAD_DIM:(h + 1) * HEAD_DIM] = o.astype(o_ref.dtype)


def _diff_attn_call(qkv, blocks, bias, lam_rows, subln_row, lambda_init, t):
    b, s, _ = qkv.shape
    nh = DIFF_HEADS_PER_STEP
    wb = nh * HEAD_DIM
    n_hg = N_HEADS // nh
    qb, kb_, vb = (blk * n_hg for blk in blocks)
    vec = lambda w: pl.BlockSpec((1, w), lambda hg, bb, qi: (0, 0))
    return pl.pallas_call(
        functools.partial(_diff_attn_body, t=t, nh=nh, lambda_init=lambda_init),
        grid=(n_hg, b, s // t),
        in_specs=[vec(DIFF_QK_DIM)] * 4 + [
            vec(HEAD_DIM),
            pl.BlockSpec((None, t, wb), lambda hg, bb, qi: (bb, qi, qb + hg)),
            pl.BlockSpec((None, s, wb), lambda hg, bb, qi: (bb, 0, kb_ + hg)),
            pl.BlockSpec((None, s, wb), lambda hg, bb, qi: (bb, 0, vb + hg)),
            pl.BlockSpec((nh, t, s), lambda hg, bb, qi: (hg, 0, 0),
                         pipeline_mode=pl.Buffered(1 if n_hg == 1 else 2))],
        out_specs=pl.BlockSpec((None, t, wb), lambda hg, bb, qi: (bb, qi, hg)),
        out_shape=jax.ShapeDtypeStruct((b, s, N_HEADS * HEAD_DIM), BF16),
        scratch_shapes=[pltpu.VMEM((2 * nh, t, LANES), F32), pltpu.VMEM((2 * nh, t, 2 * HEAD_DIM), F32)],
        compiler_params=_cparams(("parallel", "parallel", "parallel")),
        name="diff_attn",
    )(*lam_rows, subln_row, qkv, qkv, qkv, bias)


def _fox_attn_body(q_ref, k_ref, v_ref, cq_ref, ck_ref, o_ref, m_sc, acc_sc, cq_sc, *, t, nh):
    qi = pl.program_id(2)
    m_sc[...] = jnp.full(m_sc.shape, NEG, F32)
    acc_sc[...] = jnp.zeros(acc_sc.shape, F32)
    hg = pl.program_id(0)
    lane = lax.broadcasted_iota(I32, (t, LANES), 1)
    cq_all = cq_ref[...]
    for h in range(nh):
        col = jnp.sum(jnp.where(lane == hg * nh + h, cq_all, 0.0), axis=-1, keepdims=True)
        cq_sc[h] = jnp.broadcast_to(col, (t, LANES))

    def step(kb, w, rel):
        k0 = pl.multiple_of(kb * t, t)
        masked = rel is not None
        if masked:
            r = lax.broadcasted_iota(I32, (t, w), 0)
            c = lax.broadcasted_iota(I32, (t, w), 1) + rel
            ok = c <= r
        for h in range(nh):
            cols = slice(h * HEAD_DIM, (h + 1) * HEAD_DIM)
            k = k_ref[pl.ds(k0, w), cols]
            v1 = _with_ones(v_ref[pl.ds(k0, w), cols])
            s = lax.dot_general(q_ref[:, cols], k, _NT, preferred_element_type=F32)
            s = s + jnp.tile(cq_sc[h], (1, w // LANES)) - ck_ref[h, :, pl.ds(k0, w)]
            if masked:
                s = jnp.where(ok, s, NEG)
            _softmax_step(s, m_sc.at[h], acc_sc.at[h], v1)

    _causal_sweep(step, qi, t)
    for h in range(nh):
        o_ref[:, h * HEAD_DIM:(h + 1) * HEAD_DIM] = _normalised(acc_sc[h]).astype(o_ref.dtype)


def _fox_attn_call(qkv, blocks, cum, cum_row, t):
    b, s, _ = qkv.shape
    nh = FOX_HEADS_PER_STEP
    wb = nh * HEAD_DIM
    n_hg = N_HEADS // nh
    qb, kb_, vb = (blk * n_hg for blk in blocks)
    return pl.pallas_call(
        functools.partial(_fox_attn_body, t=t, nh=nh),
        grid=(n_hg, b, s // t),
        in_specs=[pl.BlockSpec((None, t, wb), lambda hg, bb, qi: (bb, qi, qb + hg)),
                  pl.BlockSpec((None, s, wb), lambda hg, bb, qi: (bb, 0, kb_ + hg)),
                  pl.BlockSpec((None, s, wb), lambda hg, bb, qi: (bb, 0, vb + hg)),
                  pl.BlockSpec((None, t, LANES), lambda hg, bb, qi: (bb, qi, 0)),
                  pl.BlockSpec((None, nh, 1, s), lambda hg, bb, qi: (bb, hg, 0, 0))],
        out_specs=pl.BlockSpec((None, t, wb), lambda hg, bb, qi: (bb, qi, hg)),
        out_shape=jax.ShapeDtypeStruct((b, s, N_HEADS * HEAD_DIM), BF16),
        scratch_shapes=[pltpu.VMEM((nh, t, LANES), F32), pltpu.VMEM((nh, t, 2 * HEAD_DIM), F32),
                        pltpu.VMEM((nh, t, LANES), F32)],
        compiler_params=_cparams(("parallel", "parallel", "parallel")),
        name="fox_attn",
    )(qkv, qkv, qkv, cum, cum_row)


def _mix_body(od_ref, of_ref, g_ref, x_ref, wa_ref, wb_ref, wo_ref, nm_ref, wrh_ref, wrl_ref, br_ref,
              x1_ref, hp_ref, lg_ref):
    d = x_ref.shape[1]
    ua = jnp.dot(od_ref[...], wa_ref[...], preferred_element_type=F32)
    ub = jnp.dot(of_ref[...], wb_ref[...], preferred_element_type=F32)
    mixed = g_ref[:, :d].astype(F32) * ua + g_ref[:, d:].astype(F32) * ub
    x1 = x_ref[...] + jnp.dot(mixed.astype(BF16), wo_ref[...], preferred_element_type=F32)
    x1_ref[...] = x1
    hm = _rmsnorm_rows(x1, nm_ref[...])
    h_hi = hm.astype(BF16)
    h_hi32 = h_hi.astype(F32)
    h_lo = (hm - h_hi32).astype(BF16)
    lg_ref[...] = (lax.dot_general(h_hi, wrh_ref[...], _NT, preferred_element_type=F32)
                   + lax.dot_general(h_lo, wrh_ref[...], _NT, preferred_element_type=F32)
                   + lax.dot_general(h_hi, wrl_ref[...], _NT, preferred_element_type=F32) + br_ref[...])
    bits = lax.bitcast_convert_type(h_hi32, U32)
    packed = (bits[:, :d // 2] >> 16) | (bits[:, d // 2:] & jnp.uint32(0xFFFF0000))
    tm = packed.shape[0]
    pieces = packed.shape[1] // LANES
    for j in range(pieces):
        hp_ref[pl.ds(j, tm, stride=pieces), :] = packed[:, j * LANES:(j + 1) * LANES]


def _resident(shape):
    return pl.BlockSpec(shape, lambda i: (0,) * len(shape), pipeline_mode=pl.Buffered(1))


def _mix_call(od, of, gates, x2d, wa, wb, wo, nm_row, wr_hi, wr_lo, br_pad, tm=256):
    n, d = x2d.shape
    tm = min(tm, n)
    wdt = od.shape[1]
    return pl.pallas_call(
        _mix_body,
        grid=(n // tm,),
        in_specs=[pl.BlockSpec((tm, wdt), lambda i: (i, 0)),
                  pl.BlockSpec((tm, wdt), lambda i: (i, 0)),
                  pl.BlockSpec((tm, 2 * d), lambda i: (i, 0)),
                  pl.BlockSpec((tm, d), lambda i: (i, 0)),
                  _resident((wdt, d)), _resident((wdt, d)), _resident((d, d)),
                  _resident((1, d)), _resident((LANES, d)), _resident((LANES, d)), _resident((1, LANES))],
        out_specs=[pl.BlockSpec((tm, d), lambda i: (i, 0)),
                   pl.BlockSpec((tm * (d // 2 // LANES), LANES), lambda i: (i, 0)),
                   pl.BlockSpec((tm, LANES), lambda i: (i, 0))],
        out_shape=[jax.ShapeDtypeStruct((n, d), F32),
                   jax.ShapeDtypeStruct((n * (d // 2 // LANES), LANES), U32),
                   jax.ShapeDtypeStruct((n, LANES), F32)],
        compiler_params=_cparams(("parallel",)),
        name="mix",
    )(od, of, gates, x2d, wa, wb, wo, nm_row, wr_hi, wr_lo, br_pad)


def _route_body(lg_ref, dest_ref, w4_ref, cnt_ref, ps_ref, rank_sc, jm_sc, wm_sc, carry_sc, ps_sc, *, tm):
    phase = pl.program_id(0)
    i = pl.program_id(1)
    rows = pl.ds(pl.multiple_of(i * tm, tm), tm)

    @pl.when(phase == 0)
    def _select():
        @pl.when(i == 0)
        def _():
            carry_sc[...] = jnp.zeros_like(carry_sc)

        vals = lg_ref[...]
        lane = lax.broadcasted_iota(I32, vals.shape, 1).astype(F32)
        jm = jnp.zeros(vals.shape, F32)
        tops = []
        for j in range(TOP_K):
            m = jnp.max(vals, axis=-1, keepdims=True)
            idx = jnp.min(jnp.where(vals == m, lane, float(LANES)), axis=-1, keepdims=True)
            sel = lane == idx
            jm = jnp.where(sel, float(j + 1), jm)
            vals = jnp.where(sel, -jnp.inf, vals)
            tops.append(m)
        es = [jnp.exp(m - tops[0]) for m in tops]
        den = es[0] + es[1] + es[2] + es[3]
        wm = jnp.zeros(vals.shape, F32)
        for j in range(TOP_K):
            wm = jnp.where(jm == float(j + 1), es[j] / den, wm)
        sel_any = jnp.where(jm > 0.0, 1.0, 0.0)
        r = lax.broadcasted_iota(I32, (tm, tm), 0)
        c = lax.broadcasted_iota(I32, (tm, tm), 1)
        tri = jnp.where(c < r, 1.0, 0.0).astype(BF16)
        carry = carry_sc[...]
        rank_sc[rows, :] = jnp.dot(tri, sel_any.astype(BF16), preferred_element_type=F32) + carry
        jm_sc[rows, :] = jm
        wm_sc[rows, :] = wm
        carry_sc[...] = carry + jnp.sum(sel_any, axis=0, keepdims=True)

    @pl.when(phase == 1)
    def _place():
        @pl.when(i == 0)
        def _():
            cnt = carry_sc[...]
            padded = jnp.ceil(cnt * (1.0 / SLOT_BLOCK)) * float(SLOT_BLOCK)
            r = lax.broadcasted_iota(I32, (LANES, LANES), 0)
            c = lax.broadcasted_iota(I32, (LANES, LANES), 1)
            before = jnp.where(r < c, 1.0, 0.0).astype(F32)
            ps_sc[...] = jnp.dot(jnp.broadcast_to(padded, (SUBLANES, LANES)), before, precision=HIGHEST,
                                 preferred_element_type=F32)[:1]
            cnt_ref[...] = cnt
            ps_ref[...] = ps_sc[...]

        slot = rank_sc[rows, :] + ps_sc[...]
        jm = jm_sc[rows, :]
        wm = wm_sc[rows, :]
        lane = lax.broadcasted_iota(I32, jm.shape, 1)
        dest = jnp.zeros(jm.shape, F32)
        w4 = jnp.zeros(jm.shape, F32)
        for j in range(TOP_K):
            sel = jm == float(j + 1)
            dj = jnp.sum(jnp.where(sel, slot, 0.0), axis=-1, keepdims=True)
            wj = jnp.sum(jnp.where(sel, wm, 0.0), axis=-1, keepdims=True)
            dest = jnp.where(lane == j, dj, dest)
            w4 = jnp.where(lane == j, wj, w4)
        dest_ref[...] = dest.astype(I32)
        w4_ref[...] = w4


def _route_call(logits, tm=1024):
    n = logits.shape[0]
    tm = min(tm, n)
    nt = n // tm
    row = pl.BlockSpec((1, LANES), lambda p, i: (0, 0))
    out_tile = pl.BlockSpec((tm, LANES), lambda p, i: (p * i, 0))
    return pl.pallas_call(
        functools.partial(_route_body, tm=tm),
        grid=(2, nt),
        in_specs=[pl.BlockSpec((tm, LANES), lambda p, i: (i * (1 - p) + (nt - 1) * p, 0))],
        out_specs=[out_tile, out_tile, row, row],
        out_shape=[jax.ShapeDtypeStruct((n, LANES), I32), jax.ShapeDtypeStruct((n, LANES), F32),
                   jax.ShapeDtypeStruct((1, LANES), F32), jax.ShapeDtypeStruct((1, LANES), F32)],
        scratch_shapes=[pltpu.VMEM((n, LANES), F32)] * 3 + [pltpu.VMEM((1, LANES), F32)] * 2,
        compiler_params=_cparams(("arbitrary", "arbitrary")),
        name="route",
    )(logits)


ROW_UNROLL = 8
SLOT_UNROLL = 16


def _build_slot_table(dest_ref, cnt_ref, pst_ref, slot_ref, n_tok):
    def pads(e, carry):
        base = pst_ref[e] + cnt_ref[e]

        def zeros(i, c2):
            for u in range(SLOT_UNROLL):
                slot_ref[base + i * SLOT_UNROLL + u] = 0
            return c2
        lax.fori_loop(0, SLOT_BLOCK // SLOT_UNROLL, zeros, 0)
        return carry
    lax.fori_loop(0, N_EXPERTS, pads, 0)

    per_trip = SLOT_UNROLL // TOP_K
    assert n_tok % per_trip == 0

    def toks(i, carry):
        for u in range(per_trip):
            t = i * per_trip + u
            for j in range(TOP_K):
                slot_ref[dest_ref[t * TOP_K + j]] = t
        return carry
    lax.fori_loop(0, n_tok // per_trip, toks, 0)


def _expert_body(gexp_ref, gsub0_ref, gns_ref, dest_ref, cnt_ref, pst_ref,
                 hp_hbm, w1g_ref, w1l_ref, b1g_ref, b1l_ref, w2_ref, b2_ref,
                 outs_hbm,
                 stage, acc, slot_ref, sem_in, sem_out, *, n_chunks, n_groups, n_tok):
    g = pl.program_id(0)
    c = pl.program_id(1)
    ns = gns_ref[g]
    s0 = gsub0_ref[g]
    sub = SLOT_BLOCK
    half = w1g_ref.shape[0] // 2
    last = n_chunks - 1
    cur = g % 2
    pieces = half // LANES

    def row_copy(tok, buf, r):
        return pltpu.make_async_copy(hp_hbm.at[pl.ds(pl.multiple_of(tok * pieces, pieces), pieces), :],
                                     stage.at[buf, pl.ds(pl.multiple_of(r * pieces, pieces), pieces), :],
                                     sem_in.at[buf])

    def issue_rows(gi, buf, blk0, n_blk):
        base = gsub0_ref[gi] * sub

        def trip(i, carry):
            for u in range(ROW_UNROLL):
                r = blk0 * sub + i * ROW_UNROLL + u
                row_copy(slot_ref[base + r], buf, r).start()
            return carry
        lax.fori_loop(0, n_blk * (sub // ROW_UNROLL), trip, 0)

    nxt = jnp.minimum(g + 1, n_groups - 1)
    need = jnp.where(jnp.logical_and(g + 1 < n_groups, gns_ref[nxt] > 0), gns_ref[nxt], 0)
    nxt_base = gsub0_ref[nxt] * sub

    def issue_block(blk):
        for u in range(sub):
            r = blk * sub + u
            row_copy(slot_ref[nxt_base + r], 1 - cur, r).start()

    def out_copy(r0, rows):
        row0 = s0 * sub + r0
        return pltpu.make_async_copy(acc.at[pl.ds(r0, rows), :], outs_hbm.at[pl.ds(row0, rows), :], sem_out)

    def compute(r0, rows, issue_blk=None):
        los, his = [], []
        for j in range(pieces):
            w = stage[cur, pl.ds(r0 * pieces + j, rows, stride=pieces), :]
            los.append(lax.bitcast_convert_type(w << 16, F32).astype(BF16))
            his.append(lax.bitcast_convert_type(w & jnp.uint32(0xFFFF0000), F32).astype(BF16))
        xl = jnp.concatenate(los, axis=1)
        xh = jnp.concatenate(his, axis=1)
        hg = (jnp.dot(xl, w1g_ref[:half, :].astype(BF16), preferred_element_type=F32)
              + jnp.dot(xh, w1g_ref[half:, :].astype(BF16), preferred_element_type=F32) + b1g_ref[...])
        hl = (jnp.dot(xl, w1l_ref[:half, :].astype(BF16), preferred_element_type=F32)
              + jnp.dot(xh, w1l_ref[half:, :].astype(BF16), preferred_element_type=F32) + b1l_ref[...])
        gate = jnp.minimum(hg, SWIGLU_LIMIT)
        lin = jnp.clip(hl, -SWIGLU_LIMIT, SWIGLU_LIMIT)
        act = gate * jax.nn.sigmoid(SWIGLU_ALPHA * gate) * (lin + 1.0)
        acc[pl.ds(r0, rows), :] += jnp.dot(act.astype(BF16), w2_ref[...].astype(BF16),
                                           preferred_element_type=F32)
        if issue_blk is not None:
            issue_block(issue_blk)

        @pl.when(c == last)
        def _():
            out_copy(r0, rows).start()

    @pl.when(ns > 0)
    def _():
        @pl.when(c == 0)
        def _load():
            @pl.when(g == 0)
            def _():
                _build_slot_table(dest_ref, cnt_ref, pst_ref, slot_ref, n_tok)
                issue_rows(0, 0, 0, ns)

            def wt(i, carry):
                for u in range(ROW_UNROLL):
                    row_copy(0, cur, i * ROW_UNROLL + u).wait()
                return carry
            lax.fori_loop(0, ns * (sub // ROW_UNROLL), wt, 0)

            def init(s, carry):
                acc[pl.ds(pl.multiple_of(s * sub, sub), sub), :] = jnp.broadcast_to(
                    b2_ref[...], (sub, acc.shape[1]))
                return carry
            lax.fori_loop(0, ns, init, 0)

        full = TRIP_SUBS * sub
        n_full = ns // TRIP_SUBS

        def for_each_trip(fn, fn_full=None):
            def trip(i, carry):
                fn(pl.multiple_of(i * full, full), full)
                return carry
            if fn_full is None:
                lax.fori_loop(0, n_full, trip, 0)
            else:
                fn_full()
            r0 = n_full * full
            width = full // 2
            while width >= sub:
                has = (ns * sub) & width

                @pl.when(has != 0)
                def _(r0=r0, width=width):
                    fn(pl.multiple_of(r0, width), width)
                r0 = r0 + has
                width //= 2

        def full_trips():
            n_issue = jnp.clip(need - c * n_full, 0, n_full)

            def with_issue(i, carry):
                compute(pl.multiple_of(i * full, full), full, issue_blk=c * n_full + i)
                return carry

            def plain(i, carry):
                compute(pl.multiple_of(i * full, full), full)
                return carry
            lax.fori_loop(0, n_issue, with_issue, 0)
            lax.fori_loop(n_issue, n_full, plain, 0)

        for_each_trip(compute, full_trips)

        @pl.when(c == last)
        def _drain():
            done = jnp.minimum(need, n_chunks * n_full)
            issue_rows(nxt, 1 - cur, done, need - done)
            for_each_trip(lambda r0, rows: out_copy(r0, rows).wait())

            @pl.when(jnp.logical_or(g + 1 >= n_groups, gns_ref[nxt] == 0))
            def _tail():
                acc[pl.ds(0, sub), :] = jnp.zeros((sub, acc.shape[1]), F32)
                first = s0 + ns
                n_tail = outs_hbm.shape[0] // sub - first

                def tail_copy(i):
                    row0 = pl.multiple_of((first + i) * sub, sub)
                    return pltpu.make_async_copy(acc.at[pl.ds(0, sub), :], outs_hbm.at[pl.ds(row0, sub), :],
                                                 sem_out)

                def st(i, carry):
                    tail_copy(i).start()
                    return carry

                def wt(i, carry):
                    tail_copy(i).wait()
                    return carry
                lax.fori_loop(0, n_tail, st, 0)
                lax.fori_loop(0, n_tail, wt, 0)


def _expert_call(gexp, gsub0, gns, dest_flat, counts, pstart, n_slots, hp, w1, b1, w2, b2):
    half = w1.shape[1] // 2
    n_exp, d, ff2 = w1.shape
    ff = ff2 // 2
    tc = FF_CHUNK
    n_chunks = ff // tc
    n_groups = gexp.shape[0]
    rows = GROUP_SUBS * SLOT_BLOCK

    def chunk(c, gn, g):
        return jnp.where(gn[g] > 0, c, n_chunks - 1)

    grid_spec = pltpu.PrefetchScalarGridSpec(
        num_scalar_prefetch=6,
        grid=(n_groups, n_chunks),
        in_specs=[
            pl.BlockSpec(memory_space=pl.ANY),
            pl.BlockSpec((None, d, tc), lambda g, c, ge, gs, gn, *_: (ge[g], 0, chunk(c, gn, g))),
            pl.BlockSpec((None, d, tc), lambda g, c, ge, gs, gn, *_: (ge[g], 0, n_chunks + chunk(c, gn, g))),
            pl.BlockSpec((None, 1, tc), lambda g, c, ge, gs, gn, *_: (ge[g], 0, chunk(c, gn, g))),
            pl.BlockSpec((None, 1, tc), lambda g, c, ge, gs, gn, *_: (ge[g], 0, n_chunks + chunk(c, gn, g))),
            pl.BlockSpec((None, tc, d), lambda g, c, ge, gs, gn, *_: (ge[g], chunk(c, gn, g), 0)),
            pl.BlockSpec((None, 1, d), lambda g, c, ge, gs, gn, *_: (ge[g], 0, 0)),
        ],
        out_specs=pl.BlockSpec(memory_space=pl.ANY),
        scratch_shapes=[
            pltpu.VMEM((2, rows * (half // LANES), LANES), U32),
            pltpu.VMEM((rows, d), F32),
            pltpu.SMEM((n_slots + SLOT_BLOCK,), I32),
            pltpu.SemaphoreType.DMA((2,)), pltpu.SemaphoreType.DMA(()),
        ],
    )
    return pl.pallas_call(
        functools.partial(_expert_body, n_chunks=n_chunks, n_groups=n_groups,
                          n_tok=dest_flat.shape[0] // TOP_K),
        grid_spec=grid_spec,
        out_shape=jax.ShapeDtypeStruct((n_slots, d), F32),
        compiler_params=_cparams(("arbitrary", "arbitrary")),
        name="expert",
    )(gexp, gsub0, gns, dest_flat, counts, pstart, hp, w1, w1, b1.reshape(n_exp, 1, ff2),
      b1.reshape(n_exp, 1, ff2), w2, b2.reshape(n_exp, 1, d))


def _final_body(dest_ref, x1_ref, w4_ref, p_ref, wple_ref, wpg_ref, bpg_ref, npl_ref, outs_hbm,
                o_ref, gbuf, sems, *, tm, n_steps):
    i = pl.program_id(0)
    cur = i % 2

    def row_copy(slot, buf, j, q, u):
        return pltpu.make_async_copy(outs_hbm.at[pl.ds(slot, 1), :], gbuf.at[buf, j, q, pl.ds(u, 1), :],
                                     sems.at[buf])

    def issue(step, buf):
        def rows(q, carry):
            for u in range(SUBLANES):
                t = step * tm + q * SUBLANES + u
                for j in range(TOP_K):
                    row_copy(dest_ref[t * TOP_K + j], buf, j, q, u).start(priority=j % 2)
            return carry
        lax.fori_loop(0, tm // SUBLANES, rows, 0)

    @pl.when(i == 0)
    def _():
        issue(0, 0)

    pe = jnp.dot(p_ref[...].astype(BF16), wple_ref[...].astype(BF16), preferred_element_type=F32)

    def wt(q, carry):
        for u in range(SUBLANES):
            for j in range(TOP_K):
                row_copy(0, cur, j, q, u).wait()
        return carry
    lax.fori_loop(0, tm // SUBLANES, wt, 0)

    w4 = w4_ref[...]
    d = o_ref.shape[1]
    y = gbuf[cur, 0].reshape(tm, d) * w4[:, 0:1]
    for j in range(1, TOP_K):
        y = y + gbuf[cur, j].reshape(tm, d) * w4[:, j:j + 1]
    x2 = x1_ref[...] + y
    hn = _rmsnorm_rows(x2, npl_ref[...])
    gate = jax.nn.sigmoid(jnp.dot(hn.astype(BF16), wpg_ref[...].astype(BF16), preferred_element_type=F32)
                          + bpg_ref[...])
    o_ref[...] = x2 + gate * pe

    nxt = jnp.minimum(i + 1, n_steps - 1)
    for q in range(tm // SUBLANES):
        for u in range(SUBLANES):
            t = nxt * tm + q * SUBLANES + u
            for j in range(TOP_K):
                row_copy(dest_ref[t * TOP_K + j], 1 - cur, j, q, u).start(priority=j % 2)

    @pl.when(i == n_steps - 1)
    def _():
        def wt_last(q, carry):
            for u in range(SUBLANES):
                for j in range(TOP_K):
                    row_copy(0, 1 - cur, j, q, u).wait()
            return carry
        lax.fori_loop(0, tm // SUBLANES, wt_last, 0)


def _final_call(dest_flat, x1, w4, p2d, wple, wpg, bpg_row, npl_row, outs, tm=256):
    n, d = x1.shape
    tm = min(tm, n)
    pd = p2d.shape[1]
    res = lambda shape: pl.BlockSpec(shape, lambda i, dst: (0,) * len(shape), pipeline_mode=pl.Buffered(1))
    grid_spec = pltpu.PrefetchScalarGridSpec(
        num_scalar_prefetch=1,
        grid=(n // tm,),
        in_specs=[pl.BlockSpec((tm, d), lambda i, dst: (i, 0)),
                  pl.BlockSpec((tm, LANES), lambda i, dst: (i, 0)),
                  pl.BlockSpec((tm, pd), lambda i, dst: (i, 0)),
                  res((pd, d)), res((d, d)), res((1, d)), res((1, d)),
                  pl.BlockSpec(memory_space=pl.ANY)],
        out_specs=pl.BlockSpec((tm, d), lambda i, dst: (i, 0)),
        scratch_shapes=[pltpu.VMEM((2, TOP_K, tm // SUBLANES, SUBLANES, d), F32),
                        pltpu.SemaphoreType.DMA((2,))],
    )
    return pl.pallas_call(
        functools.partial(_final_body, tm=tm, n_steps=n // tm),
        grid_spec=grid_spec,
        out_shape=jax.ShapeDtypeStruct((n, d), F32),
        compiler_params=_cparams(("arbitrary",)),
        name="final",
    )(dest_flat, x1, w4, p2d, wple, wpg, bpg_row, npl_row, outs)


def _t5_bucket(rel):
    n = -rel
    nb = REL_BUCKETS // 2
    ret = jnp.where(n < 0, nb, 0)
    n = jnp.abs(n)
    max_exact = nb // 2
    large = max_exact + (jnp.log(jnp.maximum(n, 1).astype(jnp.float32) / max_exact)
                         / math.log(REL_MAX_DIST / max_exact) * (nb - max_exact)).astype(jnp.int32)
    large = jnp.minimum(large, nb - 1)
    return ret + jnp.where(n < max_exact, n, large)


def _bias_body(rv_ref, o_ref, *, t):
    x = jnp.broadcast_to(rv_ref[...], (t, rv_ref.shape[1]))
    o_ref[...] = pltpu.roll(x, 1, 1, stride=1, stride_axis=0)[:, t:]


def _diff_bias_table(rel_bias, seq, t):
    rel = jnp.arange(seq + t) - (seq - 1)
    rv = (jnp.transpose(rel_bias[_t5_bucket(rel)], (1, 0)).astype(F32) * LOG2E)[:, None, :]
    return pl.pallas_call(
        functools.partial(_bias_body, t=t),
        grid=(N_HEADS,),
        in_specs=[pl.BlockSpec((None, 1, seq + t), lambda h: (h, 0, 0))],
        out_specs=pl.BlockSpec((None, t, seq), lambda h: (h, 0, 0)),
        out_shape=jax.ShapeDtypeStruct((N_HEADS, t, seq), F32),
        compiler_params=_cparams(("parallel",)),
        name="bias_table",
    )(rv)


def _pad_cols(a, width):
    return jnp.pad(a, ((0, 0), (0, width - a.shape[1])))


def _group_table(counts):
    n_sub = (counts + SLOT_BLOCK - 1) // SLOT_BLOCK
    sub_start = jnp.cumsum(n_sub) - n_sub
    n_grp = (n_sub + GROUP_SUBS - 1) // GROUP_SUBS
    grp_end = jnp.cumsum(n_grp)
    total = grp_end[-1]
    return n_sub, sub_start, n_grp, grp_end, total


def _layer(x, p_l, w_in, b_gate, b_forget, dq_norm, dk_norm, fq_norm, fk_norm, lq1, lk1, lq2, lk2,
           lambda_init, subln, w_up_a, w_up_b, w_out, rel_bias, norm_mix, norm_moe, w_router,
           b_router, w1, b1, w2, b2, norm_ple, w_ple_gate, b_ple_gate, w_ple):
    b, s, d = x.shape
    n = b * s
    width = N_HEADS * HEAD_DIM
    t = min(ATTN_T, s)
    x2d = x.reshape(n, d)
    row = lambda v: v.reshape(1, -1).astype(F32)

    wt_in = jnp.swapaxes(w_in, 0, 1)
    diff_scale = DIFF_QK_DIM ** -0.5 * LOG2E
    fox_scale = HEAD_DIM ** -0.5 * LOG2E
    aux = jnp.concatenate([jnp.tile(dq_norm * diff_scale, 2 * N_HEADS), jnp.tile(dk_norm, 2 * N_HEADS),
                           jnp.zeros((width,), F32),
                           jnp.tile(fq_norm * fox_scale, N_HEADS), jnp.tile(fk_norm, N_HEADS),
                           jnp.zeros((width,), F32), b_gate.reshape(-1)])
    proj, f_logit = _inproj_call(x2d, row(norm_mix), wt_in, row(aux))
    gates = proj
    qkv = proj.reshape(b, s, -1)
    first = INPROJ_GATE_BLOCKS * INPROJ_TN // width

    cum = _cum_call(f_logit, _pad_cols(row(b_forget), LANES), b, s)
    cum_bhs = jnp.transpose(cum[:, :N_HEADS].reshape(b, s, N_HEADS), (0, 2, 1))
    cum_row = cum_bhs[:, :, None, :]

    lam_rows = [row(v) for v in (lq1, lk1, lq2, lk2)]
    bias = _diff_bias_table(rel_bias, s, t)
    od = _diff_attn_call(qkv, (first, first + 1, first + 2), bias, lam_rows, row(subln), lambda_init, t)
    of = _fox_attn_call(qkv, (first + 3, first + 4, first + 5), cum.reshape(b, s, LANES), cum_row, t)

    br_pad = jnp.full((1, LANES), NEG, F32).at[0, :N_EXPERTS].set(b_router.astype(F32))
    wr_pad = jnp.pad(jnp.swapaxes(w_router.astype(F32), 0, 1), ((0, LANES - N_EXPERTS), (0, 0)))
    wr_hi = wr_pad.astype(BF16)
    wr_lo = (wr_pad - wr_hi.astype(F32)).astype(BF16)
    x1, hp, logits = _mix_call(od.reshape(n, width), of.reshape(n, width), gates, x2d,
                               w_up_a.astype(BF16), w_up_b.astype(BF16), w_out.astype(BF16),
                               row(norm_moe), wr_hi, wr_lo, br_pad)

    dest128, w4, cnt, ps_row = _route_call(logits)
    counts = cnt[0, :N_EXPERTS].astype(I32)
    pstart = ps_row[0, :N_EXPERTS].astype(I32)
    dest_flat = dest128[:, :TOP_K].reshape(-1)

    n_slots = n * TOP_K + N_EXPERTS * SLOT_BLOCK

    n_sub, sub_start, n_grp, grp_end, total = _group_table(counts)
    max_groups = N_EXPERTS + (n_slots // SLOT_BLOCK) // GROUP_SUBS
    gidx = jnp.arange(max_groups, dtype=I32)
    gvalid = gidx < total
    gsafe = jnp.minimum(gidx, total - 1)
    gexp = jnp.searchsorted(grp_end, gsafe, side="right").astype(I32)
    kth = gsafe - (grp_end - n_grp)[gexp]
    gsub0 = (sub_start[gexp] + kth * GROUP_SUBS).astype(I32)
    gns = jnp.where(gvalid, jnp.minimum(GROUP_SUBS, n_sub[gexp] - kth * GROUP_SUBS), 0).astype(I32)
    def run_experts(ng):
        return _expert_call(gexp[:ng], gsub0[:ng], gns[:ng], dest_flat, counts, pstart, n_slots, hp,
                            w1, b1, w2, b2)

    few = min(N_EXPERTS + 2, max_groups)
    outs = lax.cond(total <= few, lambda: run_experts(few), lambda: run_experts(max_groups))

    out = _final_call(dest_flat, x1, w4, p_l.reshape(n, -1), w_ple, w_ple_gate, row(b_ple_gate),
                      row(norm_ple), outs)
    return out.reshape(b, s, d)


def kernel(x, p, w_in, b_gate, b_forget, dq_norm, dk_norm, fq_norm, fk_norm, lambda_q1, lambda_k1,
           lambda_q2, lambda_k2, subln, w_up_a, w_up_b, w_out, rel_bias, norm_mix, norm_moe,
           w_router, b_router, w1, b1, w2, b2, norm_ple, w_ple_gate, b_ple_gate, w_ple):
    for i in range(w_in.shape[0]):
        lambda_init = 0.8 - 0.6 * math.exp(-0.3 * i)
        x = _layer(x, p[i], w_in[i], b_gate[i], b_forget[i], dq_norm[i], dk_norm[i], fq_norm[i],
                   fk_norm[i], lambda_q1[i], lambda_k1[i], lambda_q2[i], lambda_k2[i], lambda_init,
                   subln[i], w_up_a[i], w_up_b[i], w_out[i], rel_bias, norm_mix[i], norm_moe[i],
                   w_router[i], b_router[i], w1[i], b1[i], w2[i], b2[i], norm_ple[i],
                   w_ple_gate[i], b_ple_gate[i], w_ple[i])
    return x
```
